```python
import math
import jax
import jax.numpy as jnp
from jax import lax
import numpy as np


D_MODEL = 1024
BATCH = 16
SEQ = 2048
DEPTH = 2

HEAD_DIM = 64
FOX_HEADS = (D_MODEL // 2) // HEAD_DIM
DIFF_HEADS = (D_MODEL // 2) // (2 * HEAD_DIM)
SWA_HEADS = D_MODEL // HEAD_DIM
SWA_KV_HEADS = SWA_HEADS // 4
SWA_GROUP = SWA_HEADS // SWA_KV_HEADS
WINDOW = 128
Q_BLOCK = 128
NUM_BUCKETS = 32
MAX_DISTANCE = 128
REL_HEADS = max(DIFF_HEADS, SWA_HEADS)
FFN_HIDDEN = -(-8 * D_MODEL // (3 * 256)) * 256
N_EVEN = (DEPTH + 1) // 2
N_ODD = DEPTH // 2
RMS_EPS = 1e-6
NEG_INF = -1e30
FOX_W = FOX_HEADS * HEAD_DIM
DIFF_W = DIFF_HEADS * 2 * HEAD_DIM
IN_SPLITS = (FOX_W, FOX_W, FOX_W, FOX_HEADS, DIFF_W, DIFF_W, DIFF_W)
IN_SPLIT_IDX = tuple(int(i) for i in np.cumsum(IN_SPLITS)[:-1])
EVEN_IN_WIDTH = sum(IN_SPLITS)
MIX_WIDTH = FOX_W + DIFF_W
SWA_Q_W = SWA_HEADS * HEAD_DIM
SWA_KV_W = SWA_KV_HEADS * HEAD_DIM
SWA_QKV_WIDTH = SWA_Q_W + 2 * SWA_KV_W

kernel_name = 'hybrid_fox_diff_swa_block'


def rms_norm(x, gain):
    xf = x.astype(jnp.float32)
    y = xf * lax.rsqrt(jnp.mean(xf * xf, axis=-1, keepdims=True) + RMS_EPS)
    return (y * gain.astype(jnp.float32)).astype(x.dtype)


def t5_bucket(delta):
    n = jnp.maximum(delta, 0)
    max_exact = NUM_BUCKETS // 2
    nf = jnp.maximum(n, 1).astype(jnp.float32)
    large = max_exact + (jnp.log(nf / max_exact) / math.log(MAX_DISTANCE / max_exact)
                         * (NUM_BUCKETS - max_exact)).astype(jnp.int32)
    large = jnp.minimum(large, NUM_BUCKETS - 1)
    return jnp.where(n < max_exact, n, large)


def rel_bias(table, delta):
    return jnp.transpose(table[t5_bucket(delta)].astype(jnp.float32), (2, 0, 1))


def fox_attention(q, k, v, c):
    S = q.shape[1]
    scale = HEAD_DIM ** -0.5
    outs = []
    for i in range(S // Q_BLOCK):
        start, end = i * Q_BLOCK, (i + 1) * Q_BLOCK
        qpos = jnp.arange(start, end)
        kpos = jnp.arange(end)
        causal = kpos[None, :] <= qpos[:, None]
        s = jnp.einsum('bqhd,bkhd->bhqk', q[:, start:end], k[:, :end]).astype(jnp.float32) * scale
        s = s + c[:, :, start:end, None] - c[:, :, None, :end]
        p = jax.nn.softmax(jnp.where(causal, s, NEG_INF), axis=-1)
        outs.append(jnp.einsum('bhqk,bkhd->bqhd', p.astype(v.dtype), v[:, :end]))
    return jnp.concatenate(outs, axis=1)


def diff_attention(q, k, v, lam, table):
    S = q.shape[1]
    scale = HEAD_DIM ** -0.5
    outs = []
    for i in range(S // Q_BLOCK):
        start, end = i * Q_BLOCK, (i + 1) * Q_BLOCK
        qpos = jnp.arange(start, end)
        kpos = jnp.arange(end)
        delta = qpos[:, None] - kpos[None, :]
        s = jnp.einsum('bqhmd,bkhmd->bhmqk', q[:, start:end], k[:, :end]).astype(jnp.float32) * scale
        s = s + rel_bias(table, delta)[None, :, None]
        p = jax.nn.softmax(jnp.where(delta >= 0, s, NEG_INF), axis=-1)
        a = p[:, :, 0] - lam * p[:, :, 1]
        outs.append(jnp.einsum('bhqk,bkhe->bqhe', a.astype(v.dtype), v[:, :end]))
    return jnp.concatenate(outs, axis=1)


def even_mixer(h, w_in, b_forget, fox_q_norm, fox_k_norm, diff_q_norm, diff_k_norm,
               lambda_q1, lambda_k1, lambda_q2, lambda_k2, diff_subln, w_out, table, lambda_init):
    B, S, _ = h.shape
    proj = h @ w_in
    fq, fk, fv, ff, dq, dk, dv = jnp.split(proj, IN_SPLIT_IDX, axis=-1)
    fq = rms_norm(fq.reshape(B, S, FOX_HEADS, HEAD_DIM), fox_q_norm)
    fk = rms_norm(fk.reshape(B, S, FOX_HEADS, HEAD_DIM), fox_k_norm)
    fv = fv.reshape(B, S, FOX_HEADS, HEAD_DIM)
    log_f = jax.nn.log_sigmoid((ff + b_forget).astype(jnp.float32))
    c = jnp.transpose(jnp.cumsum(log_f, axis=1), (0, 2, 1))
    fox_out = fox_attention(fq, fk, fv, c).reshape(B, S, FOX_W)
    dq = rms_norm(dq.reshape(B, S, DIFF_HEADS, 2, HEAD_DIM), diff_q_norm)
    dk = rms_norm(dk.reshape(B, S, DIFF_HEADS, 2, HEAD_DIM), diff_k_norm)
    dv = dv.reshape(B, S, DIFF_HEADS, 2 * HEAD_DIM)
    f32 = jnp.float32
    lam = (jnp.exp(jnp.sum(lambda_q1.astype(f32) * lambda_k1.astype(f32)))
           - jnp.exp(jnp.sum(lambda_q2.astype(f32) * lambda_k2.astype(f32))) + lambda_init)
    d_out = diff_attention(dq, dk, dv, lam, table[:, :DIFF_HEADS])
    d_out = (rms_norm(d_out, diff_subln) * (1.0 - lambda_init)).reshape(B, S, DIFF_W)
    return jnp.concatenate([fox_out, d_out], axis=-1) @ w_out


def with_prev_block(t):
    prev = jnp.concatenate([jnp.zeros_like(t[:, :1]), t[:, :-1]], axis=1)
    return jnp.concatenate([prev, t], axis=2)


def odd_mixer(h, w_qkv, q_norm, k_norm, sinks, w_out, table):
    B, S, _ = h.shape
    nb = S // WINDOW
    proj = h @ w_qkv
    q, k, v = jnp.split(proj, (SWA_Q_W, SWA_Q_W + SWA_KV_W), axis=-1)
    q = rms_norm(q.reshape(B, S, SWA_HEADS, HEAD_DIM), q_norm)
    k = rms_norm(k.reshape(B, S, SWA_KV_HEADS, HEAD_DIM), k_norm)
    v = v.reshape(B, S, SWA_KV_HEADS, HEAD_DIM)
    qb = q.reshape(B, nb, WINDOW, SWA_KV_HEADS, SWA_GROUP, HEAD_DIM)
    kk = with_prev_block(k.reshape(B, nb, WINDOW, SWA_KV_HEADS, HEAD_DIM))
    vv = with_prev_block(v.reshape(B, nb, WINDOW, SWA_KV_HEADS, HEAD_DIM))
    s = jnp.einsum('bnqhgd,bnkhd->bnhgqk', qb, kk).astype(jnp.float32) * (HEAD_DIM ** -0.5)
    a_idx = jnp.arange(WINDOW)
    b_idx = jnp.arange(2 * WINDOW)
    delta = WINDOW + a_idx[:, None] - b_idx[None, :]
    bias = rel_bias(table, delta).reshape(SWA_KV_HEADS, SWA_GROUP, WINDOW, 2 * WINDOW)
    kpos = jnp.arange(nb)[:, None, None] * WINDOW - WINDOW + b_idx[None, None, :]
    valid = (delta >= 0) & (delta < WINDOW) & (kpos >= 0)
    s = jnp.where(valid[None, :, None, None], s + bias, NEG_INF)
    sink = sinks.astype(jnp.float32).reshape(SWA_KV_HEADS, SWA_GROUP)[None, None, :, :, None, None]
    m = jnp.maximum(jnp.max(s, axis=-1, keepdims=True), sink)
    e = jnp.exp(s - m)
    p = e / (jnp.sum(e, axis=-1, keepdims=True) + jnp.exp(sink - m))
    out = jnp.einsum('bnhgqk,bnkhd->bnqhgd', p.astype(v.dtype), vv).reshape(B, S, SWA_Q_W)
    return out @ w_out


def swiglu(h, w_gate, w_up, w_down):
    return (jax.nn.silu(h @ w_gate) * (h @ w_up)) @ w_down


def setup_inputs(seed: int = 0) -> dict:
    key = jax.random.key(seed)
    ks = iter(jax.random.split(key, 32))

    def nrm(shape, scale):
        return scale * jax.random.normal(next(ks), shape, jnp.float32)

    def gain(shape):
        return 1.0 + nrm(shape, 0.1)

    inv = D_MODEL ** -0.5
    return {
        'x': nrm((BATCH, SEQ, D_MODEL), 1.0),
        'rel_bias_table': nrm((NUM_BUCKETS, REL_HEADS), 0.5),
        'ev_attn_norm': gain((N_EVEN, D_MODEL)),
        'ev_w_in': nrm((N_EVEN, D_MODEL, EVEN_IN_WIDTH), inv),
        'ev_b_forget': 3.0 + nrm((N_EVEN, FOX_HEADS), 0.5),
        'ev_fox_q_norm': gain((N_EVEN, HEAD_DIM)),
        'ev_fox_k_norm': gain((N_EVEN, HEAD_DIM)),
        'ev_diff_q_norm': gain((N_EVEN, HEAD_DIM)),
        'ev_diff_k_norm': gain((N_EVEN, HEAD_DIM)),
        'ev_lambda_q1': nrm((N_EVEN, HEAD_DIM), 0.1),
        'ev_lambda_k1': nrm((N_EVEN, HEAD_DIM), 0.1),
        'ev_lambda_q2': nrm((N_EVEN, HEAD_DIM), 0.1),
        'ev_lambda_k2': nrm((N_EVEN, HEAD_DIM), 0.1),
        'ev_diff_subln': gain((N_EVEN, 2 * HEAD_DIM)),
        'ev_w_out': nrm((N_EVEN, MIX_WIDTH, D_MODEL), MIX_WIDTH ** -0.5),
        'od_attn_norm': gain((N_ODD, D_MODEL)),
        'od_w_qkv': nrm((N_ODD, D_MODEL, SWA_QKV_WIDTH), inv),
        'od_q_norm': gain((N_ODD, HEAD_DIM)),
        'od_k_norm': gain((N_ODD, HEAD_DIM)),
        'od_sinks': nrm((N_ODD, SWA_HEADS), 0.5),
        'od_w_out': nrm((N_ODD, SWA_Q_W, D_MODEL), SWA_Q_W ** -0.5),
        'ffn_norm': gain((DEPTH, D_MODEL)),
        'w_gate': nrm((DEPTH, D_MODEL, FFN_HIDDEN), inv),
        'w_up': nrm((DEPTH, D_MODEL, FFN_HIDDEN), inv),
        'w_down': nrm((DEPTH, FFN_HIDDEN, D_MODEL), FFN_HIDDEN ** -0.5),
    }


def reference(x, rel_bias_table, ev_attn_norm, ev_w_in, ev_b_forget, ev_fox_q_norm, ev_fox_k_norm,
              ev_diff_q_norm, ev_diff_k_norm, ev_lambda_q1, ev_lambda_k1, ev_lambda_q2, ev_lambda_k2,
              ev_diff_subln, ev_w_out, od_attn_norm, od_w_qkv, od_q_norm, od_k_norm, od_sinks, od_w_out,
              ffn_norm, w_gate, w_up, w_down):
    for layer in range(DEPTH):
        j = layer // 2
        if layer % 2 == 0:
            lambda_init = 0.8 - 0.6 * math.exp(-0.3 * layer)
            h = rms_norm(x, ev_attn_norm[j])
            x = x + even_mixer(h, ev_w_in[j], ev_b_forget[j], ev_fox_q_norm[j], ev_fox_k_norm[j],
                               ev_diff_q_norm[j], ev_diff_k_norm[j], ev_lambda_q1[j], ev_lambda_k1[j],
                               ev_lambda_q2[j], ev_lambda_k2[j], ev_diff_subln[j], ev_w_out[j],
                               rel_bias_table, lambda_init)
        else:
            h = rms_norm(x, od_attn_norm[j])
            x = x + odd_mixer(h, od_w_qkv[j], od_q_norm[j], od_k_norm[j], od_sinks[j], od_w_out[j],
                              rel_bias_table)
        h = rms_norm(x, ffn_norm[layer])
        x = x + swiglu(h, w_gate[layer], w_up[layer], w_down[layer])
    return x
```

```python
import functools
import math

import numpy as np
import jax
import jax.numpy as jnp
from jax import lax
from jax.experimental import pallas as pl
from jax.experimental.pallas import tpu as pltpu

F32 = jnp.float32
BF16 = jnp.bfloat16

D_MODEL = 1024
HEAD_DIM = 64
FOX_HEADS = 8
DIFF_HEADS = 4
SWA_HEADS = 16
SWA_KV_HEADS = 4
SWA_GROUP = SWA_HEADS // SWA_KV_HEADS
WINDOW = 128
NUM_BUCKETS = 32
MAX_DISTANCE = 128
FFN_HIDDEN = 2816
RMS_EPS = 1e-6
NEG_INF = -1e30
FOX_W = FOX_HEADS * HEAD_DIM
DIFF_W = DIFF_HEADS * 2 * HEAD_DIM
QK_SCALE = HEAD_DIM ** -0.5

LANES = 128
MXU_TILE = 256
ROW_TILE = 512
MLP_ROW_TILE = 512
ATT_Q_BLOCK = 256
HIDDEN_CHUNK = 1536
VMEM_LIMIT = 56 * 1024 * 1024

_NT = (((1,), (1,)), ((), ()))


def _dot(a, b):
    return jnp.dot(a, b, preferred_element_type=F32)


def _dot_nt(a, b):
    return lax.dot_general(a, b, _NT, preferred_element_type=F32)


def _rms(xf, gain):
    ms = jnp.mean(xf * xf, axis=-1, keepdims=True)
    return xf * lax.rsqrt(ms + RMS_EPS) * gain


def _head_norm(acc, bd, gain):
    ssq = _dot((acc * acc).astype(BF16), bd)
    return acc * lax.rsqrt(ssq * (1.0 / HEAD_DIM) + RMS_EPS) * gain


def _bucket_ids(delta):
    n = np.maximum(delta, 0)
    max_exact = NUM_BUCKETS // 2
    nf = np.maximum(n, 1).astype(np.float64)
    large = max_exact + (np.log(nf / max_exact) / math.log(MAX_DISTANCE / max_exact)
                         * (NUM_BUCKETS - max_exact)).astype(np.int32)
    large = np.minimum(large, NUM_BUCKETS - 1)
    return np.where(n < max_exact, n, large).astype(np.int32)


def _bias_kernel(table_ref, bsw_ref, bg_ref, sw_ref, g_ref):
    h = pl.program_id(0)
    bsw = bsw_ref[...]
    bg = bg_ref[...]
    far = table_ref[NUM_BUCKETS - 1, h]
    sw = jnp.zeros(bsw.shape, F32)
    g = jnp.zeros(bg.shape, F32)
    for b in range(NUM_BUCKETS):
        val = table_ref[b, h]
        sw = jnp.where(bsw == b, val, sw)
        g = jnp.where(bg == b, val - far, g)
    sw_ref[0] = jnp.where(bsw < 0, NEG_INF, sw)
    g_ref[0] = jnp.where(bg < 0, NEG_INF, g)


def _bias_tiles(table):
    a = np.arange(WINDOW)[:, None]
    b = np.arange(2 * WINDOW)[None, :]
    d_sw = WINDOW + a - b
    bsw = np.where((d_sw >= 0) & (d_sw < WINDOW), _bucket_ids(d_sw), -1).astype(np.int32)
    a = np.arange(ATT_Q_BLOCK)[:, None]
    b = np.arange(2 * ATT_Q_BLOCK)[None, :]
    d_g = ATT_Q_BLOCK + a - b
    bg = np.where(d_g >= 0, _bucket_ids(d_g), -1).astype(np.int32)
    nh = table.shape[1]
    return pl.pallas_call(
        _bias_kernel,
        grid=(nh,),
        in_specs=[
            pl.BlockSpec(memory_space=pltpu.SMEM),
            pl.BlockSpec(bsw.shape, lambda h: (0, 0)),
            pl.BlockSpec(bg.shape, lambda h: (0, 0)),
        ],
        out_specs=[
            pl.BlockSpec((1,) + bsw.shape, lambda h: (h, 0, 0)),
            pl.BlockSpec((1,) + bg.shape, lambda h: (h, 0, 0)),
        ],
        out_shape=[
            jax.ShapeDtypeStruct((nh,) + bsw.shape, F32),
            jax.ShapeDtypeStruct((nh,) + bg.shape, F32),
        ],
        name="rel_bias_tiles",
    )(table, jnp.asarray(bsw), jnp.asarray(bg))


def _proj0_kernel(x_ref, gain_ref, w_ref, bd_ref, hg_ref, bf_ref,
                  fq_ref, fk_ref, fv_ref, dq_ref, dk_ref, dv_ref, crow_ref,
                  carry_ref, *, tiles_per_seq):
    t = pl.program_id(0)
    h = _rms(x_ref[...], gain_ref[...]).astype(BF16)
    bd = bd_ref[...]
    rows = h.shape[0]

    def normed(col0, gidx, out_ref):
        for i in range(FOX_W // MXU_TILE):
            c0 = col0 + i * MXU_TILE
            acc = _dot(h, w_ref[:, c0:c0 + MXU_TILE])
            g = hg_ref[gidx:gidx + 1, i * MXU_TILE:(i + 1) * MXU_TILE]
            out_ref[:, i * MXU_TILE:(i + 1) * MXU_TILE] = _head_norm(acc, bd, g).astype(BF16)

    normed(0 * FOX_W, 0, fq_ref)
    normed(1 * FOX_W, 1, fk_ref)
    fv_ref[...] = _dot(h, w_ref[:, 2 * FOX_W:3 * FOX_W]).astype(BF16)
    normed(3 * FOX_W, 2, dq_ref)
    normed(4 * FOX_W, 3, dk_ref)
    dv_ref[...] = _dot(h, w_ref[:, 5 * FOX_W:6 * FOX_W]).astype(BF16)

    z = _dot(h, w_ref[:, 6 * FOX_W:6 * FOX_W + LANES])
    z8 = z.T[0:FOX_HEADS, :] + bf_ref[...]
    y = -(jnp.maximum(-z8, 0.0) + jnp.log1p(jnp.exp(-jnp.abs(z8))))
    lane = lax.broadcasted_iota(jnp.int32, y.shape, 1)
    shift = 1
    while shift < rows:
        y = y + jnp.where(lane >= shift, pltpu.roll(y, shift, axis=1), 0.0)
        shift *= 2

    @pl.when(t % tiles_per_seq == 0)
    def _():
        carry_ref[...] = jnp.zeros_like(carry_ref)

    c = y + carry_ref[:, 0:1]
    crow_ref[0] = c
    carry_ref[...] = jnp.broadcast_to(c[:, rows - 1:rows], carry_ref.shape)


def _proj0(x2d, gain, w, bd, hg, bf, batch, seq):
    n = x2d.shape[0]
    tm = ROW_TILE
    tps = seq // tm
    wcols = w.shape[1]
    row_spec = pl.BlockSpec((tm, FOX_W), lambda t: (t, 0))
    full = lambda shape: pl.BlockSpec(shape, lambda t: (0,) * len(shape))
    outs = pl.pallas_call(
        functools.partial(_proj0_kernel, tiles_per_seq=tps),
        grid=(n // tm,),
        in_specs=[
            pl.BlockSpec((tm, D_MODEL), lambda t: (t, 0)),
            full((1, D_MODEL)),
            full((D_MODEL, wcols)),
            full((MXU_TILE, MXU_TILE)),
            full(hg.shape),
            full(bf.shape),
        ],
        out_specs=[row_spec] * 6 + [pl.BlockSpec((1, FOX_HEADS, tm), lambda t: (t // tps, 0, t % tps))],
        out_shape=[jax.ShapeDtypeStruct((n, FOX_W), BF16)] * 6
        + [jax.ShapeDtypeStruct((batch, FOX_HEADS, seq), F32)],
        scratch_shapes=[pltpu.VMEM((FOX_HEADS, LANES), F32)],
        compiler_params=pltpu.CompilerParams(dimension_semantics=("arbitrary",),
                                             vmem_limit_bytes=VMEM_LIMIT),
        name="proj_even",
    )(x2d, gain, w, bd, hg, bf)
    return outs


def _fox_kernel(q_ref, k_ref, v_ref, crow_ref, o_ref, va_ref, *, seq):
    j = pl.program_id(1)
    qb = ATT_Q_BLOCK
    lane_s = lax.broadcasted_iota(jnp.int32, (seq, LANES), 1)
    vp = v_ref[0]
    one = jnp.ones((), BF16)
    va_ref[0] = jnp.where(lane_s < HEAD_DIM, vp, one)
    va_ref[1] = jnp.where(lane_s >= HEAD_DIM, vp, one)
    lane_q = lax.broadcasted_iota(jnp.int32, (qb, LANES), 1)
    row = lax.broadcasted_iota(jnp.int32, (qb, qb), 0)
    col = lax.broadcasted_iota(jnp.int32, (qb, qb), 1)
    causal = col <= row
    for qi in range(seq // qb):
        r0 = qi * qb
        end = r0 + qb
        qp = q_ref[0, r0:end, :]
        halves = []
        for hh in range(2):
            own = (lane_q < HEAD_DIM) if hh == 0 else (lane_q >= HEAD_DIM)
            qz = jnp.where(own, qp, jnp.zeros((), BF16))
            crow = crow_ref[0, pl.ds(2 * j + hh, 1), :]
            s_d = _dot_nt(qz, k_ref[0, r0:end, :]) - crow[:, r0:end]
            s_d = jnp.where(causal, s_d, NEG_INF)
            m = jnp.max(s_d, axis=-1, keepdims=True)
            if qi > 0:
                s_o = _dot_nt(qz, k_ref[0, 0:r0, :]) - crow[:, 0:r0]
                m = jnp.maximum(m, jnp.max(s_o, axis=-1, keepdims=True))
            acc = _dot(jnp.exp(s_d - m).astype(BF16), va_ref[hh, r0:end, :])
            if qi > 0:
                acc = acc + _dot(jnp.exp(s_o - m).astype(BF16), va_ref[hh, 0:r0, :])
            denom = pltpu.roll(jnp.where(own, 1.0, acc), HEAD_DIM, axis=1)
            halves.append(acc / denom)
        o_ref[0, r0:end, :] = jnp.where(lane_q < HEAD_DIM, halves[0], halves[1]).astype(BF16)


def _fox_attention(fq, fk, fv, crow, batch, seq):
    blk = pl.BlockSpec((1, seq, LANES), lambda b, j: (b, 0, j))
    return pl.pallas_call(
        functools.partial(_fox_kernel, seq=seq),
        grid=(batch, FOX_W // LANES),
        in_specs=[blk, blk, blk, pl.BlockSpec((1, FOX_HEADS, seq), lambda b, j: (b, 0, 0))],
        out_specs=blk,
        out_shape=jax.ShapeDtypeStruct((batch, seq, FOX_W), BF16),
        scratch_shapes=[pltpu.VMEM((2, seq, LANES), BF16)],
        compiler_params=pltpu.CompilerParams(dimension_semantics=("parallel", "parallel"),
                                             vmem_limit_bytes=VMEM_LIMIT),
        name="fox_attention",
    )(fq, fk, fv, crow)


def _diff_kernel(lam_ref, g_ref, q_ref, k_ref, v_ref, gb_ref, o_ref, va_ref, *, seq, lambda_init):
    qb = ATT_Q_BLOCK
    lv = lam_ref[...]
    lam = (jnp.exp(jnp.sum(lv[0:1] * lv[1:2], axis=-1, keepdims=True))
           - jnp.exp(jnp.sum(lv[2:3] * lv[3:4], axis=-1, keepdims=True)) + lambda_init)
    va_ref[:, 0:LANES] = v_ref[0]
    va_ref[:, LANES:2 * LANES] = jnp.ones((seq, LANES), BF16)
    lane_q = lax.broadcasted_iota(jnp.int32, (qb, LANES), 1)
    bias_prev = gb_ref[0, :, 0:qb]
    bias_diag = gb_ref[0, :, qb:2 * qb]
    gain = g_ref[...] * (1.0 - lambda_init)
    for qi in range(seq // qb):
        r0 = qi * qb
        end = r0 + qb
        qp = q_ref[0, r0:end, :]
        outs = []
        for mm in range(2):
            own = (lane_q < HEAD_DIM) if mm == 0 else (lane_q >= HEAD_DIM)
            qz = jnp.where(own, qp, jnp.zeros((), BF16))
            s_d = _dot_nt(qz, k_ref[0, r0:end, :]) + bias_diag
            m = jnp.max(s_d, axis=-1, keepdims=True)
            if qi >= 1:
                s_p = _dot_nt(qz, k_ref[0, r0 - qb:r0, :]) + bias_prev
                m = jnp.maximum(m, jnp.max(s_p, axis=-1, keepdims=True))
            if qi >= 2:
                s_f = _dot_nt(qz, k_ref[0, 0:r0 - qb, :])
                m = jnp.maximum(m, jnp.max(s_f, axis=-1, keepdims=True))
            acc = _dot(jnp.exp(s_d - m).astype(BF16), va_ref[r0:end, :])
            if qi >= 1:
                acc = acc + _dot(jnp.exp(s_p - m).astype(BF16), va_ref[r0 - qb:r0, :])
            if qi >= 2:
                acc = acc + _dot(jnp.exp(s_f - m).astype(BF16), va_ref[0:r0 - qb, :])
            outs.append(acc[:, 0:LANES] / acc[:, LANES:2 * LANES])
        o = outs[0] - lam * outs[1]
        o_ref[0, r0:end, :] = _rms(o, gain).astype(BF16)


def _diff_attention(lam4, subln, dq, dk, dv, gbias, batch, seq, lambda_init):
    blk = pl.BlockSpec((1, seq, LANES), lambda b, h: (b, 0, h))
    return pl.pallas_call(
        functools.partial(_diff_kernel, seq=seq, lambda_init=lambda_init),
        grid=(batch, DIFF_HEADS),
        in_specs=[
            pl.BlockSpec(lam4.shape, lambda b, h: (0, 0)),
            pl.BlockSpec(subln.shape, lambda b, h: (0, 0)),
            blk, blk, blk,
            pl.BlockSpec((1, ATT_Q_BLOCK, 2 * ATT_Q_BLOCK), lambda b, h: (h, 0, 0)),
        ],
        out_specs=blk,
        out_shape=jax.ShapeDtypeStruct((batch, seq, DIFF_W), BF16),
        scratch_shapes=[pltpu.VMEM((seq, 2 * LANES), BF16)],
        compiler_params=pltpu.CompilerParams(dimension_semantics=("parallel", "parallel"),
                                             vmem_limit_bytes=VMEM_LIMIT),
        name="diff_attention",
    )(lam4, subln, dq, dk, dv, gbias)


def _proj1_kernel(x_ref, gain_ref, w_ref, bd_ref, hg_ref, q_ref, k_ref, v_ref):
    h = _rms(x_ref[...], gain_ref[...]).astype(BF16)
    bd = bd_ref[...]
    qw = SWA_HEADS * HEAD_DIM
    kw = SWA_KV_HEADS * HEAD_DIM
    for i in range(qw // MXU_TILE):
        acc = _dot(h, w_ref[:, i * MXU_TILE:(i + 1) * MXU_TILE])
        q_ref[:, i * MXU_TILE:(i + 1) * MXU_TILE] = _head_norm(acc, bd, hg_ref[0:1, :]).astype(BF16)
    acc = _dot(h, w_ref[:, qw:qw + kw])
    k_ref[...] = _head_norm(acc, bd, hg_ref[1:2, :]).astype(BF16)
    v_ref[...] = _dot(h, w_ref[:, qw + kw:qw + 2 * kw]).astype(BF16)


def _proj1(x2d, gain, w, bd, hg):
    n = x2d.shape[0]
    tm = ROW_TILE
    qw = SWA_HEADS * HEAD_DIM
    kw = SWA_KV_HEADS * HEAD_DIM
    full = lambda shape: pl.BlockSpec(shape, lambda t: (0,) * len(shape))
    return pl.pallas_call(
        _proj1_kernel,
        grid=(n // tm,),
        in_specs=[
            pl.BlockSpec((tm, D_MODEL), lambda t: (t, 0)),
            full((1, D_MODEL)),
            full(w.shape),
            full((MXU_TILE, MXU_TILE)),
            full(hg.shape),
        ],
        out_specs=[pl.BlockSpec((tm, qw), lambda t: (t, 0)),
                   pl.BlockSpec((tm, kw), lambda t: (t, 0)),
                   pl.BlockSpec((tm, kw), lambda t: (t, 0))],
        out_shape=[jax.ShapeDtypeStruct((n, qw), BF16),
                   jax.ShapeDtypeStruct((n, kw), BF16),
                   jax.ShapeDtypeStruct((n, kw), BF16)],
        compiler_params=pltpu.CompilerParams(dimension_semantics=("parallel",),
                                             vmem_limit_bytes=VMEM_LIMIT),
        name="proj_odd",
    )(x2d, gain, w, bd, hg)


def _swa_kernel(sink_ref, q_ref, k_ref, v_ref, sb_ref, o_ref, kk_ref, vv_ref, *, seq):
    kvh = pl.program_id(1)
    w = WINDOW
    lane_s = lax.broadcasted_iota(jnp.int32, (seq, LANES), 1)
    pick = jnp.where(lane_s >= HEAD_DIM, 1, 0) == (kvh % 2)
    kf = k_ref[0].astype(F32)
    vf = v_ref[0].astype(F32)
    kk_ref[...] = jnp.where(pick, kf, pltpu.roll(kf, HEAD_DIM, axis=1)).astype(BF16)
    vv_ref[:, 0:LANES] = jnp.where(pick, vf, pltpu.roll(vf, HEAD_DIM, axis=1)).astype(BF16)
    vv_ref[:, LANES:2 * LANES] = jnp.ones((seq, LANES), BF16)
    sink = jnp.concatenate(
        [jnp.full((w, 1), sink_ref[kvh * SWA_GROUP + g], F32) for g in range(SWA_GROUP)], axis=0)
    lane_q = lax.broadcasted_iota(jnp.int32, (w, LANES), 1)
    low = lane_q < HEAD_DIM
    zero = jnp.zeros((), BF16)
    for n in range(seq // w):
        r0 = n * w
        qb = q_ref[0, r0:r0 + w, :]
        qa, qc = qb[:, 0:LANES], qb[:, LANES:2 * LANES]
        q4 = jnp.concatenate([jnp.where(low, qa, zero), jnp.where(low, zero, qa),
                              jnp.where(low, qc, zero), jnp.where(low, zero, qc)], axis=0)
        if n == 0:
            s = _dot_nt(q4, kk_ref[0:w, :]) + sb_ref[0, :, w:2 * w]
            vals = vv_ref[0:w, :]
        else:
            s = _dot_nt(q4, kk_ref[r0 - w:r0 + w, :]) + sb_ref[0]
            vals = vv_ref[r0 - w:r0 + w, :]
        m = jnp.maximum(jnp.max(s, axis=-1, keepdims=True), sink)
        acc = _dot(jnp.exp(s - m).astype(BF16), vals)
        o4 = acc[:, 0:LANES] / (acc[:, LANES:2 * LANES] + jnp.exp(sink - m))
        o_ref[0, r0:r0 + w, :] = jnp.concatenate(
            [jnp.where(low, o4[0:w], o4[w:2 * w]),
             jnp.where(low, o4[2 * w:3 * w], o4[3 * w:4 * w])], axis=1).astype(BF16)


def _swa_attention(sinks, q, k, v, swbias, batch, seq):
    gw = SWA_GROUP * HEAD_DIM
    return pl.pallas_call(
        functools.partial(_swa_kernel, seq=seq),
        grid=(batch, SWA_KV_HEADS),
        in_specs=[
            pl.BlockSpec(memory_space=pltpu.SMEM),
            pl.BlockSpec((1, seq, gw), lambda b, h: (b, 0, h)),
            pl.BlockSpec((1, seq, LANES), lambda b, h: (b, 0, h // 2)),
            pl.BlockSpec((1, seq, LANES), lambda b, h: (b, 0, h // 2)),
            pl.BlockSpec((1, SWA_GROUP * WINDOW, 2 * WINDOW), lambda b, h: (h, 0, 0)),
        ],
        out_specs=pl.BlockSpec((1, seq, gw), lambda b, h: (b, 0, h)),
        out_shape=jax.ShapeDtypeStruct((batch, seq, SWA_HEADS * HEAD_DIM), BF16),
        scratch_shapes=[pltpu.VMEM((seq, LANES), BF16), pltpu.VMEM((seq, 2 * LANES), BF16)],
        compiler_params=pltpu.CompilerParams(dimension_semantics=("parallel", "parallel"),
                                             vmem_limit_bytes=VMEM_LIMIT),
        name="swa_attention",
    )(sinks, q, k, v, swbias)


def _outmlp_kernel(*refs, n_mix):
    x_ref = refs[0]
    a_refs = refs[1:1 + n_mix]
    wo_ref, gain_ref, wg_ref, wu_ref, wd_ref, o_ref = refs[1 + n_mix:]
    mix = jnp.concatenate([a_ref[...] for a_ref in a_refs], axis=1)
    x1 = x_ref[...] + _dot(mix, wo_ref[...])
    h = _rms(x1, gain_ref[...]).astype(BF16)
    out = x1
    for c0 in range(0, FFN_HIDDEN, HIDDEN_CHUNK):
        c1 = min(c0 + HIDDEN_CHUNK, FFN_HIDDEN)
        g = _dot(h, wg_ref[:, c0:c1])
        u = _dot(h, wu_ref[:, c0:c1])
        y = (g * (1.0 / (1.0 + jnp.exp(-g))) * u).astype(BF16)
        out = out + _dot(y, wd_ref[c0:c1, :])
    o_ref[...] = out


def _outmlp(x2d, mixes, wo, gain, wg, wu, wd):
    n = x2d.shape[0]
    tm = MLP_ROW_TILE
    full = lambda shape: pl.BlockSpec(shape, lambda t: (0,) * len(shape),
                                      pipeline_mode=pl.Buffered(1))
    in_specs = [pl.BlockSpec((tm, D_MODEL), lambda t: (t, 0))]
    in_specs += [pl.BlockSpec((tm, a.shape[1]), lambda t: (t, 0)) for a in mixes]
    in_specs += [full(wo.shape), full(gain.shape), full(wg.shape), full(wu.shape), full(wd.shape)]
    return pl.pallas_call(
        functools.partial(_outmlp_kernel, n_mix=len(mixes)),
        grid=(n // tm,),
        in_specs=in_specs,
        out_specs=pl.BlockSpec((tm, D_MODEL), lambda t: (t, 0)),
        out_shape=jax.ShapeDtypeStruct((n, D_MODEL), F32),
        compiler_params=pltpu.CompilerParams(dimension_semantics=("parallel",),
                                             vmem_limit_bytes=VMEM_LIMIT),
        name="outproj_swiglu",
    )(x2d, *mixes, wo, gain, wg, wu, wd)


def _block_diag_ones():
    i = np.arange(MXU_TILE)
    return jnp.asarray((i[:, None] // HEAD_DIM) == (i[None, :] // HEAD_DIM), BF16)


def kernel(x, rel_bias_table, ev_attn_norm, ev_w_in, ev_b_forget, ev_fox_q_norm, ev_fox_k_norm,
           ev_diff_q_norm, ev_diff_k_norm, ev_lambda_q1, ev_lambda_k1, ev_lambda_q2, ev_lambda_k2,
           ev_diff_subln, ev_w_out, od_attn_norm, od_w_qkv, od_q_norm, od_k_norm, od_sinks, od_w_out,
           ffn_norm, w_gate, w_up, w_down):
    batch, seq, d = x.shape
    n = batch * seq
    x2d = x.reshape(n, d)
    bd = _block_diag_ones()
    swbias, gbias = _bias_tiles(rel_bias_table.astype(F32))
    swbias = swbias.reshape(SWA_KV_HEADS, SWA_GROUP * WINDOW, 2 * WINDOW)

    lambda_init = 0.8 - 0.6 * math.exp(-0.3 * 0)
    w_in = ev_w_in[0]
    ff0 = 3 * FOX_W
    d0 = ff0 + FOX_HEADS
    w0 = jnp.concatenate(
        [w_in[:, 0:ff0], w_in[:, d0:d0 + 3 * DIFF_W], w_in[:, ff0:d0],
         jnp.zeros((d, LANES - FOX_HEADS), w_in.dtype)], axis=1).astype(BF16)
    tile8 = lambda g, scale=1.0: jnp.tile(g.astype(F32) * scale, FOX_W // HEAD_DIM)
    hg0 = jnp.stack([tile8(ev_fox_q_norm[0], QK_SCALE), tile8(ev_fox_k_norm[0]),
                     tile8(ev_diff_q_norm[0], QK_SCALE), tile8(ev_diff_k_norm[0])])
    bf = ev_b_forget[0].astype(F32).reshape(FOX_HEADS, 1)
    fq, fk, fv, dq, dk, dv, crow = _proj0(
        x2d, ev_attn_norm[0].astype(F32).reshape(1, d), w0, bd, hg0, bf, batch, seq)
    r3 = lambda a: a.reshape(batch, seq, a.shape[-1])
    fox = _fox_attention(r3(fq), r3(fk), r3(fv), crow, batch, seq)
    lam4 = jnp.stack([ev_lambda_q1[0], ev_lambda_k1[0], ev_lambda_q2[0], ev_lambda_k2[0]]).astype(F32)
    dout = _diff_attention(lam4, ev_diff_subln[0].astype(F32).reshape(1, 2 * HEAD_DIM),
                           r3(dq), r3(dk), r3(dv), gbias, batch, seq, lambda_init)
    x2d = _outmlp(x2d, [fox.reshape(n, FOX_W), dout.reshape(n, DIFF_W)], ev_w_out[0].astype(BF16),
                  ffn_norm[0].astype(F32).reshape(1, d),
                  w_gate[0].astype(BF16), w_up[0].astype(BF16), w_down[0].astype(BF16))

    tile4 = lambda g, scale=1.0: jnp.tile(g.astype(F32) * scale, MXU_TILE // HEAD_DIM)
    hg1 = jnp.stack([tile4(od_q_norm[0], QK_SCALE), tile4(od_k_norm[0])])
    q, k, v = _proj1(x2d, od_attn_norm[0].astype(F32).reshape(1, d), od_w_qkv[0].astype(BF16), bd, hg1)
    swa = _swa_attention(od_sinks[0].astype(F32), r3(q), r3(k), r3(v), swbias, batch, seq)
    x2d = _outmlp(x2d, [swa.reshape(n, SWA_HEADS * HEAD_DIM)], od_w_out[0].astype(BF16),
                  ffn_norm[1].astype(F32).reshape(1, d),
                  w_gate[1].astype(BF16), w_up[1].astype(BF16), w_down[1].astype(BF16))
    return x2d.reshape(batch, seq, d)
```

```python
import functools
import math

import numpy as np
import jax
import jax.numpy as jnp
from jax import lax
from jax.experimental import pallas as pl
from jax.experimental.pallas import tpu as pltpu

F32 = jnp.float32
BF16 = jnp.bfloat16

D_MODEL = 1024
HEAD_DIM = 64
FOX_HEADS = 8
DIFF_HEADS = 4
SWA_HEADS = 16
SWA_KV_HEADS = 4
SWA_GROUP = SWA_HEADS // SWA_KV_HEADS
WINDOW = 128
NUM_BUCKETS = 32
MAX_DISTANCE = 128
FFN_HIDDEN = 2816
RMS_EPS = 1e-6
NEG_INF = -1e30
FOX_W = FOX_HEADS * HEAD_DIM
DIFF_W = DIFF_HEADS * 2 * HEAD_DIM
QK_SCALE = HEAD_DIM ** -0.5

LANES = 128
MXU_TILE = 256
ROW_TILE = 512
MLP_ROW_TILE = 512
ATT_Q_BLOCK = 256
HIDDEN_CHUNK = 1536
VMEM_LIMIT = 56 * 1024 * 1024

_NT = (((1,), (1,)), ((), ()))


def _dot(a, b):
    return jnp.dot(a, b, preferred_element_type=F32)


def _dot_nt(a, b):
    return lax.dot_general(a, b, _NT, preferred_element_type=F32)


def _rms(xf, gain):
    ms = jnp.mean(xf * xf, axis=-1, keepdims=True)
    return xf * lax.rsqrt(ms + RMS_EPS) * gain


def _head_norm(acc, bd, gain):
    msq = _dot((acc * acc).astype(BF16), bd)
    return acc * lax.rsqrt(msq + RMS_EPS) * gain


def _bucket_ids(delta):
    n = np.maximum(delta, 0)
    max_exact = NUM_BUCKETS // 2
    nf = np.maximum(n, 1).astype(np.float64)
    large = max_exact + (np.log(nf / max_exact) / math.log(MAX_DISTANCE / max_exact)
                         * (NUM_BUCKETS - max_exact)).astype(np.int32)
    large = np.minimum(large, NUM_BUCKETS - 1)
    return np.where(n < max_exact, n, large).astype(np.int32)


def _bias_kernel(table_ref, bsw_ref, bg_ref, sw_ref, g_ref):
    h = pl.program_id(0)
    bsw = bsw_ref[...]
    bg = bg_ref[...]
    far = table_ref[NUM_BUCKETS - 1, h]
    sw = jnp.zeros(bsw.shape, F32)
    g = jnp.zeros(bg.shape, F32)
    for b in range(NUM_BUCKETS):
        val = table_ref[b, h]
        sw = jnp.where(bsw == b, val, sw)
        g = jnp.where(bg == b, val - far, g)
    sw_ref[0] = jnp.where(bsw < 0, NEG_INF, sw)
    g_ref[0] = jnp.where(bg < 0, NEG_INF, g)


def _bias_tiles(table):
    a = np.arange(WINDOW)[:, None]
    b = np.arange(2 * WINDOW)[None, :]
    d_sw = WINDOW + a - b
    bsw = np.where((d_sw >= 0) & (d_sw < WINDOW), _bucket_ids(d_sw), -1).astype(np.int32)
    a = np.arange(ATT_Q_BLOCK)[:, None]
    b = np.arange(2 * ATT_Q_BLOCK)[None, :]
    d_g = ATT_Q_BLOCK + a - b
    bg = np.where(d_g >= 0, _bucket_ids(d_g), -1).astype(np.int32)
    nh = table.shape[1]
    return pl.pallas_call(
        _bias_kernel,
        grid=(nh,),
        in_specs=[
            pl.BlockSpec(memory_space=pltpu.SMEM),
            pl.BlockSpec(bsw.shape, lambda h: (0, 0)),
            pl.BlockSpec(bg.shape, lambda h: (0, 0)),
        ],
        out_specs=[
            pl.BlockSpec((1,) + bsw.shape, lambda h: (h, 0, 0)),
            pl.BlockSpec((1,) + bg.shape, lambda h: (h, 0, 0)),
        ],
        out_shape=[
            jax.ShapeDtypeStruct((nh,) + bsw.shape, F32),
            jax.ShapeDtypeStruct((nh,) + bg.shape, F32),
        ],
        name="rel_bias_tiles",
    )(table, jnp.asarray(bsw), jnp.asarray(bg))


def _proj0_kernel(x_ref, gain_ref, w_ref, bd_ref, hg_ref, bf_ref,
                  fq_ref, fk_ref, fv_ref, dq_ref, dk_ref, dv_ref, crow_ref,
                  carry_ref, *, tiles_per_seq):
    t = pl.program_id(0)
    h = _rms(x_ref[...], gain_ref[...]).astype(BF16)
    bd = bd_ref[...]
    rows = h.shape[0]

    proj = _dot(h, w_ref[...])

    def normed(col0, gidx, out_ref):
        for i in range(FOX_W // MXU_TILE):
            c0 = col0 + i * MXU_TILE
            g = hg_ref[gidx:gidx + 1, i * MXU_TILE:(i + 1) * MXU_TILE]
            out_ref[:, i * MXU_TILE:(i + 1) * MXU_TILE] = _head_norm(
                proj[:, c0:c0 + MXU_TILE], bd, g).astype(BF16)

    normed(0 * FOX_W, 0, fq_ref)
    normed(1 * FOX_W, 1, fk_ref)
    fv_ref[...] = proj[:, 2 * FOX_W:3 * FOX_W].astype(BF16)
    normed(3 * FOX_W, 2, dq_ref)
    normed(4 * FOX_W, 3, dk_ref)
    dv_ref[...] = proj[:, 5 * FOX_W:6 * FOX_W].astype(BF16)

    z = proj[:, 6 * FOX_W:6 * FOX_W + LANES]
    z8 = z.T[0:FOX_HEADS, :] + bf_ref[...]
    y = -(jnp.maximum(-z8, 0.0) + jnp.log1p(jnp.exp(-jnp.abs(z8))))
    lane = lax.broadcasted_iota(jnp.int32, y.shape, 1)
    shift = 1
    while shift < rows:
        y = y + jnp.where(lane >= shift, pltpu.roll(y, shift, axis=1), 0.0)
        shift *= 2

    @pl.when(t % tiles_per_seq == 0)
    def _():
        carry_ref[...] = jnp.zeros_like(carry_ref)

    c = y + carry_ref[:, 0:1]
    crow_ref[0] = c
    carry_ref[...] = jnp.broadcast_to(c[:, rows - 1:rows], carry_ref.shape)


def _proj0(x2d, gain, w, bd, hg, bf, batch, seq):
    n = x2d.shape[0]
    tm = ROW_TILE
    tps = seq // tm
    wcols = w.shape[1]
    row_spec = pl.BlockSpec((tm, FOX_W), lambda t: (t, 0))
    full = lambda shape: pl.BlockSpec(shape, lambda t: (0,) * len(shape))
    outs = pl.pallas_call(
        functools.partial(_proj0_kernel, tiles_per_seq=tps),
        grid=(n // tm,),
        in_specs=[
            pl.BlockSpec((tm, D_MODEL), lambda t: (t, 0)),
            full((1, D_MODEL)),
            full((D_MODEL, wcols)),
            full((MXU_TILE, MXU_TILE)),
            full(hg.shape),
            full(bf.shape),
        ],
        out_specs=[row_spec] * 6 + [pl.BlockSpec((1, FOX_HEADS, tm), lambda t: (t // tps, 0, t % tps))],
        out_shape=[jax.ShapeDtypeStruct((n, FOX_W), BF16)] * 6
        + [jax.ShapeDtypeStruct((batch, FOX_HEADS, seq), F32)],
        scratch_shapes=[pltpu.VMEM((FOX_HEADS, LANES), F32)],
        compiler_params=pltpu.CompilerParams(dimension_semantics=("arbitrary",),
                                             vmem_limit_bytes=VMEM_LIMIT),
        name="proj_even",
    )(x2d, gain, w, bd, hg, bf)
    return outs


def _fox_kernel(q_ref, k_ref, v_ref, crow_ref, o_ref, va_ref, *, seq):
    j = pl.program_id(1)
    qb = ATT_Q_BLOCK
    lane_s = lax.broadcasted_iota(jnp.int32, (seq, LANES), 1)
    vp = v_ref[0]
    one = jnp.ones((), BF16)
    va_ref[0] = jnp.where(lane_s < HEAD_DIM, vp, one)
    va_ref[1] = jnp.where(lane_s >= HEAD_DIM, vp, one)
    lane_q = lax.broadcasted_iota(jnp.int32, (qb, LANES), 1)
    row = lax.broadcasted_iota(jnp.int32, (qb, qb), 0)
    col = lax.broadcasted_iota(jnp.int32, (qb, qb), 1)
    causal = col <= row
    for qi in range(seq // qb):
        r0 = qi * qb
        end = r0 + qb
        qp = q_ref[0, r0:end, :]
        halves = []
        for hh in range(2):
            own = (lane_q < HEAD_DIM) if hh == 0 else (lane_q >= HEAD_DIM)
            qz = jnp.where(own, qp, jnp.zeros((), BF16))
            crow = crow_ref[0, pl.ds(2 * j + hh, 1), :]
            s_d = _dot_nt(qz, k_ref[0, r0:end, :]) - crow[:, r0:end]
            s_d = jnp.where(causal, s_d, NEG_INF)
            m = jnp.max(s_d, axis=-1, keepdims=True)
            if qi > 0:
                s_o = _dot_nt(qz, k_ref[0, 0:r0, :]) - crow[:, 0:r0]
                m = jnp.maximum(m, jnp.max(s_o, axis=-1, keepdims=True))
            acc = _dot(jnp.exp(s_d - m).astype(BF16), va_ref[hh, r0:end, :])
            if qi > 0:
                acc = acc + _dot(jnp.exp(s_o - m).astype(BF16), va_ref[hh, 0:r0, :])
            denom = pltpu.roll(jnp.where(own, 1.0, acc), HEAD_DIM, axis=1)
            halves.append(acc / denom)
        o_ref[0, r0:end, :] = jnp.where(lane_q < HEAD_DIM, halves[0], halves[1]).astype(BF16)


def _fox_attention(fq, fk, fv, crow, batch, seq):
    blk = pl.BlockSpec((1, seq, LANES), lambda b, j: (b, 0, j))
    return pl.pallas_call(
        functools.partial(_fox_kernel, seq=seq),
        grid=(batch, FOX_W // LANES),
        in_specs=[blk, blk, blk, pl.BlockSpec((1, FOX_HEADS, seq), lambda b, j: (b, 0, 0))],
        out_specs=blk,
        out_shape=jax.ShapeDtypeStruct((batch, seq, FOX_W), BF16),
        scratch_shapes=[pltpu.VMEM((2, seq, LANES), BF16)],
        compiler_params=pltpu.CompilerParams(dimension_semantics=("parallel", "parallel"),
                                             vmem_limit_bytes=VMEM_LIMIT),
        name="fox_attention",
    )(fq, fk, fv, crow)


def _diff_kernel(lam_ref, g_ref, q_ref, k_ref, v_ref, gb_ref, o_ref, va_ref, *, seq, lambda_init):
    qb = ATT_Q_BLOCK
    lv = lam_ref[...]
    lam = (jnp.exp(jnp.sum(lv[0:1] * lv[1:2], axis=-1, keepdims=True))
           - jnp.exp(jnp.sum(lv[2:3] * lv[3:4], axis=-1, keepdims=True)) + lambda_init)
    va_ref[:, 0:LANES] = v_ref[0]
    va_ref[:, LANES:2 * LANES] = jnp.ones((seq, LANES), BF16)
    lane_q = lax.broadcasted_iota(jnp.int32, (qb, LANES), 1)
    bias_prev = gb_ref[0, :, 0:qb]
    bias_diag = gb_ref[0, :, qb:2 * qb]
    gain = g_ref[...] * (1.0 - lambda_init)
    for qi in range(seq // qb):
        r0 = qi * qb
        end = r0 + qb
        qp = q_ref[0, r0:end, :]
        outs = []
        for mm in range(2):
            own = (lane_q < HEAD_DIM) if mm == 0 else (lane_q >= HEAD_DIM)
            qz = jnp.where(own, qp, jnp.zeros((), BF16))
            s_d = _dot_nt(qz, k_ref[0, r0:end, :]) + bias_diag
            m = jnp.max(s_d, axis=-1, keepdims=True)
            if qi >= 1:
                s_p = _dot_nt(qz, k_ref[0, r0 - qb:r0, :]) + bias_prev
                m = jnp.maximum(m, jnp.max(s_p, axis=-1, keepdims=True))
            if qi >= 2:
                s_f = _dot_nt(qz, k_ref[0, 0:r0 - qb, :])
                m = jnp.maximum(m, jnp.max(s_f, axis=-1, keepdims=True))
            acc = _dot(jnp.exp(s_d - m).astype(BF16), va_ref[r0:end, :])
            if qi >= 1:
                acc = acc + _dot(jnp.exp(s_p - m).astype(BF16), va_ref[r0 - qb:r0, :])
            if qi >= 2:
                acc = acc + _dot(jnp.exp(s_f - m).astype(BF16), va_ref[0:r0 - qb, :])
            outs.append(acc[:, 0:LANES] / acc[:, LANES:2 * LANES])
        o = outs[0] - lam * outs[1]
        o_ref[0, r0:end, :] = _rms(o, gain).astype(BF16)


def _diff_attention(lam4, subln, dq, dk, dv, gbias, batch, seq, lambda_init):
    blk = pl.BlockSpec((1, seq, LANES), lambda b, h: (b, 0, h))
    return pl.pallas_call(
        functools.partial(_diff_kernel, seq=seq, lambda_init=lambda_init),
        grid=(batch, DIFF_HEADS),
        in_specs=[
            pl.BlockSpec(lam4.shape, lambda b, h: (0, 0)),
            pl.BlockSpec(subln.shape, lambda b, h: (0, 0)),
            blk, blk, blk,
            pl.BlockSpec((1, ATT_Q_BLOCK, 2 * ATT_Q_BLOCK), lambda b, h: (h, 0, 0)),
        ],
        out_specs=blk,
        out_shape=jax.ShapeDtypeStruct((batch, seq, DIFF_W), BF16),
        scratch_shapes=[pltpu.VMEM((seq, 2 * LANES), BF16)],
        compiler_params=pltpu.CompilerParams(dimension_semantics=("parallel", "parallel"),
                                             vmem_limit_bytes=VMEM_LIMIT),
        name="diff_attention",
    )(lam4, subln, dq, dk, dv, gbias)


def _proj1_kernel(x_ref, gain_ref, w_ref, bd_ref, hg_ref, q_ref, k_ref, v_ref):
    h = _rms(x_ref[...], gain_ref[...]).astype(BF16)
    bd = bd_ref[...]
    qw = SWA_HEADS * HEAD_DIM
    kw = SWA_KV_HEADS * HEAD_DIM
    proj = _dot(h, w_ref[...])
    for i in range(qw // MXU_TILE):
        q_ref[:, i * MXU_TILE:(i + 1) * MXU_TILE] = _head_norm(
            proj[:, i * MXU_TILE:(i + 1) * MXU_TILE], bd, hg_ref[0:1, :]).astype(BF16)
    k_ref[...] = _head_norm(proj[:, qw:qw + kw], bd, hg_ref[1:2, :]).astype(BF16)
    v_ref[...] = proj[:, qw + kw:qw + 2 * kw].astype(BF16)


def _proj1(x2d, gain, w, bd, hg):
    n = x2d.shape[0]
    tm = ROW_TILE
    qw = SWA_HEADS * HEAD_DIM
    kw = SWA_KV_HEADS * HEAD_DIM
    full = lambda shape: pl.BlockSpec(shape, lambda t: (0,) * len(shape))
    return pl.pallas_call(
        _proj1_kernel,
        grid=(n // tm,),
        in_specs=[
            pl.BlockSpec((tm, D_MODEL), lambda t: (t, 0)),
            full((1, D_MODEL)),
            full(w.shape),
            full((MXU_TILE, MXU_TILE)),
            full(hg.shape),
        ],
        out_specs=[pl.BlockSpec((tm, qw), lambda t: (t, 0)),
                   pl.BlockSpec((tm, kw), lambda t: (t, 0)),
                   pl.BlockSpec((tm, kw), lambda t: (t, 0))],
        out_shape=[jax.ShapeDtypeStruct((n, qw), BF16),
                   jax.ShapeDtypeStruct((n, kw), BF16),
                   jax.ShapeDtypeStruct((n, kw), BF16)],
        compiler_params=pltpu.CompilerParams(dimension_semantics=("parallel",),
                                             vmem_limit_bytes=VMEM_LIMIT),
        name="proj_odd",
    )(x2d, gain, w, bd, hg)


def _swa_kernel(sink_ref, q_ref, k_ref, v_ref, sb_ref, o_ref, kk_ref, vv_ref, *, seq):
    kvh = pl.program_id(1)
    w = WINDOW
    lane_s = lax.broadcasted_iota(jnp.int32, (seq, LANES), 1)
    pick = jnp.where(lane_s >= HEAD_DIM, 1, 0) == (kvh % 2)
    kf = k_ref[0].astype(F32)
    vf = v_ref[0].astype(F32)
    kk_ref[...] = jnp.where(pick, kf, pltpu.roll(kf, HEAD_DIM, axis=1)).astype(BF16)
    vv_ref[:, 0:LANES] = jnp.where(pick, vf, pltpu.roll(vf, HEAD_DIM, axis=1)).astype(BF16)
    vv_ref[:, LANES:2 * LANES] = jnp.ones((seq, LANES), BF16)
    sink = jnp.concatenate(
        [jnp.full((w, 1), sink_ref[kvh * SWA_GROUP + g], F32) for g in range(SWA_GROUP)], axis=0)
    lane_q = lax.broadcasted_iota(jnp.int32, (w, LANES), 1)
    low = lane_q < HEAD_DIM
    zero = jnp.zeros((), BF16)
    for n in range(seq // w):
        r0 = n * w
        qb = q_ref[0, r0:r0 + w, :]
        qa, qc = qb[:, 0:LANES], qb[:, LANES:2 * LANES]
        q4 = jnp.concatenate([jnp.where(low, qa, zero), jnp.where(low, zero, qa),
                              jnp.where(low, qc, zero), jnp.where(low, zero, qc)], axis=0)
        if n == 0:
            s = _dot_nt(q4, kk_ref[0:w, :]) + sb_ref[0, :, w:2 * w]
            vals = vv_ref[0:w, :]
        else:
            s = _dot_nt(q4, kk_ref[r0 - w:r0 + w, :]) + sb_ref[0]
            vals = vv_ref[r0 - w:r0 + w, :]
        m = jnp.maximum(jnp.max(s, axis=-1, keepdims=True), sink)
        acc = _dot(jnp.exp(s - m).astype(BF16), vals)
        o4 = acc[:, 0:LANES] / (acc[:, LANES:2 * LANES] + jnp.exp(sink - m))
        o_ref[0, r0:r0 + w, :] = jnp.concatenate(
            [jnp.where(low, o4[0:w], o4[w:2 * w]),
             jnp.where(low, o4[2 * w:3 * w], o4[3 * w:4 * w])], axis=1).astype(BF16)


def _swa_attention(sinks, q, k, v, swbias, batch, seq):
    gw = SWA_GROUP * HEAD_DIM
    return pl.pallas_call(
        functools.partial(_swa_kernel, seq=seq),
        grid=(batch, SWA_KV_HEADS),
        in_specs=[
            pl.BlockSpec(memory_space=pltpu.SMEM),
            pl.BlockSpec((1, seq, gw), lambda b, h: (b, 0, h)),
            pl.BlockSpec((1, seq, LANES), lambda b, h: (b, 0, h // 2)),
            pl.BlockSpec((1, seq, LANES), lambda b, h: (b, 0, h // 2)),
            pl.BlockSpec((1, SWA_GROUP * WINDOW, 2 * WINDOW), lambda b, h: (h, 0, 0)),
        ],
        out_specs=pl.BlockSpec((1, seq, gw), lambda b, h: (b, 0, h)),
        out_shape=jax.ShapeDtypeStruct((batch, seq, SWA_HEADS * HEAD_DIM), BF16),
        scratch_shapes=[pltpu.VMEM((seq, LANES), BF16), pltpu.VMEM((seq, 2 * LANES), BF16)],
        compiler_params=pltpu.CompilerParams(dimension_semantics=("parallel", "parallel"),
                                             vmem_limit_bytes=VMEM_LIMIT),
        name="swa_attention",
    )(sinks, q, k, v, swbias)


def _outmlp_kernel(*refs, n_mix):
    x_ref = refs[0]
    a_refs = refs[1:1 + n_mix]
    wo_ref, gain_ref, wg_ref, wu_ref, wd_ref, o_ref = refs[1 + n_mix:]
    mix = jnp.concatenate([a_ref[...] for a_ref in a_refs], axis=1)
    x1 = x_ref[...] + _dot(mix, wo_ref[...])
    h = _rms(x1, gain_ref[...]).astype(BF16)
    out = x1
    for c0 in range(0, FFN_HIDDEN, HIDDEN_CHUNK):
        c1 = min(c0 + HIDDEN_CHUNK, FFN_HIDDEN)
        g = _dot(h, wg_ref[:, c0:c1])
        u = _dot(h, wu_ref[:, c0:c1])
        y = (g * (1.0 / (1.0 + jnp.exp(-g))) * u).astype(BF16)
        out = out + _dot(y, wd_ref[c0:c1, :])
    o_ref[...] = out


def _outmlp(x2d, mixes, wo, gain, wg, wu, wd):
    n = x2d.shape[0]
    tm = MLP_ROW_TILE
    full = lambda shape: pl.BlockSpec(shape, lambda t: (0,) * len(shape),
                                      pipeline_mode=pl.Buffered(1))
    in_specs = [pl.BlockSpec((tm, D_MODEL), lambda t: (t, 0))]
    in_specs += [pl.BlockSpec((tm, a.shape[1]), lambda t: (t, 0)) for a in mixes]
    in_specs += [full(wo.shape), full(gain.shape), full(wg.shape), full(wu.shape), full(wd.shape)]
    return pl.pallas_call(
        functools.partial(_outmlp_kernel, n_mix=len(mixes)),
        grid=(n // tm,),
        in_specs=in_specs,
        out_specs=pl.BlockSpec((tm, D_MODEL), lambda t: (t, 0)),
        out_shape=jax.ShapeDtypeStruct((n, D_MODEL), F32),
        compiler_params=pltpu.CompilerParams(dimension_semantics=("parallel",),
                                             vmem_limit_bytes=VMEM_LIMIT),
        name="outproj_swiglu",
    )(x2d, *mixes, wo, gain, wg, wu, wd)


def _block_diag_mean():
    i = np.arange(MXU_TILE)
    same = (i[:, None] // HEAD_DIM) == (i[None, :] // HEAD_DIM)
    return jnp.asarray(same / HEAD_DIM, BF16)


def kernel(x, rel_bias_table, ev_attn_norm, ev_w_in, ev_b_forget, ev_fox_q_norm, ev_fox_k_norm,
           ev_diff_q_norm, ev_diff_k_norm, ev_lambda_q1, ev_lambda_k1, ev_lambda_q2, ev_lambda_k2,
           ev_diff_subln, ev_w_out, od_attn_norm, od_w_qkv, od_q_norm, od_k_norm, od_sinks, od_w_out,
           ffn_norm, w_gate, w_up, w_down):
    batch, seq, d = x.shape
    n = batch * seq
    x2d = x.reshape(n, d)
    bd = _block_diag_mean()
    swbias, gbias = _bias_tiles(rel_bias_table.astype(F32))
    swbias = swbias.reshape(SWA_KV_HEADS, SWA_GROUP * WINDOW, 2 * WINDOW)

    lambda_init = 0.8 - 0.6 * math.exp(-0.3 * 0)
    w_in = ev_w_in[0]
    ff0 = 3 * FOX_W
    d0 = ff0 + FOX_HEADS
    w0 = jnp.concatenate(
        [w_in[:, 0:ff0], w_in[:, d0:d0 + 3 * DIFF_W], w_in[:, ff0:d0],
         jnp.zeros((d, LANES - FOX_HEADS), w_in.dtype)], axis=1).astype(BF16)
    tile8 = lambda g, scale=1.0: jnp.tile(g.astype(F32) * scale, FOX_W // HEAD_DIM)
    hg0 = jnp.stack([tile8(ev_fox_q_norm[0], QK_SCALE), tile8(ev_fox_k_norm[0]),
                     tile8(ev_diff_q_norm[0], QK_SCALE), tile8(ev_diff_k_norm[0])])
    bf = ev_b_forget[0].astype(F32).reshape(FOX_HEADS, 1)
    fq, fk, fv, dq, dk, dv, crow = _proj0(
        x2d, ev_attn_norm[0].astype(F32).reshape(1, d), w0, bd, hg0, bf, batch, seq)
    r3 = lambda a: a.reshape(batch, seq, a.shape[-1])
    fox = _fox_attention(r3(fq), r3(fk), r3(fv), crow, batch, seq)
    lam4 = jnp.stack([ev_lambda_q1[0], ev_lambda_k1[0], ev_lambda_q2[0], ev_lambda_k2[0]]).astype(F32)
    dout = _diff_attention(lam4, ev_diff_subln[0].astype(F32).reshape(1, 2 * HEAD_DIM),
                           r3(dq), r3(dk), r3(dv), gbias, batch, seq, lambda_init)
    x2d = _outmlp(x2d, [fox.reshape(n, FOX_W), dout.reshape(n, DIFF_W)], ev_w_out[0].astype(BF16),
                  ffn_norm[0].astype(F32).reshape(1, d),
                  w_gate[0].astype(BF16), w_up[0].astype(BF16), w_down[0].astype(BF16))

    tile4 = lambda g, scale=1.0: jnp.tile(g.astype(F32) * scale, MXU_TILE // HEAD_DIM)
    hg1 = jnp.stack([tile4(od_q_norm[0], QK_SCALE), tile4(od_k_norm[0])])
    q, k, v = _proj1(x2d, od_attn_norm[0].astype(F32).reshape(1, d), od_w_qkv[0].astype(BF16), bd, hg1)
    swa = _swa_attention(od_sinks[0].astype(F32), r3(q), r3(k), r3(v), swbias, batch, seq)
    x2d = _outmlp(x2d, [swa.reshape(n, SWA_HEADS * HEAD_DIM)], od_w_out[0].astype(BF16),
                  ffn_norm[1].astype(F32).reshape(1, d),
                  w_gate[1].astype(BF16), w_up[1].astype(BF16), w_down[1].astype(BF16))
    return x2d.reshape(batch, seq, d)
```

```python
import functools
import math

import numpy as np
import jax
import jax.numpy as jnp
from jax import lax
from jax.experimental import pallas as pl
from jax.experimental.pallas import tpu as pltpu

F32 = jnp.float32
BF16 = jnp.bfloat16

D_MODEL = 1024
HEAD_DIM = 64
FOX_HEADS = 8
DIFF_HEADS = 4
SWA_HEADS = 16
SWA_KV_HEADS = 4
SWA_GROUP = SWA_HEADS // SWA_KV_HEADS
WINDOW = 128
NUM_BUCKETS = 32
MAX_DISTANCE = 128
FFN_HIDDEN = 2816
RMS_EPS = 1e-6
NEG_INF = -1e30
FOX_W = FOX_HEADS * HEAD_DIM
DIFF_W = DIFF_HEADS * 2 * HEAD_DIM
QK_SCALE = HEAD_DIM ** -0.5
LOG2E = math.log2(math.e)
Q_GAIN_SCALE = QK_SCALE * LOG2E

LANES = 128
MXU_TILE = 256
ROW_TILE = 512
MLP_ROW_TILE = 512
ATT_Q_BLOCK = 256
SWA_LAG = 4
HIDDEN_CHUNK = 1536
VMEM_LIMIT = 56 * 1024 * 1024

_NT = (((1,), (1,)), ((), ()))


def _dot(a, b):
    return jnp.dot(a, b, preferred_element_type=F32)


def _dot_nt(a, b):
    return lax.dot_general(a, b, _NT, preferred_element_type=F32)


def _lagged(items, first, second, depth=1):
    pending = []
    for item in items:
        pending.append(first(*item))
        if len(pending) > depth:
            second(*pending.pop(0))
    for p in pending:
        second(*p)


def _rms(xf, gain):
    ms = jnp.mean(xf * xf, axis=-1, keepdims=True)
    return xf * lax.rsqrt(ms + RMS_EPS) * gain


def _head_norm(acc, bd, gain):
    msq = _dot((acc * acc).astype(BF16), bd)
    return acc * lax.rsqrt(msq + RMS_EPS) * gain


def _bucket_ids(delta):
    n = np.maximum(delta, 0)
    max_exact = NUM_BUCKETS // 2
    nf = np.maximum(n, 1).astype(np.float64)
    large = max_exact + (np.log(nf / max_exact) / math.log(MAX_DISTANCE / max_exact)
                         * (NUM_BUCKETS - max_exact)).astype(np.int32)
    large = np.minimum(large, NUM_BUCKETS - 1)
    return np.where(n < max_exact, n, large).astype(np.int32)


def _bias_kernel(table_ref, bsw_ref, bg_ref, sw_ref, g_ref):
    h = pl.program_id(0)
    bsw = bsw_ref[...]
    bg = bg_ref[...]
    far = table_ref[NUM_BUCKETS - 1, h]
    sw = jnp.zeros(bsw.shape, F32)
    g = jnp.zeros(bg.shape, F32)
    for b in range(NUM_BUCKETS):
        val = table_ref[b, h]
        sw = jnp.where(bsw == b, val * LOG2E, sw)
        g = jnp.where(bg == b, (val - far) * LOG2E, g)
    sw_ref[0] = jnp.where(bsw < 0, NEG_INF, sw)
    g_ref[0] = jnp.where(bg < 0, NEG_INF, g)


def _bias_tiles(table):
    a = np.arange(WINDOW)[:, None]
    b = np.arange(2 * WINDOW)[None, :]
    d_sw = WINDOW + a - b
    bsw = np.where((d_sw >= 0) & (d_sw < WINDOW), _bucket_ids(d_sw), -1).astype(np.int32)
    a = np.arange(ATT_Q_BLOCK)[:, None]
    b = np.arange(2 * ATT_Q_BLOCK)[None, :]
    d_g = ATT_Q_BLOCK + a - b
    bg = np.where(d_g >= 0, _bucket_ids(d_g), -1).astype(np.int32)
    nh = table.shape[1]
    return pl.pallas_call(
        _bias_kernel,
        grid=(nh,),
        in_specs=[
            pl.BlockSpec(memory_space=pltpu.SMEM),
            pl.BlockSpec(bsw.shape, lambda h: (0, 0)),
            pl.BlockSpec(bg.shape, lambda h: (0, 0)),
        ],
        out_specs=[
            pl.BlockSpec((1,) + bsw.shape, lambda h: (h, 0, 0)),
            pl.BlockSpec((1,) + bg.shape, lambda h: (h, 0, 0)),
        ],
        out_shape=[
            jax.ShapeDtypeStruct((nh,) + bsw.shape, F32),
            jax.ShapeDtypeStruct((nh,) + bg.shape, F32),
        ],
        name="rel_bias_tiles",
    )(table, jnp.asarray(bsw), jnp.asarray(bg))


def _proj0_kernel(x_ref, gain_ref, w_ref, wf_ref, bd_ref, hg_ref, bf_ref,
                  fq_ref, fk_ref, fv_ref, dq_ref, dk_ref, dv_ref, caug_ref,
                  carry_ref, *, tiles_per_seq):
    t = pl.program_id(0)

    @pl.when(t == 0)
    def _():
        carry_ref[...] = jnp.zeros_like(carry_ref)

    h = _rms(x_ref[...], gain_ref[...]).astype(BF16)
    bd = bd_ref[...]
    rows = h.shape[0]

    z = _dot(h, wf_ref[...])
    proj = _dot(h, w_ref[...])

    z8 = z.T[0:FOX_HEADS, :] + bf_ref[...]
    y = -(jnp.maximum(-z8, 0.0) + jnp.log1p(jnp.exp(-jnp.abs(z8))))
    lane = lax.broadcasted_iota(jnp.int32, y.shape, 1)
    shift = 1
    while shift < rows:
        y = y + jnp.where(lane >= shift, pltpu.roll(y, shift, axis=1), 0.0)
        shift *= 2

    carry = jnp.where(t % tiles_per_seq == 0, 0.0, carry_ref[:, 0:1])
    c = y + carry
    carry_ref[...] = jnp.broadcast_to(c[:, rows - 1:rows], carry_ref.shape)
    cn = c * (-LOG2E)
    hi = cn.astype(BF16).astype(F32)
    r1 = cn - hi
    mid = r1.astype(BF16).astype(F32)
    lo = (r1 - mid).astype(BF16).astype(F32)
    half = jnp.concatenate([hi, mid, lo, jnp.zeros((HEAD_DIM - 3 * FOX_HEADS, rows), F32)], axis=0)
    caug_ref[...] = jnp.concatenate([half, half], axis=0).T.astype(BF16)

    def normed(col0, gidx, out_ref):
        for i in range(FOX_W // MXU_TILE):
            c0 = col0 + i * MXU_TILE
            g = hg_ref[gidx:gidx + 1, i * MXU_TILE:(i + 1) * MXU_TILE]
            out_ref[:, i * MXU_TILE:(i + 1) * MXU_TILE] = _head_norm(
                proj[:, c0:c0 + MXU_TILE], bd, g).astype(BF16)

    normed(0 * FOX_W, 0, fq_ref)
    normed(1 * FOX_W, 1, fk_ref)
    fv_ref[...] = proj[:, 2 * FOX_W:3 * FOX_W].astype(BF16)
    normed(3 * FOX_W, 2, dq_ref)
    normed(4 * FOX_W, 3, dk_ref)
    dv_ref[...] = proj[:, 5 * FOX_W:6 * FOX_W].astype(BF16)


def _proj0(x2d, gain, w, wf, bd, hg, bf, seq):
    n = x2d.shape[0]
    tm = ROW_TILE
    tps = seq // tm
    row_spec = pl.BlockSpec((tm, FOX_W), lambda t: (t, 0))
    full = lambda shape: pl.BlockSpec(shape, lambda t: (0,) * len(shape))
    outs = pl.pallas_call(
        functools.partial(_proj0_kernel, tiles_per_seq=tps),
        grid=(n // tm,),
        in_specs=[
            pl.BlockSpec((tm, D_MODEL), lambda t: (t, 0)),
            full((1, D_MODEL)),
            full(w.shape),
            full(wf.shape),
            full((MXU_TILE, MXU_TILE)),
            full(hg.shape),
            full(bf.shape),
        ],
        out_specs=[row_spec] * 6 + [pl.BlockSpec((tm, LANES), lambda t: (t, 0))],
        out_shape=[jax.ShapeDtypeStruct((n, FOX_W), BF16)] * 6
        + [jax.ShapeDtypeStruct((n, LANES), BF16)],
        scratch_shapes=[pltpu.VMEM((FOX_HEADS, LANES), F32)],
        compiler_params=pltpu.CompilerParams(dimension_semantics=("arbitrary",),
                                             vmem_limit_bytes=VMEM_LIMIT),
        name="proj_even",
    )(x2d, gain, w, wf, bd, hg, bf)
    return outs


def _fox_kernel(q_ref, k_ref, v_ref, caug_ref, o_ref, ka_ref, va_ref, *, seq):
    j = pl.program_id(1)
    qb = ATT_Q_BLOCK
    lane_s = lax.broadcasted_iota(jnp.int32, (seq, LANES), 1)
    low_s = lane_s < HEAD_DIM
    vp = v_ref[0]
    kp = k_ref[0]
    ca = caug_ref[0]
    one = jnp.ones((), BF16)
    va_ref[0] = jnp.where(low_s, vp, one)
    va_ref[1] = jnp.where(low_s, one, vp)
    ka_ref[0] = jnp.where(low_s, kp, ca)
    ka_ref[1] = jnp.where(low_s, ca, kp)
    lane_q = lax.broadcasted_iota(jnp.int32, (qb, LANES), 1)
    row = lax.broadcasted_iota(jnp.int32, (qb, qb), 0)
    col = lax.broadcasted_iota(jnp.int32, (qb, qb), 1)
    causal = col <= row
    owns = (lane_q < HEAD_DIM, lane_q >= HEAD_DIM)

    def scores(qi, hh):
        r0 = qi * qb
        end = r0 + qb
        rel = lane_q - (HEAD_DIM if hh == 0 else 0) - (2 * j + hh)
        pick = (rel == 0) | (rel == FOX_HEADS) | (rel == 2 * FOX_HEADS)
        qz = jnp.where(owns[hh], q_ref[0, r0:end, :], jnp.where(pick, 1.0, 0.0).astype(BF16))
        s_d = jnp.where(causal, _dot_nt(qz, ka_ref[hh, r0:end, :]), NEG_INF)
        s_o = _dot_nt(qz, ka_ref[hh, 0:r0, :]) if qi > 0 else None
        return qi, hh, s_d, s_o

    halves = []

    def finish(qi, hh, s_d, s_o):
        r0 = qi * qb
        end = r0 + qb
        m = jnp.max(s_d, axis=-1, keepdims=True)
        if s_o is not None:
            m = jnp.maximum(m, jnp.max(s_o, axis=-1, keepdims=True))
        acc = _dot(jnp.exp2(s_d - m).astype(BF16), va_ref[hh, r0:end, :])
        if s_o is not None:
            acc = acc + _dot(jnp.exp2(s_o - m).astype(BF16), va_ref[hh, 0:r0, :])
        denom = pltpu.roll(jnp.where(owns[hh], 1.0, acc), HEAD_DIM, axis=1)
        halves.append(acc / denom)
        if hh == 1:
            o_ref[0, r0:end, :] = jnp.where(owns[0], halves[-2], halves[-1]).astype(BF16)

    _lagged([(qi, hh) for qi in range(seq // qb) for hh in range(2)], scores, finish)


def _fox_attention(fq, fk, fv, caug, batch, seq):
    blk = pl.BlockSpec((1, seq, LANES), lambda b, j: (b, 0, j))
    return pl.pallas_call(
        functools.partial(_fox_kernel, seq=seq),
        grid=(batch, FOX_W // LANES),
        in_specs=[blk, blk, blk, pl.BlockSpec((1, seq, LANES), lambda b, j: (b, 0, 0))],
        out_specs=blk,
        out_shape=jax.ShapeDtypeStruct((batch, seq, FOX_W), BF16),
        scratch_shapes=[pltpu.VMEM((2, seq, LANES), BF16), pltpu.VMEM((2, seq, LANES), BF16)],
        compiler_params=pltpu.CompilerParams(dimension_semantics=("parallel", "parallel"),
                                             vmem_limit_bytes=VMEM_LIMIT),
        name="fox_attention",
    )(fq, fk, fv, caug)


def _diff_kernel(lam_ref, g_ref, q_ref, k_ref, v_ref, gb_ref, o_ref, va_ref, *, seq, lambda_init):
    qb = ATT_Q_BLOCK
    lv = lam_ref[...]
    lam = (jnp.exp(jnp.sum(lv[0:1] * lv[1:2], axis=-1, keepdims=True))
           - jnp.exp(jnp.sum(lv[2:3] * lv[3:4], axis=-1, keepdims=True)) + lambda_init)
    va_ref[:, 0:LANES] = v_ref[0]
    va_ref[:, LANES:2 * LANES] = jnp.ones((seq, LANES), BF16)
    lane_q = lax.broadcasted_iota(jnp.int32, (qb, LANES), 1)
    bias_prev = gb_ref[0, :, 0:qb]
    bias_diag = gb_ref[0, :, qb:2 * qb]
    gain = g_ref[...] * (1.0 - lambda_init)
    owns = (lane_q < HEAD_DIM, lane_q >= HEAD_DIM)

    def scores(qi, mm):
        r0 = qi * qb
        end = r0 + qb
        qz = jnp.where(owns[mm], q_ref[0, r0:end, :], jnp.zeros((), BF16))
        s_d = _dot_nt(qz, k_ref[0, r0:end, :]) + bias_diag
        s_p = _dot_nt(qz, k_ref[0, r0 - qb:r0, :]) + bias_prev if qi >= 1 else None
        s_f = _dot_nt(qz, k_ref[0, 0:r0 - qb, :]) if qi >= 2 else None
        return qi, mm, s_d, s_p, s_f

    outs = []

    def finish(qi, mm, s_d, s_p, s_f):
        r0 = qi * qb
        end = r0 + qb
        m = jnp.max(s_d, axis=-1, keepdims=True)
        if s_p is not None:
            m = jnp.maximum(m, jnp.max(s_p, axis=-1, keepdims=True))
        if s_f is not None:
            m = jnp.maximum(m, jnp.max(s_f, axis=-1, keepdims=True))
        acc = _dot(jnp.exp2(s_d - m).astype(BF16), va_ref[r0:end, :])
        if s_p is not None:
            acc = acc + _dot(jnp.exp2(s_p - m).astype(BF16), va_ref[r0 - qb:r0, :])
        if s_f is not None:
            acc = acc + _dot(jnp.exp2(s_f - m).astype(BF16), va_ref[0:r0 - qb, :])
        outs.append(acc[:, 0:LANES] / acc[:, LANES:2 * LANES])
        if mm == 1:
            o = outs[-2] - lam * outs[-1]
            o_ref[0, r0:end, :] = _rms(o, gain).astype(BF16)

    _lagged([(qi, mm) for qi in range(seq // qb) for mm in range(2)], scores, finish)


def _diff_attention(lam4, subln, dq, dk, dv, gbias, batch, seq, lambda_init):
    blk = pl.BlockSpec((1, seq, LANES), lambda b, h: (b, 0, h))
    return pl.pallas_call(
        functools.partial(_diff_kernel, seq=seq, lambda_init=lambda_init),
        grid=(batch, DIFF_HEADS),
        in_specs=[
            pl.BlockSpec(lam4.shape, lambda b, h: (0, 0)),
            pl.BlockSpec(subln.shape, lambda b, h: (0, 0)),
            blk, blk, blk,
            pl.BlockSpec((1, ATT_Q_BLOCK, 2 * ATT_Q_BLOCK), lambda b, h: (h, 0, 0)),
        ],
        out_specs=blk,
        out_shape=jax.ShapeDtypeStruct((batch, seq, DIFF_W), BF16),
        scratch_shapes=[pltpu.VMEM((seq, 2 * LANES), BF16)],
        compiler_params=pltpu.CompilerParams(dimension_semantics=("parallel", "parallel"),
                                             vmem_limit_bytes=VMEM_LIMIT),
        name="diff_attention",
    )(lam4, subln, dq, dk, dv, gbias)


def _proj1_kernel(x_ref, gain_ref, w_ref, bd_ref, hg_ref, q_ref, k_ref, v_ref):
    h = _rms(x_ref[...], gain_ref[...]).astype(BF16)
    bd = bd_ref[...]
    qw = SWA_HEADS * HEAD_DIM
    kw = SWA_KV_HEADS * HEAD_DIM
    proj = _dot(h, w_ref[...])
    for i in range(qw // MXU_TILE):
        q_ref[:, i * MXU_TILE:(i + 1) * MXU_TILE] = _head_norm(
            proj[:, i * MXU_TILE:(i + 1) * MXU_TILE], bd, hg_ref[0:1, :]).astype(BF16)
    k_ref[...] = _head_norm(proj[:, qw:qw + kw], bd, hg_ref[1:2, :]).astype(BF16)
    v_ref[...] = proj[:, qw + kw:qw + 2 * kw].astype(BF16)


def _proj1(x2d, gain, w, bd, hg):
    n = x2d.shape[0]
    tm = ROW_TILE
    qw = SWA_HEADS * HEAD_DIM
    kw = SWA_KV_HEADS * HEAD_DIM
    full = lambda shape: pl.BlockSpec(shape, lambda t: (0,) * len(shape))
    return pl.pallas_call(
        _proj1_kernel,
        grid=(n // tm,),
        in_specs=[
            pl.BlockSpec((tm, D_MODEL), lambda t: (t, 0)),
            full((1, D_MODEL)),
            full(w.shape),
            full((MXU_TILE, MXU_TILE)),
            full(hg.shape),
        ],
        out_specs=[pl.BlockSpec((tm, qw), lambda t: (t, 0)),
                   pl.BlockSpec((tm, kw), lambda t: (t, 0)),
                   pl.BlockSpec((tm, kw), lambda t: (t, 0))],
        out_shape=[jax.ShapeDtypeStruct((n, qw), BF16),
                   jax.ShapeDtypeStruct((n, kw), BF16),
                   jax.ShapeDtypeStruct((n, kw), BF16)],
        compiler_params=pltpu.CompilerParams(dimension_semantics=("parallel",),
                                             vmem_limit_bytes=VMEM_LIMIT),
        name="proj_odd",
    )(x2d, gain, w, bd, hg)


def _swa_kernel(sink_ref, q_ref, k_ref, v_ref, sb_ref, o_ref, kk_ref, vv_ref, *, seq):
    kvh = pl.program_id(1)
    w = WINDOW
    lane_s = lax.broadcasted_iota(jnp.int32, (seq, LANES), 1)
    pick = jnp.where(lane_s >= HEAD_DIM, 1, 0) == (kvh % 2)
    kf = k_ref[0].astype(F32)
    vf = v_ref[0].astype(F32)
    kk_ref[...] = jnp.where(pick, kf, pltpu.roll(kf, HEAD_DIM, axis=1)).astype(BF16)
    vv_ref[:, 0:LANES] = jnp.where(pick, vf, pltpu.roll(vf, HEAD_DIM, axis=1)).astype(BF16)
    vv_ref[:, LANES:2 * LANES] = jnp.ones((seq, LANES), BF16)
    sink = jnp.concatenate(
        [jnp.full((w, 1), sink_ref[kvh * SWA_GROUP + g] * LOG2E, F32) for g in range(SWA_GROUP)],
        axis=0)
    lane_q = lax.broadcasted_iota(jnp.int32, (w, LANES), 1)
    low = lane_q < HEAD_DIM
    zero = jnp.zeros((), BF16)

    def scores(n):
        r0 = n * w
        out = [n]
        for half in range(2):
            qv = q_ref[0, r0:r0 + w, half * LANES:(half + 1) * LANES]
            q2 = jnp.concatenate([jnp.where(low, qv, zero), jnp.where(low, zero, qv)], axis=0)
            rows = slice(half * 2 * w, (half + 1) * 2 * w)
            if n == 0:
                out.append(_dot_nt(q2, kk_ref[0:w, :]) + sb_ref[0, rows, w:2 * w])
            else:
                out.append(_dot_nt(q2, kk_ref[r0 - w:r0 + w, :]) + sb_ref[0, rows, :])
        return out

    def finish(n, *s_halves):
        r0 = n * w
        vals = vv_ref[0:w, :] if n == 0 else vv_ref[r0 - w:r0 + w, :]
        for half, s in enumerate(s_halves):
            snk = sink[half * 2 * w:(half + 1) * 2 * w]
            m = jnp.maximum(jnp.max(s, axis=-1, keepdims=True), snk)
            acc = _dot(jnp.exp2(s - m).astype(BF16), vals)
            o2 = acc[:, 0:LANES] / (acc[:, LANES:2 * LANES] + jnp.exp2(snk - m))
            o_ref[0, r0:r0 + w, half * LANES:(half + 1) * LANES] = jnp.where(
                low, o2[0:w], o2[w:2 * w]).astype(BF16)

    _lagged([(n,) for n in range(seq // w)], scores, finish, depth=SWA_LAG)


def _swa_attention(sinks, q, k, v, swbias, batch, seq):
    gw = SWA_GROUP * HEAD_DIM
    return pl.pallas_call(
        functools.partial(_swa_kernel, seq=seq),
        grid=(batch, SWA_KV_HEADS),
        in_specs=[
            pl.BlockSpec(memory_space=pltpu.SMEM),
            pl.BlockSpec((1, seq, gw), lambda b, h: (b, 0, h)),
            pl.BlockSpec((1, seq, LANES), lambda b, h: (b, 0, h // 2)),
            pl.BlockSpec((1, seq, LANES), lambda b, h: (b, 0, h // 2)),
            pl.BlockSpec((1, SWA_GROUP * WINDOW, 2 * WINDOW), lambda b, h: (h, 0, 0)),
        ],
        out_specs=pl.BlockSpec((1, seq, gw), lambda b, h: (b, 0, h)),
        out_shape=jax.ShapeDtypeStruct((batch, seq, SWA_HEADS * HEAD_DIM), BF16),
        scratch_shapes=[pltpu.VMEM((seq, LANES), BF16), pltpu.VMEM((seq, 2 * LANES), BF16)],
        compiler_params=pltpu.CompilerParams(dimension_semantics=("parallel", "parallel"),
                                             vmem_limit_bytes=VMEM_LIMIT),
        name="swa_attention",
    )(sinks, q, k, v, swbias)


def _outmlp_kernel(*refs, n_mix):
    x_ref = refs[0]
    a_refs = refs[1:1 + n_mix]
    wo_ref, gain_ref, wg_ref, wu_ref, wd_ref, o_ref = refs[1 + n_mix:]
    mix = jnp.concatenate([a_ref[...] for a_ref in a_refs], axis=1)
    x1 = x_ref[...] + _dot(mix, wo_ref[...])
    h = _rms(x1, gain_ref[...]).astype(BF16)
    out = x1
    for c0 in range(0, FFN_HIDDEN, HIDDEN_CHUNK):
        c1 = min(c0 + HIDDEN_CHUNK, FFN_HIDDEN)
        g = _dot(h, wg_ref[:, c0:c1])
        u = _dot(h, wu_ref[:, c0:c1])
        y = (g * (1.0 / (1.0 + jnp.exp(-g))) * u).astype(BF16)
        out = out + _dot(y, wd_ref[c0:c1, :])
    o_ref[...] = out


def _outmlp(x2d, mixes, wo, gain, wg, wu, wd):
    n = x2d.shape[0]
    tm = MLP_ROW_TILE
    full = lambda shape: pl.BlockSpec(shape, lambda t: (0,) * len(shape),
                                      pipeline_mode=pl.Buffered(1))
    in_specs = [pl.BlockSpec((tm, D_MODEL), lambda t: (t, 0))]
    in_specs += [pl.BlockSpec((tm, a.shape[1]), lambda t: (t, 0)) for a in mixes]
    in_specs += [full(wo.shape), full(gain.shape), full(wg.shape), full(wu.shape), full(wd.shape)]
    return pl.pallas_call(
        functools.partial(_outmlp_kernel, n_mix=len(mixes)),
        grid=(n // tm,),
        in_specs=in_specs,
        out_specs=pl.BlockSpec((tm, D_MODEL), lambda t: (t, 0)),
        out_shape=jax.ShapeDtypeStruct((n, D_MODEL), F32),
        compiler_params=pltpu.CompilerParams(dimension_semantics=("parallel",),
                                             vmem_limit_bytes=VMEM_LIMIT),
        name="outproj_swiglu",
    )(x2d, *mixes, wo, gain, wg, wu, wd)


def _block_diag_mean():
    i = np.arange(MXU_TILE)
    same = (i[:, None] // HEAD_DIM) == (i[None, :] // HEAD_DIM)
    return jnp.asarray(same / HEAD_DIM, BF16)


def kernel(x, rel_bias_table, ev_attn_norm, ev_w_in, ev_b_forget, ev_fox_q_norm, ev_fox_k_norm,
           ev_diff_q_norm, ev_diff_k_norm, ev_lambda_q1, ev_lambda_k1, ev_lambda_q2, ev_lambda_k2,
           ev_diff_subln, ev_w_out, od_attn_norm, od_w_qkv, od_q_norm, od_k_norm, od_sinks, od_w_out,
           ffn_norm, w_gate, w_up, w_down):
    batch, seq, d = x.shape
    n = batch * seq
    x2d = x.reshape(n, d)
    bd = _block_diag_mean()
    swbias, gbias = _bias_tiles(rel_bias_table.astype(F32))
    swbias = swbias.reshape(SWA_KV_HEADS, SWA_GROUP * WINDOW, 2 * WINDOW)

    lambda_init = 0.8 - 0.6 * math.exp(-0.3 * 0)
    w_in = ev_w_in[0]
    ff0 = 3 * FOX_W
    d0 = ff0 + FOX_HEADS
    w0 = jnp.concatenate([w_in[:, 0:ff0], w_in[:, d0:d0 + 3 * DIFF_W]], axis=1).astype(BF16)
    wf = jnp.concatenate(
        [w_in[:, ff0:d0], jnp.zeros((d, LANES - FOX_HEADS), w_in.dtype)], axis=1).astype(BF16)
    tile8 = lambda g, scale=1.0: jnp.tile(g.astype(F32) * scale, FOX_W // HEAD_DIM)
    hg0 = jnp.stack([tile8(ev_fox_q_norm[0], Q_GAIN_SCALE), tile8(ev_fox_k_norm[0]),
                     tile8(ev_diff_q_norm[0], Q_GAIN_SCALE), tile8(ev_diff_k_norm[0])])
    bf = ev_b_forget[0].astype(F32).reshape(FOX_HEADS, 1)
    fq, fk, fv, dq, dk, dv, caug = _proj0(
        x2d, ev_attn_norm[0].astype(F32).reshape(1, d), w0, wf, bd, hg0, bf, seq)
    r3 = lambda a: a.reshape(batch, seq, a.shape[-1])
    fox = _fox_attention(r3(fq), r3(fk), r3(fv), r3(caug), batch, seq)
    lam4 = jnp.stack([ev_lambda_q1[0], ev_lambda_k1[0], ev_lambda_q2[0], ev_lambda_k2[0]]).astype(F32)
    dout = _diff_attention(lam4, ev_diff_subln[0].astype(F32).reshape(1, 2 * HEAD_DIM),
                           r3(dq), r3(dk), r3(dv), gbias, batch, seq, lambda_init)
    x2d = _outmlp(x2d, [fox.reshape(n, FOX_W), dout.reshape(n, DIFF_W)], ev_w_out[0].astype(BF16),
                  ffn_norm[0].astype(F32).reshape(1, d),
                  w_gate[0].astype(BF16), w_up[0].astype(BF16), w_down[0].astype(BF16))

    tile4 = lambda g, scale=1.0: jnp.tile(g.astype(F32) * scale, MXU_TILE // HEAD_DIM)
    hg1 = jnp.stack([tile4(od_q_norm[0], Q_GAIN_SCALE), tile4(od_k_norm[0])])
    q, k, v = _proj1(x2d, od_attn_norm[0].astype(F32).reshape(1, d), od_w_qkv[0].astype(BF16), bd, hg1)
    swa = _swa_attention(od_sinks[0].astype(F32), r3(q), r3(k), r3(v), swbias, batch, seq)
    x2d = _outmlp(x2d, [swa.reshape(n, SWA_HEADS * HEAD_DIM)], od_w_out[0].astype(BF16),
                  ffn_norm[1].astype(F32).reshape(1, d),
                  w_gate[1].astype(BF16), w_up[1].astype(BF16), w_down[1].astype(BF16))
    return x2d.reshape(batch, seq, d)
```

```python
import functools
import math

import numpy as np
import jax
import jax.numpy as jnp
from jax import lax
from jax.experimental import pallas as pl
from jax.experimental.pallas import tpu as pltpu

F32 = jnp.float32
BF16 = jnp.bfloat16

D_MODEL = 1024
HEAD_DIM = 64
FOX_HEADS = 8
DIFF_HEADS = 4
SWA_HEADS = 16
SWA_KV_HEADS = 4
SWA_GROUP = SWA_HEADS // SWA_KV_HEADS
WINDOW = 128
NUM_BUCKETS = 32
MAX_DISTANCE = 128
FFN_HIDDEN = 2816
RMS_EPS = 1e-6
NEG_INF = -1e30
FOX_W = FOX_HEADS * HEAD_DIM
DIFF_W = DIFF_HEADS * 2 * HEAD_DIM
QK_SCALE = HEAD_DIM ** -0.5
LOG2E = math.log2(math.e)
Q_GAIN_SCALE = QK_SCALE * LOG2E

LANES = 128
MXU_TILE = 256
ROW_TILE = 512
MLP_ROW_TILE = 512
ATT_Q_BLOCK = 256
SWA_LAG = 4
HIDDEN_CHUNK = 1536
VMEM_LIMIT = 56 * 1024 * 1024

_NT = (((1,), (1,)), ((), ()))


def _dot(a, b):
    return jnp.dot(a, b, preferred_element_type=F32)


def _dot_nt(a, b):
    return lax.dot_general(a, b, _NT, preferred_element_type=F32)


def _lagged(items, first, second, depth=1):
    pending = []
    for item in items:
        pending.append(first(*item))
        if len(pending) > depth:
            second(*pending.pop(0))
    for p in pending:
        second(*p)


def _rms(xf, gain):
    ms = jnp.mean(xf * xf, axis=-1, keepdims=True)
    return xf * lax.rsqrt(ms + RMS_EPS) * gain


def _head_norm(acc, bd, gain):
    msq = _dot((acc * acc).astype(BF16), bd)
    return acc * lax.rsqrt(msq + RMS_EPS) * gain


def _bucket_ids(delta):
    n = np.maximum(delta, 0)
    max_exact = NUM_BUCKETS // 2
    nf = np.maximum(n, 1).astype(np.float64)
    large = max_exact + (np.log(nf / max_exact) / math.log(MAX_DISTANCE / max_exact)
                         * (NUM_BUCKETS - max_exact)).astype(np.int32)
    large = np.minimum(large, NUM_BUCKETS - 1)
    return np.where(n < max_exact, n, large).astype(np.int32)


def _bias_kernel(table_ref, bsw_ref, bg_ref, sw_ref, g_ref):
    h = pl.program_id(0)
    bsw = bsw_ref[...]
    bg = bg_ref[...]
    far = table_ref[NUM_BUCKETS - 1, h]
    sw = jnp.zeros(bsw.shape, F32)
    g = jnp.zeros(bg.shape, F32)
    for b in range(NUM_BUCKETS):
        val = table_ref[b, h]
        sw = jnp.where(bsw == b, val * LOG2E, sw)
        g = jnp.where(bg == b, (val - far) * LOG2E, g)
    sw_ref[0] = jnp.where(bsw < 0, NEG_INF, sw)
    g_ref[0] = jnp.where(bg < 0, NEG_INF, g)


def _bias_tiles(table):
    a = np.arange(WINDOW)[None, :]
    b = np.arange(2 * WINDOW)[:, None]
    d_sw = WINDOW + a - b
    bsw = np.where((d_sw >= 0) & (d_sw < WINDOW), _bucket_ids(d_sw), -1).astype(np.int32)
    a = np.arange(ATT_Q_BLOCK)[:, None]
    b = np.arange(2 * ATT_Q_BLOCK)[None, :]
    d_g = ATT_Q_BLOCK + a - b
    bg = np.where(d_g >= 0, _bucket_ids(d_g), -1).astype(np.int32)
    nh = table.shape[1]
    return pl.pallas_call(
        _bias_kernel,
        grid=(nh,),
        in_specs=[
            pl.BlockSpec(memory_space=pltpu.SMEM),
            pl.BlockSpec(bsw.shape, lambda h: (0, 0)),
            pl.BlockSpec(bg.shape, lambda h: (0, 0)),
        ],
        out_specs=[
            pl.BlockSpec((1,) + bsw.shape, lambda h: (h, 0, 0)),
            pl.BlockSpec((1,) + bg.shape, lambda h: (h, 0, 0)),
        ],
        out_shape=[
            jax.ShapeDtypeStruct((nh,) + bsw.shape, F32),
            jax.ShapeDtypeStruct((nh,) + bg.shape, F32),
        ],
        name="rel_bias_tiles",
    )(table, jnp.asarray(bsw), jnp.asarray(bg))


def _proj0_kernel(x_ref, gain_ref, w_ref, wf_ref, bd_ref, hg_ref, bf_ref,
                  fq_ref, fk_ref, fv_ref, dq_ref, dk_ref, dv_ref, caug_ref,
                  carry_ref, *, tiles_per_seq):
    t = pl.program_id(0)

    @pl.when(t == 0)
    def _():
        carry_ref[...] = jnp.zeros_like(carry_ref)

    h = _rms(x_ref[...], gain_ref[...]).astype(BF16)
    bd = bd_ref[...]
    rows = h.shape[0]

    z = _dot(h, wf_ref[...])
    proj = _dot(h, w_ref[...])

    z8 = z.T[0:FOX_HEADS, :] + bf_ref[...]
    y = -(jnp.maximum(-z8, 0.0) + jnp.log1p(jnp.exp(-jnp.abs(z8))))
    lane = lax.broadcasted_iota(jnp.int32, y.shape, 1)
    shift = 1
    while shift < rows:
        y = y + jnp.where(lane >= shift, pltpu.roll(y, shift, axis=1), 0.0)
        shift *= 2

    carry = jnp.where(t % tiles_per_seq == 0, 0.0, carry_ref[:, 0:1])
    c = y + carry
    carry_ref[...] = jnp.broadcast_to(c[:, rows - 1:rows], carry_ref.shape)
    cn = c * (-LOG2E)
    hi = cn.astype(BF16).astype(F32)
    r1 = cn - hi
    mid = r1.astype(BF16).astype(F32)
    lo = (r1 - mid).astype(BF16).astype(F32)
    half = jnp.concatenate([hi, mid, lo, jnp.zeros((HEAD_DIM - 3 * FOX_HEADS, rows), F32)], axis=0)
    caug_ref[...] = jnp.concatenate([half, half], axis=0).T.astype(BF16)

    def normed(col0, gidx, out_ref):
        for i in range(FOX_W // MXU_TILE):
            c0 = col0 + i * MXU_TILE
            g = hg_ref[gidx:gidx + 1, i * MXU_TILE:(i + 1) * MXU_TILE]
            out_ref[:, i * MXU_TILE:(i + 1) * MXU_TILE] = _head_norm(
                proj[:, c0:c0 + MXU_TILE], bd, g).astype(BF16)

    normed(0 * FOX_W, 0, fq_ref)
    normed(1 * FOX_W, 1, fk_ref)
    fv_ref[...] = proj[:, 2 * FOX_W:3 * FOX_W].astype(BF16)
    normed(3 * FOX_W, 2, dq_ref)
    normed(4 * FOX_W, 3, dk_ref)
    dv_ref[...] = proj[:, 5 * FOX_W:6 * FOX_W].astype(BF16)


def _proj0(x2d, gain, w, wf, bd, hg, bf, seq):
    n = x2d.shape[0]
    tm = ROW_TILE
    tps = seq // tm
    row_spec = pl.BlockSpec((tm, FOX_W), lambda t: (t, 0))
    full = lambda shape: pl.BlockSpec(shape, lambda t: (0,) * len(shape))
    outs = pl.pallas_call(
        functools.partial(_proj0_kernel, tiles_per_seq=tps),
        grid=(n // tm,),
        in_specs=[
            pl.BlockSpec((tm, D_MODEL), lambda t: (t, 0)),
            full((1, D_MODEL)),
            full(w.shape),
            full(wf.shape),
            full((MXU_TILE, MXU_TILE)),
            full(hg.shape),
            full(bf.shape),
        ],
        out_specs=[row_spec] * 6 + [pl.BlockSpec((tm, LANES), lambda t: (t, 0))],
        out_shape=[jax.ShapeDtypeStruct((n, FOX_W), BF16)] * 6
        + [jax.ShapeDtypeStruct((n, LANES), BF16)],
        scratch_shapes=[pltpu.VMEM((FOX_HEADS, LANES), F32)],
        compiler_params=pltpu.CompilerParams(dimension_semantics=("arbitrary",),
                                             vmem_limit_bytes=VMEM_LIMIT),
        name="proj_even",
    )(x2d, gain, w, wf, bd, hg, bf)
    return outs


def _fox_kernel(q_ref, k_ref, v_ref, caug_ref, o_ref, ka_ref, va_ref, *, seq):
    j = pl.program_id(1)
    qb = ATT_Q_BLOCK
    lane_s = lax.broadcasted_iota(jnp.int32, (seq, LANES), 1)
    low_s = lane_s < HEAD_DIM
    vp = v_ref[0]
    kp = k_ref[0]
    ca = caug_ref[0]
    one = jnp.ones((), BF16)
    va_ref[0] = jnp.where(low_s, vp, one)
    va_ref[1] = jnp.where(low_s, one, vp)
    ka_ref[0] = jnp.where(low_s, kp, ca)
    ka_ref[1] = jnp.where(low_s, ca, kp)
    lane_q = lax.broadcasted_iota(jnp.int32, (qb, LANES), 1)
    row = lax.broadcasted_iota(jnp.int32, (qb, qb), 0)
    col = lax.broadcasted_iota(jnp.int32, (qb, qb), 1)
    causal = col <= row
    owns = (lane_q < HEAD_DIM, lane_q >= HEAD_DIM)

    def scores(qi, hh):
        r0 = qi * qb
        end = r0 + qb
        rel = lane_q - (HEAD_DIM if hh == 0 else 0) - (2 * j + hh)
        pick = (rel == 0) | (rel == FOX_HEADS) | (rel == 2 * FOX_HEADS)
        qz = jnp.where(owns[hh], q_ref[0, r0:end, :], jnp.where(pick, 1.0, 0.0).astype(BF16))
        s_d = jnp.where(causal, _dot_nt(qz, ka_ref[hh, r0:end, :]), NEG_INF)
        s_o = _dot_nt(qz, ka_ref[hh, 0:r0, :]) if qi > 0 else None
        return qi, hh, s_d, s_o

    halves = []

    def finish(qi, hh, s_d, s_o):
        r0 = qi * qb
        end = r0 + qb
        m = jnp.max(s_d, axis=-1, keepdims=True)
        if s_o is not None:
            m = jnp.maximum(m, jnp.max(s_o, axis=-1, keepdims=True))
        acc = _dot(jnp.exp2(s_d - m).astype(BF16), va_ref[hh, r0:end, :])
        if s_o is not None:
            acc = acc + _dot(jnp.exp2(s_o - m).astype(BF16), va_ref[hh, 0:r0, :])
        denom = pltpu.roll(jnp.where(owns[hh], 1.0, acc), HEAD_DIM, axis=1)
        halves.append(acc / denom)
        if hh == 1:
            o_ref[0, r0:end, :] = jnp.where(owns[0], halves[-2], halves[-1]).astype(BF16)

    _lagged([(qi, hh) for qi in range(seq // qb) for hh in range(2)], scores, finish)


def _fox_attention(fq, fk, fv, caug, batch, seq):
    blk = pl.BlockSpec((1, seq, LANES), lambda b, j: (b, 0, j))
    return pl.pallas_call(
        functools.partial(_fox_kernel, seq=seq),
        grid=(batch, FOX_W // LANES),
        in_specs=[blk, blk, blk, pl.BlockSpec((1, seq, LANES), lambda b, j: (b, 0, 0))],
        out_specs=blk,
        out_shape=jax.ShapeDtypeStruct((batch, seq, FOX_W), BF16),
        scratch_shapes=[pltpu.VMEM((2, seq, LANES), BF16), pltpu.VMEM((2, seq, LANES), BF16)],
        compiler_params=pltpu.CompilerParams(dimension_semantics=("parallel", "parallel"),
                                             vmem_limit_bytes=VMEM_LIMIT),
        name="fox_attention",
    )(fq, fk, fv, caug)


def _diff_kernel(lam_ref, g_ref, q_ref, k_ref, v_ref, gb_ref, o_ref, va_ref, *, seq, lambda_init):
    qb = ATT_Q_BLOCK
    lv = lam_ref[...]
    lam = (jnp.exp(jnp.sum(lv[0:1] * lv[1:2], axis=-1, keepdims=True))
           - jnp.exp(jnp.sum(lv[2:3] * lv[3:4], axis=-1, keepdims=True)) + lambda_init)
    va_ref[:, 0:LANES] = v_ref[0]
    va_ref[:, LANES:2 * LANES] = jnp.ones((seq, LANES), BF16)
    lane_q = lax.broadcasted_iota(jnp.int32, (qb, LANES), 1)
    bias_prev = jnp.concatenate([gb_ref[0, :, 0:qb]] * 2, axis=0)
    bias_diag = jnp.concatenate([gb_ref[0, :, qb:2 * qb]] * 2, axis=0)
    gain = g_ref[...] * (1.0 - lambda_init)
    zero = jnp.zeros((), BF16)

    def scores(qi):
        r0 = qi * qb
        end = r0 + qb
        qp = q_ref[0, r0:end, :]
        qz = jnp.concatenate([jnp.where(lane_q < HEAD_DIM, qp, zero),
                              jnp.where(lane_q < HEAD_DIM, zero, qp)], axis=0)
        s_d = _dot_nt(qz, k_ref[0, r0:end, :]) + bias_diag
        s_p = _dot_nt(qz, k_ref[0, r0 - qb:r0, :]) + bias_prev if qi >= 1 else None
        s_f = _dot_nt(qz, k_ref[0, 0:r0 - qb, :]) if qi >= 2 else None
        return qi, s_d, s_p, s_f

    def finish(qi, s_d, s_p, s_f):
        r0 = qi * qb
        end = r0 + qb
        m = jnp.max(s_d, axis=-1, keepdims=True)
        if s_p is not None:
            m = jnp.maximum(m, jnp.max(s_p, axis=-1, keepdims=True))
        if s_f is not None:
            m = jnp.maximum(m, jnp.max(s_f, axis=-1, keepdims=True))
        acc = _dot(jnp.exp2(s_d - m).astype(BF16), va_ref[r0:end, :])
        if s_p is not None:
            acc = acc + _dot(jnp.exp2(s_p - m).astype(BF16), va_ref[r0 - qb:r0, :])
        if s_f is not None:
            acc = acc + _dot(jnp.exp2(s_f - m).astype(BF16), va_ref[0:r0 - qb, :])
        a = acc[:, 0:LANES] / acc[:, LANES:2 * LANES]
        o = a[0:qb] - lam * a[qb:2 * qb]
        o_ref[0, r0:end, :] = _rms(o, gain).astype(BF16)

    _lagged([(qi,) for qi in range(seq // qb)], scores, finish)


def _diff_attention(lam4, subln, dq, dk, dv, gbias, batch, seq, lambda_init):
    blk = pl.BlockSpec((1, seq, LANES), lambda b, h: (b, 0, h))
    return pl.pallas_call(
        functools.partial(_diff_kernel, seq=seq, lambda_init=lambda_init),
        grid=(batch, DIFF_HEADS),
        in_specs=[
            pl.BlockSpec(lam4.shape, lambda b, h: (0, 0)),
            pl.BlockSpec(subln.shape, lambda b, h: (0, 0)),
            blk, blk, blk,
            pl.BlockSpec((1, ATT_Q_BLOCK, 2 * ATT_Q_BLOCK), lambda b, h: (h, 0, 0)),
        ],
        out_specs=blk,
        out_shape=jax.ShapeDtypeStruct((batch, seq, DIFF_W), BF16),
        scratch_shapes=[pltpu.VMEM((seq, 2 * LANES), BF16)],
        compiler_params=pltpu.CompilerParams(dimension_semantics=("parallel", "parallel"),
                                             vmem_limit_bytes=VMEM_LIMIT),
        name="diff_attention",
    )(lam4, subln, dq, dk, dv, gbias)


def _proj1_kernel(x_ref, gain_ref, w_ref, bd_ref, hg_ref, q_ref, kk_ref, vt_ref):
    h = _rms(x_ref[...], gain_ref[...]).astype(BF16)
    bd = bd_ref[...]
    qw = SWA_HEADS * HEAD_DIM
    kw = SWA_KV_HEADS * HEAD_DIM
    proj = _dot(h, w_ref[...])
    for i in range(qw // MXU_TILE):
        q_ref[:, i * MXU_TILE:(i + 1) * MXU_TILE] = _head_norm(
            proj[:, i * MXU_TILE:(i + 1) * MXU_TILE], bd, hg_ref[0:1, :]).astype(BF16)
    kn = _head_norm(proj[:, qw:qw + kw], bd, hg_ref[1:2, :])
    low = lax.broadcasted_iota(jnp.int32, (kn.shape[0], LANES), 1) < HEAD_DIM
    for p in range(kw // LANES):
        pair = kn[:, p * LANES:(p + 1) * LANES]
        swapped = pltpu.roll(pair, HEAD_DIM, axis=1)
        kk_ref[:, (2 * p) * LANES:(2 * p + 1) * LANES] = jnp.where(low, pair, swapped).astype(BF16)
        kk_ref[:, (2 * p + 1) * LANES:(2 * p + 2) * LANES] = jnp.where(low, swapped, pair).astype(BF16)
    vt_ref[0] = proj[:, qw + kw:qw + 2 * kw].T.astype(BF16)


def _proj1(x2d, gain, w, bd, hg, seq):
    n = x2d.shape[0]
    tm = ROW_TILE
    tps = seq // tm
    qw = SWA_HEADS * HEAD_DIM
    kw = SWA_KV_HEADS * HEAD_DIM
    full = lambda shape: pl.BlockSpec(shape, lambda t: (0,) * len(shape))
    return pl.pallas_call(
        _proj1_kernel,
        grid=(n // tm,),
        in_specs=[
            pl.BlockSpec((tm, D_MODEL), lambda t: (t, 0)),
            full((1, D_MODEL)),
            full(w.shape),
            full((MXU_TILE, MXU_TILE)),
            full(hg.shape),
        ],
        out_specs=[pl.BlockSpec((tm, qw), lambda t: (t, 0)),
                   pl.BlockSpec((tm, 2 * kw), lambda t: (t, 0)),
                   pl.BlockSpec((1, kw, tm), lambda t: (t // tps, 0, t % tps))],
        out_shape=[jax.ShapeDtypeStruct((n, qw), BF16),
                   jax.ShapeDtypeStruct((n, 2 * kw), BF16),
                   jax.ShapeDtypeStruct((n // seq, kw, seq), BF16)],
        compiler_params=pltpu.CompilerParams(dimension_semantics=("parallel",),
                                             vmem_limit_bytes=VMEM_LIMIT),
        name="proj_odd",
    )(x2d, gain, w, bd, hg)


def _swa_kernel(sink_ref, q_ref, kk_ref, vt_ref, sb_ref, o_ref, va_ref, *, seq):
    kvh = pl.program_id(1)
    w = WINDOW
    va_ref[0:HEAD_DIM, :] = vt_ref[0]
    sub = lax.broadcasted_iota(jnp.int32, (LANES - HEAD_DIM, seq), 0)
    va_ref[HEAD_DIM:LANES, :] = jnp.where(sub == 0, 1.0, 0.0).astype(BF16)
    sink = jnp.concatenate(
        [jnp.full((1, w), sink_ref[kvh * SWA_GROUP + g] * LOG2E, F32) for g in range(SWA_GROUP)],
        axis=1)
    low = lax.broadcasted_iota(jnp.int32, (w, LANES), 1) < HEAD_DIM
    zero = jnp.zeros((), BF16)

    def scores(n):
        r0 = n * w
        qb = q_ref[0, r0:r0 + w, :]
        qa, qc = qb[:, 0:LANES], qb[:, LANES:2 * LANES]
        q4 = jnp.concatenate([jnp.where(low, qa, zero), jnp.where(low, zero, qa),
                              jnp.where(low, qc, zero), jnp.where(low, zero, qc)], axis=0)
        if n == 0:
            return n, _dot_nt(kk_ref[0, 0:w, :], q4) + sb_ref[0, w:2 * w, :]
        return n, _dot_nt(kk_ref[0, r0 - w:r0 + w, :], q4) + sb_ref[0]

    def finish(n, st):
        r0 = n * w
        vals = va_ref[:, 0:w] if n == 0 else va_ref[:, r0 - w:r0 + w]
        m = jnp.maximum(jnp.max(st, axis=0, keepdims=True), sink)
        acc = _dot(vals, jnp.exp2(st - m).astype(BF16))
        denom = acc[HEAD_DIM:HEAD_DIM + 1, :] + jnp.exp2(sink - m)
        ot = acc[0:HEAD_DIM, :] / denom
        o_ref[0, r0:r0 + w, :] = jnp.concatenate(
            [jnp.concatenate([ot[:, 0:w], ot[:, w:2 * w]], axis=0).T,
             jnp.concatenate([ot[:, 2 * w:3 * w], ot[:, 3 * w:4 * w]], axis=0).T],
            axis=1).astype(BF16)

    _lagged([(n,) for n in range(seq // w)], scores, finish, depth=SWA_LAG)


def _swa_attention(sinks, q, kk, vt, swbias, batch, seq):
    gw = SWA_GROUP * HEAD_DIM
    return pl.pallas_call(
        functools.partial(_swa_kernel, seq=seq),
        grid=(batch, SWA_KV_HEADS),
        in_specs=[
            pl.BlockSpec(memory_space=pltpu.SMEM),
            pl.BlockSpec((1, seq, gw), lambda b, h: (b, 0, h)),
            pl.BlockSpec((1, seq, LANES), lambda b, h: (b, 0, h)),
            pl.BlockSpec((1, HEAD_DIM, seq), lambda b, h: (b, h, 0)),
            pl.BlockSpec((1, 2 * WINDOW, SWA_GROUP * WINDOW), lambda b, h: (h, 0, 0)),
        ],
        out_specs=pl.BlockSpec((1, seq, gw), lambda b, h: (b, 0, h)),
        out_shape=jax.ShapeDtypeStruct((batch, seq, SWA_HEADS * HEAD_DIM), BF16),
        scratch_shapes=[pltpu.VMEM((LANES, seq), BF16)],
        compiler_params=pltpu.CompilerParams(dimension_semantics=("parallel", "parallel"),
                                             vmem_limit_bytes=VMEM_LIMIT),
        name="swa_attention",
    )(sinks, q, kk, vt, swbias)


def _outmlp_kernel(*refs, n_mix):
    x_ref = refs[0]
    a_refs = refs[1:1 + n_mix]
    wo_ref, gain_ref, wg_ref, wu_ref, wd_ref, o_ref = refs[1 + n_mix:]
    mix = jnp.concatenate([a_ref[...] for a_ref in a_refs], axis=1)
    x1 = x_ref[...] + _dot(mix, wo_ref[...])
    h = _rms(x1, gain_ref[...]).astype(BF16)
    out = x1
    for c0 in range(0, FFN_HIDDEN, HIDDEN_CHUNK):
        c1 = min(c0 + HIDDEN_CHUNK, FFN_HIDDEN)
        g = _dot(h, wg_ref[:, c0:c1])
        u = _dot(h, wu_ref[:, c0:c1])
        y = (g * (1.0 / (1.0 + jnp.exp(-g))) * u).astype(BF16)
        out = out + _dot(y, wd_ref[c0:c1, :])
    o_ref[...] = out


def _outmlp(x2d, mixes, wo, gain, wg, wu, wd):
    n = x2d.shape[0]
    tm = MLP_ROW_TILE
    full = lambda shape: pl.BlockSpec(shape, lambda t: (0,) * len(shape),
                                      pipeline_mode=pl.Buffered(1))
    in_specs = [pl.BlockSpec((tm, D_MODEL), lambda t: (t, 0))]
    in_specs += [pl.BlockSpec((tm, a.shape[1]), lambda t: (t, 0)) for a in mixes]
    in_specs += [full(wo.shape), full(gain.shape), full(wg.shape), full(wu.shape), full(wd.shape)]
    return pl.pallas_call(
        functools.partial(_outmlp_kernel, n_mix=len(mixes)),
        grid=(n // tm,),
        in_specs=in_specs,
        out_specs=pl.BlockSpec((tm, D_MODEL), lambda t: (t, 0)),
        out_shape=jax.ShapeDtypeStruct((n, D_MODEL), F32),
        compiler_params=pltpu.CompilerParams(dimension_semantics=("parallel",),
                                             vmem_limit_bytes=VMEM_LIMIT),
        name="outproj_swiglu",
    )(x2d, *mixes, wo, gain, wg, wu, wd)


def _block_diag_mean():
    i = np.arange(MXU_TILE)
    same = (i[:, None] // HEAD_DIM) == (i[None, :] // HEAD_DIM)
    return jnp.asarray(same / HEAD_DIM, BF16)


def kernel(x, rel_bias_table, ev_attn_norm, ev_w_in, ev_b_forget, ev_fox_q_norm, ev_fox_k_norm,
           ev_diff_q_norm, ev_diff_k_norm, ev_lambda_q1, ev_lambda_k1, ev_lambda_q2, ev_lambda_k2,
           ev_diff_subln, ev_w_out, od_attn_norm, od_w_qkv, od_q_norm, od_k_norm, od_sinks, od_w_out,
           ffn_norm, w_gate, w_up, w_down):
    batch, seq, d = x.shape
    n = batch * seq
    x2d = x.reshape(n, d)
    bd = _block_diag_mean()
    swbias, gbias = _bias_tiles(rel_bias_table.astype(F32))
    swbias = swbias.reshape(SWA_KV_HEADS, SWA_GROUP, 2 * WINDOW, WINDOW).transpose(0, 2, 1, 3)
    swbias = swbias.reshape(SWA_KV_HEADS, 2 * WINDOW, SWA_GROUP * WINDOW)

    lambda_init = 0.8 - 0.6 * math.exp(-0.3 * 0)
    w_in = ev_w_in[0]
    ff0 = 3 * FOX_W
    d0 = ff0 + FOX_HEADS
    w0 = jnp.concatenate([w_in[:, 0:ff0], w_in[:, d0:d0 + 3 * DIFF_W]], axis=1).astype(BF16)
    wf = jnp.concatenate(
        [w_in[:, ff0:d0], jnp.zeros((d, LANES - FOX_HEADS), w_in.dtype)], axis=1).astype(BF16)
    tile8 = lambda g, scale=1.0: jnp.tile(g.astype(F32) * scale, FOX_W // HEAD_DIM)
    hg0 = jnp.stack([tile8(ev_fox_q_norm[0], Q_GAIN_SCALE), tile8(ev_fox_k_norm[0]),
                     tile8(ev_diff_q_norm[0], Q_GAIN_SCALE), tile8(ev_diff_k_norm[0])])
    bf = ev_b_forget[0].astype(F32).reshape(FOX_HEADS, 1)
    fq, fk, fv, dq, dk, dv, caug = _proj0(
        x2d, ev_attn_norm[0].astype(F32).reshape(1, d), w0, wf, bd, hg0, bf, seq)
    r3 = lambda a: a.reshape(batch, seq, a.shape[-1])
    fox = _fox_attention(r3(fq), r3(fk), r3(fv), r3(caug), batch, seq)
    lam4 = jnp.stack([ev_lambda_q1[0], ev_lambda_k1[0], ev_lambda_q2[0], ev_lambda_k2[0]]).astype(F32)
    dout = _diff_attention(lam4, ev_diff_subln[0].astype(F32).reshape(1, 2 * HEAD_DIM),
                           r3(dq), r3(dk), r3(dv), gbias, batch, seq, lambda_init)
    x2d = _outmlp(x2d, [fox.reshape(n, FOX_W), dout.reshape(n, DIFF_W)], ev_w_out[0].astype(BF16),
                  ffn_norm[0].astype(F32).reshape(1, d),
                  w_gate[0].astype(BF16), w_up[0].astype(BF16), w_down[0].astype(BF16))

    tile4 = lambda g, scale=1.0: jnp.tile(g.astype(F32) * scale, MXU_TILE // HEAD_DIM)
    hg1 = jnp.stack([tile4(od_q_norm[0], Q_GAIN_SCALE), tile4(od_k_norm[0])])
    q, kk, vt = _proj1(x2d, od_attn_norm[0].astype(F32).reshape(1, d), od_w_qkv[0].astype(BF16),
                       bd, hg1, seq)
    swa = _swa_attention(od_sinks[0].astype(F32), r3(q), r3(kk), vt, swbias, batch, seq)
    x2d = _outmlp(x2d, [swa.reshape(n, SWA_HEADS * HEAD_DIM)], od_w_out[0].astype(BF16),
                  ffn_norm[1].astype(F32).reshape(1, d),
                  w_gate[1].astype(BF16), w_up[1].astype(BF16), w_down[1].astype(BF16))
    return x2d.reshape(batch, seq, d)
```

```python
import functools
import math

import numpy as np
import jax
import jax.numpy as jnp
from jax import lax
from jax.experimental import pallas as pl
from jax.experimental.pallas import tpu as pltpu

F32 = jnp.float32
BF16 = jnp.bfloat16

D_MODEL = 1024
HEAD_DIM = 64
FOX_HEADS = 8
DIFF_HEADS = 4
SWA_HEADS = 16
SWA_KV_HEADS = 4
SWA_GROUP = SWA_HEADS // SWA_KV_HEADS
WINDOW = 128
NUM_BUCKETS = 32
MAX_DISTANCE = 128
FFN_HIDDEN = 2816
RMS_EPS = 1e-6
NEG_INF = -1e30
FOX_W = FOX_HEADS * HEAD_DIM
DIFF_W = DIFF_HEADS * 2 * HEAD_DIM
QK_SCALE = HEAD_DIM ** -0.5
LOG2E = math.log2(math.e)
Q_GAIN_SCALE = QK_SCALE * LOG2E

LANES = 128
MXU_TILE = 256
ROW_TILE = 512
MLP_ROW_TILE = 512
ATT_Q_BLOCK = 256
SWA_LAG = 4
HIDDEN_CHUNK = 1536
VMEM_LIMIT = 56 * 1024 * 1024

_NT = (((1,), (1,)), ((), ()))


def _dot(a, b):
    return jnp.dot(a, b, preferred_element_type=F32)


def _dot_nt(a, b):
    return lax.dot_general(a, b, _NT, preferred_element_type=F32)


def _lagged(items, first, second, depth=1):
    pending = []
    for item in items:
        pending.append(first(*item))
        if len(pending) > depth:
            second(*pending.pop(0))
    for p in pending:
        second(*p)


def _interleaved(items, first, second, depth=2):
    def drive(gens):
        done = {}
        while gens:
            for g in list(gens):
                try:
                    next(g)
                except StopIteration as stop:
                    done[id(g)] = stop.value
                    gens.remove(g)
        return done

    pending = []
    for item in items:
        g1 = first(*item)
        gens = [g1] if len(pending) < depth else [g1, second(*pending.pop(0))]
        pending.append(drive(gens)[id(g1)])
    for p in pending:
        drive([second(*p)])


def _hill_order(nblocks):
    return list(range(0, nblocks, 2)) + list(range(nblocks - 1 - nblocks % 2, 0, -2))


def _rms(xf, gain):
    ms = jnp.mean(xf * xf, axis=-1, keepdims=True)
    return xf * lax.rsqrt(ms + RMS_EPS) * gain


def _head_norm(acc, bd, gain):
    msq = _dot((acc * acc).astype(BF16), bd)
    return acc * lax.rsqrt(msq + RMS_EPS) * gain


def _bucket_ids(delta):
    n = np.maximum(delta, 0)
    max_exact = NUM_BUCKETS // 2
    nf = np.maximum(n, 1).astype(np.float64)
    large = max_exact + (np.log(nf / max_exact) / math.log(MAX_DISTANCE / max_exact)
                         * (NUM_BUCKETS - max_exact)).astype(np.int32)
    large = np.minimum(large, NUM_BUCKETS - 1)
    return np.where(n < max_exact, n, large).astype(np.int32)


def _bias_kernel(table_ref, bsw_ref, bg_ref, sw_ref, g_ref):
    h = pl.program_id(0)
    bsw = bsw_ref[...]
    sw = jnp.zeros(bsw.shape, F32)
    for b in range(NUM_BUCKETS):
        sw = jnp.where(bsw == b, table_ref[b, h] * LOG2E, sw)
    sw_ref[0] = jnp.where(bsw < 0, NEG_INF, sw)

    @pl.when(h < DIFF_HEADS)
    def _():
        bg = bg_ref[...]
        far = table_ref[NUM_BUCKETS - 1, h]
        g = jnp.zeros(bg.shape, F32)
        for b in range(NUM_BUCKETS):
            g = jnp.where(bg == b, (table_ref[b, h] - far) * LOG2E, g)
        g_ref[0] = jnp.where(bg < 0, NEG_INF, g)


def _bias_tiles(table):
    a = np.arange(WINDOW)[None, :]
    b = np.arange(2 * WINDOW)[:, None]
    d_sw = WINDOW + a - b
    bsw = np.where((d_sw >= 0) & (d_sw < WINDOW), _bucket_ids(d_sw), -1).astype(np.int32)
    a = np.arange(ATT_Q_BLOCK)[:, None]
    b = np.arange(2 * ATT_Q_BLOCK)[None, :]
    d_g = ATT_Q_BLOCK + a - b
    bg = np.where(d_g >= 0, _bucket_ids(d_g), -1).astype(np.int32)
    nh = table.shape[1]
    return pl.pallas_call(
        _bias_kernel,
        grid=(nh,),
        in_specs=[
            pl.BlockSpec(memory_space=pltpu.SMEM),
            pl.BlockSpec(bsw.shape, lambda h: (0, 0)),
            pl.BlockSpec(bg.shape, lambda h: (0, 0)),
        ],
        out_specs=[
            pl.BlockSpec((1,) + bsw.shape, lambda h: (h, 0, 0)),
            pl.BlockSpec((1,) + bg.shape, lambda h: (jnp.minimum(h, DIFF_HEADS - 1), 0, 0)),
        ],
        out_shape=[
            jax.ShapeDtypeStruct((nh,) + bsw.shape, F32),
            jax.ShapeDtypeStruct((DIFF_HEADS,) + bg.shape, F32),
        ],
        compiler_params=pltpu.CompilerParams(dimension_semantics=("arbitrary",)),
        name="rel_bias_tiles",
    )(table, jnp.asarray(bsw), jnp.asarray(bg))


def _proj0_kernel(x_ref, gain_ref, w_ref, wf_ref, bd_ref, hg_ref, bf_ref,
                  fq_ref, fk_ref, fv_ref, dq_ref, dk_ref, dv_ref, caug_ref,
                  carry_ref, *, tiles_per_seq):
    t = pl.program_id(0)

    @pl.when(t == 0)
    def _():
        carry_ref[...] = jnp.zeros_like(carry_ref)

    h = _rms(x_ref[...], gain_ref[...]).astype(BF16)
    bd = bd_ref[...]
    rows = h.shape[0]

    z = _dot(h, wf_ref[...])
    proj = _dot(h, w_ref[...])

    z8 = z.T[0:FOX_HEADS, :] + bf_ref[...]
    y = -(jnp.maximum(-z8, 0.0) + jnp.log1p(jnp.exp(-jnp.abs(z8))))
    lane = lax.broadcasted_iota(jnp.int32, y.shape, 1)
    shift = 1
    while shift < rows:
        y = y + jnp.where(lane >= shift, pltpu.roll(y, shift, axis=1), 0.0)
        shift *= 2

    carry = jnp.where(t % tiles_per_seq == 0, 0.0, carry_ref[:, 0:1])
    c = y + carry
    carry_ref[...] = jnp.broadcast_to(c[:, rows - 1:rows], carry_ref.shape)
    cn = c * (-LOG2E)
    hi = cn.astype(BF16).astype(F32)
    r1 = cn - hi
    mid = r1.astype(BF16).astype(F32)
    lo = (r1 - mid).astype(BF16).astype(F32)
    half = jnp.concatenate([hi, mid, lo, jnp.zeros((HEAD_DIM - 3 * FOX_HEADS, rows), F32)], axis=0)
    caug_ref[...] = jnp.concatenate([half, half], axis=0).T.astype(BF16)

    def normed(col0, gidx, out_ref):
        for i in range(FOX_W // MXU_TILE):
            c0 = col0 + i * MXU_TILE
            g = hg_ref[gidx:gidx + 1, i * MXU_TILE:(i + 1) * MXU_TILE]
            out_ref[:, i * MXU_TILE:(i + 1) * MXU_TILE] = _head_norm(
                proj[:, c0:c0 + MXU_TILE], bd, g).astype(BF16)

    normed(0 * FOX_W, 0, fq_ref)
    normed(1 * FOX_W, 1, fk_ref)
    fv_ref[...] = proj[:, 2 * FOX_W:3 * FOX_W].astype(BF16)
    normed(3 * FOX_W, 2, dq_ref)
    normed(4 * FOX_W, 3, dk_ref)
    dv_ref[...] = proj[:, 5 * FOX_W:6 * FOX_W].astype(BF16)


def _proj0(x2d, gain, w, wf, bd, hg, bf, seq):
    n = x2d.shape[0]
    tm = ROW_TILE
    tps = seq // tm
    row_spec = pl.BlockSpec((tm, FOX_W), lambda t: (t, 0))
    full = lambda shape: pl.BlockSpec(shape, lambda t: (0,) * len(shape))
    outs = pl.pallas_call(
        functools.partial(_proj0_kernel, tiles_per_seq=tps),
        grid=(n // tm,),
        in_specs=[
            pl.BlockSpec((tm, D_MODEL), lambda t: (t, 0)),
            full((1, D_MODEL)),
            full(w.shape),
            full(wf.shape),
            full((MXU_TILE, MXU_TILE)),
            full(hg.shape),
            full(bf.shape),
        ],
        out_specs=[row_spec] * 6 + [pl.BlockSpec((tm, LANES), lambda t: (t, 0))],
        out_shape=[jax.ShapeDtypeStruct((n, FOX_W), BF16)] * 6
        + [jax.ShapeDtypeStruct((n, LANES), BF16)],
        scratch_shapes=[pltpu.VMEM((FOX_HEADS, LANES), F32)],
        compiler_params=pltpu.CompilerParams(dimension_semantics=("arbitrary",),
                                             vmem_limit_bytes=VMEM_LIMIT),
        name="proj_even",
    )(x2d, gain, w, wf, bd, hg, bf)
    return outs


def _fox_kernel(q_ref, k_ref, v_ref, caug_ref, o_ref, ka_ref, va_ref, *, seq):
    j = pl.program_id(1)
    qb = ATT_Q_BLOCK
    lane_s = lax.broadcasted_iota(jnp.int32, (seq, LANES), 1)
    low_s = lane_s < HEAD_DIM
    vp = v_ref[0]
    kp = k_ref[0]
    ca = caug_ref[0]
    one = jnp.ones((), BF16)
    va_ref[0] = jnp.where(low_s, vp, one)
    va_ref[1] = jnp.where(low_s, one, vp)
    ka_ref[0] = jnp.where(low_s, kp, ca)
    ka_ref[1] = jnp.where(low_s, ca, kp)
    lane_q = lax.broadcasted_iota(jnp.int32, (qb, LANES), 1)
    row = lax.broadcasted_iota(jnp.int32, (qb, qb), 0)
    col = lax.broadcasted_iota(jnp.int32, (qb, qb), 1)
    causal = col <= row
    owns = (lane_q < HEAD_DIM, lane_q >= HEAD_DIM)

    def scores(qi, hh):
        r0 = qi * qb
        rel = lane_q - (HEAD_DIM if hh == 0 else 0) - (2 * j + hh)
        pick = (rel == 0) | (rel == FOX_HEADS) | (rel == 2 * FOX_HEADS)
        qz = jnp.where(owns[hh], q_ref[0, r0:r0 + qb, :], jnp.where(pick, 1.0, 0.0).astype(BF16))
        tiles = []
        for t in range(qi + 1):
            s = _dot_nt(qz, ka_ref[hh, t * qb:(t + 1) * qb, :])
            tiles.append(jnp.where(causal, s, NEG_INF) if t == qi else s)
            yield
        return qi, hh, tiles

    halves = []

    def finish(qi, hh, tiles):
        r0 = qi * qb
        m = jnp.max(functools.reduce(jnp.maximum, tiles), axis=-1, keepdims=True)
        acc = None
        for t, s in enumerate(tiles):
            part = _dot(jnp.exp2(s - m).astype(BF16), va_ref[hh, t * qb:(t + 1) * qb, :])
            acc = part if acc is None else acc + part
            yield
        denom = pltpu.roll(jnp.where(owns[hh], 1.0, acc), HEAD_DIM, axis=1)
        halves.append(acc / denom)
        if hh == 1:
            o_ref[0, r0:r0 + qb, :] = jnp.where(owns[0], halves[-2], halves[-1]).astype(BF16)

    _interleaved([(qi, hh) for qi in _hill_order(seq // qb) for hh in range(2)], scores, finish)


def _fox_attention(fq, fk, fv, caug, batch, seq):
    blk = pl.BlockSpec((1, seq, LANES), lambda b, j: (b, 0, j))
    return pl.pallas_call(
        functools.partial(_fox_kernel, seq=seq),
        grid=(batch, FOX_W // LANES),
        in_specs=[blk, blk, blk, pl.BlockSpec((1, seq, LANES), lambda b, j: (b, 0, 0))],
        out_specs=blk,
        out_shape=jax.ShapeDtypeStruct((batch, seq, FOX_W), BF16),
        scratch_shapes=[pltpu.VMEM((2, seq, LANES), BF16), pltpu.VMEM((2, seq, LANES), BF16)],
        compiler_params=pltpu.CompilerParams(dimension_semantics=("parallel", "parallel"),
                                             vmem_limit_bytes=VMEM_LIMIT),
        name="fox_attention",
    )(fq, fk, fv, caug)


def _diff_kernel(lam_ref, g_ref, q_ref, k_ref, v_ref, gb_ref, o_ref, va_ref, *, seq, lambda_init):
    qb = ATT_Q_BLOCK
    lv = lam_ref[...]
    lam = (jnp.exp(jnp.sum(lv[0:1] * lv[1:2], axis=-1, keepdims=True))
           - jnp.exp(jnp.sum(lv[2:3] * lv[3:4], axis=-1, keepdims=True)) + lambda_init)
    va_ref[:, 0:LANES] = v_ref[0]
    va_ref[:, LANES:2 * LANES] = jnp.ones((seq, LANES), BF16)
    lane_q = lax.broadcasted_iota(jnp.int32, (qb, LANES), 1)
    bias_prev = jnp.concatenate([gb_ref[0, :, 0:qb]] * 2, axis=0)
    bias_diag = jnp.concatenate([gb_ref[0, :, qb:2 * qb]] * 2, axis=0)
    gain = g_ref[...] * (1.0 - lambda_init)
    zero = jnp.zeros((), BF16)

    def scores(qi):
        r0 = qi * qb
        end = r0 + qb
        qp = q_ref[0, r0:end, :]
        qz = jnp.concatenate([jnp.where(lane_q < HEAD_DIM, qp, zero),
                              jnp.where(lane_q < HEAD_DIM, zero, qp)], axis=0)
        tiles = []
        for t in range(qi + 1):
            s = _dot_nt(qz, k_ref[0, t * qb:(t + 1) * qb, :])
            tiles.append(s + bias_diag if t == qi else s + bias_prev if t == qi - 1 else s)
            yield
        return qi, tiles

    def finish(qi, tiles):
        r0 = qi * qb
        end = r0 + qb
        m = jnp.max(functools.reduce(jnp.maximum, tiles), axis=-1, keepdims=True)
        acc = None
        for t, s in enumerate(tiles):
            part = _dot(jnp.exp2(s - m).astype(BF16), va_ref[t * qb:(t + 1) * qb, :])
            acc = part if acc is None else acc + part
            yield
        a = acc[:, 0:LANES] / acc[:, LANES:2 * LANES]
        o = a[0:qb] - lam * a[qb:2 * qb]
        o_ref[0, r0:end, :] = _rms(o, gain).astype(BF16)

    _interleaved([(qi,) for qi in _hill_order(seq // qb)], scores, finish)


def _diff_attention(lam4, subln, dq, dk, dv, gbias, batch, seq, lambda_init):
    blk = pl.BlockSpec((1, seq, LANES), lambda b, h: (b, 0, h))
    return pl.pallas_call(
        functools.partial(_diff_kernel, seq=seq, lambda_init=lambda_init),
        grid=(batch, DIFF_HEADS),
        in_specs=[
            pl.BlockSpec(lam4.shape, lambda b, h: (0, 0)),
            pl.BlockSpec(subln.shape, lambda b, h: (0, 0)),
            blk, blk, blk,
            pl.BlockSpec((1, ATT_Q_BLOCK, 2 * ATT_Q_BLOCK), lambda b, h: (h, 0, 0)),
        ],
        out_specs=blk,
        out_shape=jax.ShapeDtypeStruct((batch, seq, DIFF_W), BF16),
        scratch_shapes=[pltpu.VMEM((seq, 2 * LANES), BF16)],
        compiler_params=pltpu.CompilerParams(dimension_semantics=("parallel", "parallel"),
                                             vmem_limit_bytes=VMEM_LIMIT),
        name="diff_attention",
    )(lam4, subln, dq, dk, dv, gbias)


def _proj1_kernel(x_ref, gain_ref, w_ref, bd_ref, hg_ref, q_ref, kk_ref, vt_ref):
    h = _rms(x_ref[...], gain_ref[...]).astype(BF16)
    bd = bd_ref[...]
    qw = SWA_HEADS * HEAD_DIM
    kw = SWA_KV_HEADS * HEAD_DIM
    proj = _dot(h, w_ref[...])
    for i in range(qw // MXU_TILE):
        q_ref[:, i * MXU_TILE:(i + 1) * MXU_TILE] = _head_norm(
            proj[:, i * MXU_TILE:(i + 1) * MXU_TILE], bd, hg_ref[0:1, :]).astype(BF16)
    kn = _head_norm(proj[:, qw:qw + kw], bd, hg_ref[1:2, :])
    low = lax.broadcasted_iota(jnp.int32, (kn.shape[0], LANES), 1) < HEAD_DIM
    for p in range(kw // LANES):
        pair = kn[:, p * LANES:(p + 1) * LANES]
        swapped = pltpu.roll(pair, HEAD_DIM, axis=1)
        kk_ref[:, (2 * p) * LANES:(2 * p + 1) * LANES] = jnp.where(low, pair, swapped).astype(BF16)
        kk_ref[:, (2 * p + 1) * LANES:(2 * p + 2) * LANES] = jnp.where(low, swapped, pair).astype(BF16)
    vt_ref[0] = proj[:, qw + kw:qw + 2 * kw].T.astype(BF16)


def _proj1(x2d, gain, w, bd, hg, seq):
    n = x2d.shape[0]
    tm = ROW_TILE
    tps = seq // tm
    qw = SWA_HEADS * HEAD_DIM
    kw = SWA_KV_HEADS * HEAD_DIM
    full = lambda shape: pl.BlockSpec(shape, lambda t: (0,) * len(shape))
    return pl.pallas_call(
        _proj1_kernel,
        grid=(n // tm,),
        in_specs=[
            pl.BlockSpec((tm, D_MODEL), lambda t: (t, 0)),
            full((1, D_MODEL)),
            full(w.shape),
            full((MXU_TILE, MXU_TILE)),
            full(hg.shape),
        ],
        out_specs=[pl.BlockSpec((tm, qw), lambda t: (t, 0)),
                   pl.BlockSpec((tm, 2 * kw), lambda t: (t, 0)),
                   pl.BlockSpec((1, kw, tm), lambda t: (t // tps, 0, t % tps))],
        out_shape=[jax.ShapeDtypeStruct((n, qw), BF16),
                   jax.ShapeDtypeStruct((n, 2 * kw), BF16),
                   jax.ShapeDtypeStruct((n // seq, kw, seq), BF16)],
        compiler_params=pltpu.CompilerParams(dimension_semantics=("parallel",),
                                             vmem_limit_bytes=VMEM_LIMIT),
        name="proj_odd",
    )(x2d, gain, w, bd, hg)


def _swa_kernel(sink_ref, q_ref, kk_ref, vt_ref, sb_ref, o_ref, va_ref, *, seq):
    kvh = pl.program_id(1)
    w = WINDOW
    va_ref[0:HEAD_DIM, :] = vt_ref[0]
    sub = lax.broadcasted_iota(jnp.int32, (LANES - HEAD_DIM, seq), 0)
    va_ref[HEAD_DIM:LANES, :] = jnp.where(sub == 0, 1.0, 0.0).astype(BF16)
    sink = jnp.concatenate(
        [jnp.full((1, w), sink_ref[kvh * SWA_GROUP + g] * LOG2E, F32) for g in range(SWA_GROUP)],
        axis=1)
    low = lax.broadcasted_iota(jnp.int32, (w, LANES), 1) < HEAD_DIM
    zero = jnp.zeros((), BF16)

    def scores(n):
        r0 = n * w
        qb = q_ref[0, r0:r0 + w, :]
        qa, qc = qb[:, 0:LANES], qb[:, LANES:2 * LANES]
        q4 = jnp.concatenate([jnp.where(low, qa, zero), jnp.where(low, zero, qa),
                              jnp.where(low, qc, zero), jnp.where(low, zero, qc)], axis=0)
        if n == 0:
            return n, _dot_nt(kk_ref[0, 0:w, :], q4) + sb_ref[0, w:2 * w, :]
        return n, _dot_nt(kk_ref[0, r0 - w:r0 + w, :], q4) + sb_ref[0]

    def finish(n, st):
        r0 = n * w
        vals = va_ref[:, 0:w] if n == 0 else va_ref[:, r0 - w:r0 + w]
        m = jnp.maximum(jnp.max(st, axis=0, keepdims=True), sink)
        acc = _dot(vals, jnp.exp2(st - m).astype(BF16))
        denom = acc[HEAD_DIM:HEAD_DIM + 1, :] + jnp.exp2(sink - m)
        ot = acc[0:HEAD_DIM, :] / denom
        o_ref[0, r0:r0 + w, :] = jnp.concatenate(
            [jnp.concatenate([ot[:, 0:w], ot[:, w:2 * w]], axis=0).T,
             jnp.concatenate([ot[:, 2 * w:3 * w], ot[:, 3 * w:4 * w]], axis=0).T],
            axis=1).astype(BF16)

    _lagged([(n,) for n in range(seq // w)], scores, finish, depth=SWA_LAG)


def _swa_attention(sinks, q, kk, vt, swbias, batch, seq):
    gw = SWA_GROUP * HEAD_DIM
    return pl.pallas_call(
        functools.partial(_swa_kernel, seq=seq),
        grid=(batch, SWA_KV_HEADS),
        in_specs=[
            pl.BlockSpec(memory_space=pltpu.SMEM),
            pl.BlockSpec((1, seq, gw), lambda b, h: (b, 0, h)),
            pl.BlockSpec((1, seq, LANES), lambda b, h: (b, 0, h)),
            pl.BlockSpec((1, HEAD_DIM, seq), lambda b, h: (b, h, 0)),
            pl.BlockSpec((1, 2 * WINDOW, SWA_GROUP * WINDOW), lambda b, h: (h, 0, 0)),
        ],
        out_specs=pl.BlockSpec((1, seq, gw), lambda b, h: (b, 0, h)),
        out_shape=jax.ShapeDtypeStruct((batch, seq, SWA_HEADS * HEAD_DIM), BF16),
        scratch_shapes=[pltpu.VMEM((LANES, seq), BF16)],
        compiler_params=pltpu.CompilerParams(dimension_semantics=("parallel", "parallel"),
                                             vmem_limit_bytes=VMEM_LIMIT),
        name="swa_attention",
    )(sinks, q, kk, vt, swbias)


def _outmlp_kernel(*refs, n_mix):
    x_ref = refs[0]
    a_refs = refs[1:1 + n_mix]
    wo_ref, gain_ref, wg_ref, wu_ref, wd_ref, o_ref = refs[1 + n_mix:]
    mix = jnp.concatenate([a_ref[...] for a_ref in a_refs], axis=1)
    x1 = x_ref[...] + _dot(mix, wo_ref[...])
    h = _rms(x1, gain_ref[...]).astype(BF16)
    out = x1
    for c0 in range(0, FFN_HIDDEN, HIDDEN_CHUNK):
        c1 = min(c0 + HIDDEN_CHUNK, FFN_HIDDEN)
        g = _dot(h, wg_ref[:, c0:c1])
        u = _dot(h, wu_ref[:, c0:c1])
        y = (g * (1.0 / (1.0 + jnp.exp(-g))) * u).astype(BF16)
        out = out + _dot(y, wd_ref[c0:c1, :])
    o_ref[...] = out


def _outmlp(x2d, mixes, wo, gain, wg, wu, wd, layer):
    n = x2d.shape[0]
    tm = MLP_ROW_TILE
    full = lambda shape: pl.BlockSpec(shape, lambda t: (0,) * len(shape),
                                      pipeline_mode=pl.Buffered(1))
    of_layer = lambda w: pl.BlockSpec((None,) + w.shape[1:], lambda t: (layer, 0, 0),
                                      pipeline_mode=pl.Buffered(1))
    in_specs = [pl.BlockSpec((tm, D_MODEL), lambda t: (t, 0))]
    in_specs += [pl.BlockSpec((tm, a.shape[1]), lambda t: (t, 0)) for a in mixes]
    in_specs += [full(wo.shape), full(gain.shape), of_layer(wg), of_layer(wu), of_layer(wd)]
    return pl.pallas_call(
        functools.partial(_outmlp_kernel, n_mix=len(mixes)),
        grid=(n // tm,),
        in_specs=in_specs,
        out_specs=pl.BlockSpec((tm, D_MODEL), lambda t: (t, 0)),
        out_shape=jax.ShapeDtypeStruct((n, D_MODEL), F32),
        compiler_params=pltpu.CompilerParams(dimension_semantics=("parallel",),
                                             vmem_limit_bytes=VMEM_LIMIT),
        name="outproj_swiglu",
    )(x2d, *mixes, wo, gain, wg, wu, wd)


def _block_diag_mean():
    i = np.arange(MXU_TILE)
    same = (i[:, None] // HEAD_DIM) == (i[None, :] // HEAD_DIM)
    return jnp.asarray(same / HEAD_DIM, BF16)


def kernel(x, rel_bias_table, ev_attn_norm, ev_w_in, ev_b_forget, ev_fox_q_norm, ev_fox_k_norm,
           ev_diff_q_norm, ev_diff_k_norm, ev_lambda_q1, ev_lambda_k1, ev_lambda_q2, ev_lambda_k2,
           ev_diff_subln, ev_w_out, od_attn_norm, od_w_qkv, od_q_norm, od_k_norm, od_sinks, od_w_out,
           ffn_norm, w_gate, w_up, w_down):
    batch, seq, d = x.shape
    n = batch * seq
    x2d = x.reshape(n, d)
    bd = _block_diag_mean()
    wg, wu, wd = w_gate.astype(BF16), w_up.astype(BF16), w_down.astype(BF16)
    swbias, gbias = _bias_tiles(rel_bias_table.astype(F32))
    swbias = swbias.reshape(SWA_KV_HEADS, SWA_GROUP, 2 * WINDOW, WINDOW).transpose(0, 2, 1, 3)
    swbias = swbias.reshape(SWA_KV_HEADS, 2 * WINDOW, SWA_GROUP * WINDOW)

    lambda_init = 0.8 - 0.6 * math.exp(-0.3 * 0)
    w_in = ev_w_in[0]
    ff0 = 3 * FOX_W
    d0 = ff0 + FOX_HEADS
    w0 = jnp.concatenate([w_in[:, 0:ff0], w_in[:, d0:d0 + 3 * DIFF_W]], axis=1).astype(BF16)
    wf = jnp.concatenate(
        [w_in[:, ff0:d0], jnp.zeros((d, LANES - FOX_HEADS), w_in.dtype)], axis=1).astype(BF16)
    tile8 = lambda g, scale=1.0: jnp.tile(g.astype(F32) * scale, FOX_W // HEAD_DIM)
    hg0 = jnp.stack([tile8(ev_fox_q_norm[0], Q_GAIN_SCALE), tile8(ev_fox_k_norm[0]),
                     tile8(ev_diff_q_norm[0], Q_GAIN_SCALE), tile8(ev_diff_k_norm[0])])
    bf = ev_b_forget[0].astype(F32).reshape(FOX_HEADS, 1)
    fq, fk, fv, dq, dk, dv, caug = _proj0(
        x2d, ev_attn_norm[0].astype(F32).reshape(1, d), w0, wf, bd, hg0, bf, seq)
    r3 = lambda a: a.reshape(batch, seq, a.shape[-1])
    fox = _fox_attention(r3(fq), r3(fk), r3(fv), r3(caug), batch, seq)
    lam4 = jnp.stack([ev_lambda_q1[0], ev_lambda_k1[0], ev_lambda_q2[0], ev_lambda_k2[0]]).astype(F32)
    dout = _diff_attention(lam4, ev_diff_subln[0].astype(F32).reshape(1, 2 * HEAD_DIM),
                           r3(dq), r3(dk), r3(dv), gbias, batch, seq, lambda_init)
    x2d = _outmlp(x2d, [fox.reshape(n, FOX_W), dout.reshape(n, DIFF_W)], ev_w_out[0].astype(BF16),
                  ffn_norm[0].astype(F32).reshape(1, d), wg, wu, wd, 0)

    tile4 = lambda g, scale=1.0: jnp.tile(g.astype(F32) * scale, MXU_TILE // HEAD_DIM)
    hg1 = jnp.stack([tile4(od_q_norm[0], Q_GAIN_SCALE), tile4(od_k_norm[0])])
    q, kk, vt = _proj1(x2d, od_attn_norm[0].astype(F32).reshape(1, d), od_w_qkv[0].astype(BF16),
                       bd, hg1, seq)
    swa = _swa_attention(od_sinks[0].astype(F32), r3(q), r3(kk), vt, swbias, batch, seq)
    x2d = _outmlp(x2d, [swa.reshape(n, SWA_HEADS * HEAD_DIM)], od_w_out[0].astype(BF16),
                  ffn_norm[1].astype(F32).reshape(1, d), wg, wu, wd, 1)
    return x2d.reshape(batch, seq, d)
```

```python
import functools
import math

import numpy as np
import jax
import jax.numpy as jnp
from jax import lax
from jax.experimental import pallas as pl
from jax.experimental.pallas import tpu as pltpu

F32 = jnp.float32
BF16 = jnp.bfloat16

D_MODEL = 1024
HEAD_DIM = 64
FOX_HEADS = 8
DIFF_HEADS = 4
SWA_HEADS = 16
SWA_KV_HEADS = 4
SWA_GROUP = SWA_HEADS // SWA_KV_HEADS
WINDOW = 128
NUM_BUCKETS = 32
MAX_DISTANCE = 128
FFN_HIDDEN = 2816
RMS_EPS = 1e-6
NEG_INF = -1e30
FOX_W = FOX_HEADS * HEAD_DIM
DIFF_W = DIFF_HEADS * 2 * HEAD_DIM
QK_SCALE = HEAD_DIM ** -0.5
LOG2E = math.log2(math.e)
Q_GAIN_SCALE = QK_SCALE * LOG2E

LANES = 128
BF16_SUBLANES = 16
MXU_TILE = 256
ROW_TILE = 1024
MLP_ROW_TILE = 1024
ATT_Q_BLOCK = 256
FOX_PAIRS_PER_STEP = 2
DIFF_HEADS_PER_STEP = 2
SWA_KV_PER_STEP = 2
PROJ_LAG = 2
SWA_LAG = 4
HIDDEN_CHUNK = 768
VMEM_LIMIT = 56 * 1024 * 1024

_NT = (((1,), (1,)), ((), ()))


def _dot(a, b):
    return jnp.dot(a, b, preferred_element_type=F32)


def _dot_nt(a, b):
    return lax.dot_general(a, b, _NT, preferred_element_type=F32)


def _lagged(items, first, second, depth=1):
    pending = []
    for item in items:
        pending.append(first(*item))
        if len(pending) > depth:
            second(*pending.pop(0))
    for p in pending:
        second(*p)


def _interleaved(items, first, second, depth=2):
    def drive(gens):
        done = {}
        while gens:
            for g in list(gens):
                try:
                    next(g)
                except StopIteration as stop:
                    done[id(g)] = stop.value
                    gens.remove(g)
        return done

    pending = []
    for item in items:
        g1 = first(*item)
        gens = [g1] if len(pending) < depth else [g1, second(*pending.pop(0))]
        pending.append(drive(gens)[id(g1)])
    for p in pending:
        drive([second(*p)])


def _hill_order(nblocks):
    return list(range(0, nblocks, 2)) + list(range(nblocks - 1 - nblocks % 2, 0, -2))


def _rms(xf, gain):
    ms = jnp.mean(xf * xf, axis=-1, keepdims=True)
    return xf * lax.rsqrt(ms + RMS_EPS) * gain


def _head_norm(acc, bd, gain):
    msq = _dot((acc * acc).astype(BF16), bd)
    return acc * lax.rsqrt(msq + RMS_EPS) * gain


def _bucket_ids(delta):
    n = np.maximum(delta, 0)
    max_exact = NUM_BUCKETS // 2
    nf = np.maximum(n, 1).astype(np.float64)
    large = max_exact + (np.log(nf / max_exact) / math.log(MAX_DISTANCE / max_exact)
                         * (NUM_BUCKETS - max_exact)).astype(np.int32)
    large = np.minimum(large, NUM_BUCKETS - 1)
    return np.where(n < max_exact, n, large).astype(np.int32)


def _bias_kernel(table_ref, bsw_ref, bg_ref, sw_ref, g_ref):
    h = pl.program_id(0)
    bsw = bsw_ref[...]
    sw = jnp.zeros(bsw.shape, F32)
    for b in range(NUM_BUCKETS):
        sw = jnp.where(bsw == b, table_ref[b, h] * LOG2E, sw)
    sw_ref[0] = jnp.where(bsw < 0, NEG_INF, sw)

    @pl.when(h < DIFF_HEADS)
    def _():
        bg = bg_ref[...]
        far = table_ref[NUM_BUCKETS - 1, h]
        g = jnp.zeros(bg.shape, F32)
        for b in range(NUM_BUCKETS):
            g = jnp.where(bg == b, (table_ref[b, h] - far) * LOG2E, g)
        g_ref[0] = jnp.where(bg < 0, NEG_INF, g)


def _bias_tiles(table):
    a = np.arange(WINDOW)[None, :]
    b = np.arange(2 * WINDOW)[:, None]
    d_sw = WINDOW + a - b
    bsw = np.where((d_sw >= 0) & (d_sw < WINDOW), _bucket_ids(d_sw), -1).astype(np.int32)
    a = np.arange(ATT_Q_BLOCK)[:, None]
    b = np.arange(2 * ATT_Q_BLOCK)[None, :]
    d_g = ATT_Q_BLOCK + a - b
    bg = np.where(d_g >= 0, _bucket_ids(d_g), -1).astype(np.int32)
    nh = table.shape[1]
    return pl.pallas_call(
        _bias_kernel,
        grid=(nh,),
        in_specs=[
            pl.BlockSpec(memory_space=pltpu.SMEM),
            pl.BlockSpec(bsw.shape, lambda h: (0, 0)),
            pl.BlockSpec(bg.shape, lambda h: (0, 0)),
        ],
        out_specs=[
            pl.BlockSpec((1,) + bsw.shape, lambda h: (h, 0, 0)),
            pl.BlockSpec((1,) + bg.shape, lambda h: (jnp.minimum(h, DIFF_HEADS - 1), 0, 0)),
        ],
        out_shape=[
            jax.ShapeDtypeStruct((nh,) + bsw.shape, F32),
            jax.ShapeDtypeStruct((DIFF_HEADS,) + bg.shape, F32),
        ],
        compiler_params=pltpu.CompilerParams(dimension_semantics=("arbitrary",)),
        name="rel_bias_tiles",
    )(table, jnp.asarray(bsw), jnp.asarray(bg))


def _proj0_kernel(x_ref, gain_ref, w_ref, wf_ref, bd_ref, hg_ref, bf_ref,
                  fq_ref, fk_ref, fv_ref, dq_ref, dk_ref, dv_ref, caug_ref,
                  carry_ref, *, tiles_per_seq):
    t = pl.program_id(0)

    @pl.when(t == 0)
    def _():
        carry_ref[...] = jnp.zeros_like(carry_ref)

    h = _rms(x_ref[...], gain_ref[...]).astype(BF16)
    bd = bd_ref[...]
    rows = h.shape[0]

    outs = (fq_ref, fk_ref, fv_ref, dq_ref, dk_ref, dv_ref)
    gains = (0, 1, None, 2, 3, None)

    def project(g):
        return g, _dot(h, w_ref[:, g * FOX_W:(g + 1) * FOX_W])

    def epilogue(g, proj):
        if gains[g] is None:
            outs[g][...] = proj.astype(BF16)
            return
        for i in range(FOX_W // MXU_TILE):
            cols = slice(i * MXU_TILE, (i + 1) * MXU_TILE)
            outs[g][:, cols] = _head_norm(
                proj[:, cols], bd, hg_ref[gains[g]:gains[g] + 1, cols]).astype(BF16)

    pending = [project(g) for g in range(PROJ_LAG)]

    z8 = _dot_nt(wf_ref[...], h)[0:FOX_HEADS, :] + bf_ref[...]
    y = -(jnp.maximum(-z8, 0.0) + jnp.log1p(jnp.exp(-jnp.abs(z8))))
    lane = lax.broadcasted_iota(jnp.int32, y.shape, 1)
    shift = 1
    while shift < rows:
        y = y + jnp.where(lane >= shift, pltpu.roll(y, shift, axis=1), 0.0)
        shift *= 2

    carry = jnp.where(t % tiles_per_seq == 0, 0.0, carry_ref[:, 0:1])
    c = y + carry
    carry_ref[...] = jnp.broadcast_to(c[:, rows - 1:rows], carry_ref.shape)
    cn = c * (-LOG2E)
    hi = cn.astype(BF16).astype(F32)
    r1 = cn - hi
    mid = r1.astype(BF16).astype(F32)
    lo = (r1 - mid).astype(BF16).astype(F32)
    half = jnp.concatenate([hi, mid, lo, jnp.zeros((HEAD_DIM - 3 * FOX_HEADS, rows), F32)], axis=0)
    caug_ref[...] = jnp.concatenate([half, half], axis=0).T.astype(BF16)

    for g in range(PROJ_LAG, len(outs)):
        pending.append(project(g))
        epilogue(*pending.pop(0))
    for p in pending:
        epilogue(*p)


def _proj0(x2d, gain, w, wf, bd, hg, bf, seq):
    n = x2d.shape[0]
    tm = ROW_TILE
    tps = seq // tm
    row_spec = pl.BlockSpec((tm, FOX_W), lambda t: (t, 0))
    full = lambda shape: pl.BlockSpec(shape, lambda t: (0,) * len(shape))
    outs = pl.pallas_call(
        functools.partial(_proj0_kernel, tiles_per_seq=tps),
        grid=(n // tm,),
        in_specs=[
            pl.BlockSpec((tm, D_MODEL), lambda t: (t, 0)),
            full((1, D_MODEL)),
            full(w.shape),
            full(wf.shape),
            full((MXU_TILE, MXU_TILE)),
            full(hg.shape),
            full(bf.shape),
        ],
        out_specs=[row_spec] * 6 + [pl.BlockSpec((tm, LANES), lambda t: (t, 0))],
        out_shape=[jax.ShapeDtypeStruct((n, FOX_W), BF16)] * 6
        + [jax.ShapeDtypeStruct((n, LANES), BF16)],
        scratch_shapes=[pltpu.VMEM((FOX_HEADS, LANES), F32)],
        compiler_params=pltpu.CompilerParams(dimension_semantics=("arbitrary",),
                                             vmem_limit_bytes=VMEM_LIMIT),
        name="proj_even",
    )(x2d, gain, w, wf, bd, hg, bf)
    return outs


def _fox_kernel(q_ref, k_ref, v_ref, caug_ref, o_ref, ka_ref, va_ref, *, seq):
    jj = pl.program_id(1)
    qb = ATT_Q_BLOCK
    lane_s = lax.broadcasted_iota(jnp.int32, (seq, LANES), 1)
    low_s = lane_s < HEAD_DIM
    ca = caug_ref[0]
    one = jnp.ones((), BF16)
    for p in range(FOX_PAIRS_PER_STEP):
        vp = v_ref[0, :, p * LANES:(p + 1) * LANES]
        kp = k_ref[0, :, p * LANES:(p + 1) * LANES]
        va_ref[2 * p] = jnp.where(low_s, vp, one)
        va_ref[2 * p + 1] = jnp.where(low_s, one, vp)
        ka_ref[2 * p] = jnp.where(low_s, kp, ca)
        ka_ref[2 * p + 1] = jnp.where(low_s, ca, kp)
    lane_q = lax.broadcasted_iota(jnp.int32, (qb, LANES), 1)
    row = lax.broadcasted_iota(jnp.int32, (qb, qb), 0)
    col = lax.broadcasted_iota(jnp.int32, (qb, qb), 1)
    causal = col <= row
    owns = (lane_q < HEAD_DIM, lane_q >= HEAD_DIM)

    def scores(p, qi, hh):
        r0 = qi * qb
        head = 2 * (FOX_PAIRS_PER_STEP * jj + p) + hh
        rel = lane_q - (HEAD_DIM if hh == 0 else 0) - head
        pick = (rel == 0) | (rel == FOX_HEADS) | (rel == 2 * FOX_HEADS)
        qz = jnp.where(owns[hh], q_ref[0, r0:r0 + qb, p * LANES:(p + 1) * LANES],
                       jnp.where(pick, 1.0, 0.0).astype(BF16))
        tiles = []
        for t in range(qi + 1):
            s = _dot_nt(qz, ka_ref[2 * p + hh, t * qb:(t + 1) * qb, :])
            tiles.append(jnp.where(causal, s, NEG_INF) if t == qi else s)
            yield
        return p, qi, hh, tiles

    halves = []

    def finish(p, qi, hh, tiles):
        r0 = qi * qb
        m = jnp.max(functools.reduce(jnp.maximum, tiles), axis=-1, keepdims=True)
        acc = None
        for t, s in enumerate(tiles):
            part = _dot(jnp.exp2(s - m).astype(BF16), va_ref[2 * p + hh, t * qb:(t + 1) * qb, :])
            acc = part if acc is None else acc + part
            yield
        denom = pltpu.roll(jnp.where(owns[hh], 1.0, acc), HEAD_DIM, axis=1)
        halves.append(acc / denom)
        if hh == 1:
            o_ref[0, r0:r0 + qb, p * LANES:(p + 1) * LANES] = jnp.where(
                owns[0], halves[-2], halves[-1]).astype(BF16)

    _interleaved([(p, qi, hh) for p in range(FOX_PAIRS_PER_STEP)
                  for qi in _hill_order(seq // qb) for hh in range(2)], scores, finish)


def _fox_attention(fq, fk, fv, caug, batch, seq):
    width = FOX_PAIRS_PER_STEP * LANES
    blk = pl.BlockSpec((1, seq, width), lambda b, j: (b, 0, j))
    nscratch = 2 * FOX_PAIRS_PER_STEP
    return pl.pallas_call(
        functools.partial(_fox_kernel, seq=seq),
        grid=(batch, FOX_W // width),
        in_specs=[blk, blk, blk, pl.BlockSpec((1, seq, LANES), lambda b, j: (b, 0, 0))],
        out_specs=blk,
        out_shape=jax.ShapeDtypeStruct((batch, seq, FOX_W), BF16),
        scratch_shapes=[pltpu.VMEM((nscratch, seq, LANES), BF16),
                        pltpu.VMEM((nscratch, seq, LANES), BF16)],
        compiler_params=pltpu.CompilerParams(dimension_semantics=("parallel", "parallel"),
                                             vmem_limit_bytes=VMEM_LIMIT),
        name="fox_attention",
    )(fq, fk, fv, caug)


def _diff_kernel(lam_ref, g_ref, q_ref, k_ref, v_ref, gb_ref, o_ref, va_ref, *, seq, lambda_init):
    qb = ATT_Q_BLOCK
    lv = lam_ref[...]
    lam = (jnp.exp(jnp.sum(lv[0:1] * lv[1:2], axis=-1, keepdims=True))
           - jnp.exp(jnp.sum(lv[2:3] * lv[3:4], axis=-1, keepdims=True)) + lambda_init)
    for p in range(DIFF_HEADS_PER_STEP):
        va_ref[p, :, 0:LANES] = v_ref[0, :, p * LANES:(p + 1) * LANES]
        va_ref[p, :, LANES:2 * LANES] = jnp.ones((seq, LANES), BF16)
    lane_q = lax.broadcasted_iota(jnp.int32, (qb, LANES), 1)
    bias_prev = [jnp.concatenate([gb_ref[p, :, 0:qb]] * 2, axis=0)
                 for p in range(DIFF_HEADS_PER_STEP)]
    bias_diag = [jnp.concatenate([gb_ref[p, :, qb:2 * qb]] * 2, axis=0)
                 for p in range(DIFF_HEADS_PER_STEP)]
    gain = g_ref[...] * (1.0 - lambda_init)
    zero = jnp.zeros((), BF16)

    def scores(p, qi):
        r0 = qi * qb
        qp = q_ref[0, r0:r0 + qb, p * LANES:(p + 1) * LANES]
        qz = jnp.concatenate([jnp.where(lane_q < HEAD_DIM, qp, zero),
                              jnp.where(lane_q < HEAD_DIM, zero, qp)], axis=0)
        tiles = []
        for t in range(qi + 1):
            s = _dot_nt(qz, k_ref[0, t * qb:(t + 1) * qb, p * LANES:(p + 1) * LANES])
            tiles.append(s + bias_diag[p] if t == qi else s + bias_prev[p] if t == qi - 1 else s)
            yield
        return p, qi, tiles

    def finish(p, qi, tiles):
        r0 = qi * qb
        m = jnp.max(functools.reduce(jnp.maximum, tiles), axis=-1, keepdims=True)
        acc = None
        for t, s in enumerate(tiles):
            part = _dot(jnp.exp2(s - m).astype(BF16), va_ref[p, t * qb:(t + 1) * qb, :])
            acc = part if acc is None else acc + part
            yield
        a = acc[:, 0:LANES] / acc[:, LANES:2 * LANES]
        o = a[0:qb] - lam * a[qb:2 * qb]
        o_ref[0, r0:r0 + qb, p * LANES:(p + 1) * LANES] = _rms(o, gain).astype(BF16)

    _interleaved([(p, qi) for p in range(DIFF_HEADS_PER_STEP) for qi in _hill_order(seq // qb)],
                 scores, finish)


def _diff_attention(lam4, subln, dq, dk, dv, gbias, batch, seq, lambda_init):
    hps = DIFF_HEADS_PER_STEP
    blk = pl.BlockSpec((1, seq, hps * LANES), lambda b, h: (b, 0, h))
    return pl.pallas_call(
        functools.partial(_diff_kernel, seq=seq, lambda_init=lambda_init),
        grid=(batch, DIFF_HEADS // hps),
        in_specs=[
            pl.BlockSpec(lam4.shape, lambda b, h: (0, 0)),
            pl.BlockSpec(subln.shape, lambda b, h: (0, 0)),
            blk, blk, blk,
            pl.BlockSpec((hps, ATT_Q_BLOCK, 2 * ATT_Q_BLOCK), lambda b, h: (h, 0, 0)),
        ],
        out_specs=blk,
        out_shape=jax.ShapeDtypeStruct((batch, seq, DIFF_W), BF16),
        scratch_shapes=[pltpu.VMEM((hps, seq, 2 * LANES), BF16)],
        compiler_params=pltpu.CompilerParams(dimension_semantics=("parallel", "parallel"),
                                             vmem_limit_bytes=VMEM_LIMIT),
        name="diff_attention",
    )(lam4, subln, dq, dk, dv, gbias)


def _proj1_kernel(x_ref, gain_ref, w_ref, bd_ref, hg_ref, q_ref, kk_ref, vt_ref):
    h = _rms(x_ref[...], gain_ref[...]).astype(BF16)
    bd = bd_ref[...]
    qw = SWA_HEADS * HEAD_DIM
    kw = SWA_KV_HEADS * HEAD_DIM
    low = lax.broadcasted_iota(jnp.int32, (h.shape[0], LANES), 1) < HEAD_DIM

    def project(c0):
        return c0, _dot(h, w_ref[:, c0:c0 + 2 * MXU_TILE])

    def epilogue(c0, proj):
        if c0 < qw:
            for i in range(2):
                cols = slice(i * MXU_TILE, (i + 1) * MXU_TILE)
                q_ref[:, c0 + i * MXU_TILE:c0 + (i + 1) * MXU_TILE] = _head_norm(
                    proj[:, cols], bd, hg_ref[0:1, :]).astype(BF16)
            return
        kn = _head_norm(proj[:, 0:kw], bd, hg_ref[1:2, :])
        for p in range(kw // LANES):
            pair = kn[:, p * LANES:(p + 1) * LANES]
            swapped = pltpu.roll(pair, HEAD_DIM, axis=1)
            kk_ref[:, (2 * p) * LANES:(2 * p + 1) * LANES] = jnp.where(low, pair, swapped).astype(BF16)
            kk_ref[:, (2 * p + 1) * LANES:(2 * p + 2) * LANES] = jnp.where(low, swapped, pair).astype(BF16)
        vt_ref[0] = proj[:, kw:2 * kw].T.astype(BF16)

    _lagged([(qw,)] + [(c0,) for c0 in range(0, qw, 2 * MXU_TILE)], project, epilogue,
            depth=PROJ_LAG)


def _proj1(x2d, gain, w, bd, hg, seq):
    n = x2d.shape[0]
    tm = ROW_TILE
    tps = seq // tm
    qw = SWA_HEADS * HEAD_DIM
    kw = SWA_KV_HEADS * HEAD_DIM
    full = lambda shape: pl.BlockSpec(shape, lambda t: (0,) * len(shape))
    return pl.pallas_call(
        _proj1_kernel,
        grid=(n // tm,),
        in_specs=[
            pl.BlockSpec((tm, D_MODEL), lambda t: (t, 0)),
            full((1, D_MODEL)),
            full(w.shape),
            full((MXU_TILE, MXU_TILE)),
            full(hg.shape),
        ],
        out_specs=[pl.BlockSpec((tm, qw), lambda t: (t, 0)),
                   pl.BlockSpec((tm, 2 * kw), lambda t: (t, 0)),
                   pl.BlockSpec((1, kw, tm), lambda t: (t // tps, 0, t % tps))],
        out_shape=[jax.ShapeDtypeStruct((n, qw), BF16),
                   jax.ShapeDtypeStruct((n, 2 * kw), BF16),
                   jax.ShapeDtypeStruct((n // seq, kw, seq), BF16)],
        compiler_params=pltpu.CompilerParams(dimension_semantics=("parallel",),
                                             vmem_limit_bytes=VMEM_LIMIT),
        name="proj_odd",
    )(x2d, gain, w, bd, hg)


def _swa_kernel(sink_ref, q_ref, kk_ref, vt_ref, sb_ref, o_ref, va_ref, *, seq):
    jj = pl.program_id(1)
    w = WINDOW
    gw = SWA_GROUP * HEAD_DIM
    sub = lax.broadcasted_iota(jnp.int32, (LANES - HEAD_DIM, seq), 0)
    sinks = []
    for p in range(SWA_KV_PER_STEP):
        va_ref[p, 0:HEAD_DIM, :] = vt_ref[0, p * HEAD_DIM:(p + 1) * HEAD_DIM, :]
        va_ref[p, HEAD_DIM:LANES, :] = jnp.where(sub == 0, 1.0, 0.0).astype(BF16)
        head0 = (jj * SWA_KV_PER_STEP + p) * SWA_GROUP
        sinks.append(jnp.concatenate(
            [jnp.full((1, w), sink_ref[head0 + g] * LOG2E, F32) for g in range(SWA_GROUP)],
            axis=1))
    low = lax.broadcasted_iota(jnp.int32, (w, LANES), 1) < HEAD_DIM
    zero = jnp.zeros((), BF16)

    def scores(p, n):
        r0 = n * w
        qb = q_ref[0, r0:r0 + w, p * gw:(p + 1) * gw]
        qa, qc = qb[:, 0:LANES], qb[:, LANES:2 * LANES]
        q4 = jnp.concatenate([jnp.where(low, qa, zero), jnp.where(low, zero, qa),
                              jnp.where(low, qc, zero), jnp.where(low, zero, qc)], axis=0)
        lanes = slice(p * LANES, (p + 1) * LANES)
        if n == 0:
            return p, n, _dot_nt(kk_ref[0, 0:w, lanes], q4) + sb_ref[p, w:2 * w, :]
        return p, n, _dot_nt(kk_ref[0, r0 - w:r0 + w, lanes], q4) + sb_ref[p]

    def finish(p, n, st):
        r0 = n * w
        vals = va_ref[p, :, 0:w] if n == 0 else va_ref[p, :, r0 - w:r0 + w]
        m = jnp.maximum(jnp.max(st, axis=0, keepdims=True), sinks[p])
        acc = _dot(vals, jnp.exp2(st - m).astype(BF16))
        denom = acc[HEAD_DIM:HEAD_DIM + 1, :] + jnp.exp2(sinks[p] - m)
        ot = acc[0:HEAD_DIM, :] / denom
        o_ref[0, r0:r0 + w, p * gw:(p + 1) * gw] = jnp.concatenate(
            [jnp.concatenate([ot[:, 0:w], ot[:, w:2 * w]], axis=0).T,
             jnp.concatenate([ot[:, 2 * w:3 * w], ot[:, 3 * w:4 * w]], axis=0).T],
            axis=1).astype(BF16)

    _lagged([(p, n) for p in range(SWA_KV_PER_STEP) for n in range(seq // w)], scores, finish,
            depth=SWA_LAG)


def _swa_attention(sinks, q, kk, vt, swbias, batch, seq):
    kps = SWA_KV_PER_STEP
    gw = kps * SWA_GROUP * HEAD_DIM
    return pl.pallas_call(
        functools.partial(_swa_kernel, seq=seq),
        grid=(batch, SWA_KV_HEADS // kps),
        in_specs=[
            pl.BlockSpec(memory_space=pltpu.SMEM),
            pl.BlockSpec((1, seq, gw), lambda b, h: (b, 0, h)),
            pl.BlockSpec((1, seq, kps * LANES), lambda b, h: (b, 0, h)),
            pl.BlockSpec((1, kps * HEAD_DIM, seq), lambda b, h: (b, h, 0)),
            pl.BlockSpec((kps, 2 * WINDOW, SWA_GROUP * WINDOW), lambda b, h: (h, 0, 0)),
        ],
        out_specs=pl.BlockSpec((1, seq, gw), lambda b, h: (b, 0, h)),
        out_shape=jax.ShapeDtypeStruct((batch, seq, SWA_HEADS * HEAD_DIM), BF16),
        scratch_shapes=[pltpu.VMEM((kps, LANES, seq), BF16)],
        compiler_params=pltpu.CompilerParams(dimension_semantics=("parallel", "parallel"),
                                             vmem_limit_bytes=VMEM_LIMIT),
        name="swa_attention",
    )(sinks, q, kk, vt, swbias)


def _outmlp_kernel(*refs, n_mix):
    x_ref = refs[0]
    a_refs = refs[1:1 + n_mix]
    wo_ref, gain_ref, wg_ref, wu_ref, wd_ref, o_ref = refs[1 + n_mix:]
    mix = jnp.concatenate([a_ref[...] for a_ref in a_refs], axis=1)
    x1 = x_ref[...] + _dot(mix, wo_ref[...])
    h = _rms(x1, gain_ref[...]).astype(BF16)
    out = x1
    for c0 in range(0, FFN_HIDDEN, HIDDEN_CHUNK):
        c1 = min(c0 + HIDDEN_CHUNK, FFN_HIDDEN)
        g = _dot(h, wg_ref[:, c0:c1])
        u = _dot(h, wu_ref[:, c0:c1])
        y = (g * (1.0 / (1.0 + jnp.exp(-g))) * u).astype(BF16)
        out = out + _dot(y, wd_ref[c0:c1, :])
    o_ref[...] = out


def _outmlp(x2d, mixes, wo, gain, wg, wu, wd, layer):
    n = x2d.shape[0]
    tm = MLP_ROW_TILE
    full = lambda shape: pl.BlockSpec(shape, lambda t: (0,) * len(shape),
                                      pipeline_mode=pl.Buffered(1))
    of_layer = lambda w: pl.BlockSpec((None,) + w.shape[1:], lambda t: (layer, 0, 0),
                                      pipeline_mode=pl.Buffered(1))
    in_specs = [pl.BlockSpec((tm, D_MODEL), lambda t: (t, 0))]
    in_specs += [pl.BlockSpec((tm, a.shape[1]), lambda t: (t, 0)) for a in mixes]
    in_specs += [full(wo.shape), full(gain.shape), of_layer(wg), of_layer(wu), of_layer(wd)]
    return pl.pallas_call(
        functools.partial(_outmlp_kernel, n_mix=len(mixes)),
        grid=(n // tm,),
        in_specs=in_specs,
        out_specs=pl.BlockSpec((tm, D_MODEL), lambda t: (t, 0)),
        out_shape=jax.ShapeDtypeStruct((n, D_MODEL), F32),
        compiler_params=pltpu.CompilerParams(dimension_semantics=("parallel",),
                                             vmem_limit_bytes=VMEM_LIMIT),
        name="outproj_swiglu",
    )(x2d, *mixes, wo, gain, wg, wu, wd)


def _block_diag_mean():
    i = np.arange(MXU_TILE)
    same = (i[:, None] // HEAD_DIM) == (i[None, :] // HEAD_DIM)
    return jnp.asarray(same / HEAD_DIM, BF16)


def kernel(x, rel_bias_table, ev_attn_norm, ev_w_in, ev_b_forget, ev_fox_q_norm, ev_fox_k_norm,
           ev_diff_q_norm, ev_diff_k_norm, ev_lambda_q1, ev_lambda_k1, ev_lambda_q2, ev_lambda_k2,
           ev_diff_subln, ev_w_out, od_attn_norm, od_w_qkv, od_q_norm, od_k_norm, od_sinks, od_w_out,
           ffn_norm, w_gate, w_up, w_down):
    batch, seq, d = x.shape
    n = batch * seq
    x2d = x.reshape(n, d)
    bd = _block_diag_mean()
    wg, wu, wd = w_gate.astype(BF16), w_up.astype(BF16), w_down.astype(BF16)
    swbias, gbias = _bias_tiles(rel_bias_table.astype(F32))
    swbias = swbias.reshape(SWA_KV_HEADS, SWA_GROUP, 2 * WINDOW, WINDOW).transpose(0, 2, 1, 3)
    swbias = swbias.reshape(SWA_KV_HEADS, 2 * WINDOW, SWA_GROUP * WINDOW)

    lambda_init = 0.8 - 0.6 * math.exp(-0.3 * 0)
    w_in = ev_w_in[0]
    ff0 = 3 * FOX_W
    d0 = ff0 + FOX_HEADS
    w0 = jnp.concatenate([w_in[:, 0:ff0], w_in[:, d0:d0 + 3 * DIFF_W]], axis=1).astype(BF16)
    wf = jnp.concatenate(
        [w_in[:, ff0:d0].T, jnp.zeros((BF16_SUBLANES - FOX_HEADS, d), w_in.dtype)],
        axis=0).astype(BF16)
    tile8 = lambda g, scale=1.0: jnp.tile(g.astype(F32) * scale, FOX_W // HEAD_DIM)
    hg0 = jnp.stack([tile8(ev_fox_q_norm[0], Q_GAIN_SCALE), tile8(ev_fox_k_norm[0]),
                     tile8(ev_diff_q_norm[0], Q_GAIN_SCALE), tile8(ev_diff_k_norm[0])])
    bf = ev_b_forget[0].astype(F32).reshape(FOX_HEADS, 1)
    fq, fk, fv, dq, dk, dv, caug = _proj0(
        x2d, ev_attn_norm[0].astype(F32).reshape(1, d), w0, wf, bd, hg0, bf, seq)
    r3 = lambda a: a.reshape(batch, seq, a.shape[-1])
    fox = _fox_attention(r3(fq), r3(fk), r3(fv), r3(caug), batch, seq)
    lam4 = jnp.stack([ev_lambda_q1[0], ev_lambda_k1[0], ev_lambda_q2[0], ev_lambda_k2[0]]).astype(F32)
    dout = _diff_attention(lam4, ev_diff_subln[0].astype(F32).reshape(1, 2 * HEAD_DIM),
                           r3(dq), r3(dk), r3(dv), gbias, batch, seq, lambda_init)
    x2d = _outmlp(x2d, [fox.reshape(n, FOX_W), dout.reshape(n, DIFF_W)], ev_w_out[0].astype(BF16),
                  ffn_norm[0].astype(F32).reshape(1, d), wg, wu, wd, 0)

    tile4 = lambda g, scale=1.0: jnp.tile(g.astype(F32) * scale, MXU_TILE // HEAD_DIM)
    hg1 = jnp.stack([tile4(od_q_norm[0], Q_GAIN_SCALE), tile4(od_k_norm[0])])
    q, kk, vt = _proj1(x2d, od_attn_norm[0].astype(F32).reshape(1, d), od_w_qkv[0].astype(BF16),
                       bd, hg1, seq)
    swa = _swa_attention(od_sinks[0].astype(F32), r3(q), r3(kk), vt, swbias, batch, seq)
    x2d = _outmlp(x2d, [swa.reshape(n, SWA_HEADS * HEAD_DIM)], od_w_out[0].astype(BF16),
                  ffn_norm[1].astype(F32).reshape(1, d), wg, wu, wd, 1)
    return x2d.reshape(batch, seq, d)
```

```python
import functools
import math

import numpy as np
import jax
import jax.numpy as jnp
from jax import lax
from jax.experimental import pallas as pl
from jax.experimental.pallas import tpu as pltpu

F32 = jnp.float32
BF16 = jnp.bfloat16

D_MODEL = 1024
HEAD_DIM = 64
FOX_HEADS = 8
DIFF_HEADS = 4
SWA_HEADS = 16
SWA_KV_HEADS = 4
SWA_GROUP = SWA_HEADS // SWA_KV_HEADS
WINDOW = 128
NUM_BUCKETS = 32
MAX_DISTANCE = 128
FFN_HIDDEN = 2816
RMS_EPS = 1e-6
NEG_INF = -1e30
FOX_W = FOX_HEADS * HEAD_DIM
DIFF_W = DIFF_HEADS * 2 * HEAD_DIM
QK_SCALE = HEAD_DIM ** -0.5
LOG2E = math.log2(math.e)
Q_GAIN_SCALE = QK_SCALE * LOG2E

LANES = 128
BF16_SUBLANES = 16
MXU_TILE = 256
ROW_TILE = 1024
MLP_ROW_TILE = 1024
ATT_Q_BLOCK = 256
FOX_PAIRS_PER_STEP = 2
DIFF_HEADS_PER_STEP = 2
SWA_KV_PER_STEP = 2
PROJ_LAG = 2
SWA_LAG = 4
HIDDEN_CHUNK = 768
VMEM_LIMIT = 56 * 1024 * 1024

_NT = (((1,), (1,)), ((), ()))


def _dot(a, b):
    return jnp.dot(a, b, preferred_element_type=F32)


def _dot_nt(a, b):
    return lax.dot_general(a, b, _NT, preferred_element_type=F32)


def _lagged(items, first, second, depth=1):
    pending = []
    for item in items:
        pending.append(first(*item))
        if len(pending) > depth:
            second(*pending.pop(0))
    for p in pending:
        second(*p)


def _interleaved(items, first, second, depth=2):
    def drive(gens):
        done = {}
        while gens:
            for g in list(gens):
                try:
                    next(g)
                except StopIteration as stop:
                    done[id(g)] = stop.value
                    gens.remove(g)
        return done

    pending = []
    for item in items:
        g1 = first(*item)
        gens = [g1] if len(pending) < depth else [g1, second(*pending.pop(0))]
        pending.append(drive(gens)[id(g1)])
    for p in pending:
        drive([second(*p)])


def _hill_order(nblocks):
    return list(range(0, nblocks, 2)) + list(range(nblocks - 1 - nblocks % 2, 0, -2))


def _rms(xf, gain):
    ms = jnp.mean(xf * xf, axis=-1, keepdims=True)
    return xf * lax.rsqrt(ms + RMS_EPS) * gain


def _head_norm(acc, bd, gain):
    msq = _dot((acc * acc).astype(BF16), bd)
    return acc * lax.rsqrt(msq + RMS_EPS) * gain


def _bucket_ids(delta):
    n = np.maximum(delta, 0)
    max_exact = NUM_BUCKETS // 2
    nf = np.maximum(n, 1).astype(np.float64)
    large = max_exact + (np.log(nf / max_exact) / math.log(MAX_DISTANCE / max_exact)
                         * (NUM_BUCKETS - max_exact)).astype(np.int32)
    large = np.minimum(large, NUM_BUCKETS - 1)
    return np.where(n < max_exact, n, large).astype(np.int32)


def _bias_kernel(table_ref, bsw_ref, bg_ref, sw_ref, g_ref):
    h = pl.program_id(0)
    bsw = bsw_ref[...]
    sw = jnp.zeros(bsw.shape, F32)
    for b in range(NUM_BUCKETS):
        sw = jnp.where(bsw == b, table_ref[b, h] * LOG2E, sw)
    sw_ref[0] = jnp.where(bsw < 0, NEG_INF, sw)

    @pl.when(h < DIFF_HEADS)
    def _():
        bg = bg_ref[...]
        far = table_ref[NUM_BUCKETS - 1, h]
        g = jnp.zeros(bg.shape, F32)
        for b in range(NUM_BUCKETS):
            g = jnp.where(bg == b, (table_ref[b, h] - far) * LOG2E, g)
        g_ref[0] = jnp.where(bg < 0, NEG_INF, g)


def _bias_tiles(table):
    a = np.arange(WINDOW)[None, :]
    b = np.arange(2 * WINDOW)[:, None]
    d_sw = WINDOW + a - b
    bsw = np.where((d_sw >= 0) & (d_sw < WINDOW), _bucket_ids(d_sw), -1).astype(np.int32)
    a = np.arange(ATT_Q_BLOCK)[:, None]
    b = np.arange(2 * ATT_Q_BLOCK)[None, :]
    d_g = ATT_Q_BLOCK + a - b
    bg = np.where(d_g >= 0, _bucket_ids(d_g), -1).astype(np.int32)
    nh = table.shape[1]
    return pl.pallas_call(
        _bias_kernel,
        grid=(nh,),
        in_specs=[
            pl.BlockSpec(memory_space=pltpu.SMEM),
            pl.BlockSpec(bsw.shape, lambda h: (0, 0)),
            pl.BlockSpec(bg.shape, lambda h: (0, 0)),
        ],
        out_specs=[
            pl.BlockSpec((1,) + bsw.shape, lambda h: (h, 0, 0)),
            pl.BlockSpec((1,) + bg.shape, lambda h: (jnp.minimum(h, DIFF_HEADS - 1), 0, 0)),
        ],
        out_shape=[
            jax.ShapeDtypeStruct((nh,) + bsw.shape, F32),
            jax.ShapeDtypeStruct((DIFF_HEADS,) + bg.shape, F32),
        ],
        compiler_params=pltpu.CompilerParams(dimension_semantics=("arbitrary",)),
        name="rel_bias_tiles",
    )(table, jnp.asarray(bsw), jnp.asarray(bg))


def _proj0_kernel(x_ref, gain_ref, w_ref, wf_ref, bd_ref, hg_ref, bf_ref,
                  fq_ref, fk_ref, fv_ref, dq_ref, dk_ref, dv_ref, caug_ref,
                  carry_ref, *, tiles_per_seq):
    t = pl.program_id(0)

    @pl.when(t == 0)
    def _():
        carry_ref[...] = jnp.zeros_like(carry_ref)

    h = _rms(x_ref[...], gain_ref[...]).astype(BF16)
    bd = bd_ref[...]
    rows = h.shape[0]

    outs = (fq_ref, fk_ref, fv_ref, dq_ref, dk_ref, dv_ref)
    gains = (0, 1, None, 2, 3, None)

    def project(g):
        return g, _dot(h, w_ref[:, g * FOX_W:(g + 1) * FOX_W])

    def epilogue(g, proj):
        if gains[g] is None:
            outs[g][...] = proj.astype(BF16)
            return
        for i in range(FOX_W // MXU_TILE):
            cols = slice(i * MXU_TILE, (i + 1) * MXU_TILE)
            outs[g][:, cols] = _head_norm(
                proj[:, cols], bd, hg_ref[gains[g]:gains[g] + 1, cols]).astype(BF16)

    pending = [project(g) for g in range(PROJ_LAG)]

    z8 = _dot_nt(wf_ref[...], h)[0:FOX_HEADS, :] + bf_ref[...]
    y = -(jnp.maximum(-z8, 0.0) + jnp.log1p(jnp.exp(-jnp.abs(z8))))
    lane = lax.broadcasted_iota(jnp.int32, y.shape, 1)
    shift = 1
    while shift < rows:
        y = y + jnp.where(lane >= shift, pltpu.roll(y, shift, axis=1), 0.0)
        shift *= 2

    carry = jnp.where(t % tiles_per_seq == 0, 0.0, carry_ref[:, 0:1])
    c = y + carry
    carry_ref[...] = jnp.broadcast_to(c[:, rows - 1:rows], carry_ref.shape)
    cn = c * (-LOG2E)
    hi = cn.astype(BF16).astype(F32)
    r1 = cn - hi
    mid = r1.astype(BF16).astype(F32)
    lo = (r1 - mid).astype(BF16).astype(F32)
    half = jnp.concatenate([hi, mid, lo, jnp.zeros((HEAD_DIM - 3 * FOX_HEADS, rows), F32)], axis=0)
    caug_ref[...] = jnp.concatenate([half, half], axis=0).T.astype(BF16)

    for g in range(PROJ_LAG, len(outs)):
        pending.append(project(g))
        epilogue(*pending.pop(0))
    for p in pending:
        epilogue(*p)


def _proj0(x2d, gain, w, wf, bd, hg, bf, seq):
    n = x2d.shape[0]
    tm = ROW_TILE
    tps = seq // tm
    row_spec = pl.BlockSpec((tm, FOX_W), lambda t: (t, 0))
    full = lambda shape: pl.BlockSpec(shape, lambda t: (0,) * len(shape))
    outs = pl.pallas_call(
        functools.partial(_proj0_kernel, tiles_per_seq=tps),
        grid=(n // tm,),
        in_specs=[
            pl.BlockSpec((tm, D_MODEL), lambda t: (t, 0)),
            full((1, D_MODEL)),
            full(w.shape),
            full(wf.shape),
            full((MXU_TILE, MXU_TILE)),
            full(hg.shape),
            full(bf.shape),
        ],
        out_specs=[row_spec] * 6 + [pl.BlockSpec((tm, LANES), lambda t: (t, 0))],
        out_shape=[jax.ShapeDtypeStruct((n, FOX_W), BF16)] * 6
        + [jax.ShapeDtypeStruct((n, LANES), BF16)],
        scratch_shapes=[pltpu.VMEM((FOX_HEADS, LANES), F32)],
        compiler_params=pltpu.CompilerParams(dimension_semantics=("arbitrary",),
                                             vmem_limit_bytes=VMEM_LIMIT),
        name="proj_even",
    )(x2d, gain, w, wf, bd, hg, bf)
    return outs


def _fox_kernel(q_ref, k_ref, v_ref, caug_ref, o_ref, ka_ref, va_ref, *, seq):
    jj = pl.program_id(1)
    qb = ATT_Q_BLOCK
    lane_s = lax.broadcasted_iota(jnp.int32, (seq, LANES), 1)
    low_s = lane_s < HEAD_DIM
    ca = caug_ref[0]
    one = jnp.ones((), BF16)
    for p in range(FOX_PAIRS_PER_STEP):
        vp = v_ref[0, :, p * LANES:(p + 1) * LANES]
        kp = k_ref[0, :, p * LANES:(p + 1) * LANES]
        va_ref[2 * p] = jnp.where(low_s, vp, one)
        va_ref[2 * p + 1] = jnp.where(low_s, one, vp)
        ka_ref[2 * p] = jnp.where(low_s, kp, ca)
        ka_ref[2 * p + 1] = jnp.where(low_s, ca, kp)
    lane_q = lax.broadcasted_iota(jnp.int32, (qb, LANES), 1)
    row = lax.broadcasted_iota(jnp.int32, (qb, qb), 0)
    col = lax.broadcasted_iota(jnp.int32, (qb, qb), 1)
    causal = col <= row
    owns = (lane_q < HEAD_DIM, lane_q >= HEAD_DIM)

    def scores(p, qi, hh):
        r0 = qi * qb
        head = 2 * (FOX_PAIRS_PER_STEP * jj + p) + hh
        rel = lane_q - (HEAD_DIM if hh == 0 else 0) - head
        pick = (rel == 0) | (rel == FOX_HEADS) | (rel == 2 * FOX_HEADS)
        qz = jnp.where(owns[hh], q_ref[0, r0:r0 + qb, p * LANES:(p + 1) * LANES],
                       jnp.where(pick, 1.0, 0.0).astype(BF16))
        tiles = []
        for t in range(qi + 1):
            s = _dot_nt(qz, ka_ref[2 * p + hh, t * qb:(t + 1) * qb, :])
            tiles.append(jnp.where(causal, s, NEG_INF) if t == qi else s)
            yield
        return p, qi, hh, tiles

    halves = []

    def finish(p, qi, hh, tiles):
        r0 = qi * qb
        m = jnp.max(functools.reduce(jnp.maximum, tiles), axis=-1, keepdims=True)
        acc = None
        for t, s in enumerate(tiles):
            part = _dot(jnp.exp2(s - m).astype(BF16), va_ref[2 * p + hh, t * qb:(t + 1) * qb, :])
            acc = part if acc is None else acc + part
            yield
        denom = pltpu.roll(jnp.where(owns[hh], 1.0, acc), HEAD_DIM, axis=1)
        halves.append(acc / denom)
        if hh == 1:
            o_ref[0, r0:r0 + qb, p * LANES:(p + 1) * LANES] = jnp.where(
                owns[0], halves[-2], halves[-1]).astype(BF16)

    _interleaved([(p, qi, hh) for p in range(FOX_PAIRS_PER_STEP)
                  for qi in _hill_order(seq // qb) for hh in range(2)], scores, finish)


def _fox_attention(fq, fk, fv, caug, batch, seq):
    width = FOX_PAIRS_PER_STEP * LANES
    blk = pl.BlockSpec((1, seq, width), lambda b, j: (b, 0, j))
    nscratch = 2 * FOX_PAIRS_PER_STEP
    return pl.pallas_call(
        functools.partial(_fox_kernel, seq=seq),
        grid=(batch, FOX_W // width),
        in_specs=[blk, blk, blk, pl.BlockSpec((1, seq, LANES), lambda b, j: (b, 0, 0))],
        out_specs=blk,
        out_shape=jax.ShapeDtypeStruct((batch, seq, FOX_W), BF16),
        scratch_shapes=[pltpu.VMEM((nscratch, seq, LANES), BF16),
                        pltpu.VMEM((nscratch, seq, LANES), BF16)],
        compiler_params=pltpu.CompilerParams(dimension_semantics=("parallel", "parallel"),
                                             vmem_limit_bytes=VMEM_LIMIT),
        name="fox_attention",
    )(fq, fk, fv, caug)


def _diff_kernel(lam_ref, g_ref, q_ref, k_ref, v_ref, gb_ref, o_ref, va_ref, *, seq, lambda_init):
    qb = ATT_Q_BLOCK
    lv = lam_ref[...]
    lam = (jnp.exp(jnp.sum(lv[0:1] * lv[1:2], axis=-1, keepdims=True))
           - jnp.exp(jnp.sum(lv[2:3] * lv[3:4], axis=-1, keepdims=True)) + lambda_init)
    for p in range(DIFF_HEADS_PER_STEP):
        va_ref[p, :, 0:LANES] = v_ref[0, :, p * LANES:(p + 1) * LANES]
        va_ref[p, :, LANES:2 * LANES] = jnp.ones((seq, LANES), BF16)
    lane_q = lax.broadcasted_iota(jnp.int32, (qb, LANES), 1)
    bias_prev = [jnp.concatenate([gb_ref[p, :, 0:qb]] * 2, axis=0)
                 for p in range(DIFF_HEADS_PER_STEP)]
    bias_diag = [jnp.concatenate([gb_ref[p, :, qb:2 * qb]] * 2, axis=0)
                 for p in range(DIFF_HEADS_PER_STEP)]
    gain = g_ref[...] * (1.0 - lambda_init)
    zero = jnp.zeros((), BF16)

    def scores(p, qi):
        r0 = qi * qb
        qp = q_ref[0, r0:r0 + qb, p * LANES:(p + 1) * LANES]
        qz = jnp.concatenate([jnp.where(lane_q < HEAD_DIM, qp, zero),
                              jnp.where(lane_q < HEAD_DIM, zero, qp)], axis=0)
        tiles = []
        for t in range(qi + 1):
            s = _dot_nt(qz, k_ref[0, t * qb:(t + 1) * qb, p * LANES:(p + 1) * LANES])
            tiles.append(s + bias_diag[p] if t == qi else s + bias_prev[p] if t == qi - 1 else s)
            yield
        return p, qi, tiles

    def finish(p, qi, tiles):
        r0 = qi * qb
        m = jnp.max(functools.reduce(jnp.maximum, tiles), axis=-1, keepdims=True)
        acc = None
        for t, s in enumerate(tiles):
            part = _dot(jnp.exp2(s - m).astype(BF16), va_ref[p, t * qb:(t + 1) * qb, :])
            acc = part if acc is None else acc + part
            yield
        a = acc[:, 0:LANES] / acc[:, LANES:2 * LANES]
        o = a[0:qb] - lam * a[qb:2 * qb]
        o_ref[0, r0:r0 + qb, p * LANES:(p + 1) * LANES] = _rms(o, gain).astype(BF16)

    _interleaved([(p, qi) for p in range(DIFF_HEADS_PER_STEP) for qi in _hill_order(seq // qb)],
                 scores, finish)


def _diff_attention(lam4, subln, dq, dk, dv, gbias, batch, seq, lambda_init):
    hps = DIFF_HEADS_PER_STEP
    blk = pl.BlockSpec((1, seq, hps * LANES), lambda h, b: (b, 0, h))
    return pl.pallas_call(
        functools.partial(_diff_kernel, seq=seq, lambda_init=lambda_init),
        grid=(DIFF_HEADS // hps, batch),
        in_specs=[
            pl.BlockSpec(lam4.shape, lambda h, b: (0, 0)),
            pl.BlockSpec(subln.shape, lambda h, b: (0, 0)),
            blk, blk, blk,
            pl.BlockSpec((hps, ATT_Q_BLOCK, 2 * ATT_Q_BLOCK), lambda h, b: (h, 0, 0)),
        ],
        out_specs=blk,
        out_shape=jax.ShapeDtypeStruct((batch, seq, DIFF_W), BF16),
        scratch_shapes=[pltpu.VMEM((hps, seq, 2 * LANES), BF16)],
        compiler_params=pltpu.CompilerParams(dimension_semantics=("parallel", "parallel"),
                                             vmem_limit_bytes=VMEM_LIMIT),
        name="diff_attention",
    )(lam4, subln, dq, dk, dv, gbias)


def _proj1_kernel(x_ref, gain_ref, w_ref, bd_ref, hg_ref, q_ref, kk_ref, vt_ref):
    h = _rms(x_ref[...], gain_ref[...]).astype(BF16)
    bd = bd_ref[...]
    qw = SWA_HEADS * HEAD_DIM
    kw = SWA_KV_HEADS * HEAD_DIM
    low = lax.broadcasted_iota(jnp.int32, (h.shape[0], LANES), 1) < HEAD_DIM

    def project(c0):
        return c0, _dot(h, w_ref[:, c0:c0 + 2 * MXU_TILE])

    def epilogue(c0, proj):
        if c0 < qw:
            for i in range(2):
                cols = slice(i * MXU_TILE, (i + 1) * MXU_TILE)
                q_ref[:, c0 + i * MXU_TILE:c0 + (i + 1) * MXU_TILE] = _head_norm(
                    proj[:, cols], bd, hg_ref[0:1, :]).astype(BF16)
            return
        kn = _head_norm(proj[:, 0:kw], bd, hg_ref[1:2, :])
        for p in range(kw // LANES):
            pair = kn[:, p * LANES:(p + 1) * LANES]
            swapped = pltpu.roll(pair, HEAD_DIM, axis=1)
            kk_ref[:, (2 * p) * LANES:(2 * p + 1) * LANES] = jnp.where(low, pair, swapped).astype(BF16)
            kk_ref[:, (2 * p + 1) * LANES:(2 * p + 2) * LANES] = jnp.where(low, swapped, pair).astype(BF16)
        vt_ref[0] = proj[:, kw:2 * kw].T.astype(BF16)

    _lagged([(qw,)] + [(c0,) for c0 in range(0, qw, 2 * MXU_TILE)], project, epilogue,
            depth=PROJ_LAG)


def _proj1(x2d, gain, w, bd, hg, seq):
    n = x2d.shape[0]
    tm = ROW_TILE
    tps = seq // tm
    qw = SWA_HEADS * HEAD_DIM
    kw = SWA_KV_HEADS * HEAD_DIM
    full = lambda shape: pl.BlockSpec(shape, lambda t: (0,) * len(shape))
    return pl.pallas_call(
        _proj1_kernel,
        grid=(n // tm,),
        in_specs=[
            pl.BlockSpec((tm, D_MODEL), lambda t: (t, 0)),
            full((1, D_MODEL)),
            full(w.shape),
            full((MXU_TILE, MXU_TILE)),
            full(hg.shape),
        ],
        out_specs=[pl.BlockSpec((tm, qw), lambda t: (t, 0)),
                   pl.BlockSpec((tm, 2 * kw), lambda t: (t, 0)),
                   pl.BlockSpec((1, kw, tm), lambda t: (t // tps, 0, t % tps))],
        out_shape=[jax.ShapeDtypeStruct((n, qw), BF16),
                   jax.ShapeDtypeStruct((n, 2 * kw), BF16),
                   jax.ShapeDtypeStruct((n // seq, kw, seq), BF16)],
        compiler_params=pltpu.CompilerParams(dimension_semantics=("parallel",),
                                             vmem_limit_bytes=VMEM_LIMIT),
        name="proj_odd",
    )(x2d, gain, w, bd, hg)


def _swa_kernel(sink_ref, q_ref, kk_ref, vt_ref, sb_ref, o_ref, va_ref, *, seq):
    jj = pl.program_id(0)
    w = WINDOW
    gw = SWA_GROUP * HEAD_DIM
    sub = lax.broadcasted_iota(jnp.int32, (LANES - HEAD_DIM, seq), 0)
    sinks = []
    for p in range(SWA_KV_PER_STEP):
        va_ref[p, 0:HEAD_DIM, :] = vt_ref[0, p * HEAD_DIM:(p + 1) * HEAD_DIM, :]
        va_ref[p, HEAD_DIM:LANES, :] = jnp.where(sub == 0, 1.0, 0.0).astype(BF16)
        head0 = (jj * SWA_KV_PER_STEP + p) * SWA_GROUP
        sinks.append(jnp.concatenate(
            [jnp.full((1, w), sink_ref[head0 + g] * LOG2E, F32) for g in range(SWA_GROUP)],
            axis=1))
    low = lax.broadcasted_iota(jnp.int32, (w, LANES), 1) < HEAD_DIM
    zero = jnp.zeros((), BF16)

    def scores(p, n):
        r0 = n * w
        qb = q_ref[0, r0:r0 + w, p * gw:(p + 1) * gw]
        qa, qc = qb[:, 0:LANES], qb[:, LANES:2 * LANES]
        q4 = jnp.concatenate([jnp.where(low, qa, zero), jnp.where(low, zero, qa),
                              jnp.where(low, qc, zero), jnp.where(low, zero, qc)], axis=0)
        lanes = slice(p * LANES, (p + 1) * LANES)
        if n == 0:
            return p, n, _dot_nt(kk_ref[0, 0:w, lanes], q4) + sb_ref[p, w:2 * w, :]
        return p, n, _dot_nt(kk_ref[0, r0 - w:r0 + w, lanes], q4) + sb_ref[p]

    def finish(p, n, st):
        r0 = n * w
        vals = va_ref[p, :, 0:w] if n == 0 else va_ref[p, :, r0 - w:r0 + w]
        m = jnp.maximum(jnp.max(st, axis=0, keepdims=True), sinks[p])
        acc = _dot(vals, jnp.exp2(st - m).astype(BF16))
        denom = acc[HEAD_DIM:HEAD_DIM + 1, :] + jnp.exp2(sinks[p] - m)
        ot = acc[0:HEAD_DIM, :] / denom
        o_ref[0, r0:r0 + w, p * gw:(p + 1) * gw] = jnp.concatenate(
            [jnp.concatenate([ot[:, 0:w], ot[:, w:2 * w]], axis=0).T,
             jnp.concatenate([ot[:, 2 * w:3 * w], ot[:, 3 * w:4 * w]], axis=0).T],
            axis=1).astype(BF16)

    _lagged([(p, n) for p in range(SWA_KV_PER_STEP) for n in range(seq // w)], scores, finish,
            depth=SWA_LAG)


def _swa_attention(sinks, q, kk, vt, swbias, batch, seq):
    kps = SWA_KV_PER_STEP
    gw = kps * SWA_GROUP * HEAD_DIM
    return pl.pallas_call(
        functools.partial(_swa_kernel, seq=seq),
        grid=(SWA_KV_HEADS // kps, batch),
        in_specs=[
            pl.BlockSpec(memory_space=pltpu.SMEM),
            pl.BlockSpec((1, seq, gw), lambda h, b: (b, 0, h)),
            pl.BlockSpec((1, seq, kps * LANES), lambda h, b: (b, 0, h)),
            pl.BlockSpec((1, kps * HEAD_DIM, seq), lambda h, b: (b, h, 0)),
            pl.BlockSpec((kps, 2 * WINDOW, SWA_GROUP * WINDOW), lambda h, b: (h, 0, 0)),
        ],
        out_specs=pl.BlockSpec((1, seq, gw), lambda h, b: (b, 0, h)),
        out_shape=jax.ShapeDtypeStruct((batch, seq, SWA_HEADS * HEAD_DIM), BF16),
        scratch_shapes=[pltpu.VMEM((kps, LANES, seq), BF16)],
        compiler_params=pltpu.CompilerParams(dimension_semantics=("parallel", "parallel"),
                                             vmem_limit_bytes=VMEM_LIMIT),
        name="swa_attention",
    )(sinks, q, kk, vt, swbias)


def _outmlp_kernel(*refs, n_mix):
    x_ref = refs[0]
    a_refs = refs[1:1 + n_mix]
    wo_ref, gain_ref, wg_ref, wu_ref, wd_ref, o_ref = refs[1 + n_mix:]
    mix = jnp.concatenate([a_ref[...] for a_ref in a_refs], axis=1)
    x1 = x_ref[...] + _dot(mix, wo_ref[...])
    h = _rms(x1, gain_ref[...]).astype(BF16)
    out = x1
    for c0 in range(0, FFN_HIDDEN, HIDDEN_CHUNK):
        c1 = min(c0 + HIDDEN_CHUNK, FFN_HIDDEN)
        g = _dot(h, wg_ref[:, c0:c1])
        u = _dot(h, wu_ref[:, c0:c1])
        y = (g * (1.0 / (1.0 + jnp.exp(-g))) * u).astype(BF16)
        out = out + _dot(y, wd_ref[c0:c1, :])
    o_ref[...] = out


def _outmlp(x2d, mixes, wo, gain, wg, wu, wd, layer):
    n = x2d.shape[0]
    tm = MLP_ROW_TILE
    full = lambda shape: pl.BlockSpec(shape, lambda t: (0,) * len(shape),
                                      pipeline_mode=pl.Buffered(1))
    of_layer = lambda w: pl.BlockSpec((None,) + w.shape[1:], lambda t: (layer, 0, 0),
                                      pipeline_mode=pl.Buffered(1))
    in_specs = [pl.BlockSpec((tm, D_MODEL), lambda t: (t, 0))]
    in_specs += [pl.BlockSpec((tm, a.shape[1]), lambda t: (t, 0)) for a in mixes]
    in_specs += [full(wo.shape), full(gain.shape), of_layer(wg), of_layer(wu), of_layer(wd)]
    return pl.pallas_call(
        functools.partial(_outmlp_kernel, n_mix=len(mixes)),
        grid=(n // tm,),
        in_specs=in_specs,
        out_specs=pl.BlockSpec((tm, D_MODEL), lambda t: (t, 0)),
        out_shape=jax.ShapeDtypeStruct((n, D_MODEL), F32),
        compiler_params=pltpu.CompilerParams(dimension_semantics=("parallel",),
                                             vmem_limit_bytes=VMEM_LIMIT),
        name="outproj_swiglu",
    )(x2d, *mixes, wo, gain, wg, wu, wd)


def _block_diag_mean():
    i = np.arange(MXU_TILE)
    same = (i[:, None] // HEAD_DIM) == (i[None, :] // HEAD_DIM)
    return jnp.asarray(same / HEAD_DIM, BF16)


def kernel(x, rel_bias_table, ev_attn_norm, ev_w_in, ev_b_forget, ev_fox_q_norm, ev_fox_k_norm,
           ev_diff_q_norm, ev_diff_k_norm, ev_lambda_q1, ev_lambda_k1, ev_lambda_q2, ev_lambda_k2,
           ev_diff_subln, ev_w_out, od_attn_norm, od_w_qkv, od_q_norm, od_k_norm, od_sinks, od_w_out,
           ffn_norm, w_gate, w_up, w_down):
    batch, seq, d = x.shape
    n = batch * seq
    x2d = x.reshape(n, d)
    bd = _block_diag_mean()
    wg, wu, wd = w_gate.astype(BF16), w_up.astype(BF16), w_down.astype(BF16)
    swbias, gbias = _bias_tiles(rel_bias_table.astype(F32))
    swbias = swbias.reshape(SWA_KV_HEADS, SWA_GROUP, 2 * WINDOW, WINDOW).transpose(0, 2, 1, 3)
    swbias = swbias.reshape(SWA_KV_HEADS, 2 * WINDOW, SWA_GROUP * WINDOW)

    lambda_init = 0.8 - 0.6 * math.exp(-0.3 * 0)
    w_in = ev_w_in[0]
    ff0 = 3 * FOX_W
    d0 = ff0 + FOX_HEADS
    w0 = jnp.concatenate([w_in[:, 0:ff0], w_in[:, d0:d0 + 3 * DIFF_W]], axis=1).astype(BF16)
    wf = jnp.concatenate(
        [w_in[:, ff0:d0].T, jnp.zeros((BF16_SUBLANES - FOX_HEADS, d), w_in.dtype)],
        axis=0).astype(BF16)
    tile8 = lambda g, scale=1.0: jnp.tile(g.astype(F32) * scale, FOX_W // HEAD_DIM)
    hg0 = jnp.stack([tile8(ev_fox_q_norm[0], Q_GAIN_SCALE), tile8(ev_fox_k_norm[0]),
                     tile8(ev_diff_q_norm[0], Q_GAIN_SCALE), tile8(ev_diff_k_norm[0])])
    bf = ev_b_forget[0].astype(F32).reshape(FOX_HEADS, 1)
    fq, fk, fv, dq, dk, dv, caug = _proj0(
        x2d, ev_attn_norm[0].astype(F32).reshape(1, d), w0, wf, bd, hg0, bf, seq)
    r3 = lambda a: a.reshape(batch, seq, a.shape[-1])
    fox = _fox_attention(r3(fq), r3(fk), r3(fv), r3(caug), batch, seq)
    lam4 = jnp.stack([ev_lambda_q1[0], ev_lambda_k1[0], ev_lambda_q2[0], ev_lambda_k2[0]]).astype(F32)
    dout = _diff_attention(lam4, ev_diff_subln[0].astype(F32).reshape(1, 2 * HEAD_DIM),
                           r3(dq), r3(dk), r3(dv), gbias, batch, seq, lambda_init)
    x2d = _outmlp(x2d, [fox.reshape(n, FOX_W), dout.reshape(n, DIFF_W)], ev_w_out[0].astype(BF16),
                  ffn_norm[0].astype(F32).reshape(1, d), wg, wu, wd, 0)

    tile4 = lambda g, scale=1.0: jnp.tile(g.astype(F32) * scale, MXU_TILE // HEAD_DIM)
    hg1 = jnp.stack([tile4(od_q_norm[0], Q_GAIN_SCALE), tile4(od_k_norm[0])])
    q, kk, vt = _proj1(x2d, od_attn_norm[0].astype(F32).reshape(1, d), od_w_qkv[0].astype(BF16),
                       bd, hg1, seq)
    swa = _swa_attention(od_sinks[0].astype(F32), r3(q), r3(kk), vt, swbias, batch, seq)
    x2d = _outmlp(x2d, [swa.reshape(n, SWA_HEADS * HEAD_DIM)], od_w_out[0].astype(BF16),
                  ffn_norm[1].astype(F32).reshape(1, d), wg, wu, wd, 1)
    return x2d.reshape(batch, seq, d)
```

```python
import functools
import math

import numpy as np
import jax
import jax.numpy as jnp
from jax import lax
from jax.experimental import pallas as pl
from jax.experimental.pallas import tpu as pltpu

F32 = jnp.float32
BF16 = jnp.bfloat16

D_MODEL = 1024
HEAD_DIM = 64
FOX_HEADS = 8
DIFF_HEADS = 4
SWA_HEADS = 16
SWA_KV_HEADS = 4
SWA_GROUP = SWA_HEADS // SWA_KV_HEADS
WINDOW = 128
NUM_BUCKETS = 32
MAX_DISTANCE = 128
FFN_HIDDEN = 2816
RMS_EPS = 1e-6
NEG_INF = -1e30
FOX_W = FOX_HEADS * HEAD_DIM
DIFF_W = DIFF_HEADS * 2 * HEAD_DIM
QK_SCALE = HEAD_DIM ** -0.5
LOG2E = math.log2(math.e)
Q_GAIN_SCALE = QK_SCALE * LOG2E

LANES = 128
BF16_SUBLANES = 16
MXU_TILE = 256
ROW_TILE = 1024
MLP_ROW_TILE = 1024
ATT_Q_BLOCK = 256
FOX_PAIRS_PER_STEP = 2
DIFF_HEADS_PER_STEP = 2
SWA_KV_PER_STEP = 2
FOX_LAG = 3
DIFF_LAG = 2
PROJ_LAG = 2
SWA_LAG = 4
HIDDEN_CHUNK = 768
VMEM_LIMIT = 56 * 1024 * 1024

_NT = (((1,), (1,)), ((), ()))


def _dot(a, b):
    return jnp.dot(a, b, preferred_element_type=F32)


def _dot_nt(a, b):
    return lax.dot_general(a, b, _NT, preferred_element_type=F32)


def _lagged(items, first, second, depth=1):
    pending = []
    for item in items:
        pending.append(first(*item))
        if len(pending) > depth:
            second(*pending.pop(0))
    for p in pending:
        second(*p)


def _interleaved(items, first, second, depth):
    def drive(gens):
        done = {}
        while gens:
            for g in list(gens):
                try:
                    next(g)
                except StopIteration as stop:
                    done[id(g)] = stop.value
                    gens.remove(g)
        return done

    pending = []
    for item in items:
        g1 = first(*item)
        gens = [g1] if len(pending) < depth else [g1, second(*pending.pop(0))]
        pending.append(drive(gens)[id(g1)])
    for p in pending:
        drive([second(*p)])


def _hill_order(nblocks):
    return list(range(0, nblocks, 2)) + list(range(nblocks - 1 - nblocks % 2, 0, -2))


def _rms(xf, gain):
    ms = jnp.mean(xf * xf, axis=-1, keepdims=True)
    return xf * lax.rsqrt(ms + RMS_EPS) * gain


def _head_norm(acc, bd, gain):
    msq = _dot((acc * acc).astype(BF16), bd)
    return acc * lax.rsqrt(msq + RMS_EPS) * gain


def _bucket_ids(delta):
    n = np.maximum(delta, 0)
    max_exact = NUM_BUCKETS // 2
    nf = np.maximum(n, 1).astype(np.float64)
    large = max_exact + (np.log(nf / max_exact) / math.log(MAX_DISTANCE / max_exact)
                         * (NUM_BUCKETS - max_exact)).astype(np.int32)
    large = np.minimum(large, NUM_BUCKETS - 1)
    return np.where(n < max_exact, n, large).astype(np.int32)


def _bias_kernel(table_ref, bsw_ref, bg_ref, sw_ref, g_ref):
    h = pl.program_id(0)
    bsw = bsw_ref[...]
    sw = jnp.zeros(bsw.shape, F32)
    for b in range(NUM_BUCKETS):
        sw = jnp.where(bsw == b, table_ref[b, h] * LOG2E, sw)
    sw_ref[0] = jnp.where(bsw < 0, NEG_INF, sw)

    @pl.when(h < DIFF_HEADS)
    def _():
        bg = bg_ref[...]
        far = table_ref[NUM_BUCKETS - 1, h]
        g = jnp.zeros(bg.shape, F32)
        for b in range(NUM_BUCKETS):
            g = jnp.where(bg == b, (table_ref[b, h] - far) * LOG2E, g)
        g_ref[0] = jnp.where(bg < 0, NEG_INF, g)


def _bias_tiles(table):
    a = np.arange(WINDOW)[None, :]
    b = np.arange(2 * WINDOW)[:, None]
    d_sw = WINDOW + a - b
    bsw = np.where((d_sw >= 0) & (d_sw < WINDOW), _bucket_ids(d_sw), -1).astype(np.int32)
    a = np.arange(ATT_Q_BLOCK)[None, :]
    b = np.arange(2 * ATT_Q_BLOCK)[:, None]
    d_g = ATT_Q_BLOCK + a - b
    bg = np.where(d_g >= 0, _bucket_ids(d_g), -1).astype(np.int32)
    nh = table.shape[1]
    return pl.pallas_call(
        _bias_kernel,
        grid=(nh,),
        in_specs=[
            pl.BlockSpec(memory_space=pltpu.SMEM),
            pl.BlockSpec(bsw.shape, lambda h: (0, 0)),
            pl.BlockSpec(bg.shape, lambda h: (0, 0)),
        ],
        out_specs=[
            pl.BlockSpec((1,) + bsw.shape, lambda h: (h, 0, 0)),
            pl.BlockSpec((1,) + bg.shape, lambda h: (jnp.minimum(h, DIFF_HEADS - 1), 0, 0)),
        ],
        out_shape=[
            jax.ShapeDtypeStruct((nh,) + bsw.shape, F32),
            jax.ShapeDtypeStruct((DIFF_HEADS,) + bg.shape, F32),
        ],
        compiler_params=pltpu.CompilerParams(dimension_semantics=("arbitrary",)),
        name="rel_bias_tiles",
    )(table, jnp.asarray(bsw), jnp.asarray(bg))


def _proj0_kernel(x_ref, gain_ref, w_ref, wf_ref, bd_ref, hg_ref, bf_ref,
                  fq_ref, fk_ref, fv_ref, dq_ref, dk_ref, dv_ref, caug_ref,
                  carry_ref, *, tiles_per_seq):
    t = pl.program_id(0)

    @pl.when(t == 0)
    def _():
        carry_ref[...] = jnp.zeros_like(carry_ref)

    h = _rms(x_ref[...], gain_ref[...]).astype(BF16)
    bd = bd_ref[...]
    rows = h.shape[0]

    outs = (fq_ref, fk_ref, fv_ref, dq_ref, dk_ref, dv_ref)
    gains = (0, 1, None, 2, 3, None)

    def project(g):
        return g, _dot(h, w_ref[:, g * FOX_W:(g + 1) * FOX_W])

    def epilogue(g, proj):
        if gains[g] is None:
            outs[g][0] = proj.T.astype(BF16)
            return
        for i in range(FOX_W // MXU_TILE):
            cols = slice(i * MXU_TILE, (i + 1) * MXU_TILE)
            outs[g][:, cols] = _head_norm(
                proj[:, cols], bd, hg_ref[gains[g]:gains[g] + 1, cols]).astype(BF16)

    pending = [project(g) for g in range(PROJ_LAG)]

    z8 = _dot_nt(wf_ref[...], h)[0:FOX_HEADS, :] + bf_ref[...]
    y = -(jnp.maximum(-z8, 0.0) + jnp.log1p(jnp.exp(-jnp.abs(z8))))
    lane = lax.broadcasted_iota(jnp.int32, y.shape, 1)
    shift = 1
    while shift < rows:
        y = y + jnp.where(lane >= shift, pltpu.roll(y, shift, axis=1), 0.0)
        shift *= 2

    carry = jnp.where(t % tiles_per_seq == 0, 0.0, carry_ref[:, 0:1])
    c = y + carry
    carry_ref[...] = jnp.broadcast_to(c[:, rows - 1:rows], carry_ref.shape)
    cn = c * (-LOG2E)
    hi = cn.astype(BF16).astype(F32)
    r1 = cn - hi
    mid = r1.astype(BF16).astype(F32)
    lo = (r1 - mid).astype(BF16).astype(F32)
    half = jnp.concatenate([hi, mid, lo, jnp.zeros((HEAD_DIM - 3 * FOX_HEADS, rows), F32)], axis=0)
    caug_ref[...] = jnp.concatenate([half, half], axis=0).T.astype(BF16)

    for g in range(PROJ_LAG, len(outs)):
        pending.append(project(g))
        epilogue(*pending.pop(0))
    for p in pending:
        epilogue(*p)


def _proj0(x2d, gain, w, wf, bd, hg, bf, seq):
    n = x2d.shape[0]
    tm = ROW_TILE
    tps = seq // tm
    row_spec = pl.BlockSpec((tm, FOX_W), lambda t: (t, 0))
    col_spec = pl.BlockSpec((1, FOX_W, tm), lambda t: (t // tps, 0, t % tps))
    rows_shape = jax.ShapeDtypeStruct((n, FOX_W), BF16)
    cols_shape = jax.ShapeDtypeStruct((n // seq, FOX_W, seq), BF16)
    full = lambda shape: pl.BlockSpec(shape, lambda t: (0,) * len(shape))
    outs = pl.pallas_call(
        functools.partial(_proj0_kernel, tiles_per_seq=tps),
        grid=(n // tm,),
        in_specs=[
            pl.BlockSpec((tm, D_MODEL), lambda t: (t, 0)),
            full((1, D_MODEL)),
            full(w.shape),
            full(wf.shape),
            full((MXU_TILE, MXU_TILE)),
            full(hg.shape),
            full(bf.shape),
        ],
        out_specs=[row_spec, row_spec, col_spec, row_spec, row_spec, col_spec,
                   pl.BlockSpec((tm, LANES), lambda t: (t, 0))],
        out_shape=[rows_shape, rows_shape, cols_shape, rows_shape, rows_shape, cols_shape,
                   jax.ShapeDtypeStruct((n, LANES), BF16)],
        scratch_shapes=[pltpu.VMEM((FOX_HEADS, LANES), F32)],
        compiler_params=pltpu.CompilerParams(dimension_semantics=("arbitrary",),
                                             vmem_limit_bytes=VMEM_LIMIT),
        name="proj_even",
    )(x2d, gain, w, wf, bd, hg, bf)
    return outs


def _fox_kernel(q_ref, k_ref, vt_ref, caug_ref, o_ref, ka_ref, va_ref, *, seq):
    jj = pl.program_id(1)
    qb = ATT_Q_BLOCK
    low_s = lax.broadcasted_iota(jnp.int32, (seq, LANES), 1) < HEAD_DIM
    ca = caug_ref[0]
    sub = lax.broadcasted_iota(jnp.int32, (LANES - HEAD_DIM, seq), 0)
    ones_row = jnp.where(sub == 0, 1.0, 0.0).astype(BF16)
    for p in range(FOX_PAIRS_PER_STEP):
        kp = k_ref[0, :, p * LANES:(p + 1) * LANES]
        ka_ref[2 * p] = jnp.where(low_s, kp, ca)
        ka_ref[2 * p + 1] = jnp.where(low_s, ca, kp)
        for hh in range(2):
            r = p * LANES + hh * HEAD_DIM
            va_ref[2 * p + hh, 0:HEAD_DIM, :] = vt_ref[0, r:r + HEAD_DIM, :]
            va_ref[2 * p + hh, HEAD_DIM:LANES, :] = ones_row
    lane_q = lax.broadcasted_iota(jnp.int32, (qb, LANES), 1)
    key_i = lax.broadcasted_iota(jnp.int32, (qb, qb), 0)
    qry_i = lax.broadcasted_iota(jnp.int32, (qb, qb), 1)
    causal = key_i <= qry_i
    owns = (lane_q < HEAD_DIM, lane_q >= HEAD_DIM)

    def scores(p, qi, hh):
        r0 = qi * qb
        head = 2 * (FOX_PAIRS_PER_STEP * jj + p) + hh
        rel = lane_q - (HEAD_DIM if hh == 0 else 0) - head
        pick = (rel == 0) | (rel == FOX_HEADS) | (rel == 2 * FOX_HEADS)
        qz = jnp.where(owns[hh], q_ref[0, r0:r0 + qb, p * LANES:(p + 1) * LANES],
                       jnp.where(pick, 1.0, 0.0).astype(BF16))
        s_all = _dot_nt(ka_ref[2 * p + hh, 0:(qi + 1) * qb, :], qz)
        yield
        tiles = [s_all[t * qb:(t + 1) * qb] for t in range(qi)]
        tiles.append(jnp.where(causal, s_all[qi * qb:(qi + 1) * qb], NEG_INF))
        return p, qi, hh, tiles

    halves = []

    def finish(p, qi, hh, tiles):
        r0 = qi * qb
        m = jnp.max(functools.reduce(jnp.maximum, tiles), axis=0, keepdims=True)
        acc = None
        for t, s in enumerate(tiles):
            part = _dot(va_ref[2 * p + hh, :, t * qb:(t + 1) * qb], jnp.exp2(s - m).astype(BF16))
            acc = part if acc is None else acc + part
            yield
        halves.append(acc[0:HEAD_DIM] / acc[HEAD_DIM:HEAD_DIM + 1])
        if hh == 1:
            o_ref[0, r0:r0 + qb, p * LANES:(p + 1) * LANES] = jnp.concatenate(
                [halves[-2], halves[-1]], axis=0).T.astype(BF16)

    _interleaved([(p, qi, hh) for p in range(FOX_PAIRS_PER_STEP)
                  for qi in _hill_order(seq // qb) for hh in range(2)], scores, finish,
                 depth=FOX_LAG)


def _fox_attention(fq, fk, fvt, caug, batch, seq):
    width = FOX_PAIRS_PER_STEP * LANES
    blk = pl.BlockSpec((1, seq, width), lambda b, j: (b, 0, j))
    nheads = 2 * FOX_PAIRS_PER_STEP
    return pl.pallas_call(
        functools.partial(_fox_kernel, seq=seq),
        grid=(batch, FOX_W // width),
        in_specs=[blk, blk, pl.BlockSpec((1, width, seq), lambda b, j: (b, j, 0)),
                  pl.BlockSpec((1, seq, LANES), lambda b, j: (b, 0, 0))],
        out_specs=blk,
        out_shape=jax.ShapeDtypeStruct((batch, seq, FOX_W), BF16),
        scratch_shapes=[pltpu.VMEM((nheads, seq, LANES), BF16),
                        pltpu.VMEM((nheads, LANES, seq), BF16)],
        compiler_params=pltpu.CompilerParams(dimension_semantics=("parallel", "parallel"),
                                             vmem_limit_bytes=VMEM_LIMIT),
        name="fox_attention",
    )(fq, fk, fvt, caug)


def _diff_kernel(lam_ref, g_ref, q_ref, k_ref, vt_ref, gb_ref, o_ref, va_ref, *, seq, lambda_init):
    qb = ATT_Q_BLOCK
    vdim = 2 * HEAD_DIM
    lv = lam_ref[...]
    lam = (jnp.exp(jnp.sum(lv[0:1] * lv[1:2], axis=-1, keepdims=True))
           - jnp.exp(jnp.sum(lv[2:3] * lv[3:4], axis=-1, keepdims=True)) + lambda_init)
    sub = lax.broadcasted_iota(jnp.int32, (BF16_SUBLANES, seq), 0)
    ones_row = jnp.where(sub == 0, 1.0, 0.0).astype(BF16)
    for p in range(DIFF_HEADS_PER_STEP):
        va_ref[p, 0:vdim, :] = vt_ref[0, p * vdim:(p + 1) * vdim, :]
        va_ref[p, vdim:vdim + BF16_SUBLANES, :] = ones_row
    lane_q = lax.broadcasted_iota(jnp.int32, (qb, LANES), 1)
    bias_prev = [jnp.concatenate([gb_ref[p, 0:qb, :]] * 2, axis=1)
                 for p in range(DIFF_HEADS_PER_STEP)]
    bias_diag = [jnp.concatenate([gb_ref[p, qb:2 * qb, :]] * 2, axis=1)
                 for p in range(DIFF_HEADS_PER_STEP)]
    gain = g_ref[...] * (1.0 - lambda_init)
    zero = jnp.zeros((), BF16)

    def scores(p, qi):
        r0 = qi * qb
        qp = q_ref[0, r0:r0 + qb, p * LANES:(p + 1) * LANES]
        qz = jnp.concatenate([jnp.where(lane_q < HEAD_DIM, qp, zero),
                              jnp.where(lane_q < HEAD_DIM, zero, qp)], axis=0)
        s_all = _dot_nt(k_ref[0, 0:(qi + 1) * qb, p * LANES:(p + 1) * LANES], qz)
        yield
        tiles = [s_all[t * qb:(t + 1) * qb] for t in range(qi - 1)]
        if qi >= 1:
            tiles.append(s_all[(qi - 1) * qb:qi * qb] + bias_prev[p])
        tiles.append(s_all[qi * qb:(qi + 1) * qb] + bias_diag[p])
        return p, qi, tiles

    def finish(p, qi, tiles):
        r0 = qi * qb
        m = jnp.max(functools.reduce(jnp.maximum, tiles), axis=0, keepdims=True)
        acc = None
        for t, s in enumerate(tiles):
            part = _dot(va_ref[p, :, t * qb:(t + 1) * qb], jnp.exp2(s - m).astype(BF16))
            acc = part if acc is None else acc + part
            yield
        a = acc[0:vdim] / acc[vdim:vdim + 1]
        o = a[:, 0:qb] - lam * a[:, qb:2 * qb]
        y = o * lax.rsqrt(jnp.mean(o * o, axis=0, keepdims=True) + RMS_EPS) * gain
        o_ref[0, r0:r0 + qb, p * LANES:(p + 1) * LANES] = y.T.astype(BF16)

    _interleaved([(p, qi) for p in range(DIFF_HEADS_PER_STEP) for qi in _hill_order(seq // qb)],
                 scores, finish, depth=DIFF_LAG)


def _diff_attention(lam4, subln, dq, dk, dvt, gbias, batch, seq, lambda_init):
    hps = DIFF_HEADS_PER_STEP
    blk = pl.BlockSpec((1, seq, hps * LANES), lambda h, b: (b, 0, h))
    return pl.pallas_call(
        functools.partial(_diff_kernel, seq=seq, lambda_init=lambda_init),
        grid=(DIFF_HEADS // hps, batch),
        in_specs=[
            pl.BlockSpec(lam4.shape, lambda h, b: (0, 0)),
            pl.BlockSpec(subln.shape, lambda h, b: (0, 0)),
            blk, blk,
            pl.BlockSpec((1, hps * LANES, seq), lambda h, b: (b, h, 0)),
            pl.BlockSpec((hps, 2 * ATT_Q_BLOCK, ATT_Q_BLOCK), lambda h, b: (h, 0, 0)),
        ],
        out_specs=blk,
        out_shape=jax.ShapeDtypeStruct((batch, seq, DIFF_W), BF16),
        scratch_shapes=[pltpu.VMEM((hps, 2 * HEAD_DIM + BF16_SUBLANES, seq), BF16)],
        compiler_params=pltpu.CompilerParams(dimension_semantics=("parallel", "parallel"),
                                             vmem_limit_bytes=VMEM_LIMIT),
        name="diff_attention",
    )(lam4, subln, dq, dk, dvt, gbias)


def _proj1_kernel(x_ref, gain_ref, w_ref, bd_ref, hg_ref, q_ref, kk_ref, vt_ref):
    h = _rms(x_ref[...], gain_ref[...]).astype(BF16)
    bd = bd_ref[...]
    qw = SWA_HEADS * HEAD_DIM
    kw = SWA_KV_HEADS * HEAD_DIM
    low = lax.broadcasted_iota(jnp.int32, (h.shape[0], LANES), 1) < HEAD_DIM

    def project(c0):
        return c0, _dot(h, w_ref[:, c0:c0 + 2 * MXU_TILE])

    def epilogue(c0, proj):
        if c0 < qw:
            for i in range(2):
                cols = slice(i * MXU_TILE, (i + 1) * MXU_TILE)
                q_ref[:, c0 + i * MXU_TILE:c0 + (i + 1) * MXU_TILE] = _head_norm(
                    proj[:, cols], bd, hg_ref[0:1, :]).astype(BF16)
            return
        kn = _head_norm(proj[:, 0:kw], bd, hg_ref[1:2, :])
        for p in range(kw // LANES):
            pair = kn[:, p * LANES:(p + 1) * LANES]
            swapped = pltpu.roll(pair, HEAD_DIM, axis=1)
            kk_ref[:, (2 * p) * LANES:(2 * p + 1) * LANES] = jnp.where(low, pair, swapped).astype(BF16)
            kk_ref[:, (2 * p + 1) * LANES:(2 * p + 2) * LANES] = jnp.where(low, swapped, pair).astype(BF16)
        vt_ref[0] = proj[:, kw:2 * kw].T.astype(BF16)

    _lagged([(qw,)] + [(c0,) for c0 in range(0, qw, 2 * MXU_TILE)], project, epilogue,
            depth=PROJ_LAG)


def _proj1(x2d, gain, w, bd, hg, seq):
    n = x2d.shape[0]
    tm = ROW_TILE
    tps = seq // tm
    qw = SWA_HEADS * HEAD_DIM
    kw = SWA_KV_HEADS * HEAD_DIM
    full = lambda shape: pl.BlockSpec(shape, lambda t: (0,) * len(shape))
    return pl.pallas_call(
        _proj1_kernel,
        grid=(n // tm,),
        in_specs=[
            pl.BlockSpec((tm, D_MODEL), lambda t: (t, 0)),
            full((1, D_MODEL)),
            full(w.shape),
            full((MXU_TILE, MXU_TILE)),
            full(hg.shape),
        ],
        out_specs=[pl.BlockSpec((tm, qw), lambda t: (t, 0)),
                   pl.BlockSpec((tm, 2 * kw), lambda t: (t, 0)),
                   pl.BlockSpec((1, kw, tm), lambda t: (t // tps, 0, t % tps))],
        out_shape=[jax.ShapeDtypeStruct((n, qw), BF16),
                   jax.ShapeDtypeStruct((n, 2 * kw), BF16),
                   jax.ShapeDtypeStruct((n // seq, kw, seq), BF16)],
        compiler_params=pltpu.CompilerParams(dimension_semantics=("parallel",),
                                             vmem_limit_bytes=VMEM_LIMIT),
        name="proj_odd",
    )(x2d, gain, w, bd, hg)


def _swa_kernel(sink_ref, q_ref, kk_ref, vt_ref, sb_ref, o_ref, va_ref, *, seq):
    jj = pl.program_id(0)
    w = WINDOW
    gw = SWA_GROUP * HEAD_DIM
    sub = lax.broadcasted_iota(jnp.int32, (LANES - HEAD_DIM, seq), 0)
    sinks = []
    for p in range(SWA_KV_PER_STEP):
        va_ref[p, 0:HEAD_DIM, :] = vt_ref[0, p * HEAD_DIM:(p + 1) * HEAD_DIM, :]
        va_ref[p, HEAD_DIM:LANES, :] = jnp.where(sub == 0, 1.0, 0.0).astype(BF16)
        head0 = (jj * SWA_KV_PER_STEP + p) * SWA_GROUP
        sinks.append(jnp.concatenate(
            [jnp.full((1, w), sink_ref[head0 + g] * LOG2E, F32) for g in range(SWA_GROUP)],
            axis=1))
    low = lax.broadcasted_iota(jnp.int32, (w, LANES), 1) < HEAD_DIM
    zero = jnp.zeros((), BF16)

    def scores(p, n):
        r0 = n * w
        qb = q_ref[0, r0:r0 + w, p * gw:(p + 1) * gw]
        qa, qc = qb[:, 0:LANES], qb[:, LANES:2 * LANES]
        q4 = jnp.concatenate([jnp.where(low, qa, zero), jnp.where(low, zero, qa),
                              jnp.where(low, qc, zero), jnp.where(low, zero, qc)], axis=0)
        lanes = slice(p * LANES, (p + 1) * LANES)
        if n == 0:
            return p, n, _dot_nt(kk_ref[0, 0:w, lanes], q4) + sb_ref[p, w:2 * w, :]
        return p, n, _dot_nt(kk_ref[0, r0 - w:r0 + w, lanes], q4) + sb_ref[p]

    def finish(p, n, st):
        r0 = n * w
        vals = va_ref[p, :, 0:w] if n == 0 else va_ref[p, :, r0 - w:r0 + w]
        m = jnp.maximum(jnp.max(st, axis=0, keepdims=True), sinks[p])
        acc = _dot(vals, jnp.exp2(st - m).astype(BF16))
        denom = acc[HEAD_DIM:HEAD_DIM + 1, :] + jnp.exp2(sinks[p] - m)
        ot = acc[0:HEAD_DIM, :] / denom
        o_ref[0, r0:r0 + w, p * gw:(p + 1) * gw] = jnp.concatenate(
            [jnp.concatenate([ot[:, 0:w], ot[:, w:2 * w]], axis=0).T,
             jnp.concatenate([ot[:, 2 * w:3 * w], ot[:, 3 * w:4 * w]], axis=0).T],
            axis=1).astype(BF16)

    _lagged([(p, n) for p in range(SWA_KV_PER_STEP) for n in range(seq // w)], scores, finish,
            depth=SWA_LAG)


def _swa_attention(sinks, q, kk, vt, swbias, batch, seq):
    kps = SWA_KV_PER_STEP
    gw = kps * SWA_GROUP * HEAD_DIM
    return pl.pallas_call(
        functools.partial(_swa_kernel, seq=seq),
        grid=(SWA_KV_HEADS // kps, batch),
        in_specs=[
            pl.BlockSpec(memory_space=pltpu.SMEM),
            pl.BlockSpec((1, seq, gw), lambda h, b: (b, 0, h)),
            pl.BlockSpec((1, seq, kps * LANES), lambda h, b: (b, 0, h)),
            pl.BlockSpec((1, kps * HEAD_DIM, seq), lambda h, b: (b, h, 0)),
            pl.BlockSpec((kps, 2 * WINDOW, SWA_GROUP * WINDOW), lambda h, b: (h, 0, 0)),
        ],
        out_specs=pl.BlockSpec((1, seq, gw), lambda h, b: (b, 0, h)),
        out_shape=jax.ShapeDtypeStruct((batch, seq, SWA_HEADS * HEAD_DIM), BF16),
        scratch_shapes=[pltpu.VMEM((kps, LANES, seq), BF16)],
        compiler_params=pltpu.CompilerParams(dimension_semantics=("parallel", "parallel"),
                                             vmem_limit_bytes=VMEM_LIMIT),
        name="swa_attention",
    )(sinks, q, kk, vt, swbias)


def _outmlp_kernel(*refs, n_mix):
    x_ref = refs[0]
    a_refs = refs[1:1 + n_mix]
    wo_ref, gain_ref, wg_ref, wu_ref, wd_ref, o_ref = refs[1 + n_mix:]
    mix = jnp.concatenate([a_ref[...] for a_ref in a_refs], axis=1)
    x1 = x_ref[...] + _dot(mix, wo_ref[...])
    h = _rms(x1, gain_ref[...]).astype(BF16)
    out = x1
    for c0 in range(0, FFN_HIDDEN, HIDDEN_CHUNK):
        c1 = min(c0 + HIDDEN_CHUNK, FFN_HIDDEN)
        g = _dot(h, wg_ref[:, c0:c1])
        u = _dot(h, wu_ref[:, c0:c1])
        y = (g * (1.0 / (1.0 + jnp.exp(-g))) * u).astype(BF16)
        out = out + _dot(y, wd_ref[c0:c1, :])
    o_ref[...] = out


def _outmlp(x2d, mixes, wo, gain, wg, wu, wd, layer):
    n = x2d.shape[0]
    tm = MLP_ROW_TILE
    full = lambda shape: pl.BlockSpec(shape, lambda t: (0,) * len(shape),
                                      pipeline_mode=pl.Buffered(1))
    of_layer = lambda w: pl.BlockSpec((None,) + w.shape[1:], lambda t: (layer, 0, 0),
                                      pipeline_mode=pl.Buffered(1))
    in_specs = [pl.BlockSpec((tm, D_MODEL), lambda t: (t, 0))]
    in_specs += [pl.BlockSpec((tm, a.shape[1]), lambda t: (t, 0)) for a in mixes]
    in_specs += [full(wo.shape), full(gain.shape), of_layer(wg), of_layer(wu), of_layer(wd)]
    return pl.pallas_call(
        functools.partial(_outmlp_kernel, n_mix=len(mixes)),
        grid=(n // tm,),
        in_specs=in_specs,
        out_specs=pl.BlockSpec((tm, D_MODEL), lambda t: (t, 0)),
        out_shape=jax.ShapeDtypeStruct((n, D_MODEL), F32),
        compiler_params=pltpu.CompilerParams(dimension_semantics=("parallel",),
                                             vmem_limit_bytes=VMEM_LIMIT),
        name="outproj_swiglu",
    )(x2d, *mixes, wo, gain, wg, wu, wd)


def _block_diag_mean():
    i = np.arange(MXU_TILE)
    same = (i[:, None] // HEAD_DIM) == (i[None, :] // HEAD_DIM)
    return jnp.asarray(same / HEAD_DIM, BF16)


def kernel(x, rel_bias_table, ev_attn_norm, ev_w_in, ev_b_forget, ev_fox_q_norm, ev_fox_k_norm,
           ev_diff_q_norm, ev_diff_k_norm, ev_lambda_q1, ev_lambda_k1, ev_lambda_q2, ev_lambda_k2,
           ev_diff_subln, ev_w_out, od_attn_norm, od_w_qkv, od_q_norm, od_k_norm, od_sinks, od_w_out,
           ffn_norm, w_gate, w_up, w_down):
    batch, seq, d = x.shape
    n = batch * seq
    x2d = x.reshape(n, d)
    bd = _block_diag_mean()
    wg, wu, wd = w_gate.astype(BF16), w_up.astype(BF16), w_down.astype(BF16)
    swbias, gbias = _bias_tiles(rel_bias_table.astype(F32))
    swbias = swbias.reshape(SWA_KV_HEADS, SWA_GROUP, 2 * WINDOW, WINDOW).transpose(0, 2, 1, 3)
    swbias = swbias.reshape(SWA_KV_HEADS, 2 * WINDOW, SWA_GROUP * WINDOW)

    lambda_init = 0.8 - 0.6 * math.exp(-0.3 * 0)
    w_in = ev_w_in[0]
    ff0 = 3 * FOX_W
    d0 = ff0 + FOX_HEADS
    w0 = jnp.concatenate([w_in[:, 0:ff0], w_in[:, d0:d0 + 3 * DIFF_W]], axis=1).astype(BF16)
    wf = jnp.concatenate(
        [w_in[:, ff0:d0].T, jnp.zeros((BF16_SUBLANES - FOX_HEADS, d), w_in.dtype)],
        axis=0).astype(BF16)
    tile8 = lambda g, scale=1.0: jnp.tile(g.astype(F32) * scale, FOX_W // HEAD_DIM)
    hg0 = jnp.stack([tile8(ev_fox_q_norm[0], Q_GAIN_SCALE), tile8(ev_fox_k_norm[0]),
                     tile8(ev_diff_q_norm[0], Q_GAIN_SCALE), tile8(ev_diff_k_norm[0])])
    bf = ev_b_forget[0].astype(F32).reshape(FOX_HEADS, 1)
    fq, fk, fvt, dq, dk, dvt, caug = _proj0(
        x2d, ev_attn_norm[0].astype(F32).reshape(1, d), w0, wf, bd, hg0, bf, seq)
    r3 = lambda a: a.reshape(batch, seq, a.shape[-1])
    fox = _fox_attention(r3(fq), r3(fk), fvt, r3(caug), batch, seq)
    lam4 = jnp.stack([ev_lambda_q1[0], ev_lambda_k1[0], ev_lambda_q2[0], ev_lambda_k2[0]]).astype(F32)
    dout = _diff_attention(lam4, ev_diff_subln[0].astype(F32).reshape(2 * HEAD_DIM, 1),
                           r3(dq), r3(dk), dvt, gbias, batch, seq, lambda_init)
    x2d = _outmlp(x2d, [fox.reshape(n, FOX_W), dout.reshape(n, DIFF_W)], ev_w_out[0].astype(BF16),
                  ffn_norm[0].astype(F32).reshape(1, d), wg, wu, wd, 0)

    tile4 = lambda g, scale=1.0: jnp.tile(g.astype(F32) * scale, MXU_TILE // HEAD_DIM)
    hg1 = jnp.stack([tile4(od_q_norm[0], Q_GAIN_SCALE), tile4(od_k_norm[0])])
    q, kk, vt = _proj1(x2d, od_attn_norm[0].astype(F32).reshape(1, d), od_w_qkv[0].astype(BF16),
                       bd, hg1, seq)
    swa = _swa_attention(od_sinks[0].astype(F32), r3(q), r3(kk), vt, swbias, batch, seq)
    x2d = _outmlp(x2d, [swa.reshape(n, SWA_HEADS * HEAD_DIM)], od_w_out[0].astype(BF16),
                  ffn_norm[1].astype(F32).reshape(1, d), wg, wu, wd, 1)
    return x2d.reshape(batch, seq, d)
```

```python
import functools
import math

import numpy as np
import jax
import jax.numpy as jnp
from jax import lax
from jax.experimental import pallas as pl
from jax.experimental.pallas import tpu as pltpu

F32 = jnp.float32
BF16 = jnp.bfloat16

D_MODEL = 1024
HEAD_DIM = 64
FOX_HEADS = 8
DIFF_HEADS = 4
SWA_HEADS = 16
SWA_KV_HEADS = 4
SWA_GROUP = SWA_HEADS // SWA_KV_HEADS
WINDOW = 128
NUM_BUCKETS = 32
MAX_DISTANCE = 128
FFN_HIDDEN = 2816
RMS_EPS = 1e-6
NEG_INF = -1e30
FOX_W = FOX_HEADS * HEAD_DIM
DIFF_W = DIFF_HEADS * 2 * HEAD_DIM
QK_SCALE = HEAD_DIM ** -0.5
LOG2E = math.log2(math.e)
Q_GAIN_SCALE = QK_SCALE * LOG2E

LANES = 128
BF16_SUBLANES = 16
MXU_TILE = 256
ROW_TILE = 1024
MLP_ROW_TILE = 1024
ATT_Q_BLOCK = 256
FOX_PAIRS_PER_STEP = 2
DIFF_HEADS_PER_STEP = 2
SWA_KV_PER_STEP = 2
FOX_LAG = 3
DIFF_LAG = 2
PROJ_LAG = 2
SWA_LAG = 4
HIDDEN_CHUNK = 768
VMEM_LIMIT = 56 * 1024 * 1024

_NT = (((1,), (1,)), ((), ()))


def _dot(a, b):
    return jnp.dot(a, b, preferred_element_type=F32)


def _dot_nt(a, b):
    return lax.dot_general(a, b, _NT, preferred_element_type=F32)


def _lagged(items, first, second, depth=1):
    pending = []
    for item in items:
        pending.append(first(*item))
        if len(pending) > depth:
            second(*pending.pop(0))
    for p in pending:
        second(*p)


def _interleaved(items, first, second, depth):
    def drive(gens):
        done = {}
        while gens:
            for g in list(gens):
                try:
                    next(g)
                except StopIteration as stop:
                    done[id(g)] = stop.value
                    gens.remove(g)
        return done

    pending = []
    for item in items:
        g1 = first(*item)
        gens = [g1] if len(pending) < depth else [g1, second(*pending.pop(0))]
        pending.append(drive(gens)[id(g1)])
    for p in pending:
        drive([second(*p)])


def _hill_order(nblocks):
    return list(range(0, nblocks, 2)) + list(range(nblocks - 1 - nblocks % 2, 0, -2))


def _rms(xf, gain):
    ms = jnp.mean(xf * xf, axis=-1, keepdims=True)
    return xf * lax.rsqrt(ms + RMS_EPS) * gain


def _head_norm(acc, bd, gain):
    msq = _dot((acc * acc).astype(BF16), bd)
    return acc * lax.rsqrt(msq + RMS_EPS) * gain


def _bucket_ids(delta):
    n = np.maximum(delta, 0)
    max_exact = NUM_BUCKETS // 2
    nf = np.maximum(n, 1).astype(np.float64)
    large = max_exact + (np.log(nf / max_exact) / math.log(MAX_DISTANCE / max_exact)
                         * (NUM_BUCKETS - max_exact)).astype(np.int32)
    large = np.minimum(large, NUM_BUCKETS - 1)
    return np.where(n < max_exact, n, large).astype(np.int32)


def _bias_kernel(table_ref, bsw_ref, bg_ref, sw_ref, g_ref):
    h = pl.program_id(0)
    bsw = bsw_ref[...]
    sw = jnp.zeros(bsw.shape, F32)
    for b in range(NUM_BUCKETS):
        sw = jnp.where(bsw == b, table_ref[b, h] * LOG2E, sw)
    sw_ref[0] = jnp.where(bsw < 0, NEG_INF, sw)

    @pl.when(h < DIFF_HEADS)
    def _():
        bg = bg_ref[...]
        far = table_ref[NUM_BUCKETS - 1, h]
        g = jnp.zeros(bg.shape, F32)
        for b in range(NUM_BUCKETS):
            g = jnp.where(bg == b, (table_ref[b, h] - far) * LOG2E, g)
        g_ref[0] = jnp.where(bg < 0, NEG_INF, g)


def _bias_tiles(table):
    a = np.arange(WINDOW)[None, :]
    b = np.arange(2 * WINDOW)[:, None]
    d_sw = WINDOW + a - b
    bsw = np.where((d_sw >= 0) & (d_sw < WINDOW), _bucket_ids(d_sw), -1).astype(np.int32)
    a = np.arange(ATT_Q_BLOCK)[:, None]
    b = np.arange(2 * ATT_Q_BLOCK)[None, :]
    d_g = ATT_Q_BLOCK + a - b
    bg = np.where(d_g >= 0, _bucket_ids(d_g), -1).astype(np.int32)
    nh = table.shape[1]
    return pl.pallas_call(
        _bias_kernel,
        grid=(nh,),
        in_specs=[
            pl.BlockSpec(memory_space=pltpu.SMEM),
            pl.BlockSpec(bsw.shape, lambda h: (0, 0)),
            pl.BlockSpec(bg.shape, lambda h: (0, 0)),
        ],
        out_specs=[
            pl.BlockSpec((1,) + bsw.shape, lambda h: (h, 0, 0)),
            pl.BlockSpec((1,) + bg.shape, lambda h: (jnp.minimum(h, DIFF_HEADS - 1), 0, 0)),
        ],
        out_shape=[
            jax.ShapeDtypeStruct((nh,) + bsw.shape, F32),
            jax.ShapeDtypeStruct((DIFF_HEADS,) + bg.shape, F32),
        ],
        compiler_params=pltpu.CompilerParams(dimension_semantics=("arbitrary",)),
        name="rel_bias_tiles",
    )(table, jnp.asarray(bsw), jnp.asarray(bg))


def _proj0_kernel(x_ref, gain_ref, w_ref, wf_ref, bd_ref, hg_ref, bf_ref,
                  fq_ref, fk_ref, fv_ref, dq_ref, dk_ref, dv_ref, caug_ref,
                  carry_ref, *, tiles_per_seq):
    t = pl.program_id(0)

    @pl.when(t == 0)
    def _():
        carry_ref[...] = jnp.zeros_like(carry_ref)

    h = _rms(x_ref[...], gain_ref[...]).astype(BF16)
    bd = bd_ref[...]
    rows = h.shape[0]

    outs = (fq_ref, fk_ref, fv_ref, dq_ref, dk_ref, dv_ref)
    gains = (0, 1, None, 2, 3, None)
    feature_major = (False, False, True, False, False, False)

    def project(g):
        return g, _dot(h, w_ref[:, g * FOX_W:(g + 1) * FOX_W])

    def epilogue(g, proj):
        if feature_major[g]:
            outs[g][0] = proj.T.astype(BF16)
            return
        if gains[g] is None:
            outs[g][...] = proj.astype(BF16)
            return
        for i in range(FOX_W // MXU_TILE):
            cols = slice(i * MXU_TILE, (i + 1) * MXU_TILE)
            outs[g][:, cols] = _head_norm(
                proj[:, cols], bd, hg_ref[gains[g]:gains[g] + 1, cols]).astype(BF16)

    pending = [project(g) for g in range(PROJ_LAG)]

    z8 = _dot_nt(wf_ref[...], h)[0:FOX_HEADS, :] + bf_ref[...]
    y = -(jnp.maximum(-z8, 0.0) + jnp.log1p(jnp.exp(-jnp.abs(z8))))
    lane = lax.broadcasted_iota(jnp.int32, y.shape, 1)
    shift = 1
    while shift < rows:
        y = y + jnp.where(lane >= shift, pltpu.roll(y, shift, axis=1), 0.0)
        shift *= 2

    carry = jnp.where(t % tiles_per_seq == 0, 0.0, carry_ref[:, 0:1])
    c = y + carry
    carry_ref[...] = jnp.broadcast_to(c[:, rows - 1:rows], carry_ref.shape)
    cn = c * (-LOG2E)
    hi = cn.astype(BF16).astype(F32)
    r1 = cn - hi
    mid = r1.astype(BF16).astype(F32)
    lo = (r1 - mid).astype(BF16).astype(F32)
    half = jnp.concatenate([hi, mid, lo, jnp.zeros((HEAD_DIM - 3 * FOX_HEADS, rows), F32)], axis=0)
    caug_ref[...] = jnp.concatenate([half, half], axis=0).T.astype(BF16)

    for g in range(PROJ_LAG, len(outs)):
        pending.append(project(g))
        epilogue(*pending.pop(0))
    for p in pending:
        epilogue(*p)


def _proj0(x2d, gain, w, wf, bd, hg, bf, seq):
    n = x2d.shape[0]
    tm = ROW_TILE
    tps = seq // tm
    row_spec = pl.BlockSpec((tm, FOX_W), lambda t: (t, 0))
    col_spec = pl.BlockSpec((1, FOX_W, tm), lambda t: (t // tps, 0, t % tps))
    rows_shape = jax.ShapeDtypeStruct((n, FOX_W), BF16)
    cols_shape = jax.ShapeDtypeStruct((n // seq, FOX_W, seq), BF16)
    full = lambda shape: pl.BlockSpec(shape, lambda t: (0,) * len(shape))
    outs = pl.pallas_call(
        functools.partial(_proj0_kernel, tiles_per_seq=tps),
        grid=(n // tm,),
        in_specs=[
            pl.BlockSpec((tm, D_MODEL), lambda t: (t, 0)),
            full((1, D_MODEL)),
            full(w.shape),
            full(wf.shape),
            full((MXU_TILE, MXU_TILE)),
            full(hg.shape),
            full(bf.shape),
        ],
        out_specs=[row_spec, row_spec, col_spec, row_spec, row_spec, row_spec,
                   pl.BlockSpec((tm, LANES), lambda t: (t, 0))],
        out_shape=[rows_shape, rows_shape, cols_shape, rows_shape, rows_shape, rows_shape,
                   jax.ShapeDtypeStruct((n, LANES), BF16)],
        scratch_shapes=[pltpu.VMEM((FOX_HEADS, LANES), F32)],
        compiler_params=pltpu.CompilerParams(dimension_semantics=("arbitrary",),
                                             vmem_limit_bytes=VMEM_LIMIT),
        name="proj_even",
    )(x2d, gain, w, wf, bd, hg, bf)
    return outs


def _fox_kernel(q_ref, k_ref, vt_ref, caug_ref, o_ref, ka_ref, va_ref, *, seq):
    jj = pl.program_id(1)
    qb = ATT_Q_BLOCK
    low_s = lax.broadcasted_iota(jnp.int32, (seq, LANES), 1) < HEAD_DIM
    ca = caug_ref[0]
    sub = lax.broadcasted_iota(jnp.int32, (LANES - HEAD_DIM, seq), 0)
    ones_row = jnp.where(sub == 0, 1.0, 0.0).astype(BF16)
    for p in range(FOX_PAIRS_PER_STEP):
        kp = k_ref[0, :, p * LANES:(p + 1) * LANES]
        ka_ref[2 * p] = jnp.where(low_s, kp, ca)
        ka_ref[2 * p + 1] = jnp.where(low_s, ca, kp)
        for hh in range(2):
            r = p * LANES + hh * HEAD_DIM
            va_ref[2 * p + hh, 0:HEAD_DIM, :] = vt_ref[0, r:r + HEAD_DIM, :]
            va_ref[2 * p + hh, HEAD_DIM:LANES, :] = ones_row
    lane_q = lax.broadcasted_iota(jnp.int32, (qb, LANES), 1)
    key_i = lax.broadcasted_iota(jnp.int32, (qb, qb), 0)
    qry_i = lax.broadcasted_iota(jnp.int32, (qb, qb), 1)
    causal = key_i <= qry_i
    owns = (lane_q < HEAD_DIM, lane_q >= HEAD_DIM)

    def scores(p, qi, hh):
        r0 = qi * qb
        head = 2 * (FOX_PAIRS_PER_STEP * jj + p) + hh
        rel = lane_q - (HEAD_DIM if hh == 0 else 0) - head
        pick = (rel == 0) | (rel == FOX_HEADS) | (rel == 2 * FOX_HEADS)
        qz = jnp.where(owns[hh], q_ref[0, r0:r0 + qb, p * LANES:(p + 1) * LANES],
                       jnp.where(pick, 1.0, 0.0).astype(BF16))
        s_all = _dot_nt(ka_ref[2 * p + hh, 0:(qi + 1) * qb, :], qz)
        yield
        tiles = [s_all[t * qb:(t + 1) * qb] for t in range(qi)]
        tiles.append(jnp.where(causal, s_all[qi * qb:(qi + 1) * qb], NEG_INF))
        return p, qi, hh, tiles

    halves = []

    def finish(p, qi, hh, tiles):
        r0 = qi * qb
        m = jnp.max(functools.reduce(jnp.maximum, tiles), axis=0, keepdims=True)
        acc = None
        for t, s in enumerate(tiles):
            part = _dot(va_ref[2 * p + hh, :, t * qb:(t + 1) * qb], jnp.exp2(s - m).astype(BF16))
            acc = part if acc is None else acc + part
            yield
        halves.append(acc[0:HEAD_DIM] / acc[HEAD_DIM:HEAD_DIM + 1])
        if hh == 1:
            o_ref[0, r0:r0 + qb, p * LANES:(p + 1) * LANES] = jnp.concatenate(
                [halves[-2], halves[-1]], axis=0).T.astype(BF16)

    _interleaved([(p, qi, hh) for p in range(FOX_PAIRS_PER_STEP)
                  for qi in _hill_order(seq // qb) for hh in range(2)], scores, finish,
                 depth=FOX_LAG)


def _fox_attention(fq, fk, fvt, caug, batch, seq):
    width = FOX_PAIRS_PER_STEP * LANES
    blk = pl.BlockSpec((1, seq, width), lambda b, j: (b, 0, j))
    nheads = 2 * FOX_PAIRS_PER_STEP
    return pl.pallas_call(
        functools.partial(_fox_kernel, seq=seq),
        grid=(batch, FOX_W // width),
        in_specs=[blk, blk, pl.BlockSpec((1, width, seq), lambda b, j: (b, j, 0)),
                  pl.BlockSpec((1, seq, LANES), lambda b, j: (b, 0, 0))],
        out_specs=blk,
        out_shape=jax.ShapeDtypeStruct((batch, seq, FOX_W), BF16),
        scratch_shapes=[pltpu.VMEM((nheads, seq, LANES), BF16),
                        pltpu.VMEM((nheads, LANES, seq), BF16)],
        compiler_params=pltpu.CompilerParams(dimension_semantics=("parallel", "parallel"),
                                             vmem_limit_bytes=VMEM_LIMIT),
        name="fox_attention",
    )(fq, fk, fvt, caug)


def _diff_kernel(lam_ref, g_ref, q_ref, k_ref, v_ref, gb_ref, o_ref, va_ref, *, seq, lambda_init):
    qb = ATT_Q_BLOCK
    lv = lam_ref[...]
    lam = (jnp.exp(jnp.sum(lv[0:1] * lv[1:2], axis=-1, keepdims=True))
           - jnp.exp(jnp.sum(lv[2:3] * lv[3:4], axis=-1, keepdims=True)) + lambda_init)
    for p in range(DIFF_HEADS_PER_STEP):
        va_ref[p, :, 0:LANES] = v_ref[0, :, p * LANES:(p + 1) * LANES]
        va_ref[p, :, LANES:2 * LANES] = jnp.ones((seq, LANES), BF16)
    lane_q = lax.broadcasted_iota(jnp.int32, (qb, LANES), 1)
    bias_prev = [jnp.concatenate([gb_ref[p, :, 0:qb]] * 2, axis=0)
                 for p in range(DIFF_HEADS_PER_STEP)]
    bias_diag = [jnp.concatenate([gb_ref[p, :, qb:2 * qb]] * 2, axis=0)
                 for p in range(DIFF_HEADS_PER_STEP)]
    gain = g_ref[...] * (1.0 - lambda_init)
    zero = jnp.zeros((), BF16)

    def scores(p, qi):
        r0 = qi * qb
        qp = q_ref[0, r0:r0 + qb, p * LANES:(p + 1) * LANES]
        qz = jnp.concatenate([jnp.where(lane_q < HEAD_DIM, qp, zero),
                              jnp.where(lane_q < HEAD_DIM, zero, qp)], axis=0)
        tiles = []
        for t in range(qi + 1):
            s = _dot_nt(qz, k_ref[0, t * qb:(t + 1) * qb, p * LANES:(p + 1) * LANES])
            tiles.append(s + bias_diag[p] if t == qi else s + bias_prev[p] if t == qi - 1 else s)
            yield
        return p, qi, tiles

    def finish(p, qi, tiles):
        r0 = qi * qb
        m = jnp.max(functools.reduce(jnp.maximum, tiles), axis=-1, keepdims=True)
        acc = None
        for t, s in enumerate(tiles):
            part = _dot(jnp.exp2(s - m).astype(BF16), va_ref[p, t * qb:(t + 1) * qb, :])
            acc = part if acc is None else acc + part
            yield
        a = acc[:, 0:LANES] / acc[:, LANES:2 * LANES]
        o = a[0:qb] - lam * a[qb:2 * qb]
        o_ref[0, r0:r0 + qb, p * LANES:(p + 1) * LANES] = _rms(o, gain).astype(BF16)

    _interleaved([(p, qi) for p in range(DIFF_HEADS_PER_STEP) for qi in _hill_order(seq // qb)],
                 scores, finish, depth=DIFF_LAG)


def _diff_attention(lam4, subln, dq, dk, dv, gbias, batch, seq, lambda_init):
    hps = DIFF_HEADS_PER_STEP
    blk = pl.BlockSpec((1, seq, hps * LANES), lambda h, b: (b, 0, h))
    return pl.pallas_call(
        functools.partial(_diff_kernel, seq=seq, lambda_init=lambda_init),
        grid=(DIFF_HEADS // hps, batch),
        in_specs=[
            pl.BlockSpec(lam4.shape, lambda h, b: (0, 0)),
            pl.BlockSpec(subln.shape, lambda h, b: (0, 0)),
            blk, blk, blk,
            pl.BlockSpec((hps, ATT_Q_BLOCK, 2 * ATT_Q_BLOCK), lambda h, b: (h, 0, 0)),
        ],
        out_specs=blk,
        out_shape=jax.ShapeDtypeStruct((batch, seq, DIFF_W), BF16),
        scratch_shapes=[pltpu.VMEM((hps, seq, 2 * LANES), BF16)],
        compiler_params=pltpu.CompilerParams(dimension_semantics=("parallel", "parallel"),
                                             vmem_limit_bytes=VMEM_LIMIT),
        name="diff_attention",
    )(lam4, subln, dq, dk, dv, gbias)


def _proj1_kernel(x_ref, gain_ref, w_ref, bd_ref, hg_ref, q_ref, kk_ref, vt_ref):
    h = _rms(x_ref[...], gain_ref[...]).astype(BF16)
    bd = bd_ref[...]
    qw = SWA_HEADS * HEAD_DIM
    kw = SWA_KV_HEADS * HEAD_DIM
    low = lax.broadcasted_iota(jnp.int32, (h.shape[0], LANES), 1) < HEAD_DIM

    def project(c0):
        return c0, _dot(h, w_ref[:, c0:c0 + 2 * MXU_TILE])

    def epilogue(c0, proj):
        if c0 < qw:
            for i in range(2):
                cols = slice(i * MXU_TILE, (i + 1) * MXU_TILE)
                q_ref[:, c0 + i * MXU_TILE:c0 + (i + 1) * MXU_TILE] = _head_norm(
                    proj[:, cols], bd, hg_ref[0:1, :]).astype(BF16)
            return
        kn = _head_norm(proj[:, 0:kw], bd, hg_ref[1:2, :])
        for p in range(kw // LANES):
            pair = kn[:, p * LANES:(p + 1) * LANES]
            swapped = pltpu.roll(pair, HEAD_DIM, axis=1)
            kk_ref[:, (2 * p) * LANES:(2 * p + 1) * LANES] = jnp.where(low, pair, swapped).astype(BF16)
            kk_ref[:, (2 * p + 1) * LANES:(2 * p + 2) * LANES] = jnp.where(low, swapped, pair).astype(BF16)
        vt_ref[0] = proj[:, kw:2 * kw].T.astype(BF16)

    _lagged([(qw,)] + [(c0,) for c0 in range(0, qw, 2 * MXU_TILE)], project, epilogue,
            depth=PROJ_LAG)


def _proj1(x2d, gain, w, bd, hg, seq):
    n = x2d.shape[0]
    tm = ROW_TILE
    tps = seq // tm
    qw = SWA_HEADS * HEAD_DIM
    kw = SWA_KV_HEADS * HEAD_DIM
    full = lambda shape: pl.BlockSpec(shape, lambda t: (0,) * len(shape))
    return pl.pallas_call(
        _proj1_kernel,
        grid=(n // tm,),
        in_specs=[
            pl.BlockSpec((tm, D_MODEL), lambda t: (t, 0)),
            full((1, D_MODEL)),
            full(w.shape),
            full((MXU_TILE, MXU_TILE)),
            full(hg.shape),
        ],
        out_specs=[pl.BlockSpec((tm, qw), lambda t: (t, 0)),
                   pl.BlockSpec((tm, 2 * kw), lambda t: (t, 0)),
                   pl.BlockSpec((1, kw, tm), lambda t: (t // tps, 0, t % tps))],
        out_shape=[jax.ShapeDtypeStruct((n, qw), BF16),
                   jax.ShapeDtypeStruct((n, 2 * kw), BF16),
                   jax.ShapeDtypeStruct((n // seq, kw, seq), BF16)],
        compiler_params=pltpu.CompilerParams(dimension_semantics=("parallel",),
                                             vmem_limit_bytes=VMEM_LIMIT),
        name="proj_odd",
    )(x2d, gain, w, bd, hg)


def _swa_kernel(sink_ref, q_ref, kk_ref, vt_ref, sb_ref, o_ref, va_ref, *, seq):
    jj = pl.program_id(0)
    w = WINDOW
    gw = SWA_GROUP * HEAD_DIM
    sub = lax.broadcasted_iota(jnp.int32, (LANES - HEAD_DIM, seq), 0)
    sinks = []
    for p in range(SWA_KV_PER_STEP):
        va_ref[p, 0:HEAD_DIM, :] = vt_ref[0, p * HEAD_DIM:(p + 1) * HEAD_DIM, :]
        va_ref[p, HEAD_DIM:LANES, :] = jnp.where(sub == 0, 1.0, 0.0).astype(BF16)
        head0 = (jj * SWA_KV_PER_STEP + p) * SWA_GROUP
        sinks.append(jnp.concatenate(
            [jnp.full((1, w), sink_ref[head0 + g] * LOG2E, F32) for g in range(SWA_GROUP)],
            axis=1))
    low = lax.broadcasted_iota(jnp.int32, (w, LANES), 1) < HEAD_DIM
    zero = jnp.zeros((), BF16)

    def scores(p, n):
        r0 = n * w
        qb = q_ref[0, r0:r0 + w, p * gw:(p + 1) * gw]
        qa, qc = qb[:, 0:LANES], qb[:, LANES:2 * LANES]
        q4 = jnp.concatenate([jnp.where(low, qa, zero), jnp.where(low, zero, qa),
                              jnp.where(low, qc, zero), jnp.where(low, zero, qc)], axis=0)
        lanes = slice(p * LANES, (p + 1) * LANES)
        if n == 0:
            return p, n, _dot_nt(kk_ref[0, 0:w, lanes], q4) + sb_ref[p, w:2 * w, :]
        return p, n, _dot_nt(kk_ref[0, r0 - w:r0 + w, lanes], q4) + sb_ref[p]

    def finish(p, n, st):
        r0 = n * w
        vals = va_ref[p, :, 0:w] if n == 0 else va_ref[p, :, r0 - w:r0 + w]
        m = jnp.maximum(jnp.max(st, axis=0, keepdims=True), sinks[p])
        acc = _dot(vals, jnp.exp2(st - m).astype(BF16))
        denom = acc[HEAD_DIM:HEAD_DIM + 1, :] + jnp.exp2(sinks[p] - m)
        ot = acc[0:HEAD_DIM, :] / denom
        o_ref[0, r0:r0 + w, p * gw:(p + 1) * gw] = jnp.concatenate(
            [jnp.concatenate([ot[:, 0:w], ot[:, w:2 * w]], axis=0).T,
             jnp.concatenate([ot[:, 2 * w:3 * w], ot[:, 3 * w:4 * w]], axis=0).T],
            axis=1).astype(BF16)

    _lagged([(p, n) for p in range(SWA_KV_PER_STEP) for n in range(seq // w)], scores, finish,
            depth=SWA_LAG)


def _swa_attention(sinks, q, kk, vt, swbias, batch, seq):
    kps = SWA_KV_PER_STEP
    gw = kps * SWA_GROUP * HEAD_DIM
    return pl.pallas_call(
        functools.partial(_swa_kernel, seq=seq),
        grid=(SWA_KV_HEADS // kps, batch),
        in_specs=[
            pl.BlockSpec(memory_space=pltpu.SMEM),
            pl.BlockSpec((1, seq, gw), lambda h, b: (b, 0, h)),
            pl.BlockSpec((1, seq, kps * LANES), lambda h, b: (b, 0, h)),
            pl.BlockSpec((1, kps * HEAD_DIM, seq), lambda h, b: (b, h, 0)),
            pl.BlockSpec((kps, 2 * WINDOW, SWA_GROUP * WINDOW), lambda h, b: (h, 0, 0)),
        ],
        out_specs=pl.BlockSpec((1, seq, gw), lambda h, b: (b, 0, h)),
        out_shape=jax.ShapeDtypeStruct((batch, seq, SWA_HEADS * HEAD_DIM), BF16),
        scratch_shapes=[pltpu.VMEM((kps, LANES, seq), BF16)],
        compiler_params=pltpu.CompilerParams(dimension_semantics=("parallel", "parallel"),
                                             vmem_limit_bytes=VMEM_LIMIT),
        name="swa_attention",
    )(sinks, q, kk, vt, swbias)


def _outmlp_kernel(*refs, n_mix):
    x_ref = refs[0]
    a_refs = refs[1:1 + n_mix]
    wo_ref, gain_ref, wg_ref, wu_ref, wd_ref, o_ref = refs[1 + n_mix:]
    mix = jnp.concatenate([a_ref[...] for a_ref in a_refs], axis=1)
    x1 = x_ref[...] + _dot(mix, wo_ref[...])
    h = _rms(x1, gain_ref[...]).astype(BF16)
    out = x1
    for c0 in range(0, FFN_HIDDEN, HIDDEN_CHUNK):
        c1 = min(c0 + HIDDEN_CHUNK, FFN_HIDDEN)
        g = _dot(h, wg_ref[:, c0:c1])
        u = _dot(h, wu_ref[:, c0:c1])
        y = (g * (1.0 / (1.0 + jnp.exp(-g))) * u).astype(BF16)
        out = out + _dot(y, wd_ref[c0:c1, :])
    o_ref[...] = out


def _outmlp(x2d, mixes, wo, gain, wg, wu, wd, layer):
    n = x2d.shape[0]
    tm = MLP_ROW_TILE
    full = lambda shape: pl.BlockSpec(shape, lambda t: (0,) * len(shape),
                                      pipeline_mode=pl.Buffered(1))
    of_layer = lambda w: pl.BlockSpec((None,) + w.shape[1:], lambda t: (layer, 0, 0),
                                      pipeline_mode=pl.Buffered(1))
    in_specs = [pl.BlockSpec((tm, D_MODEL), lambda t: (t, 0))]
    in_specs += [pl.BlockSpec((tm, a.shape[1]), lambda t: (t, 0)) for a in mixes]
    in_specs += [full(wo.shape), full(gain.shape), of_layer(wg), of_layer(wu), of_layer(wd)]
    return pl.pallas_call(
        functools.partial(_outmlp_kernel, n_mix=len(mixes)),
        grid=(n // tm,),
        in_specs=in_specs,
        out_specs=pl.BlockSpec((tm, D_MODEL), lambda t: (t, 0)),
        out_shape=jax.ShapeDtypeStruct((n, D_MODEL), F32),
        compiler_params=pltpu.CompilerParams(dimension_semantics=("parallel",),
                                             vmem_limit_bytes=VMEM_LIMIT),
        name="outproj_swiglu",
    )(x2d, *mixes, wo, gain, wg, wu, wd)


def _block_diag_mean():
    i = np.arange(MXU_TILE)
    same = (i[:, None] // HEAD_DIM) == (i[None, :] // HEAD_DIM)
    return jnp.asarray(same / HEAD_DIM, BF16)


def kernel(x, rel_bias_table, ev_attn_norm, ev_w_in, ev_b_forget, ev_fox_q_norm, ev_fox_k_norm,
           ev_diff_q_norm, ev_diff_k_norm, ev_lambda_q1, ev_lambda_k1, ev_lambda_q2, ev_lambda_k2,
           ev_diff_subln, ev_w_out, od_attn_norm, od_w_qkv, od_q_norm, od_k_norm, od_sinks, od_w_out,
           ffn_norm, w_gate, w_up, w_down):
    batch, seq, d = x.shape
    n = batch * seq
    x2d = x.reshape(n, d)
    bd = _block_diag_mean()
    wg, wu, wd = w_gate.astype(BF16), w_up.astype(BF16), w_down.astype(BF16)
    swbias, gbias = _bias_tiles(rel_bias_table.astype(F32))
    swbias = swbias.reshape(SWA_KV_HEADS, SWA_GROUP, 2 * WINDOW, WINDOW).transpose(0, 2, 1, 3)
    swbias = swbias.reshape(SWA_KV_HEADS, 2 * WINDOW, SWA_GROUP * WINDOW)

    lambda_init = 0.8 - 0.6 * math.exp(-0.3 * 0)
    w_in = ev_w_in[0]
    ff0 = 3 * FOX_W
    d0 = ff0 + FOX_HEADS
    w0 = jnp.concatenate([w_in[:, 0:ff0], w_in[:, d0:d0 + 3 * DIFF_W]], axis=1).astype(BF16)
    wf = jnp.concatenate(
        [w_in[:, ff0:d0].T, jnp.zeros((BF16_SUBLANES - FOX_HEADS, d), w_in.dtype)],
        axis=0).astype(BF16)
    tile8 = lambda g, scale=1.0: jnp.tile(g.astype(F32) * scale, FOX_W // HEAD_DIM)
    hg0 = jnp.stack([tile8(ev_fox_q_norm[0], Q_GAIN_SCALE), tile8(ev_fox_k_norm[0]),
                     tile8(ev_diff_q_norm[0], Q_GAIN_SCALE), tile8(ev_diff_k_norm[0])])
    bf = ev_b_forget[0].astype(F32).reshape(FOX_HEADS, 1)
    fq, fk, fvt, dq, dk, dv, caug = _proj0(
        x2d, ev_attn_norm[0].astype(F32).reshape(1, d), w0, wf, bd, hg0, bf, seq)
    r3 = lambda a: a.reshape(batch, seq, a.shape[-1])
    fox = _fox_attention(r3(fq), r3(fk), fvt, r3(caug), batch, seq)
    lam4 = jnp.stack([ev_lambda_q1[0], ev_lambda_k1[0], ev_lambda_q2[0], ev_lambda_k2[0]]).astype(F32)
    dout = _diff_attention(lam4, ev_diff_subln[0].astype(F32).reshape(1, 2 * HEAD_DIM),
                           r3(dq), r3(dk), r3(dv), gbias, batch, seq, lambda_init)
    x2d = _outmlp(x2d, [fox.reshape(n, FOX_W), dout.reshape(n, DIFF_W)], ev_w_out[0].astype(BF16),
                  ffn_norm[0].astype(F32).reshape(1, d), wg, wu, wd, 0)

    tile4 = lambda g, scale=1.0: jnp.tile(g.astype(F32) * scale, MXU_TILE // HEAD_DIM)
    hg1 = jnp.stack([tile4(od_q_norm[0], Q_GAIN_SCALE), tile4(od_k_norm[0])])
    q, kk, vt = _proj1(x2d, od_attn_norm[0].astype(F32).reshape(1, d), od_w_qkv[0].astype(BF16),
                       bd, hg1, seq)
    swa = _swa_attention(od_sinks[0].astype(F32), r3(q), r3(kk), vt, swbias, batch, seq)
    x2d = _outmlp(x2d, [swa.reshape(n, SWA_HEADS * HEAD_DIM)], od_w_out[0].astype(BF16),
                  ffn_norm[1].astype(F32).reshape(1, d), wg, wu, wd, 1)
    return x2d.reshape(batch, seq, d)
```

```python
import functools
import math

import numpy as np
import jax
import jax.numpy as jnp
from jax import lax
from jax.experimental import pallas as pl
from jax.experimental.pallas import tpu as pltpu

F32 = jnp.float32
BF16 = jnp.bfloat16

D_MODEL = 1024
HEAD_DIM = 64
FOX_HEADS = 8
DIFF_HEADS = 4
SWA_HEADS = 16
SWA_KV_HEADS = 4
SWA_GROUP = SWA_HEADS // SWA_KV_HEADS
WINDOW = 128
NUM_BUCKETS = 32
MAX_DISTANCE = 128
FFN_HIDDEN = 2816
RMS_EPS = 1e-6
NEG_INF = -1e30
FOX_W = FOX_HEADS * HEAD_DIM
DIFF_W = DIFF_HEADS * 2 * HEAD_DIM
QK_SCALE = HEAD_DIM ** -0.5
LOG2E = math.log2(math.e)
Q_GAIN_SCALE = QK_SCALE * LOG2E

LANES = 128
BF16_SUBLANES = 16
MXU_TILE = 256
ROW_TILE = 1024
ATT_Q_BLOCK = 256
FOX_PAIRS_PER_STEP = 2
DIFF_HEADS_PER_STEP = 2
SWA_KV_PER_STEP = 2
FOX_LAG = 3
DIFF_LAG = 2
PROJ_LAG = 1
SWA_LAG = 4
HIDDEN_CHUNK = 768
V7X_VMEM_BYTES = 64 * 1024 * 1024
VMEM_LIMIT = V7X_VMEM_BYTES - 8 * 1024 * 1024

_NT = (((1,), (1,)), ((), ()))


def _dot(a, b):
    return jnp.dot(a, b, preferred_element_type=F32)


def _dot_nt(a, b):
    return lax.dot_general(a, b, _NT, preferred_element_type=F32)


def _lagged(items, first, second, depth=1):
    pending = []
    for item in items:
        pending.append(first(*item))
        if len(pending) > depth:
            second(*pending.pop(0))
    for p in pending:
        second(*p)


def _interleaved(items, first, second, depth):
    def drive(gens):
        done = {}
        while gens:
            for g in list(gens):
                try:
                    next(g)
                except StopIteration as stop:
                    done[id(g)] = stop.value
                    gens.remove(g)
        return done

    pending = []
    for item in items:
        g1 = first(*item)
        gens = [g1] if len(pending) < depth else [g1, second(*pending.pop(0))]
        pending.append(drive(gens)[id(g1)])
    for p in pending:
        drive([second(*p)])


def _hill_order(nblocks):
    return list(range(0, nblocks, 2)) + list(range(nblocks - 1 - nblocks % 2, 0, -2))


def _rms(xf, gain):
    ms = jnp.mean(xf * xf, axis=-1, keepdims=True)
    return xf * lax.rsqrt(ms + RMS_EPS) * gain


def _head_norm(acc, bd, gain):
    msq = _dot((acc * acc).astype(BF16), bd)
    return acc * lax.rsqrt(msq + RMS_EPS) * gain


def _bucket_ids(delta):
    n = np.maximum(delta, 0)
    max_exact = NUM_BUCKETS // 2
    nf = np.maximum(n, 1).astype(np.float64)
    large = max_exact + (np.log(nf / max_exact) / math.log(MAX_DISTANCE / max_exact)
                         * (NUM_BUCKETS - max_exact)).astype(np.int32)
    large = np.minimum(large, NUM_BUCKETS - 1)
    return np.where(n < max_exact, n, large).astype(np.int32)


def _bias_kernel(table_ref, bsw_ref, bg_ref, sw_ref, g_ref):
    h = pl.program_id(0)
    bsw = bsw_ref[...]
    sw = jnp.zeros(bsw.shape, F32)
    for b in range(NUM_BUCKETS):
        sw = jnp.where(bsw == b, table_ref[b, h] * LOG2E, sw)
    sw_ref[0] = jnp.where(bsw < 0, NEG_INF, sw)

    @pl.when(h < DIFF_HEADS)
    def _():
        bg = bg_ref[...]
        far = table_ref[NUM_BUCKETS - 1, h]
        g = jnp.zeros(bg.shape, F32)
        for b in range(NUM_BUCKETS):
            g = jnp.where(bg == b, (table_ref[b, h] - far) * LOG2E, g)
        g_ref[0] = jnp.where(bg < 0, NEG_INF, g)


def _bias_tiles(table):
    a = np.arange(WINDOW)[None, :]
    b = np.arange(2 * WINDOW)[:, None]
    d_sw = WINDOW + a - b
    bsw = np.where((d_sw >= 0) & (d_sw < WINDOW), _bucket_ids(d_sw), -1).astype(np.int32)
    a = np.arange(ATT_Q_BLOCK)[:, None]
    b = np.arange(2 * ATT_Q_BLOCK)[None, :]
    d_g = ATT_Q_BLOCK + a - b
    bg = np.where(d_g >= 0, _bucket_ids(d_g), -1).astype(np.int32)
    nh = table.shape[1]
    return pl.pallas_call(
        _bias_kernel,
        grid=(nh,),
        in_specs=[
            pl.BlockSpec(memory_space=pltpu.SMEM),
            pl.BlockSpec(bsw.shape, lambda h: (0, 0)),
            pl.BlockSpec(bg.shape, lambda h: (0, 0)),
        ],
        out_specs=[
            pl.BlockSpec((1,) + bsw.shape, lambda h: (h, 0, 0)),
            pl.BlockSpec((1,) + bg.shape, lambda h: (jnp.minimum(h, DIFF_HEADS - 1), 0, 0)),
        ],
        out_shape=[
            jax.ShapeDtypeStruct((nh,) + bsw.shape, F32),
            jax.ShapeDtypeStruct((DIFF_HEADS,) + bg.shape, F32),
        ],
        compiler_params=pltpu.CompilerParams(dimension_semantics=("arbitrary",)),
        name="rel_bias_tiles",
    )(table, jnp.asarray(bsw), jnp.asarray(bg))


def _proj0_kernel(x_ref, gain_ref, w_ref, wf_ref, bd_ref, hg_ref, bf_ref,
                  fq_ref, fk_ref, fv_ref, dq_ref, dk_ref, dv_ref, caug_ref,
                  carry_ref, *, tiles_per_seq):
    t = pl.program_id(0)

    @pl.when(t == 0)
    def _():
        carry_ref[...] = jnp.zeros_like(carry_ref)

    h = _rms(x_ref[...], gain_ref[...]).astype(BF16)
    bd = bd_ref[...]
    rows = h.shape[0]

    outs = (fq_ref, fk_ref, fv_ref, dq_ref, dk_ref, dv_ref)
    gains = (0, 1, None, 2, 3, None)
    feature_major = (False, False, True, False, False, False)

    def project(g):
        return g, _dot(h, w_ref[:, g * FOX_W:(g + 1) * FOX_W])

    def epilogue(g, proj):
        if feature_major[g]:
            outs[g][0] = proj.T.astype(BF16)
            return
        if gains[g] is None:
            outs[g][...] = proj.astype(BF16)
            return
        for i in range(FOX_W // MXU_TILE):
            cols = slice(i * MXU_TILE, (i + 1) * MXU_TILE)
            outs[g][:, cols] = _head_norm(
                proj[:, cols], bd, hg_ref[gains[g]:gains[g] + 1, cols]).astype(BF16)

    pending = [project(g) for g in range(PROJ_LAG)]

    z8 = _dot_nt(wf_ref[...], h)[0:FOX_HEADS, :] + bf_ref[...]
    y = -(jnp.maximum(-z8, 0.0) + jnp.log1p(jnp.exp(-jnp.abs(z8))))
    lane = lax.broadcasted_iota(jnp.int32, y.shape, 1)
    shift = 1
    while shift < rows:
        y = y + jnp.where(lane >= shift, pltpu.roll(y, shift, axis=1), 0.0)
        shift *= 2

    carry = jnp.where(t % tiles_per_seq == 0, 0.0, carry_ref[:, 0:1])
    c = y + carry
    carry_ref[...] = jnp.broadcast_to(c[:, rows - 1:rows], carry_ref.shape)
    cn = c * (-LOG2E)
    hi = cn.astype(BF16).astype(F32)
    r1 = cn - hi
    mid = r1.astype(BF16).astype(F32)
    lo = (r1 - mid).astype(BF16).astype(F32)
    half = jnp.concatenate([hi, mid, lo, jnp.zeros((HEAD_DIM - 3 * FOX_HEADS, rows), F32)], axis=0)
    caug_ref[...] = jnp.concatenate([half, half], axis=0).T.astype(BF16)

    for g in range(PROJ_LAG, len(outs)):
        pending.append(project(g))
        epilogue(*pending.pop(0))
    for p in pending:
        epilogue(*p)


def _proj0(x2d, gain, w, wf, bd, hg, bf, seq):
    n = x2d.shape[0]
    tm = ROW_TILE
    tps = seq // tm
    row_spec = pl.BlockSpec((tm, FOX_W), lambda t: (t, 0))
    col_spec = pl.BlockSpec((1, FOX_W, tm), lambda t: (t // tps, 0, t % tps))
    rows_shape = jax.ShapeDtypeStruct((n, FOX_W), BF16)
    cols_shape = jax.ShapeDtypeStruct((n // seq, FOX_W, seq), BF16)
    full = lambda shape: pl.BlockSpec(shape, lambda t: (0,) * len(shape))
    outs = pl.pallas_call(
        functools.partial(_proj0_kernel, tiles_per_seq=tps),
        grid=(n // tm,),
        in_specs=[
            pl.BlockSpec((tm, D_MODEL), lambda t: (t, 0)),
            full((1, D_MODEL)),
            full(w.shape),
            full(wf.shape),
            full((MXU_TILE, MXU_TILE)),
            full(hg.shape),
            full(bf.shape),
        ],
        out_specs=[row_spec, row_spec, col_spec, row_spec, row_spec, row_spec,
                   pl.BlockSpec((tm, LANES), lambda t: (t, 0))],
        out_shape=[rows_shape, rows_shape, cols_shape, rows_shape, rows_shape, rows_shape,
                   jax.ShapeDtypeStruct((n, LANES), BF16)],
        scratch_shapes=[pltpu.VMEM((FOX_HEADS, LANES), F32)],
        compiler_params=pltpu.CompilerParams(dimension_semantics=("arbitrary",),
                                             vmem_limit_bytes=VMEM_LIMIT),
        name="proj_even",
    )(x2d, gain, w, wf, bd, hg, bf)
    return outs


def _fox_kernel(q_ref, k_ref, vt_ref, caug_ref, o_ref, ka_ref, va_ref, *, seq):
    jj = pl.program_id(1)
    qb = ATT_Q_BLOCK
    low_s = lax.broadcasted_iota(jnp.int32, (seq, LANES), 1) < HEAD_DIM
    ca = caug_ref[0]
    sub = lax.broadcasted_iota(jnp.int32, (LANES - HEAD_DIM, seq), 0)
    ones_row = jnp.where(sub == 0, 1.0, 0.0).astype(BF16)
    for p in range(FOX_PAIRS_PER_STEP):
        kp = k_ref[0, :, p * LANES:(p + 1) * LANES]
        ka_ref[2 * p] = jnp.where(low_s, kp, ca)
        ka_ref[2 * p + 1] = jnp.where(low_s, ca, kp)
        for hh in range(2):
            r = p * LANES + hh * HEAD_DIM
            va_ref[2 * p + hh, 0:HEAD_DIM, :] = vt_ref[0, r:r + HEAD_DIM, :]
            va_ref[2 * p + hh, HEAD_DIM:LANES, :] = ones_row
    lane_q = lax.broadcasted_iota(jnp.int32, (qb, LANES), 1)
    key_i = lax.broadcasted_iota(jnp.int32, (qb, qb), 0)
    qry_i = lax.broadcasted_iota(jnp.int32, (qb, qb), 1)
    causal = key_i <= qry_i
    owns = (lane_q < HEAD_DIM, lane_q >= HEAD_DIM)

    def scores(p, qi, hh):
        r0 = qi * qb
        head = 2 * (FOX_PAIRS_PER_STEP * jj + p) + hh
        rel = lane_q - (HEAD_DIM if hh == 0 else 0) - head
        pick = (rel == 0) | (rel == FOX_HEADS) | (rel == 2 * FOX_HEADS)
        qz = jnp.where(owns[hh], q_ref[0, r0:r0 + qb, p * LANES:(p + 1) * LANES],
                       jnp.where(pick, 1.0, 0.0).astype(BF16))
        s_all =_dot_nt(ka_ref[2 * p + hh, 0:(qi + 1) * qb, :], qz)
        yield
        tiles = [s_all[t * qb:(t + 1) * qb] for t in range(qi)]
        tiles.append(jnp.where(causal, s_all[qi * qb:(qi + 1) * qb], NEG_INF))
        return p, qi, hh, tiles

    halves = []

    def finish(p, qi, hh, tiles):
        r0 = qi * qb
        m = jnp.max(functools.reduce(jnp.maximum, tiles), axis=0, keepdims=True)
        acc = None
        for t, s in enumerate(tiles):
            part = _dot(va_ref[2 * p + hh, :, t * qb:(t + 1) * qb], jnp.exp2(s - m).astype(BF16))
            acc = part if acc is None else acc + part
            yield
        halves.append(acc[0:HEAD_DIM] / acc[HEAD_DIM:HEAD_DIM + 1])
        if hh == 1:
            o_ref[0, r0:r0 + qb, p * LANES:(p + 1) * LANES] = jnp.concatenate(
                [halves[-2], halves[-1]], axis=0).T.astype(BF16)

    _interleaved([(p, qi, hh) for p in range(FOX_PAIRS_PER_STEP)
                  for qi in _hill_order(seq // qb) for hh in range(2)], scores, finish,
                 depth=FOX_LAG)


def _fox_attention(fq, fk, fvt, caug, batch, seq):
    width = FOX_PAIRS_PER_STEP * LANES
    blk = pl.BlockSpec((1, seq, width), lambda b, j: (b, 0, j))
    nheads = 2 * FOX_PAIRS_PER_STEP
    return pl.pallas_call(
        functools.partial(_fox_kernel, seq=seq),
        grid=(batch, FOX_W // width),
        in_specs=[blk, blk, pl.BlockSpec((1, width, seq), lambda b, j: (b, j, 0)),
                  pl.BlockSpec((1, seq, LANES), lambda b, j: (b, 0, 0))],
        out_specs=blk,
        out_shape=jax.ShapeDtypeStruct((batch, seq, FOX_W), BF16),
        scratch_shapes=[pltpu.VMEM((nheads, seq, LANES), BF16),
                        pltpu.VMEM((nheads, LANES, seq), BF16)],
        compiler_params=pltpu.CompilerParams(dimension_semantics=("parallel", "parallel"),
                                             vmem_limit_bytes=VMEM_LIMIT),
        name="fox_attention",
    )(fq, fk, fvt, caug)


def _diff_kernel(lam_ref, g_ref, q_ref, k_ref, v_ref, gb_ref, o_ref, va_ref, *, seq, lambda_init):
    qb = ATT_Q_BLOCK
    lv = lam_ref[...]
    lam = (jnp.exp(jnp.sum(lv[0:1] * lv[1:2], axis=-1, keepdims=True))
           - jnp.exp(jnp.sum(lv[2:3] * lv[3:4], axis=-1, keepdims=True)) + lambda_init)
    for p in range(DIFF_HEADS_PER_STEP):
        va_ref[p, :, 0:LANES] = v_ref[0, :, p * LANES:(p + 1) * LANES]
        va_ref[p, :, LANES:2 * LANES] = jnp.ones((seq, LANES), BF16)
    lane_q = lax.broadcasted_iota(jnp.int32, (qb, LANES), 1)
    bias_prev = [jnp.concatenate([gb_ref[p, :, 0:qb]] * 2, axis=0)
                 for p in range(DIFF_HEADS_PER_STEP)]
    bias_diag = [jnp.concatenate([gb_ref[p, :, qb:2 * qb]] * 2, axis=0)
                 for p in range(DIFF_HEADS_PER_STEP)]
    gain = g_ref[...] * (1.0 - lambda_init)
    zero = jnp.zeros((), BF16)

    def scores(p, qi):
        r0 = qi * qb
        qp = q_ref[0, r0:r0 + qb, p * LANES:(p + 1) * LANES]
        qz = jnp.concatenate([jnp.where(lane_q < HEAD_DIM, qp, zero),
                              jnp.where(lane_q < HEAD_DIM, zero, qp)], axis=0)
        tiles = []
        for t in range(qi + 1):
            s = _dot_nt(qz, k_ref[0, t * qb:(t + 1) * qb, p * LANES:(p + 1) * LANES])
            tiles.append(s + bias_diag[p] if t == qi else s + bias_prev[p] if t == qi - 1 else s)
            yield
        return p, qi, tiles

    def finish(p, qi, tiles):
        r0 = qi * qb
        m = jnp.max(functools.reduce(jnp.maximum, tiles), axis=-1, keepdims=True)
        acc = None
        for t, s in enumerate(tiles):
            part = _dot(jnp.exp2(s - m).astype(BF16), va_ref[p, t * qb:(t + 1) * qb, :])
            acc = part if acc is None else acc + part
            yield
        a = acc[:, 0:LANES] / acc[:, LANES:2 * LANES]
        o = a[0:qb] - lam * a[qb:2 * qb]
        o_ref[0, r0:r0 + qb, p * LANES:(p + 1) * LANES] = _rms(o, gain).astype(BF16)

    _interleaved([(p, qi) for p in range(DIFF_HEADS_PER_STEP) for qi in _hill_order(seq // qb)],
                 scores, finish, depth=DIFF_LAG)


def _diff_attention(lam4, subln, dq, dk, dv, gbias, batch, seq, lambda_init):
    hps = DIFF_HEADS_PER_STEP
    blk = pl.BlockSpec((1, seq, hps * LANES), lambda h, b: (b, 0, h))
    return pl.pallas_call(
        functools.partial(_diff_kernel, seq=seq, lambda_init=lambda_init),
        grid=(DIFF_HEADS // hps, batch),
        in_specs=[
            pl.BlockSpec(lam4.shape, lambda h, b: (0, 0)),
            pl.BlockSpec(subln.shape, lambda h, b: (0, 0)),
            blk, blk, blk,
            pl.BlockSpec((hps, ATT_Q_BLOCK, 2 * ATT_Q_BLOCK), lambda h, b: (h, 0, 0)),
        ],
        out_specs=blk,
        out_shape=jax.ShapeDtypeStruct((batch, seq, DIFF_W), BF16),
        scratch_shapes=[pltpu.VMEM((hps, seq, 2 * LANES), BF16)],
        compiler_params=pltpu.CompilerParams(dimension_semantics=("parallel", "parallel"),
                                             vmem_limit_bytes=VMEM_LIMIT),
        name="diff_attention",
    )(lam4, subln, dq, dk, dv, gbias)


def _proj1_kernel(x_ref, gain_ref, w_ref, bd_ref, hg_ref, q_ref, kk_ref, vt_ref):
    h = _rms(x_ref[...], gain_ref[...]).astype(BF16)
    bd = bd_ref[...]
    qw = SWA_HEADS * HEAD_DIM
    kw = SWA_KV_HEADS * HEAD_DIM
    low = lax.broadcasted_iota(jnp.int32, (h.shape[0], LANES), 1) < HEAD_DIM

    def project(c0):
        return c0, _dot(h, w_ref[:, c0:c0 + 2 * MXU_TILE])

    def epilogue(c0, proj):
        if c0 < qw:
            for i in range(2):
                cols = slice(i * MXU_TILE, (i + 1) * MXU_TILE)
                q_ref[:, c0 + i * MXU_TILE:c0 + (i + 1) * MXU_TILE] = _head_norm(
                    proj[:, cols], bd, hg_ref[0:1, :]).astype(BF16)
            return
        kn = _head_norm(proj[:, 0:kw], bd, hg_ref[1:2, :])
        for p in range(kw // LANES):
            pair = kn[:, p * LANES:(p + 1) * LANES]
            swapped = pltpu.roll(pair, HEAD_DIM, axis=1)
            kk_ref[:, (2 * p) * LANES:(2 * p + 1) * LANES] = jnp.where(low, pair, swapped).astype(BF16)
            kk_ref[:, (2 * p + 1) * LANES:(2 * p + 2) * LANES] = jnp.where(low, swapped, pair).astype(BF16)
        vt_ref[0] = proj[:, kw:2 * kw].T.astype(BF16)

    _lagged([(qw,)] + [(c0,) for c0 in range(0, qw, 2 * MXU_TILE)], project, epilogue,
            depth=PROJ_LAG)


def _proj1(x2d, gain, w, bd, hg, seq):
    n = x2d.shape[0]
    tm = ROW_TILE
    tps = seq // tm
    qw = SWA_HEADS * HEAD_DIM
    kw = SWA_KV_HEADS * HEAD_DIM
    full = lambda shape: pl.BlockSpec(shape, lambda t: (0,) * len(shape))
    return pl.pallas_call(
        _proj1_kernel,
        grid=(n // tm,),
        in_specs=[
            pl.BlockSpec((tm, D_MODEL), lambda t: (t, 0)),
            full((1, D_MODEL)),
            full(w.shape),
            full((MXU_TILE, MXU_TILE)),
            full(hg.shape),
        ],
        out_specs=[pl.BlockSpec((tm, qw), lambda t: (t, 0)),
                   pl.BlockSpec((tm, 2 * kw), lambda t: (t, 0)),
                   pl.BlockSpec((1, kw, tm), lambda t: (t // tps, 0, t % tps))],
        out_shape=[jax.ShapeDtypeStruct((n, qw), BF16),
                   jax.ShapeDtypeStruct((n, 2 * kw), BF16),
                   jax.ShapeDtypeStruct((n // seq, kw, seq), BF16)],
        compiler_params=pltpu.CompilerParams(dimension_semantics=("parallel",),
                                             vmem_limit_bytes=VMEM_LIMIT),
        name="proj_odd",
    )(x2d, gain, w, bd, hg)


def _swa_kernel(sink_ref, q_ref, kk_ref, vt_ref, sb_ref, o_ref, va_ref, *, seq):
    jj = pl.program_id(0)
    w = WINDOW
    gw = SWA_GROUP * HEAD_DIM
    sub = lax.broadcasted_iota(jnp.int32, (LANES - HEAD_DIM, seq), 0)
    sinks = []
    for p in range(SWA_KV_PER_STEP):
        va_ref[p, 0:HEAD_DIM, :] = vt_ref[0, p * HEAD_DIM:(p + 1) * HEAD_DIM, :]
        va_ref[p, HEAD_DIM:LANES, :] = jnp.where(sub == 0, 1.0, 0.0).astype(BF16)
        head0 = (jj * SWA_KV_PER_STEP + p) * SWA_GROUP
        sinks.append(jnp.concatenate(
            [jnp.full((1, w), sink_ref[head0 + g] * LOG2E, F32) for g in range(SWA_GROUP)],
            axis=1))
    low = lax.broadcasted_iota(jnp.int32, (w, LANES), 1) < HEAD_DIM
    zero = jnp.zeros((), BF16)

    def scores(p, n):
        r0 = n * w
        qb = q_ref[0, r0:r0 + w, p * gw:(p + 1) * gw]
        qa, qc = qb[:, 0:LANES], qb[:, LANES:2 * LANES]
        q4 = jnp.concatenate([jnp.where(low, qa, zero), jnp.where(low, zero, qa),
                              jnp.where(low, qc, zero), jnp.where(low, zero, qc)], axis=0)
        lanes = slice(p * LANES, (p + 1) * LANES)
        if n == 0:
            return p, n, _dot_nt(kk_ref[0, 0:w, lanes], q4) + sb_ref[p, w:2 * w, :]
        return p, n, _dot_nt(kk_ref[0, r0 - w:r0 + w, lanes], q4) + sb_ref[p]

    def finish(p, n, st):
        r0 = n * w
        vals = va_ref[p, :, 0:w] if n == 0 else va_ref[p, :, r0 - w:r0 + w]
        m = jnp.maximum(jnp.max(st, axis=0, keepdims=True), sinks[p])
        acc = _dot(vals, jnp.exp2(st - m).astype(BF16))
        denom = acc[HEAD_DIM:HEAD_DIM + 1, :] + jnp.exp2(sinks[p] - m)
        ot = acc[0:HEAD_DIM, :] / denom
        o_ref[0, r0:r0 + w, p * gw:(p + 1) * gw] = jnp.concatenate(
            [jnp.concatenate([ot[:, 0:w], ot[:, w:2 * w]], axis=0).T,
             jnp.concatenate([ot[:, 2 * w:3 * w], ot[:, 3 * w:4 * w]], axis=0).T],
            axis=1).astype(BF16)

    _lagged([(p, n) for p in range(SWA_KV_PER_STEP) for n in range(seq // w)], scores, finish,
            depth=SWA_LAG)


def _swa_attention(sinks, q, kk, vt, swbias, batch, seq):
    kps = SWA_KV_PER_STEP
    gw = kps * SWA_GROUP * HEAD_DIM
    return pl.pallas_call(
        functools.partial(_swa_kernel, seq=seq),
        grid=(SWA_KV_HEADS // kps, batch),
        in_specs=[
            pl.BlockSpec(memory_space=pltpu.SMEM),
            pl.BlockSpec((1, seq, gw), lambda h, b: (b, 0, h)),
            pl.BlockSpec((1, seq, kps * LANES), lambda h, b: (b, 0, h)),
            pl.BlockSpec((1, kps * HEAD_DIM, seq), lambda h, b: (b, h, 0)),
            pl.BlockSpec((kps, 2 * WINDOW, SWA_GROUP * WINDOW), lambda h, b: (h, 0, 0)),
        ],
        out_specs=pl.BlockSpec((1, seq, gw), lambda h, b: (b, 0, h)),
        out_shape=jax.ShapeDtypeStruct((batch, seq, SWA_HEADS * HEAD_DIM), BF16),
        scratch_shapes=[pltpu.VMEM((kps, LANES, seq), BF16)],
        compiler_params=pltpu.CompilerParams(dimension_semantics=("parallel", "parallel"),
                                             vmem_limit_bytes=VMEM_LIMIT),
        name="swa_attention",
    )(sinks, q, kk, vt, swbias)


def _outmlp_kernel(*refs, n_mix):
    x_ref = refs[0]
    a_refs = refs[1:1 + n_mix]
    wo_ref, gain_ref, wg_ref, wu_ref, wd_ref, o_ref = refs[1 + n_mix:]
    mix = jnp.concatenate([a_ref[...] for a_ref in a_refs], axis=1)
    x1 = x_ref[...] + _dot(mix, wo_ref[...])
    h = _rms(x1, gain_ref[...]).astype(BF16)
    out = x1
    for c0 in range(0, FFN_HIDDEN, HIDDEN_CHUNK):
        c1 = min(c0 + HIDDEN_CHUNK, FFN_HIDDEN)
        g = _dot(h, wg_ref[:, c0:c1])
        u = _dot(h, wu_ref[:, c0:c1])
        y = (g * (1.0 / (1.0 + jnp.exp(-g))) * u).astype(BF16)
        out = out + _dot(y, wd_ref[c0:c1, :])
    o_ref[...] = out


def _outmlp(x2d, mixes, wo, gain, wg, wu, wd, layer):
    n = x2d.shape[0]
    tm = ROW_TILE
    full = lambda shape: pl.BlockSpec(shape, lambda t: (0,) * len(shape),
                                      pipeline_mode=pl.Buffered(1))
    of_layer = lambda w: pl.BlockSpec((None,) + w.shape[1:], lambda t: (layer, 0, 0),
                                      pipeline_mode=pl.Buffered(1))
    in_specs = [pl.BlockSpec((tm, D_MODEL), lambda t: (t, 0))]
    in_specs += [pl.BlockSpec((tm, a.shape[1]), lambda t: (t, 0)) for a in mixes]
    in_specs += [full(wo.shape), full(gain.shape), of_layer(wg), of_layer(wu), of_layer(wd)]
    return pl.pallas_call(
        functools.partial(_outmlp_kernel, n_mix=len(mixes)),
        grid=(n // tm,),
        in_specs=in_specs,
        out_specs=pl.BlockSpec((tm, D_MODEL), lambda t: (t, 0)),
        out_shape=jax.ShapeDtypeStruct((n, D_MODEL), F32),
        compiler_params=pltpu.CompilerParams(dimension_semantics=("parallel",),
                                             vmem_limit_bytes=VMEM_LIMIT),
        name="outproj_swiglu",
    )(x2d, *mixes, wo, gain, wg, wu, wd)


def _block_diag_mean():
    i = np.arange(MXU_TILE)
    same = (i[:, None] // HEAD_DIM) == (i[None, :] // HEAD_DIM)
    return jnp.asarray(same / HEAD_DIM, BF16)


def kernel(x, rel_bias_table, ev_attn_norm, ev_w_in, ev_b_forget, ev_fox_q_norm, ev_fox_k_norm,
           ev_diff_q_norm, ev_diff_k_norm, ev_lambda_q1, ev_lambda_k1, ev_lambda_q2, ev_lambda_k2,
           ev_diff_subln, ev_w_out, od_attn_norm, od_w_qkv, od_q_norm, od_k_norm, od_sinks, od_w_out,
           ffn_norm, w_gate, w_up, w_down):
    batch, seq, d = x.shape
    n = batch * seq
    x2d = x.reshape(n, d)
    bd = _block_diag_mean()
    wg, wu, wd = w_gate.astype(BF16), w_up.astype(BF16), w_down.astype(BF16)
    swbias, gbias = _bias_tiles(rel_bias_table.astype(F32))
    swbias = swbias.reshape(SWA_KV_HEADS, SWA_GROUP, 2 * WINDOW, WINDOW).transpose(0, 2, 1, 3)
    swbias = swbias.reshape(SWA_KV_HEADS, 2 * WINDOW, SWA_GROUP * WINDOW)

    lambda_init = 0.8 - 0.6 * math.exp(-0.3 * 0)
    w_in = ev_w_in[0]
    ff0 = 3 * FOX_W
    d0 = ff0 + FOX_HEADS
    w0 = jnp.concatenate([w_in[:, 0:ff0], w_in[:, d0:d0 + 3 * DIFF_W]], axis=1).astype(BF16)
    wf = jnp.concatenate(
        [w_in[:, ff0:d0].T, jnp.zeros((BF16_SUBLANES - FOX_HEADS, d), w_in.dtype)],
        axis=0).astype(BF16)
    tile8 = lambda g, scale=1.0: jnp.tile(g.astype(F32) * scale, FOX_W // HEAD_DIM)
    hg0 = jnp.stack([tile8(ev_fox_q_norm[0], Q_GAIN_SCALE), tile8(ev_fox_k_norm[0]),
                     tile8(ev_diff_q_norm[0], Q_GAIN_SCALE), tile8(ev_diff_k_norm[0])])
    bf = ev_b_forget[0].astype(F32).reshape(FOX_HEADS, 1)
    fq, fk, fvt, dq, dk, dv, caug = _proj0(
        x2d, ev_attn_norm[0].astype(F32).reshape(1, d), w0, wf, bd, hg0, bf, seq)
    r3 = lambda a: a.reshape(batch, seq, a.shape[-1])
    fox = _fox_attention(r3(fq), r3(fk), fvt, r3(caug), batch, seq)
    lam4 = jnp.stack([ev_lambda_q1[0], ev_lambda_k1[0], ev_lambda_q2[0], ev_lambda_k2[0]]).astype(F32)
    dout = _diff_attention(lam4, ev_diff_subln[0].astype(F32).reshape(1, 2 * HEAD_DIM),
                           r3(dq), r3(dk), r3(dv), gbias, batch, seq, lambda_init)
    x2d = _outmlp(x2d, [fox.reshape(n, FOX_W), dout.reshape(n, DIFF_W)], ev_w_out[0].astype(BF16),
                  ffn_norm[0].astype(F32).reshape(1, d), wg, wu, wd, 0)

    tile4 = lambda g, scale=1.0: jnp.tile(g.astype(F32) * scale, MXU_TILE // HEAD_DIM)
    hg1 = jnp.stack([tile4(od_q_norm[0], Q_GAIN_SCALE), tile4(od_k_norm[0])])
    q, kk, vt = _proj1(x2d, od_attn_norm[0].astype(F32).reshape(1, d), od_w_qkv[0].astype(BF16),
                       bd, hg1, seq)
    swa = _swa_attention(od_sinks[0].astype(F32), r3(q), r3(kk), vt, swbias, batch, seq)
    x2d = _outmlp(x2d, [swa.reshape(n, SWA_HEADS * HEAD_DIM)], od_w_out[0].astype(BF16),
                  ffn_norm[1].astype(F32).reshape(1, d), wg, wu, wd, 1)
    return x2d.reshape(batch, seq, d)
```

```python
import functools
import math

import numpy as np
import jax
import jax.numpy as jnp
from jax import lax
from jax.experimental import pallas as pl
from jax.experimental.pallas import tpu as pltpu

F32 = jnp.float32
BF16 = jnp.bfloat16

D_MODEL = 1024
HEAD_DIM = 64
FOX_HEADS = 8
DIFF_HEADS = 4
SWA_HEADS = 16
SWA_KV_HEADS = 4
SWA_GROUP = SWA_HEADS // SWA_KV_HEADS
WINDOW = 128
NUM_BUCKETS = 32
MAX_DISTANCE = 128
FFN_HIDDEN = 2816
RMS_EPS = 1e-6
NEG_INF = -1e30
FOX_W = FOX_HEADS * HEAD_DIM
DIFF_W = DIFF_HEADS * 2 * HEAD_DIM
QK_SCALE = HEAD_DIM ** -0.5
LOG2E = math.log2(math.e)
Q_GAIN_SCALE = QK_SCALE * LOG2E

LANES = 128
BF16_SUBLANES = 16
MXU_TILE = 256
ROW_TILE = 1024
ATT_Q_BLOCK = 256
FOX_PAIRS_PER_STEP = 2
DIFF_HEADS_PER_STEP = 2
SWA_KV_PER_STEP = 2
FOX_LAG = 3
DIFF_LAG = 2
PROJ_LAG = 1
SWA_LAG = 4
HIDDEN_CHUNK = 768
V7X_VMEM_BYTES = 64 * 1024 * 1024
VMEM_LIMIT = V7X_VMEM_BYTES - 8 * 1024 * 1024

_NT = (((1,), (1,)), ((), ()))


def _dot(a, b):
    return jnp.dot(a, b, preferred_element_type=F32)


def _dot_nt(a, b):
    return lax.dot_general(a, b, _NT, preferred_element_type=F32)


def _lagged(items, first, second, depth=1):
    pending = []
    for item in items:
        pending.append(first(*item))
        if len(pending) > depth:
            second(*pending.pop(0))
    for p in pending:
        second(*p)


def _interleaved(items, first, second, depth):
    def drive(gens):
        done = {}
        while gens:
            for g in list(gens):
                try:
                    next(g)
                except StopIteration as stop:
                    done[id(g)] = stop.value
                    gens.remove(g)
        return done

    pending = []
    for item in items:
        g1 = first(*item)
        gens = [g1] if len(pending) < depth else [g1, second(*pending.pop(0))]
        pending.append(drive(gens)[id(g1)])
    for p in pending:
        drive([second(*p)])


def _hill_order(nblocks):
    return list(range(0, nblocks, 2)) + list(range(nblocks - 1 - nblocks % 2, 0, -2))


def _rms(xf, gain):
    ms = jnp.mean(xf * xf, axis=-1, keepdims=True)
    return xf * lax.rsqrt(ms + RMS_EPS) * gain


def _head_norm(acc, bd, gain):
    msq = _dot((acc * acc).astype(BF16), bd)
    return acc * lax.rsqrt(msq + RMS_EPS) * gain


def _bucket_ids(delta):
    n = np.maximum(delta, 0)
    max_exact = NUM_BUCKETS // 2
    nf = np.maximum(n, 1).astype(np.float64)
    large = max_exact + (np.log(nf / max_exact) / math.log(MAX_DISTANCE / max_exact)
                         * (NUM_BUCKETS - max_exact)).astype(np.int32)
    large = np.minimum(large, NUM_BUCKETS - 1)
    return np.where(n < max_exact, n, large).astype(np.int32)


def _bias_kernel(table_ref, bsw_ref, bg_ref, sw_ref, g_ref):
    h = pl.program_id(0)
    bsw = bsw_ref[...]
    sw = jnp.zeros(bsw.shape, F32)
    for b in range(NUM_BUCKETS):
        sw = jnp.where(bsw == b, table_ref[b, h] * LOG2E, sw)
    sw_ref[0] = jnp.where(bsw < 0, NEG_INF, sw)

    @pl.when(h < DIFF_HEADS)
    def _():
        bg = bg_ref[...]
        far = table_ref[NUM_BUCKETS - 1, h]
        g = jnp.zeros(bg.shape, F32)
        for b in range(NUM_BUCKETS):
            g = jnp.where(bg == b, (table_ref[b, h] - far) * LOG2E, g)
        g_ref[0] = jnp.where(bg < 0, NEG_INF, g)


def _bias_tiles(table):
    a = np.arange(WINDOW)[None, :]
    b = np.arange(2 * WINDOW)[:, None]
    d_sw = WINDOW + a - b
    bsw = np.where((d_sw >= 0) & (d_sw < WINDOW), _bucket_ids(d_sw), -1).astype(np.int32)
    a = np.arange(ATT_Q_BLOCK)[:, None]
    b = np.arange(2 * ATT_Q_BLOCK)[None, :]
    d_g = ATT_Q_BLOCK + a - b
    bg = np.where(d_g >= 0, _bucket_ids(d_g), -1).astype(np.int32)
    nh = table.shape[1]
    return pl.pallas_call(
        _bias_kernel,
        grid=(nh,),
        in_specs=[
            pl.BlockSpec(memory_space=pltpu.SMEM),
            pl.BlockSpec(bsw.shape, lambda h: (0, 0)),
            pl.BlockSpec(bg.shape, lambda h: (0, 0)),
        ],
        out_specs=[
            pl.BlockSpec((1,) + bsw.shape, lambda h: (h, 0, 0)),
            pl.BlockSpec((1,) + bg.shape, lambda h: (jnp.minimum(h, DIFF_HEADS - 1), 0, 0)),
        ],
        out_shape=[
            jax.ShapeDtypeStruct((nh,) + bsw.shape, F32),
            jax.ShapeDtypeStruct((DIFF_HEADS,) + bg.shape, F32),
        ],
        compiler_params=pltpu.CompilerParams(dimension_semantics=("arbitrary",)),
        name="rel_bias_tiles",
    )(table, jnp.asarray(bsw), jnp.asarray(bg))


def _proj0_kernel(*refs, tiles_per_seq, n_cast):
    x_ref, gain_ref, w_ref, wf_ref, bd_ref, hg_ref, bf_ref = refs[0:7]
    cast_src = refs[7:7 + n_cast]
    fq_ref, fk_ref, fv_ref, dq_ref, dk_ref, dv_ref, caug_ref = refs[7 + n_cast:14 + n_cast]
    cast_dst = refs[14 + n_cast:14 + 2 * n_cast]
    carry_ref = refs[14 + 2 * n_cast]
    t = pl.program_id(0)

    @pl.when(t == 0)
    def _():
        carry_ref[...] = jnp.zeros_like(carry_ref)

    h = _rms(x_ref[...], gain_ref[...]).astype(BF16)
    bd = bd_ref[...]
    rows = h.shape[0]

    outs = (fq_ref, fk_ref, fv_ref, dq_ref, dk_ref, dv_ref)
    gains = (0, 1, None, 2, 3, None)
    feature_major = (False, False, True, False, False, False)

    def project(g):
        return g, _dot(h, w_ref[:, g * FOX_W:(g + 1) * FOX_W])

    def epilogue(g, proj):
        if feature_major[g]:
            outs[g][0] = proj.T.astype(BF16)
            return
        if gains[g] is None:
            outs[g][...] = proj.astype(BF16)
            return
        for i in range(FOX_W // MXU_TILE):
            cols = slice(i * MXU_TILE, (i + 1) * MXU_TILE)
            outs[g][:, cols] = _head_norm(
                proj[:, cols], bd, hg_ref[gains[g]:gains[g] + 1, cols]).astype(BF16)

    pending = [project(g) for g in range(PROJ_LAG)]

    z8 = _dot_nt(wf_ref[...], h)[0:FOX_HEADS, :] + bf_ref[...]
    y = -(jnp.maximum(-z8, 0.0) + jnp.log1p(jnp.exp(-jnp.abs(z8))))
    lane = lax.broadcasted_iota(jnp.int32, y.shape, 1)
    shift = 1
    while shift < rows:
        y = y + jnp.where(lane >= shift, pltpu.roll(y, shift, axis=1), 0.0)
        shift *= 2

    carry = jnp.where(t % tiles_per_seq == 0, 0.0, carry_ref[:, 0:1])
    c = y + carry
    carry_ref[...] = jnp.broadcast_to(c[:, rows - 1:rows], carry_ref.shape)
    cn = c * (-LOG2E)
    hi = cn.astype(BF16).astype(F32)
    r1 = cn - hi
    mid = r1.astype(BF16).astype(F32)
    lo = (r1 - mid).astype(BF16).astype(F32)
    half = jnp.concatenate([hi, mid, lo, jnp.zeros((HEAD_DIM - 3 * FOX_HEADS, rows), F32)], axis=0)
    caug_ref[...] = jnp.concatenate([half, half], axis=0).T.astype(BF16)

    for g in range(PROJ_LAG, len(outs)):
        pending.append(project(g))
        epilogue(*pending.pop(0))
    for p in pending:
        epilogue(*p)

    for src, dst in zip(cast_src, cast_dst):
        dst[...] = src[...].astype(BF16)


def _proj0(x2d, gain, w, wf, bd, hg, bf, seq, later_weights):
    n = x2d.shape[0]
    tm = ROW_TILE
    tps = seq // tm
    steps = n // tm
    slab = lambda a: pl.BlockSpec((a.shape[0] // steps, a.shape[1]), lambda t: (t, 0))
    for a in later_weights:
        assert a.shape[0] % (steps * BF16_SUBLANES) == 0, a.shape
    row_spec = pl.BlockSpec((tm, FOX_W), lambda t: (t, 0))
    col_spec = pl.BlockSpec((1, FOX_W, tm), lambda t: (t // tps, 0, t % tps))
    rows_shape = jax.ShapeDtypeStruct((n, FOX_W), BF16)
    cols_shape = jax.ShapeDtypeStruct((n // seq, FOX_W, seq), BF16)
    full = lambda shape: pl.BlockSpec(shape, lambda t: (0,) * len(shape))
    outs = pl.pallas_call(
        functools.partial(_proj0_kernel, tiles_per_seq=tps, n_cast=len(later_weights)),
        grid=(steps,),
        in_specs=[
            pl.BlockSpec((tm, D_MODEL), lambda t: (t, 0)),
            full((1, D_MODEL)),
            full(w.shape),
            full(wf.shape),
            full((MXU_TILE, MXU_TILE)),
            full(hg.shape),
            full(bf.shape),
        ] + [slab(a) for a in later_weights],
        out_specs=[row_spec, row_spec, col_spec, row_spec, row_spec, row_spec,
                   pl.BlockSpec((tm, LANES), lambda t: (t, 0))] + [slab(a) for a in later_weights],
        out_shape=[rows_shape, rows_shape, cols_shape, rows_shape, rows_shape, rows_shape,
                   jax.ShapeDtypeStruct((n, LANES), BF16)]
        + [jax.ShapeDtypeStruct(a.shape, BF16) for a in later_weights],
        scratch_shapes=[pltpu.VMEM((FOX_HEADS, LANES), F32)],
        compiler_params=pltpu.CompilerParams(dimension_semantics=("arbitrary",),
                                             vmem_limit_bytes=VMEM_LIMIT),
        name="proj_even",
    )(x2d, gain, w, wf, bd, hg, bf, *later_weights)
    return outs


def _fox_kernel(q_ref, k_ref, vt_ref, caug_ref, o_ref, ka_ref, va_ref, *, seq):
    jj = pl.program_id(1)
    qb = ATT_Q_BLOCK
    low_s = lax.broadcasted_iota(jnp.int32, (seq, LANES), 1) < HEAD_DIM
    ca = caug_ref[0]
    sub = lax.broadcasted_iota(jnp.int32, (LANES - HEAD_DIM, seq), 0)
    ones_row = jnp.where(sub == 0, 1.0, 0.0).astype(BF16)
    for p in range(FOX_PAIRS_PER_STEP):
        kp = k_ref[0, :, p * LANES:(p + 1) * LANES]
        ka_ref[2 * p] = jnp.where(low_s, kp, ca)
        ka_ref[2 * p + 1] = jnp.where(low_s, ca, kp)
        for hh in range(2):
            r = p * LANES + hh * HEAD_DIM
            va_ref[2 * p + hh, 0:HEAD_DIM, :] = vt_ref[0, r:r + HEAD_DIM, :]
            va_ref[2 * p + hh, HEAD_DIM:LANES, :] = ones_row
    lane_q = lax.broadcasted_iota(jnp.int32, (qb, LANES), 1)
    key_i = lax.broadcasted_iota(jnp.int32, (qb, qb), 0)
    qry_i = lax.broadcasted_iota(jnp.int32, (qb, qb), 1)
    causal = key_i <= qry_i
    owns = (lane_q < HEAD_DIM, lane_q >= HEAD_DIM)

    def scores(p, qi, hh):
        r0 = qi * qb
        head = 2 * (FOX_PAIRS_PER_STEP * jj + p) + hh
        rel = lane_q - (HEAD_DIM if hh == 0 else 0) - head
        pick = (rel == 0) | (rel == FOX_HEADS) | (rel == 2 * FOX_HEADS)
        qz = jnp.where(owns[hh], q_ref[0, r0:r0 + qb, p * LANES:(p + 1) * LANES],
                       jnp.where(pick, 1.0, 0.0).astype(BF16))
        s_all =_dot_nt(ka_ref[2 * p + hh, 0:(qi + 1) * qb, :], qz)
        yield
        tiles = [s_all[t * qb:(t + 1) * qb] for t in range(qi)]
        tiles.append(jnp.where(causal, s_all[qi * qb:(qi + 1) * qb], NEG_INF))
        return p, qi, hh, tiles

    halves = []

    def finish(p, qi, hh, tiles):
        r0 = qi * qb
        m = jnp.max(functools.reduce(jnp.maximum, tiles), axis=0, keepdims=True)
        acc = None
        for t, s in enumerate(tiles):
            part = _dot(va_ref[2 * p + hh, :, t * qb:(t + 1) * qb], jnp.exp2(s - m).astype(BF16))
            acc = part if acc is None else acc + part
            yield
        halves.append(acc[0:HEAD_DIM] / acc[HEAD_DIM:HEAD_DIM + 1])
        if hh == 1:
            o_ref[0, r0:r0 + qb, p * LANES:(p + 1) * LANES] = jnp.concatenate(
                [halves[-2], halves[-1]], axis=0).T.astype(BF16)

    _interleaved([(p, qi, hh) for p in range(FOX_PAIRS_PER_STEP)
                  for qi in _hill_order(seq // qb) for hh in range(2)], scores, finish,
                 depth=FOX_LAG)


def _fox_attention(fq, fk, fvt, caug, batch, seq):
    width = FOX_PAIRS_PER_STEP * LANES
    blk = pl.BlockSpec((1, seq, width), lambda b, j: (b, 0, j))
    nheads = 2 * FOX_PAIRS_PER_STEP
    return pl.pallas_call(
        functools.partial(_fox_kernel, seq=seq),
        grid=(batch, FOX_W // width),
        in_specs=[blk, blk, pl.BlockSpec((1, width, seq), lambda b, j: (b, j, 0)),
                  pl.BlockSpec((1, seq, LANES), lambda b, j: (b, 0, 0))],
        out_specs=blk,
        out_shape=jax.ShapeDtypeStruct((batch, seq, FOX_W), BF16),
        scratch_shapes=[pltpu.VMEM((nheads, seq, LANES), BF16),
                        pltpu.VMEM((nheads, LANES, seq), BF16)],
        compiler_params=pltpu.CompilerParams(dimension_semantics=("parallel", "parallel"),
                                             vmem_limit_bytes=VMEM_LIMIT),
        name="fox_attention",
    )(fq, fk, fvt, caug)


def _diff_kernel(lam_ref, g_ref, q_ref, k_ref, v_ref, gb_ref, o_ref, va_ref, *, seq, lambda_init):
    qb = ATT_Q_BLOCK
    lv = lam_ref[...]
    lam = (jnp.exp(jnp.sum(lv[0:1] * lv[1:2], axis=-1, keepdims=True))
           - jnp.exp(jnp.sum(lv[2:3] * lv[3:4], axis=-1, keepdims=True)) + lambda_init)
    for p in range(DIFF_HEADS_PER_STEP):
        va_ref[p, :, 0:LANES] = v_ref[0, :, p * LANES:(p + 1) * LANES]
        va_ref[p, :, LANES:2 * LANES] = jnp.ones((seq, LANES), BF16)
    lane_q = lax.broadcasted_iota(jnp.int32, (qb, LANES), 1)
    bias_prev = [jnp.concatenate([gb_ref[p, :, 0:qb]] * 2, axis=0)
                 for p in range(DIFF_HEADS_PER_STEP)]
    bias_diag = [jnp.concatenate([gb_ref[p, :, qb:2 * qb]] * 2, axis=0)
                 for p in range(DIFF_HEADS_PER_STEP)]
    gain = g_ref[...] * (1.0 - lambda_init)
    zero = jnp.zeros((), BF16)

    def scores(p, qi):
        r0 = qi * qb
        qp = q_ref[0, r0:r0 + qb, p * LANES:(p + 1) * LANES]
        qz = jnp.concatenate([jnp.where(lane_q < HEAD_DIM, qp, zero),
                              jnp.where(lane_q < HEAD_DIM, zero, qp)], axis=0)
        tiles = []
        for t in range(qi + 1):
            s = _dot_nt(qz, k_ref[0, t * qb:(t + 1) * qb, p * LANES:(p + 1) * LANES])
            tiles.append(s + bias_diag[p] if t == qi else s + bias_prev[p] if t == qi - 1 else s)
            yield
        return p, qi, tiles

    def finish(p, qi, tiles):
        r0 = qi * qb
        m = jnp.max(functools.reduce(jnp.maximum, tiles), axis=-1, keepdims=True)
        acc = None
        for t, s in enumerate(tiles):
            part = _dot(jnp.exp2(s - m).astype(BF16), va_ref[p, t * qb:(t + 1) * qb, :])
            acc = part if acc is None else acc + part
            yield
        a = acc[:, 0:LANES] / acc[:, LANES:2 * LANES]
        o = a[0:qb] - lam * a[qb:2 * qb]
        o_ref[0, r0:r0 + qb, p * LANES:(p + 1) * LANES] = _rms(o, gain).astype(BF16)

    _interleaved([(p, qi) for p in range(DIFF_HEADS_PER_STEP) for qi in _hill_order(seq // qb)],
                 scores, finish, depth=DIFF_LAG)


def _diff_attention(lam4, subln, dq, dk, dv, gbias, batch, seq, lambda_init):
    hps = DIFF_HEADS_PER_STEP
    blk = pl.BlockSpec((1, seq, hps * LANES), lambda h, b: (b, 0, h))
    return pl.pallas_call(
        functools.partial(_diff_kernel, seq=seq, lambda_init=lambda_init),
        grid=(DIFF_HEADS // hps, batch),
        in_specs=[
            pl.BlockSpec(lam4.shape, lambda h, b: (0, 0)),
            pl.BlockSpec(subln.shape, lambda h, b: (0, 0)),
            blk, blk, blk,
            pl.BlockSpec((hps, ATT_Q_BLOCK, 2 * ATT_Q_BLOCK), lambda h, b: (h, 0, 0)),
        ],
        out_specs=blk,
        out_shape=jax.ShapeDtypeStruct((batch, seq, DIFF_W), BF16),
        scratch_shapes=[pltpu.VMEM((hps, seq, 2 * LANES), BF16)],
        compiler_params=pltpu.CompilerParams(dimension_semantics=("parallel", "parallel"),
                                             vmem_limit_bytes=VMEM_LIMIT),
        name="diff_attention",
    )(lam4, subln, dq, dk, dv, gbias)


def _proj1_kernel(x_ref, gain_ref, w_ref, bd_ref, hg_ref, q_ref, kk_ref, vt_ref):
    h = _rms(x_ref[...], gain_ref[...]).astype(BF16)
    bd = bd_ref[...]
    qw = SWA_HEADS * HEAD_DIM
    kw = SWA_KV_HEADS * HEAD_DIM
    low = lax.broadcasted_iota(jnp.int32, (h.shape[0], LANES), 1) < HEAD_DIM

    def project(c0):
        return c0, _dot(h, w_ref[:, c0:c0 + 2 * MXU_TILE])

    def epilogue(c0, proj):
        if c0 < qw:
            for i in range(2):
                cols = slice(i * MXU_TILE, (i + 1) * MXU_TILE)
                q_ref[:, c0 + i * MXU_TILE:c0 + (i + 1) * MXU_TILE] = _head_norm(
                    proj[:, cols], bd, hg_ref[0:1, :]).astype(BF16)
            return
        kn = _head_norm(proj[:, 0:kw], bd, hg_ref[1:2, :])
        for p in range(kw // LANES):
            pair = kn[:, p * LANES:(p + 1) * LANES]
            swapped = pltpu.roll(pair, HEAD_DIM, axis=1)
            kk_ref[:, (2 * p) * LANES:(2 * p + 1) * LANES] = jnp.where(low, pair, swapped).astype(BF16)
            kk_ref[:, (2 * p + 1) * LANES:(2 * p + 2) * LANES] = jnp.where(low, swapped, pair).astype(BF16)
        vt_ref[0] = proj[:, kw:2 * kw].T.astype(BF16)

    _lagged([(qw,)] + [(c0,) for c0 in range(0, qw, 2 * MXU_TILE)], project, epilogue,
            depth=PROJ_LAG)


def _proj1(x2d, gain, w, bd, hg, seq):
    n = x2d.shape[0]
    tm = ROW_TILE
    tps = seq // tm
    qw = SWA_HEADS * HEAD_DIM
    kw = SWA_KV_HEADS * HEAD_DIM
    full = lambda shape: pl.BlockSpec(shape, lambda t: (0,) * len(shape))
    return pl.pallas_call(
        _proj1_kernel,
        grid=(n // tm,),
        in_specs=[
            pl.BlockSpec((tm, D_MODEL), lambda t: (t, 0)),
            full((1, D_MODEL)),
            full(w.shape),
            full((MXU_TILE, MXU_TILE)),
            full(hg.shape),
        ],
        out_specs=[pl.BlockSpec((tm, qw), lambda t: (t, 0)),
                   pl.BlockSpec((tm, 2 * kw), lambda t: (t, 0)),
                   pl.BlockSpec((1, kw, tm), lambda t: (t // tps, 0, t % tps))],
        out_shape=[jax.ShapeDtypeStruct((n, qw), BF16),
                   jax.ShapeDtypeStruct((n, 2 * kw), BF16),
                   jax.ShapeDtypeStruct((n // seq, kw, seq), BF16)],
        compiler_params=pltpu.CompilerParams(dimension_semantics=("parallel",),
                                             vmem_limit_bytes=VMEM_LIMIT),
        name="proj_odd",
    )(x2d, gain, w, bd, hg)


def _swa_kernel(sink_ref, q_ref, kk_ref, vt_ref, sb_ref, o_ref, va_ref, *, seq):
    jj = pl.program_id(0)
    w = WINDOW
    gw = SWA_GROUP * HEAD_DIM
    sub = lax.broadcasted_iota(jnp.int32, (LANES - HEAD_DIM, seq), 0)
    sinks = []
    for p in range(SWA_KV_PER_STEP):
        va_ref[p, 0:HEAD_DIM, :] = vt_ref[0, p * HEAD_DIM:(p + 1) * HEAD_DIM, :]
        va_ref[p, HEAD_DIM:LANES, :] = jnp.where(sub == 0, 1.0, 0.0).astype(BF16)
        head0 = (jj * SWA_KV_PER_STEP + p) * SWA_GROUP
        sinks.append(jnp.concatenate(
            [jnp.full((1, w), sink_ref[head0 + g] * LOG2E, F32) for g in range(SWA_GROUP)],
            axis=1))
    low = lax.broadcasted_iota(jnp.int32, (w, LANES), 1) < HEAD_DIM
    zero = jnp.zeros((), BF16)

    def scores(p, n):
        r0 = n * w
        qb = q_ref[0, r0:r0 + w, p * gw:(p + 1) * gw]
        qa, qc = qb[:, 0:LANES], qb[:, LANES:2 * LANES]
        q4 = jnp.concatenate([jnp.where(low, qa, zero), jnp.where(low, zero, qa),
                              jnp.where(low, qc, zero), jnp.where(low, zero, qc)], axis=0)
        lanes = slice(p * LANES, (p + 1) * LANES)
        if n == 0:
            return p, n, _dot_nt(kk_ref[0, 0:w, lanes], q4) + sb_ref[p, w:2 * w, :]
        return p, n, _dot_nt(kk_ref[0, r0 - w:r0 + w, lanes], q4) + sb_ref[p]

    def finish(p, n, st):
        r0 = n * w
        vals = va_ref[p, :, 0:w] if n == 0 else va_ref[p, :, r0 - w:r0 + w]
        m = jnp.maximum(jnp.max(st, axis=0, keepdims=True), sinks[p])
        acc = _dot(vals, jnp.exp2(st - m).astype(BF16))
        denom = acc[HEAD_DIM:HEAD_DIM + 1, :] + jnp.exp2(sinks[p] - m)
        ot = acc[0:HEAD_DIM, :] / denom
        o_ref[0, r0:r0 + w, p * gw:(p + 1) * gw] = jnp.concatenate(
            [jnp.concatenate([ot[:, 0:w], ot[:, w:2 * w]], axis=0).T,
             jnp.concatenate([ot[:, 2 * w:3 * w], ot[:, 3 * w:4 * w]], axis=0).T],
            axis=1).astype(BF16)

    _lagged([(p, n) for p in range(SWA_KV_PER_STEP) for n in range(seq // w)], scores, finish,
            depth=SWA_LAG)


def _swa_attention(sinks, q, kk, vt, swbias, batch, seq):
    kps = SWA_KV_PER_STEP
    gw = kps * SWA_GROUP * HEAD_DIM
    return pl.pallas_call(
        functools.partial(_swa_kernel, seq=seq),
        grid=(SWA_KV_HEADS // kps, batch),
        in_specs=[
            pl.BlockSpec(memory_space=pltpu.SMEM),
            pl.BlockSpec((1, seq, gw), lambda h, b: (b, 0, h)),
            pl.BlockSpec((1, seq, kps * LANES), lambda h, b: (b, 0, h)),
            pl.BlockSpec((1, kps * HEAD_DIM, seq), lambda h, b: (b, h, 0)),
            pl.BlockSpec((kps, 2 * WINDOW, SWA_GROUP * WINDOW), lambda h, b: (h, 0, 0)),
        ],
        out_specs=pl.BlockSpec((1, seq, gw), lambda h, b: (b, 0, h)),
        out_shape=jax.ShapeDtypeStruct((batch, seq, SWA_HEADS * HEAD_DIM), BF16),
        scratch_shapes=[pltpu.VMEM((kps, LANES, seq), BF16)],
        compiler_params=pltpu.CompilerParams(dimension_semantics=("parallel", "parallel"),
                                             vmem_limit_bytes=VMEM_LIMIT),
        name="swa_attention",
    )(sinks, q, kk, vt, swbias)


def _outmlp_kernel(*refs, n_mix):
    x_ref = refs[0]
    a_refs = refs[1:1 + n_mix]
    wo_ref, gain_ref, wg_ref, wu_ref, wd_ref, o_ref = refs[1 + n_mix:]
    mix = jnp.concatenate([a_ref[...] for a_ref in a_refs], axis=1)
    x1 = x_ref[...] + _dot(mix, wo_ref[...])
    h = _rms(x1, gain_ref[...]).astype(BF16)
    out = x1
    for c0 in range(0, FFN_HIDDEN, HIDDEN_CHUNK):
        c1 = min(c0 + HIDDEN_CHUNK, FFN_HIDDEN)
        g = _dot(h, wg_ref[:, c0:c1])
        u = _dot(h, wu_ref[:, c0:c1])
        y = (g * (1.0 / (1.0 + jnp.exp(-g))) * u).astype(BF16)
        out = out + _dot(y, wd_ref[c0:c1, :])
    o_ref[...] = out


def _outmlp(x2d, mixes, wo, gain, wg, wu, wd, layer):
    n = x2d.shape[0]
    tm = ROW_TILE
    full = lambda shape: pl.BlockSpec(shape, lambda t: (0,) * len(shape),
                                      pipeline_mode=pl.Buffered(1))
    of_layer = lambda w: pl.BlockSpec((None,) + w.shape[1:], lambda t: (layer, 0, 0),
                                      pipeline_mode=pl.Buffered(1))
    in_specs = [pl.BlockSpec((tm, D_MODEL), lambda t: (t, 0))]
    in_specs += [pl.BlockSpec((tm, a.shape[1]), lambda t: (t, 0)) for a in mixes]
    in_specs += [full(wo.shape), full(gain.shape), of_layer(wg), of_layer(wu), of_layer(wd)]
    return pl.pallas_call(
        functools.partial(_outmlp_kernel, n_mix=len(mixes)),
        grid=(n // tm,),
        in_specs=in_specs,
        out_specs=pl.BlockSpec((tm, D_MODEL), lambda t: (t, 0)),
        out_shape=jax.ShapeDtypeStruct((n, D_MODEL), F32),
        compiler_params=pltpu.CompilerParams(dimension_semantics=("parallel",),
                                             vmem_limit_bytes=VMEM_LIMIT),
        name="outproj_swiglu",
    )(x2d, *mixes, wo, gain, wg, wu, wd)


def _block_diag_mean():
    i = np.arange(MXU_TILE)
    same = (i[:, None] // HEAD_DIM) == (i[None, :] // HEAD_DIM)
    return jnp.asarray(same / HEAD_DIM, BF16)


def kernel(x, rel_bias_table, ev_attn_norm, ev_w_in, ev_b_forget, ev_fox_q_norm, ev_fox_k_norm,
           ev_diff_q_norm, ev_diff_k_norm, ev_lambda_q1, ev_lambda_k1, ev_lambda_q2, ev_lambda_k2,
           ev_diff_subln, ev_w_out, od_attn_norm, od_w_qkv, od_q_norm, od_k_norm, od_sinks, od_w_out,
           ffn_norm, w_gate, w_up, w_down):
    batch, seq, d = x.shape
    n = batch * seq
    x2d = x.reshape(n, d)
    bd = _block_diag_mean()
    swbias, gbias = _bias_tiles(rel_bias_table.astype(F32))
    swbias = swbias.reshape(SWA_KV_HEADS, SWA_GROUP, 2 * WINDOW, WINDOW).transpose(0, 2, 1, 3)
    swbias = swbias.reshape(SWA_KV_HEADS, 2 * WINDOW, SWA_GROUP * WINDOW)

    lambda_init = 0.8 - 0.6 * math.exp(-0.3 * 0)
    w_in = ev_w_in[0]
    ff0 = 3 * FOX_W
    d0 = ff0 + FOX_HEADS
    w0 = jnp.concatenate([w_in[:, 0:ff0], w_in[:, d0:d0 + 3 * DIFF_W]], axis=1).astype(BF16)
    wf = jnp.concatenate(
        [w_in[:, ff0:d0].T, jnp.zeros((BF16_SUBLANES - FOX_HEADS, d), w_in.dtype)],
        axis=0).astype(BF16)
    tile8 = lambda g, scale=1.0: jnp.tile(g.astype(F32) * scale, FOX_W // HEAD_DIM)
    hg0 = jnp.stack([tile8(ev_fox_q_norm[0], Q_GAIN_SCALE), tile8(ev_fox_k_norm[0]),
                     tile8(ev_diff_q_norm[0], Q_GAIN_SCALE), tile8(ev_diff_k_norm[0])])
    bf = ev_b_forget[0].astype(F32).reshape(FOX_HEADS, 1)
    depth, hidden = w_gate.shape[0], w_gate.shape[2]
    later = [w_gate.reshape(depth * d, hidden), w_up.reshape(depth * d, hidden),
             w_down.reshape(depth * hidden, d), ev_w_out[0], od_w_qkv[0], od_w_out[0]]
    fq, fk, fvt, dq, dk, dv, caug, wg, wu, wd, wo0, wqkv, wo1 = _proj0(
        x2d, ev_attn_norm[0].astype(F32).reshape(1, d), w0, wf, bd, hg0, bf, seq,
        [a.astype(F32) for a in later])
    wg, wu = wg.reshape(depth, d, hidden), wu.reshape(depth, d, hidden)
    wd = wd.reshape(depth, hidden, d)
    r3 = lambda a: a.reshape(batch, seq, a.shape[-1])
    fox = _fox_attention(r3(fq), r3(fk), fvt, r3(caug), batch, seq)
    lam4 = jnp.stack([ev_lambda_q1[0], ev_lambda_k1[0], ev_lambda_q2[0], ev_lambda_k2[0]]).astype(F32)
    dout = _diff_attention(lam4, ev_diff_subln[0].astype(F32).reshape(1, 2 * HEAD_DIM),
                           r3(dq), r3(dk), r3(dv), gbias, batch, seq, lambda_init)
    x2d = _outmlp(x2d, [fox.reshape(n, FOX_W), dout.reshape(n, DIFF_W)], wo0,
                  ffn_norm[0].astype(F32).reshape(1, d), wg, wu, wd, 0)

    tile4 = lambda g, scale=1.0: jnp.tile(g.astype(F32) * scale, MXU_TILE // HEAD_DIM)
    hg1 = jnp.stack([tile4(od_q_norm[0], Q_GAIN_SCALE), tile4(od_k_norm[0])])
    q, kk, vt = _proj1(x2d, od_attn_norm[0].astype(F32).reshape(1, d), wqkv,
                       bd, hg1, seq)
    swa = _swa_attention(od_sinks[0].astype(F32), r3(q), r3(kk), vt, swbias, batch, seq)
    x2d = _outmlp(x2d, [swa.reshape(n, SWA_HEADS * HEAD_DIM)], wo1,
                  ffn_norm[1].astype(F32).reshape(1, d), wg, wu, wd, 1)
    return x2d.reshape(batch, seq, d)
```

```python
import functools
import math

import numpy as np
import jax
import jax.numpy as jnp
from jax import lax
from jax.experimental import pallas as pl
from jax.experimental.pallas import tpu as pltpu

F32 = jnp.float32
BF16 = jnp.bfloat16

D_MODEL = 1024
HEAD_DIM = 64
FOX_HEADS = 8
DIFF_HEADS = 4
SWA_HEADS = 16
SWA_KV_HEADS = 4
SWA_GROUP = SWA_HEADS // SWA_KV_HEADS
WINDOW = 128
NUM_BUCKETS = 32
MAX_DISTANCE = 128
FFN_HIDDEN = 2816
RMS_EPS = 1e-6
NEG_INF = -1e30
FOX_W = FOX_HEADS * HEAD_DIM
DIFF_W = DIFF_HEADS * 2 * HEAD_DIM
QK_SCALE = HEAD_DIM ** -0.5
LOG2E = math.log2(math.e)
Q_GAIN_SCALE = QK_SCALE * LOG2E

LANES = 128
BF16_SUBLANES = 16
VT_ROWS = HEAD_DIM + BF16_SUBLANES
MXU_TILE = 256
ROW_TILE = 1024
ATT_Q_BLOCK = 256
FOX_PAIRS_PER_STEP = 2
DIFF_HEADS_PER_STEP = 2
SWA_KV_PER_STEP = 2
FOX_LAG = 3
DIFF_LAG = 2
PROJ_LAG = 1
SWA_LAG = 4
HIDDEN_CHUNK = 768
V7X_VMEM_BYTES = 64 * 1024 * 1024
VMEM_LIMIT = V7X_VMEM_BYTES - 8 * 1024 * 1024

_NT = (((1,), (1,)), ((), ()))


def _dot(a, b):
    return jnp.dot(a, b, preferred_element_type=F32)


def _dot_nt(a, b):
    return lax.dot_general(a, b, _NT, preferred_element_type=F32)


def _lagged(items, first, second, depth=1):
    pending = []
    for item in items:
        pending.append(first(*item))
        if len(pending) > depth:
            second(*pending.pop(0))
    for p in pending:
        second(*p)


def _interleaved(items, first, second, depth):
    def drive(gens):
        done = {}
        while gens:
            for g in list(gens):
                try:
                    next(g)
                except StopIteration as stop:
                    done[id(g)] = stop.value
                    gens.remove(g)
        return done

    pending = []
    for item in items:
        g1 = first(*item)
        gens = [g1] if len(pending) < depth else [g1, second(*pending.pop(0))]
        pending.append(drive(gens)[id(g1)])
    for p in pending:
        drive([second(*p)])


def _hill_order(nblocks):
    return list(range(0, nblocks, 2)) + list(range(nblocks - 1 - nblocks % 2, 0, -2))


def _rms(xf, gain):
    ms = jnp.mean(xf * xf, axis=-1, keepdims=True)
    return xf * lax.rsqrt(ms + RMS_EPS) * gain


def _head_norm(acc, bd, gain):
    msq = _dot((acc * acc).astype(BF16), bd)
    return acc * lax.rsqrt(msq + RMS_EPS) * gain


def _bucket_ids(delta):
    n = np.maximum(delta, 0)
    max_exact = NUM_BUCKETS // 2
    nf = np.maximum(n, 1).astype(np.float64)
    large = max_exact + (np.log(nf / max_exact) / math.log(MAX_DISTANCE / max_exact)
                         * (NUM_BUCKETS - max_exact)).astype(np.int32)
    large = np.minimum(large, NUM_BUCKETS - 1)
    return np.where(n < max_exact, n, large).astype(np.int32)


def _bias_kernel(table_ref, bsw_ref, bg_ref, sw_ref, g_ref):
    h = pl.program_id(0)
    bsw = bsw_ref[...]
    sw = jnp.zeros(bsw.shape, F32)
    for b in range(NUM_BUCKETS):
        sw = jnp.where(bsw == b, table_ref[b, h] * LOG2E, sw)
    sw_ref[0] = jnp.where(bsw < 0, NEG_INF, sw)

    @pl.when(h < DIFF_HEADS)
    def _():
        bg = bg_ref[...]
        far = table_ref[NUM_BUCKETS - 1, h]
        g = jnp.zeros(bg.shape, F32)
        for b in range(NUM_BUCKETS):
            g = jnp.where(bg == b, (table_ref[b, h] - far) * LOG2E, g)
        g_ref[0] = jnp.where(bg < 0, NEG_INF, g)


def _bias_tiles(table):
    a = np.arange(WINDOW)[None, :]
    b = np.arange(2 * WINDOW)[:, None]
    d_sw = WINDOW + a - b
    bsw = np.where((d_sw >= 0) & (d_sw < WINDOW), _bucket_ids(d_sw), -1).astype(np.int32)
    a = np.arange(ATT_Q_BLOCK)[:, None]
    b = np.arange(2 * ATT_Q_BLOCK)[None, :]
    d_g = ATT_Q_BLOCK + a - b
    bg = np.where(d_g >= 0, _bucket_ids(d_g), -1).astype(np.int32)
    nh = table.shape[1]
    return pl.pallas_call(
        _bias_kernel,
        grid=(nh,),
        in_specs=[
            pl.BlockSpec(memory_space=pltpu.SMEM),
            pl.BlockSpec(bsw.shape, lambda h: (0, 0)),
            pl.BlockSpec(bg.shape, lambda h: (0, 0)),
        ],
        out_specs=[
            pl.BlockSpec((1,) + bsw.shape, lambda h: (h, 0, 0)),
            pl.BlockSpec((1,) + bg.shape, lambda h: (jnp.minimum(h, DIFF_HEADS - 1), 0, 0)),
        ],
        out_shape=[
            jax.ShapeDtypeStruct((nh,) + bsw.shape, F32),
            jax.ShapeDtypeStruct((DIFF_HEADS,) + bg.shape, F32),
        ],
        compiler_params=pltpu.CompilerParams(dimension_semantics=("arbitrary",)),
        name="rel_bias_tiles",
    )(table, jnp.asarray(bsw), jnp.asarray(bg))


def _proj0_kernel(*refs, tiles_per_seq, n_cast):
    x_ref, gain_ref, w_ref, wf_ref, bd_ref, hg_ref, bf_ref = refs[0:7]
    cast_src = refs[7:7 + n_cast]
    fq_ref, fk_ref, fv_ref, dq_ref, dk_ref, dv_ref, caug_ref = refs[7 + n_cast:14 + n_cast]
    cast_dst = refs[14 + n_cast:14 + 2 * n_cast]
    carry_ref = refs[14 + 2 * n_cast]
    t = pl.program_id(0)

    @pl.when(t == 0)
    def _():
        carry_ref[...] = jnp.zeros_like(carry_ref)

    h = _rms(x_ref[...], gain_ref[...]).astype(BF16)
    bd = bd_ref[...]
    rows = h.shape[0]

    outs = (fq_ref, fk_ref, fv_ref, dq_ref, dk_ref, dv_ref)
    gains = (0, 1, None, 2, 3, None)
    feature_major = (False, False, True, False, False, False)

    def project(g):
        return g, _dot(h, w_ref[:, g * FOX_W:(g + 1) * FOX_W])

    def epilogue(g, proj):
        if feature_major[g]:
            outs[g][0] = proj.T.astype(BF16)
            return
        if gains[g] is None:
            outs[g][...] = proj.astype(BF16)
            return
        for i in range(FOX_W // MXU_TILE):
            cols = slice(i * MXU_TILE, (i + 1) * MXU_TILE)
            outs[g][:, cols] = _head_norm(
                proj[:, cols], bd, hg_ref[gains[g]:gains[g] + 1, cols]).astype(BF16)

    pending = [project(g) for g in range(PROJ_LAG)]

    z8 = _dot_nt(wf_ref[...], h)[0:FOX_HEADS, :] + bf_ref[...]
    y = -(jnp.maximum(-z8, 0.0) + jnp.log1p(jnp.exp(-jnp.abs(z8))))
    lane = lax.broadcasted_iota(jnp.int32, y.shape, 1)
    shift = 1
    while shift < rows:
        y = y + jnp.where(lane >= shift, pltpu.roll(y, shift, axis=1), 0.0)
        shift *= 2

    carry = jnp.where(t % tiles_per_seq == 0, 0.0, carry_ref[:, 0:1])
    c = y + carry
    carry_ref[...] = jnp.broadcast_to(c[:, rows - 1:rows], carry_ref.shape)
    cn = c * (-LOG2E)
    hi = cn.astype(BF16).astype(F32)
    r1 = cn - hi
    mid = r1.astype(BF16).astype(F32)
    lo = (r1 - mid).astype(BF16).astype(F32)
    half = jnp.concatenate([hi, mid, lo, jnp.zeros((HEAD_DIM - 3 * FOX_HEADS, rows), F32)], axis=0)
    caug_ref[...] = jnp.concatenate([half, half], axis=0).T.astype(BF16)

    for g in range(PROJ_LAG, len(outs)):
        pending.append(project(g))
        epilogue(*pending.pop(0))
    for p in pending:
        epilogue(*p)

    for src, dst in zip(cast_src, cast_dst):
        dst[...] = src[...].astype(BF16)


def _proj0(x2d, gain, w, wf, bd, hg, bf, seq, later_weights):
    n = x2d.shape[0]
    tm = ROW_TILE
    tps = seq // tm
    steps = n // tm
    slab = lambda a: pl.BlockSpec((a.shape[0] // steps, a.shape[1]), lambda t: (t, 0))
    for a in later_weights:
        assert a.shape[0] % (steps * BF16_SUBLANES) == 0, a.shape
    row_spec = pl.BlockSpec((tm, FOX_W), lambda t: (t, 0))
    col_spec = pl.BlockSpec((1, FOX_W, tm), lambda t: (t // tps, 0, t % tps))
    rows_shape = jax.ShapeDtypeStruct((n, FOX_W), BF16)
    cols_shape = jax.ShapeDtypeStruct((n // seq, FOX_W, seq), BF16)
    full = lambda shape: pl.BlockSpec(shape, lambda t: (0,) * len(shape))
    outs = pl.pallas_call(
        functools.partial(_proj0_kernel, tiles_per_seq=tps, n_cast=len(later_weights)),
        grid=(steps,),
        in_specs=[
            pl.BlockSpec((tm, D_MODEL), lambda t: (t, 0)),
            full((1, D_MODEL)),
            full(w.shape),
            full(wf.shape),
            full((MXU_TILE, MXU_TILE)),
            full(hg.shape),
            full(bf.shape),
        ] + [slab(a) for a in later_weights],
        out_specs=[row_spec, row_spec, col_spec, row_spec, row_spec, row_spec,
                   pl.BlockSpec((tm, LANES), lambda t: (t, 0))] + [slab(a) for a in later_weights],
        out_shape=[rows_shape, rows_shape, cols_shape, rows_shape, rows_shape, rows_shape,
                   jax.ShapeDtypeStruct((n, LANES), BF16)]
        + [jax.ShapeDtypeStruct(a.shape, BF16) for a in later_weights],
        scratch_shapes=[pltpu.VMEM((FOX_HEADS, LANES), F32)],
        compiler_params=pltpu.CompilerParams(dimension_semantics=("arbitrary",),
                                             vmem_limit_bytes=VMEM_LIMIT),
        name="proj_even",
    )(x2d, gain, w, wf, bd, hg, bf, *later_weights)
    return outs


def _fox_kernel(q_ref, k_ref, vt_ref, caug_ref, o_ref, ka_ref, va_ref, *, seq):
    jj = pl.program_id(1)
    qb = ATT_Q_BLOCK
    low_s = lax.broadcasted_iota(jnp.int32, (seq, LANES), 1) < HEAD_DIM
    ca = caug_ref[0]
    sub = lax.broadcasted_iota(jnp.int32, (VT_ROWS - HEAD_DIM, seq), 0)
    ones_row = jnp.where(sub == 0, 1.0, 0.0).astype(BF16)
    for p in range(FOX_PAIRS_PER_STEP):
        kp = k_ref[0, :, p * LANES:(p + 1) * LANES]
        ka_ref[2 * p] = jnp.where(low_s, kp, ca)
        ka_ref[2 * p + 1] = jnp.where(low_s, ca, kp)
        for hh in range(2):
            r = p * LANES + hh * HEAD_DIM
            va_ref[2 * p + hh, 0:HEAD_DIM, :] = vt_ref[0, r:r + HEAD_DIM, :]
            va_ref[2 * p + hh, HEAD_DIM:VT_ROWS, :] = ones_row
    lane_q = lax.broadcasted_iota(jnp.int32, (qb, LANES), 1)
    key_i = lax.broadcasted_iota(jnp.int32, (qb, qb), 0)
    qry_i = lax.broadcasted_iota(jnp.int32, (qb, qb), 1)
    causal = key_i <= qry_i
    owns = (lane_q < HEAD_DIM, lane_q >= HEAD_DIM)

    def scores(p, qi, hh):
        r0 = qi * qb
        head = 2 * (FOX_PAIRS_PER_STEP * jj + p) + hh
        rel = lane_q - (HEAD_DIM if hh == 0 else 0) - head
        pick = (rel == 0) | (rel == FOX_HEADS) | (rel == 2 * FOX_HEADS)
        qz = jnp.where(owns[hh], q_ref[0, r0:r0 + qb, p * LANES:(p + 1) * LANES],
                       jnp.where(pick, 1.0, 0.0).astype(BF16))
        s_all =_dot_nt(ka_ref[2 * p + hh, 0:(qi + 1) * qb, :], qz)
        yield
        tiles = [s_all[t * qb:(t + 1) * qb] for t in range(qi)]
        tiles.append(jnp.where(causal, s_all[qi * qb:(qi + 1) * qb], NEG_INF))
        return p, qi, hh, tiles

    halves = []

    def finish(p, qi, hh, tiles):
        r0 = qi * qb
        m = jnp.max(functools.reduce(jnp.maximum, tiles), axis=0, keepdims=True)
        acc = None
        for t, s in enumerate(tiles):
            part = _dot(va_ref[2 * p + hh, :, t * qb:(t + 1) * qb], jnp.exp2(s - m).astype(BF16))
            acc = part if acc is None else acc + part
            yield
        halves.append(acc[0:HEAD_DIM] / acc[HEAD_DIM:HEAD_DIM + 1])
        if hh == 1:
            o_ref[0, r0:r0 + qb, p * LANES:(p + 1) * LANES] = jnp.concatenate(
                [halves[-2], halves[-1]], axis=0).T.astype(BF16)

    _interleaved([(p, qi, hh) for p in range(FOX_PAIRS_PER_STEP)
                  for qi in _hill_order(seq // qb) for hh in range(2)], scores, finish,
                 depth=FOX_LAG)


def _fox_attention(fq, fk, fvt, caug, batch, seq):
    width = FOX_PAIRS_PER_STEP * LANES
    blk = pl.BlockSpec((1, seq, width), lambda b, j: (b, 0, j))
    nheads = 2 * FOX_PAIRS_PER_STEP
    return pl.pallas_call(
        functools.partial(_fox_kernel, seq=seq),
        grid=(batch, FOX_W // width),
        in_specs=[blk, blk, pl.BlockSpec((1, width, seq), lambda b, j: (b, j, 0)),
                  pl.BlockSpec((1, seq, LANES), lambda b, j: (b, 0, 0))],
        out_specs=blk,
        out_shape=jax.ShapeDtypeStruct((batch, seq, FOX_W), BF16),
        scratch_shapes=[pltpu.VMEM((nheads, seq, LANES), BF16),
                        pltpu.VMEM((nheads, VT_ROWS, seq), BF16)],
        compiler_params=pltpu.CompilerParams(dimension_semantics=("parallel", "parallel"),
                                             vmem_limit_bytes=VMEM_LIMIT),
        name="fox_attention",
    )(fq, fk, fvt, caug)


def _diff_kernel(lam_ref, g_ref, q_ref, k_ref, v_ref, gb_ref, o_ref, va_ref, *, seq, lambda_init):
    qb = ATT_Q_BLOCK
    lv = lam_ref[...]
    lam = (jnp.exp(jnp.sum(lv[0:1] * lv[1:2], axis=-1, keepdims=True))
           - jnp.exp(jnp.sum(lv[2:3] * lv[3:4], axis=-1, keepdims=True)) + lambda_init)
    for p in range(DIFF_HEADS_PER_STEP):
        va_ref[p, :, 0:LANES] = v_ref[0, :, p * LANES:(p + 1) * LANES]
        va_ref[p, :, LANES:2 * LANES] = jnp.ones((seq, LANES), BF16)
    lane_q = lax.broadcasted_iota(jnp.int32, (qb, LANES), 1)
    bias_prev = [jnp.concatenate([gb_ref[p, :, 0:qb]] * 2, axis=0)
                 for p in range(DIFF_HEADS_PER_STEP)]
    bias_diag = [jnp.concatenate([gb_ref[p, :, qb:2 * qb]] * 2, axis=0)
                 for p in range(DIFF_HEADS_PER_STEP)]
    gain = g_ref[...] * (1.0 - lambda_init)
    zero = jnp.zeros((), BF16)

    def scores(p, qi):
        r0 = qi * qb
        qp = q_ref[0, r0:r0 + qb, p * LANES:(p + 1) * LANES]
        qz = jnp.concatenate([jnp.where(lane_q < HEAD_DIM, qp, zero),
                              jnp.where(lane_q < HEAD_DIM, zero, qp)], axis=0)
        tiles = []
        for t in range(qi + 1):
            s = _dot_nt(qz, k_ref[0, t * qb:(t + 1) * qb, p * LANES:(p + 1) * LANES])
            tiles.append(s + bias_diag[p] if t == qi else s + bias_prev[p] if t == qi - 1 else s)
            yield
        return p, qi, tiles

    def finish(p, qi, tiles):
        r0 = qi * qb
        m = jnp.max(functools.reduce(jnp.maximum, tiles), axis=-1, keepdims=True)
        acc = None
        for t, s in enumerate(tiles):
            part = _dot(jnp.exp2(s - m).astype(BF16), va_ref[p, t * qb:(t + 1) * qb, :])
            acc = part if acc is None else acc + part
            yield
        a = acc[:, 0:LANES] / acc[:, LANES:2 * LANES]
        o = a[0:qb] - lam * a[qb:2 * qb]
        o_ref[0, r0:r0 + qb, p * LANES:(p + 1) * LANES] = _rms(o, gain).astype(BF16)

    _interleaved([(p, qi) for p in range(DIFF_HEADS_PER_STEP) for qi in _hill_order(seq // qb)],
                 scores, finish, depth=DIFF_LAG)


def _diff_attention(lam4, subln, dq, dk, dv, gbias, batch, seq, lambda_init):
    hps = DIFF_HEADS_PER_STEP
    blk = pl.BlockSpec((1, seq, hps * LANES), lambda h, b: (b, 0, h))
    return pl.pallas_call(
        functools.partial(_diff_kernel, seq=seq, lambda_init=lambda_init),
        grid=(DIFF_HEADS // hps, batch),
        in_specs=[
            pl.BlockSpec(lam4.shape, lambda h, b: (0, 0)),
            pl.BlockSpec(subln.shape, lambda h, b: (0, 0)),
            blk, blk, blk,
            pl.BlockSpec((hps, ATT_Q_BLOCK, 2 * ATT_Q_BLOCK), lambda h, b: (h, 0, 0)),
        ],
        out_specs=blk,
        out_shape=jax.ShapeDtypeStruct((batch, seq, DIFF_W), BF16),
        scratch_shapes=[pltpu.VMEM((hps, seq, 2 * LANES), BF16)],
        compiler_params=pltpu.CompilerParams(dimension_semantics=("parallel", "parallel"),
                                             vmem_limit_bytes=VMEM_LIMIT),
        name="diff_attention",
    )(lam4, subln, dq, dk, dv, gbias)


def _proj1_kernel(x_ref, gain_ref, w_ref, bd_ref, hg_ref, q_ref, kk_ref, vt_ref):
    h = _rms(x_ref[...], gain_ref[...]).astype(BF16)
    bd = bd_ref[...]
    qw = SWA_HEADS * HEAD_DIM
    kw = SWA_KV_HEADS * HEAD_DIM
    low = lax.broadcasted_iota(jnp.int32, (h.shape[0], LANES), 1) < HEAD_DIM

    def project(c0):
        return c0, _dot(h, w_ref[:, c0:c0 + 2 * MXU_TILE])

    def epilogue(c0, proj):
        if c0 < qw:
            for i in range(2):
                cols = slice(i * MXU_TILE, (i + 1) * MXU_TILE)
                q_ref[:, c0 + i * MXU_TILE:c0 + (i + 1) * MXU_TILE] = _head_norm(
                    proj[:, cols], bd, hg_ref[0:1, :]).astype(BF16)
            return
        kn = _head_norm(proj[:, 0:kw], bd, hg_ref[1:2, :])
        for p in range(kw // LANES):
            pair = kn[:, p * LANES:(p + 1) * LANES]
            swapped = pltpu.roll(pair, HEAD_DIM, axis=1)
            kk_ref[:, (2 * p) * LANES:(2 * p + 1) * LANES] = jnp.where(low, pair, swapped).astype(BF16)
            kk_ref[:, (2 * p + 1) * LANES:(2 * p + 2) * LANES] = jnp.where(low, swapped, pair).astype(BF16)
        vt_ref[0] = proj[:, kw:2 * kw].T.astype(BF16)

    _lagged([(qw,)] + [(c0,) for c0 in range(0, qw, 2 * MXU_TILE)], project, epilogue,
            depth=PROJ_LAG)


def _proj1(x2d, gain, w, bd, hg, seq):
    n = x2d.shape[0]
    tm = ROW_TILE
    tps = seq // tm
    qw = SWA_HEADS * HEAD_DIM
    kw = SWA_KV_HEADS * HEAD_DIM
    full = lambda shape: pl.BlockSpec(shape, lambda t: (0,) * len(shape))
    return pl.pallas_call(
        _proj1_kernel,
        grid=(n // tm,),
        in_specs=[
            pl.BlockSpec((tm, D_MODEL), lambda t: (t, 0)),
            full((1, D_MODEL)),
            full(w.shape),
            full((MXU_TILE, MXU_TILE)),
            full(hg.shape),
        ],
        out_specs=[pl.BlockSpec((tm, qw), lambda t: (t, 0)),
                   pl.BlockSpec((tm, 2 * kw), lambda t: (t, 0)),
                   pl.BlockSpec((1, kw, tm), lambda t: (t // tps, 0, t % tps))],
        out_shape=[jax.ShapeDtypeStruct((n, qw), BF16),
                   jax.ShapeDtypeStruct((n, 2 * kw), BF16),
                   jax.ShapeDtypeStruct((n // seq, kw, seq), BF16)],
        compiler_params=pltpu.CompilerParams(dimension_semantics=("parallel",),
                                             vmem_limit_bytes=VMEM_LIMIT),
        name="proj_odd",
    )(x2d, gain, w, bd, hg)


def _swa_kernel(sink_ref, q_ref, kk_ref, vt_ref, sb_ref, o_ref, va_ref, *, seq):
    jj = pl.program_id(0)
    w = WINDOW
    gw = SWA_GROUP * HEAD_DIM
    sub = lax.broadcasted_iota(jnp.int32, (VT_ROWS - HEAD_DIM, seq), 0)
    sinks = []
    for p in range(SWA_KV_PER_STEP):
        va_ref[p, 0:HEAD_DIM, :] = vt_ref[0, p * HEAD_DIM:(p + 1) * HEAD_DIM, :]
        va_ref[p, HEAD_DIM:VT_ROWS, :] = jnp.where(sub == 0, 1.0, 0.0).astype(BF16)
        head0 = (jj * SWA_KV_PER_STEP + p) * SWA_GROUP
        sinks.append(jnp.concatenate(
            [jnp.full((1, w), sink_ref[head0 + g] * LOG2E, F32) for g in range(SWA_GROUP)],
            axis=1))
    low = lax.broadcasted_iota(jnp.int32, (w, LANES), 1) < HEAD_DIM
    zero = jnp.zeros((), BF16)

    def scores(p, n):
        r0 = n * w
        qb = q_ref[0, r0:r0 + w, p * gw:(p + 1) * gw]
        qa, qc = qb[:, 0:LANES], qb[:, LANES:2 * LANES]
        q4 = jnp.concatenate([jnp.where(low, qa, zero), jnp.where(low, zero, qa),
                              jnp.where(low, qc, zero), jnp.where(low, zero, qc)], axis=0)
        lanes = slice(p * LANES, (p + 1) * LANES)
        if n == 0:
            return p, n, _dot_nt(kk_ref[0, 0:w, lanes], q4) + sb_ref[p, w:2 * w, :]
        return p, n, _dot_nt(kk_ref[0, r0 - w:r0 + w, lanes], q4) + sb_ref[p]

    def finish(p, n, st):
        r0 = n * w
        vals = va_ref[p, :, 0:w] if n == 0 else va_ref[p, :, r0 - w:r0 + w]
        m = jnp.maximum(jnp.max(st, axis=0, keepdims=True), sinks[p])
        acc = _dot(vals, jnp.exp2(st - m).astype(BF16))
        denom = acc[HEAD_DIM:HEAD_DIM + 1, :] + jnp.exp2(sinks[p] - m)
        ot = acc[0:HEAD_DIM, :] / denom
        o_ref[0, r0:r0 + w, p * gw:(p + 1) * gw] = jnp.concatenate(
            [jnp.concatenate([ot[:, 0:w], ot[:, w:2 * w]], axis=0).T,
             jnp.concatenate([ot[:, 2 * w:3 * w], ot[:, 3 * w:4 * w]], axis=0).T],
            axis=1).astype(BF16)

    _lagged([(p, n) for p in range(SWA_KV_PER_STEP) for n in range(seq // w)], scores, finish,
            depth=SWA_LAG)


def _swa_attention(sinks, q, kk, vt, swbias, batch, seq):
    kps = SWA_KV_PER_STEP
    gw = kps * SWA_GROUP * HEAD_DIM
    return pl.pallas_call(
        functools.partial(_swa_kernel, seq=seq),
        grid=(SWA_KV_HEADS // kps, batch),
        in_specs=[
            pl.BlockSpec(memory_space=pltpu.SMEM),
            pl.BlockSpec((1, seq, gw), lambda h, b: (b, 0, h)),
            pl.BlockSpec((1, seq, kps * LANES), lambda h, b: (b, 0, h)),
            pl.BlockSpec((1, kps * HEAD_DIM, seq), lambda h, b: (b, h, 0)),
            pl.BlockSpec((kps, 2 * WINDOW, SWA_GROUP * WINDOW), lambda h, b: (h, 0, 0)),
        ],
        out_specs=pl.BlockSpec((1, seq, gw), lambda h, b: (b, 0, h)),
        out_shape=jax.ShapeDtypeStruct((batch, seq, SWA_HEADS * HEAD_DIM), BF16),
        scratch_shapes=[pltpu.VMEM((kps, VT_ROWS, seq), BF16)],
        compiler_params=pltpu.CompilerParams(dimension_semantics=("parallel", "parallel"),
                                             vmem_limit_bytes=VMEM_LIMIT),
        name="swa_attention",
    )(sinks, q, kk, vt, swbias)


def _outmlp_kernel(*refs, n_mix):
    x_ref = refs[0]
    a_refs = refs[1:1 + n_mix]
    wo_ref, gain_ref, wg_ref, wu_ref, wd_ref, o_ref = refs[1 + n_mix:]
    mix = jnp.concatenate([a_ref[...] for a_ref in a_refs], axis=1)
    x1 = x_ref[...] + _dot(mix, wo_ref[...])
    h = _rms(x1, gain_ref[...]).astype(BF16)
    out = x1
    for c0 in range(0, FFN_HIDDEN, HIDDEN_CHUNK):
        c1 = min(c0 + HIDDEN_CHUNK, FFN_HIDDEN)
        g = _dot(h, wg_ref[:, c0:c1])
        u = _dot(h, wu_ref[:, c0:c1])
        y = (g * (1.0 / (1.0 + jnp.exp(-g))) * u).astype(BF16)
        out = out + _dot(y, wd_ref[c0:c1, :])
    o_ref[...] = out


def _outmlp(x2d, mixes, wo, gain, wg, wu, wd, layer):
    n = x2d.shape[0]
    tm = ROW_TILE
    full = lambda shape: pl.BlockSpec(shape, lambda t: (0,) * len(shape),
                                      pipeline_mode=pl.Buffered(1))
    of_layer = lambda w: pl.BlockSpec((None,) + w.shape[1:], lambda t: (layer, 0, 0),
                                      pipeline_mode=pl.Buffered(1))
    in_specs = [pl.BlockSpec((tm, D_MODEL), lambda t: (t, 0))]
    in_specs += [pl.BlockSpec((tm, a.shape[1]), lambda t: (t, 0)) for a in mixes]
    in_specs += [full(wo.shape), full(gain.shape), of_layer(wg), of_layer(wu), of_layer(wd)]
    return pl.pallas_call(
        functools.partial(_outmlp_kernel, n_mix=len(mixes)),
        grid=(n // tm,),
        in_specs=in_specs,
        out_specs=pl.BlockSpec((tm, D_MODEL), lambda t: (t, 0)),
        out_shape=jax.ShapeDtypeStruct((n, D_MODEL), F32),
        compiler_params=pltpu.CompilerParams(dimension_semantics=("parallel",),
                                             vmem_limit_bytes=VMEM_LIMIT),
        name="outproj_swiglu",
    )(x2d, *mixes, wo, gain, wg, wu, wd)


def _block_diag_mean():
    i = np.arange(MXU_TILE)
    same = (i[:, None] // HEAD_DIM) == (i[None, :] // HEAD_DIM)
    return jnp.asarray(same / HEAD_DIM, BF16)


def kernel(x, rel_bias_table, ev_attn_norm, ev_w_in, ev_b_forget, ev_fox_q_norm, ev_fox_k_norm,
           ev_diff_q_norm, ev_diff_k_norm, ev_lambda_q1, ev_lambda_k1, ev_lambda_q2, ev_lambda_k2,
           ev_diff_subln, ev_w_out, od_attn_norm, od_w_qkv, od_q_norm, od_k_norm, od_sinks, od_w_out,
           ffn_norm, w_gate, w_up, w_down):
    batch, seq, d = x.shape
    n = batch * seq
    x2d = x.reshape(n, d)
    bd = _block_diag_mean()
    swbias, gbias = _bias_tiles(rel_bias_table.astype(F32))
    swbias = swbias.reshape(SWA_KV_HEADS, SWA_GROUP, 2 * WINDOW, WINDOW).transpose(0, 2, 1, 3)
    swbias = swbias.reshape(SWA_KV_HEADS, 2 * WINDOW, SWA_GROUP * WINDOW)

    lambda_init = 0.8 - 0.6 * math.exp(-0.3 * 0)
    w_in = ev_w_in[0]
    ff0 = 3 * FOX_W
    d0 = ff0 + FOX_HEADS
    w0 = jnp.concatenate([w_in[:, 0:ff0], w_in[:, d0:d0 + 3 * DIFF_W]], axis=1).astype(BF16)
    wf = jnp.concatenate(
        [w_in[:, ff0:d0].T, jnp.zeros((BF16_SUBLANES - FOX_HEADS, d), w_in.dtype)],
        axis=0).astype(BF16)
    tile8 = lambda g, scale=1.0: jnp.tile(g.astype(F32) * scale, FOX_W // HEAD_DIM)
    hg0 = jnp.stack([tile8(ev_fox_q_norm[0], Q_GAIN_SCALE), tile8(ev_fox_k_norm[0]),
                     tile8(ev_diff_q_norm[0], Q_GAIN_SCALE), tile8(ev_diff_k_norm[0])])
    bf = ev_b_forget[0].astype(F32).reshape(FOX_HEADS, 1)
    depth, hidden = w_gate.shape[0], w_gate.shape[2]
    later = [w_gate.reshape(depth * d, hidden), w_up.reshape(depth * d, hidden),
             w_down.reshape(depth * hidden, d), ev_w_out[0], od_w_qkv[0], od_w_out[0]]
    fq, fk, fvt, dq, dk, dv, caug, wg, wu, wd, wo0, wqkv, wo1 = _proj0(
        x2d, ev_attn_norm[0].astype(F32).reshape(1, d), w0, wf, bd, hg0, bf, seq,
        [a.astype(F32) for a in later])
    wg, wu = wg.reshape(depth, d, hidden), wu.reshape(depth, d, hidden)
    wd = wd.reshape(depth, hidden, d)
    r3 = lambda a: a.reshape(batch, seq, a.shape[-1])
    fox = _fox_attention(r3(fq), r3(fk), fvt, r3(caug), batch, seq)
    lam4 = jnp.stack([ev_lambda_q1[0], ev_lambda_k1[0], ev_lambda_q2[0], ev_lambda_k2[0]]).astype(F32)
    dout = _diff_attention(lam4, ev_diff_subln[0].astype(F32).reshape(1, 2 * HEAD_DIM),
                           r3(dq), r3(dk), r3(dv), gbias, batch, seq, lambda_init)
    x2d = _outmlp(x2d, [fox.reshape(n, FOX_W), dout.reshape(n, DIFF_W)], wo0,
                  ffn_norm[0].astype(F32).reshape(1, d), wg, wu, wd, 0)

    tile4 = lambda g, scale=1.0: jnp.tile(g.astype(F32) * scale, MXU_TILE // HEAD_DIM)
    hg1 = jnp.stack([tile4(od_q_norm[0], Q_GAIN_SCALE), tile4(od_k_norm[0])])
    q, kk, vt = _proj1(x2d, od_attn_norm[0].astype(F32).reshape(1, d), wqkv,
                       bd, hg1, seq)
    swa = _swa_attention(od_sinks[0].astype(F32), r3(q), r3(kk), vt, swbias, batch, seq)
    x2d = _outmlp(x2d, [swa.reshape(n, SWA_HEADS * HEAD_DIM)], wo1,
                  ffn_norm[1].astype(F32).reshape(1, d), wg, wu, wd, 1)
    return x2d.reshape(batch, seq, d)
```

```python
import functools
import math

import numpy as np
import jax
import jax.numpy as jnp
from jax import lax
from jax.experimental import pallas as pl
from jax.experimental.pallas import tpu as pltpu

F32 = jnp.float32
BF16 = jnp.bfloat16

D_MODEL = 1024
HEAD_DIM = 64
FOX_HEADS = 8
DIFF_HEADS = 4
SWA_HEADS = 16
SWA_KV_HEADS = 4
SWA_GROUP = SWA_HEADS // SWA_KV_HEADS
WINDOW = 128
NUM_BUCKETS = 32
MAX_DISTANCE = 128
FFN_HIDDEN = 2816
RMS_EPS = 1e-6
NEG_INF = -1e30
FOX_W = FOX_HEADS * HEAD_DIM
DIFF_W = DIFF_HEADS * 2 * HEAD_DIM
QK_SCALE = HEAD_DIM ** -0.5
LOG2E = math.log2(math.e)
Q_GAIN_SCALE = QK_SCALE * LOG2E

LANES = 128
BF16_SUBLANES = 16
FOX_VT_ROWS = HEAD_DIM + BF16_SUBLANES
MXU_TILE = 256
ROW_TILE = 1024
ATT_Q_BLOCK = 256
FOX_PAIRS_PER_STEP = 2
DIFF_HEADS_PER_STEP = 2
SWA_KV_PER_STEP = 2
FOX_LAG = 4
DIFF_LAG = 2
PROJ_LAG = 1
SWA_LAG = 4
HIDDEN_CHUNK = 768
V7X_VMEM_BYTES = 64 * 1024 * 1024
VMEM_LIMIT = V7X_VMEM_BYTES - 8 * 1024 * 1024

_NT = (((1,), (1,)), ((), ()))


def _dot(a, b):
    return jnp.dot(a, b, preferred_element_type=F32)


def _dot_nt(a, b):
    return lax.dot_general(a, b, _NT, preferred_element_type=F32)


def _lagged(items, first, second, depth=1):
    pending = []
    for item in items:
        pending.append(first(*item))
        if len(pending) > depth:
            second(*pending.pop(0))
    for p in pending:
        second(*p)


def _interleaved(items, first, second, depth):
    def drive(gens):
        done = {}
        while gens:
            for g in list(gens):
                try:
                    next(g)
                except StopIteration as stop:
                    done[id(g)] = stop.value
                    gens.remove(g)
        return done

    pending = []
    for item in items:
        g1 = first(*item)
        gens = [g1] if len(pending) < depth else [g1, second(*pending.pop(0))]
        pending.append(drive(gens)[id(g1)])
    for p in pending:
        drive([second(*p)])


def _hill_order(nblocks):
    return list(range(0, nblocks, 2)) + list(range(nblocks - 1 - nblocks % 2, 0, -2))


def _rms(xf, gain):
    ms = jnp.mean(xf * xf, axis=-1, keepdims=True)
    return xf * lax.rsqrt(ms + RMS_EPS) * gain


def _head_norm(acc, bd, gain):
    msq = _dot((acc * acc).astype(BF16), bd)
    return acc * lax.rsqrt(msq + RMS_EPS) * gain


def _bucket_ids(delta):
    n = np.maximum(delta, 0)
    max_exact = NUM_BUCKETS // 2
    nf = np.maximum(n, 1).astype(np.float64)
    large = max_exact + (np.log(nf / max_exact) / math.log(MAX_DISTANCE / max_exact)
                         * (NUM_BUCKETS - max_exact)).astype(np.int32)
    large = np.minimum(large, NUM_BUCKETS - 1)
    return np.where(n < max_exact, n, large).astype(np.int32)


def _bias_kernel(table_ref, bsw_ref, bg_ref, sw_ref, g_ref):
    h = pl.program_id(0)
    bsw = bsw_ref[...]
    sw = jnp.zeros(bsw.shape, F32)
    for b in range(NUM_BUCKETS):
        sw = jnp.where(bsw == b, table_ref[b, h] * LOG2E, sw)
    sw_ref[0] = jnp.where(bsw < 0, NEG_INF, sw)

    @pl.when(h < DIFF_HEADS)
    def _():
        bg = bg_ref[...]
        far = table_ref[NUM_BUCKETS - 1, h]
        g = jnp.zeros(bg.shape, F32)
        for b in range(NUM_BUCKETS):
            g = jnp.where(bg == b, (table_ref[b, h] - far) * LOG2E, g)
        g_ref[0] = jnp.where(bg < 0, NEG_INF, g)


def _bias_tiles(table):
    a = np.arange(WINDOW)[None, :]
    b = np.arange(2 * WINDOW)[:, None]
    d_sw = WINDOW + a - b
    bsw = np.where((d_sw >= 0) & (d_sw < WINDOW), _bucket_ids(d_sw), -1).astype(np.int32)
    a = np.arange(ATT_Q_BLOCK)[:, None]
    b = np.arange(2 * ATT_Q_BLOCK)[None, :]
    d_g = ATT_Q_BLOCK + a - b
    bg = np.where(d_g >= 0, _bucket_ids(d_g), -1).astype(np.int32)
    nh = table.shape[1]
    return pl.pallas_call(
        _bias_kernel,
        grid=(nh,),
        in_specs=[
            pl.BlockSpec(memory_space=pltpu.SMEM),
            pl.BlockSpec(bsw.shape, lambda h: (0, 0)),
            pl.BlockSpec(bg.shape, lambda h: (0, 0)),
        ],
        out_specs=[
            pl.BlockSpec((1,) + bsw.shape, lambda h: (h, 0, 0)),
            pl.BlockSpec((1,) + bg.shape, lambda h: (jnp.minimum(h, DIFF_HEADS - 1), 0, 0)),
        ],
        out_shape=[
            jax.ShapeDtypeStruct((nh,) + bsw.shape, F32),
            jax.ShapeDtypeStruct((DIFF_HEADS,) + bg.shape, F32),
        ],
        compiler_params=pltpu.CompilerParams(dimension_semantics=("arbitrary",)),
        name="rel_bias_tiles",
    )(table, jnp.asarray(bsw), jnp.asarray(bg))


def _proj0_kernel(*refs, tiles_per_seq, n_cast):
    x_ref, gain_ref, w_ref, wf_ref, bd_ref, hg_ref, bf_ref = refs[0:7]
    cast_src = refs[7:7 + n_cast]
    fq_ref, fk_ref, fv_ref, dq_ref, dk_ref, dv_ref, caug_ref = refs[7 + n_cast:14 + n_cast]
    cast_dst = refs[14 + n_cast:14 + 2 * n_cast]
    carry_ref = refs[14 + 2 * n_cast]
    t = pl.program_id(0)

    @pl.when(t == 0)
    def _():
        carry_ref[...] = jnp.zeros_like(carry_ref)

    h = _rms(x_ref[...], gain_ref[...]).astype(BF16)
    bd = bd_ref[...]
    rows = h.shape[0]

    outs = (fq_ref, fk_ref, fv_ref, dq_ref, dk_ref, dv_ref)
    gains = (0, 1, None, 2, 3, None)
    feature_major = (False, False, True, False, False, False)

    def project(g):
        return g, _dot(h, w_ref[:, g * FOX_W:(g + 1) * FOX_W])

    def epilogue(g, proj):
        if feature_major[g]:
            outs[g][0] = proj.T.astype(BF16)
            return
        if gains[g] is None:
            outs[g][...] = proj.astype(BF16)
            return
        for i in range(FOX_W // MXU_TILE):
            cols = slice(i * MXU_TILE, (i + 1) * MXU_TILE)
            outs[g][:, cols] = _head_norm(
                proj[:, cols], bd, hg_ref[gains[g]:gains[g] + 1, cols]).astype(BF16)

    pending = [project(g) for g in range(PROJ_LAG)]

    z8 = _dot_nt(wf_ref[...], h)[0:FOX_HEADS, :] + bf_ref[...]
    y = -(jnp.maximum(-z8, 0.0) + jnp.log1p(jnp.exp(-jnp.abs(z8))))
    lane = lax.broadcasted_iota(jnp.int32, y.shape, 1)
    shift = 1
    while shift < rows:
        y = y + jnp.where(lane >= shift, pltpu.roll(y, shift, axis=1), 0.0)
        shift *= 2

    carry = jnp.where(t % tiles_per_seq == 0, 0.0, carry_ref[:, 0:1])
    c = y + carry
    carry_ref[...] = jnp.broadcast_to(c[:, rows - 1:rows], carry_ref.shape)
    cn = c * (-LOG2E)
    hi = cn.astype(BF16).astype(F32)
    r1 = cn - hi
    mid = r1.astype(BF16).astype(F32)
    lo = (r1 - mid).astype(BF16).astype(F32)
    half = jnp.concatenate([hi, mid, lo, jnp.zeros((HEAD_DIM - 3 * FOX_HEADS, rows), F32)], axis=0)
    caug_ref[...] = jnp.concatenate([half, half], axis=0).T.astype(BF16)

    for g in range(PROJ_LAG, len(outs)):
        pending.append(project(g))
        epilogue(*pending.pop(0))
    for p in pending:
        epilogue(*p)

    for src, dst in zip(cast_src, cast_dst):
        dst[...] = src[...].astype(BF16)


def _proj0(x2d, gain, w, wf, bd, hg, bf, seq, later_weights):
    n = x2d.shape[0]
    tm = ROW_TILE
    tps = seq // tm
    steps = n // tm
    slab = lambda a: pl.BlockSpec((a.shape[0] // steps, a.shape[1]), lambda t: (t, 0))
    for a in later_weights:
        assert a.shape[0] % (steps * BF16_SUBLANES) == 0, a.shape
    row_spec = pl.BlockSpec((tm, FOX_W), lambda t: (t, 0))
    col_spec = pl.BlockSpec((1, FOX_W, tm), lambda t: (t // tps, 0, t % tps))
    rows_shape = jax.ShapeDtypeStruct((n, FOX_W), BF16)
    cols_shape = jax.ShapeDtypeStruct((n // seq, FOX_W, seq), BF16)
    full = lambda shape: pl.BlockSpec(shape, lambda t: (0,) * len(shape))
    outs = pl.pallas_call(
        functools.partial(_proj0_kernel, tiles_per_seq=tps, n_cast=len(later_weights)),
        grid=(steps,),
        in_specs=[
            pl.BlockSpec((tm, D_MODEL), lambda t: (t, 0)),
            full((1, D_MODEL)),
            full(w.shape),
            full(wf.shape),
            full((MXU_TILE, MXU_TILE)),
            full(hg.shape),
            full(bf.shape),
        ] + [slab(a) for a in later_weights],
        out_specs=[row_spec, row_spec, col_spec, row_spec, row_spec, row_spec,
                   pl.BlockSpec((tm, LANES), lambda t: (t, 0))] + [slab(a) for a in later_weights],
        out_shape=[rows_shape, rows_shape, cols_shape, rows_shape, rows_shape, rows_shape,
                   jax.ShapeDtypeStruct((n, LANES), BF16)]
        + [jax.ShapeDtypeStruct(a.shape, BF16) for a in later_weights],
        scratch_shapes=[pltpu.VMEM((FOX_HEADS, LANES), F32)],
        compiler_params=pltpu.CompilerParams(dimension_semantics=("arbitrary",),
                                             vmem_limit_bytes=VMEM_LIMIT),
        name="proj_even",
    )(x2d, gain, w, wf, bd, hg, bf, *later_weights)
    return outs


def _fox_kernel(q_ref, k_ref, vt_ref, caug_ref, o_ref, ka_ref, va_ref, *, seq):
    jj = pl.program_id(1)
    qb = ATT_Q_BLOCK
    low_s = lax.broadcasted_iota(jnp.int32, (seq, LANES), 1) < HEAD_DIM
    ca = caug_ref[0]
    sub = lax.broadcasted_iota(jnp.int32, (FOX_VT_ROWS - HEAD_DIM, seq), 0)
    ones_row = jnp.where(sub == 0, 1.0, 0.0).astype(BF16)
    for p in range(FOX_PAIRS_PER_STEP):
        kp = k_ref[0, :, p * LANES:(p + 1) * LANES]
        ka_ref[2 * p] = jnp.where(low_s, kp, ca)
        ka_ref[2 * p + 1] = jnp.where(low_s, ca, kp)
        for hh in range(2):
            r = p * LANES + hh * HEAD_DIM
            va_ref[2 * p + hh, 0:HEAD_DIM, :] = vt_ref[0, r:r + HEAD_DIM, :]
            va_ref[2 * p + hh, HEAD_DIM:FOX_VT_ROWS, :] = ones_row
    lane_q = lax.broadcasted_iota(jnp.int32, (qb, LANES), 1)
    key_i = lax.broadcasted_iota(jnp.int32, (qb, qb), 0)
    qry_i = lax.broadcasted_iota(jnp.int32, (qb, qb), 1)
    causal = key_i <= qry_i
    owns = (lane_q < HEAD_DIM, lane_q >= HEAD_DIM)

    def scores(p, qi, hh):
        r0 = qi * qb
        head = 2 * (FOX_PAIRS_PER_STEP * jj + p) + hh
        rel = lane_q - (HEAD_DIM if hh == 0 else 0) - head
        pick = (rel == 0) | (rel == FOX_HEADS) | (rel == 2 * FOX_HEADS)
        qz = jnp.where(owns[hh], q_ref[0, r0:r0 + qb, p * LANES:(p + 1) * LANES],
                       jnp.where(pick, 1.0, 0.0).astype(BF16))
        s_all =_dot_nt(ka_ref[2 * p + hh, 0:(qi + 1) * qb, :], qz)
        yield
        tiles = [s_all[t * qb:(t + 1) * qb] for t in range(qi)]
        tiles.append(jnp.where(causal, s_all[qi * qb:(qi + 1) * qb], NEG_INF))
        return p, qi, hh, tiles

    halves = []

    def finish(p, qi, hh, tiles):
        r0 = qi * qb
        m = jnp.max(functools.reduce(jnp.maximum, tiles), axis=0, keepdims=True)
        acc = None
        for t, s in enumerate(tiles):
            part = _dot(va_ref[2 * p + hh, :, t * qb:(t + 1) * qb], jnp.exp2(s - m).astype(BF16))
            acc = part if acc is None else acc + part
            yield
        halves.append(acc[0:HEAD_DIM] / acc[HEAD_DIM:HEAD_DIM + 1])
        if hh == 1:
            o_ref[0, r0:r0 + qb, p * LANES:(p + 1) * LANES] = jnp.concatenate(
                [halves[-2], halves[-1]], axis=0).T.astype(BF16)

    _interleaved([(p, qi, hh) for p in range(FOX_PAIRS_PER_STEP)
                  for qi in _hill_order(seq // qb) for hh in range(2)], scores, finish,
                 depth=FOX_LAG)


def _fox_attention(fq, fk, fvt, caug, batch, seq):
    width = FOX_PAIRS_PER_STEP * LANES
    blk = pl.BlockSpec((1, seq, width), lambda b, j: (b, 0, j))
    nheads = 2 * FOX_PAIRS_PER_STEP
    return pl.pallas_call(
        functools.partial(_fox_kernel, seq=seq),
        grid=(batch, FOX_W // width),
        in_specs=[blk, blk, pl.BlockSpec((1, width, seq), lambda b, j: (b, j, 0)),
                  pl.BlockSpec((1, seq, LANES), lambda b, j: (b, 0, 0))],
        out_specs=blk,
        out_shape=jax.ShapeDtypeStruct((batch, seq, FOX_W), BF16),
        scratch_shapes=[pltpu.VMEM((nheads, seq, LANES), BF16),
                        pltpu.VMEM((nheads, FOX_VT_ROWS, seq), BF16)],
        compiler_params=pltpu.CompilerParams(dimension_semantics=("parallel", "parallel"),
                                             vmem_limit_bytes=VMEM_LIMIT),
        name="fox_attention",
    )(fq, fk, fvt, caug)


def _diff_kernel(lam_ref, g_ref, q_ref, k_ref, v_ref, gb_ref, o_ref, va_ref, *, seq, lambda_init):
    qb = ATT_Q_BLOCK
    lv = lam_ref[...]
    lam = (jnp.exp(jnp.sum(lv[0:1] * lv[1:2], axis=-1, keepdims=True))
           - jnp.exp(jnp.sum(lv[2:3] * lv[3:4], axis=-1, keepdims=True)) + lambda_init)
    for p in range(DIFF_HEADS_PER_STEP):
        va_ref[p, :, 0:LANES] = v_ref[0, :, p * LANES:(p + 1) * LANES]
        va_ref[p, :, LANES:2 * LANES] = jnp.ones((seq, LANES), BF16)
    lane_q = lax.broadcasted_iota(jnp.int32, (qb, LANES), 1)
    bias_prev = [jnp.concatenate([gb_ref[p, :, 0:qb]] * 2, axis=0)
                 for p in range(DIFF_HEADS_PER_STEP)]
    bias_diag = [jnp.concatenate([gb_ref[p, :, qb:2 * qb]] * 2, axis=0)
                 for p in range(DIFF_HEADS_PER_STEP)]
    gain = g_ref[...] * (1.0 - lambda_init)
    zero = jnp.zeros((), BF16)

    def scores(p, qi):
        r0 = qi * qb
        qp = q_ref[0, r0:r0 + qb, p * LANES:(p + 1) * LANES]
        qz = jnp.concatenate([jnp.where(lane_q < HEAD_DIM, qp, zero),
                              jnp.where(lane_q < HEAD_DIM, zero, qp)], axis=0)
        tiles = []
        for t in range(qi + 1):
            s = _dot_nt(qz, k_ref[0, t * qb:(t + 1) * qb, p * LANES:(p + 1) * LANES])
            tiles.append(s + bias_diag[p] if t == qi else s + bias_prev[p] if t == qi - 1 else s)
            yield
        return p, qi, tiles

    def finish(p, qi, tiles):
        r0 = qi * qb
        m = jnp.max(functools.reduce(jnp.maximum, tiles), axis=-1, keepdims=True)
        acc = None
        for t, s in enumerate(tiles):
            part = _dot(jnp.exp2(s - m).astype(BF16), va_ref[p, t * qb:(t + 1) * qb, :])
            acc = part if acc is None else acc + part
            yield
        a = acc[:, 0:LANES] / acc[:, LANES:2 * LANES]
        o = a[0:qb] - lam * a[qb:2 * qb]
        o_ref[0, r0:r0 + qb, p * LANES:(p + 1) * LANES] = _rms(o, gain).astype(BF16)

    _interleaved([(p, qi) for p in range(DIFF_HEADS_PER_STEP) for qi in _hill_order(seq // qb)],
                 scores, finish, depth=DIFF_LAG)


def _diff_attention(lam4, subln, dq, dk, dv, gbias, batch, seq, lambda_init):
    hps = DIFF_HEADS_PER_STEP
    blk = pl.BlockSpec((1, seq, hps * LANES), lambda h, b: (b, 0, h))
    return pl.pallas_call(
        functools.partial(_diff_kernel, seq=seq, lambda_init=lambda_init),
        grid=(DIFF_HEADS // hps, batch),
        in_specs=[
            pl.BlockSpec(lam4.shape, lambda h, b: (0, 0)),
            pl.BlockSpec(subln.shape, lambda h, b: (0, 0)),
            blk, blk, blk,
            pl.BlockSpec((hps, ATT_Q_BLOCK, 2 * ATT_Q_BLOCK), lambda h, b: (h, 0, 0)),
        ],
        out_specs=blk,
        out_shape=jax.ShapeDtypeStruct((batch, seq, DIFF_W), BF16),
        scratch_shapes=[pltpu.VMEM((hps, seq, 2 * LANES), BF16)],
        compiler_params=pltpu.CompilerParams(dimension_semantics=("parallel", "parallel"),
                                             vmem_limit_bytes=VMEM_LIMIT),
        name="diff_attention",
    )(lam4, subln, dq, dk, dv, gbias)


def _proj1_kernel(x_ref, gain_ref, w_ref, bd_ref, hg_ref, q_ref, kk_ref, vt_ref):
    h = _rms(x_ref[...], gain_ref[...]).astype(BF16)
    bd = bd_ref[...]
    qw = SWA_HEADS * HEAD_DIM
    kw = SWA_KV_HEADS * HEAD_DIM
    low = lax.broadcasted_iota(jnp.int32, (h.shape[0], LANES), 1) < HEAD_DIM

    def project(c0):
        return c0, _dot(h, w_ref[:, c0:c0 + 2 * MXU_TILE])

    def epilogue(c0, proj):
        if c0 < qw:
            for i in range(2):
                cols = slice(i * MXU_TILE, (i + 1) * MXU_TILE)
                q_ref[:, c0 + i * MXU_TILE:c0 + (i + 1) * MXU_TILE] = _head_norm(
                    proj[:, cols], bd, hg_ref[0:1, :]).astype(BF16)
            return
        kn = _head_norm(proj[:, 0:kw], bd, hg_ref[1:2, :])
        for p in range(kw // LANES):
            pair = kn[:, p * LANES:(p + 1) * LANES]
            swapped = pltpu.roll(pair, HEAD_DIM, axis=1)
            kk_ref[:, (2 * p) * LANES:(2 * p + 1) * LANES] = jnp.where(low, pair, swapped).astype(BF16)
            kk_ref[:, (2 * p + 1) * LANES:(2 * p + 2) * LANES] = jnp.where(low, swapped, pair).astype(BF16)
        vt_ref[0] = proj[:, kw:2 * kw].T.astype(BF16)

    _lagged([(qw,)] + [(c0,) for c0 in range(0, qw, 2 * MXU_TILE)], project, epilogue,
            depth=PROJ_LAG)


def _proj1(x2d, gain, w, bd, hg, seq):
    n = x2d.shape[0]
    tm = ROW_TILE
    tps = seq // tm
    qw = SWA_HEADS * HEAD_DIM
    kw = SWA_KV_HEADS * HEAD_DIM
    full = lambda shape: pl.BlockSpec(shape, lambda t: (0,) * len(shape))
    return pl.pallas_call(
        _proj1_kernel,
        grid=(n // tm,),
        in_specs=[
            pl.BlockSpec((tm, D_MODEL), lambda t: (t, 0)),
            full((1, D_MODEL)),
            full(w.shape),
            full((MXU_TILE, MXU_TILE)),
            full(hg.shape),
        ],
        out_specs=[pl.BlockSpec((tm, qw), lambda t: (t, 0)),
                   pl.BlockSpec((tm, 2 * kw), lambda t: (t, 0)),
                   pl.BlockSpec((1, kw, tm), lambda t: (t // tps, 0, t % tps))],
        out_shape=[jax.ShapeDtypeStruct((n, qw), BF16),
                   jax.ShapeDtypeStruct((n, 2 * kw), BF16),
                   jax.ShapeDtypeStruct((n // seq, kw, seq), BF16)],
        compiler_params=pltpu.CompilerParams(dimension_semantics=("parallel",),
                                             vmem_limit_bytes=VMEM_LIMIT),
        name="proj_odd",
    )(x2d, gain, w, bd, hg)


def _swa_kernel(sink_ref, q_ref, kk_ref, vt_ref, sb_ref, o_ref, va_ref, *, seq):
    jj = pl.program_id(0)
    w = WINDOW
    gw = SWA_GROUP * HEAD_DIM
    sub = lax.broadcasted_iota(jnp.int32, (LANES - HEAD_DIM, seq), 0)
    sinks = []
    for p in range(SWA_KV_PER_STEP):
        va_ref[p, 0:HEAD_DIM, :] = vt_ref[0, p * HEAD_DIM:(p + 1) * HEAD_DIM, :]
        va_ref[p, HEAD_DIM:LANES, :] = jnp.where(sub == 0, 1.0, 0.0).astype(BF16)
        head0 = (jj * SWA_KV_PER_STEP + p) * SWA_GROUP
        sinks.append(jnp.concatenate(
            [jnp.full((1, w), sink_ref[head0 + g] * LOG2E, F32) for g in range(SWA_GROUP)],
            axis=1))
    low = lax.broadcasted_iota(jnp.int32, (w, LANES), 1) < HEAD_DIM
    zero = jnp.zeros((), BF16)

    def scores(p, n):
        r0 = n * w
        qb = q_ref[0, r0:r0 + w, p * gw:(p + 1) * gw]
        qa, qc = qb[:, 0:LANES], qb[:, LANES:2 * LANES]
        q4 = jnp.concatenate([jnp.where(low, qa, zero), jnp.where(low, zero, qa),
                              jnp.where(low, qc, zero), jnp.where(low, zero, qc)], axis=0)
        lanes = slice(p * LANES, (p + 1) * LANES)
        if n == 0:
            return p, n, _dot_nt(kk_ref[0, 0:w, lanes], q4) + sb_ref[p, w:2 * w, :]
        return p, n, _dot_nt(kk_ref[0, r0 - w:r0 + w, lanes], q4) + sb_ref[p]

    def finish(p, n, st):
        r0 = n * w
        vals = va_ref[p, :, 0:w] if n == 0 else va_ref[p, :, r0 - w:r0 + w]
        m = jnp.maximum(jnp.max(st, axis=0, keepdims=True), sinks[p])
        acc = _dot(vals, jnp.exp2(st - m).astype(BF16))
        denom = acc[HEAD_DIM:HEAD_DIM + 1, :] + jnp.exp2(sinks[p] - m)
        ot = acc[0:HEAD_DIM, :] / denom
        o_ref[0, r0:r0 + w, p * gw:(p + 1) * gw] = jnp.concatenate(
            [jnp.concatenate([ot[:, 0:w], ot[:, w:2 * w]], axis=0).T,
             jnp.concatenate([ot[:, 2 * w:3 * w], ot[:, 3 * w:4 * w]], axis=0).T],
            axis=1).astype(BF16)

    _lagged([(p, n) for p in range(SWA_KV_PER_STEP) for n in range(seq // w)], scores, finish,
            depth=SWA_LAG)


def _swa_attention(sinks, q, kk, vt, swbias, batch, seq):
    kps = SWA_KV_PER_STEP
    gw = kps * SWA_GROUP * HEAD_DIM
    return pl.pallas_call(
        functools.partial(_swa_kernel, seq=seq),
        grid=(SWA_KV_HEADS // kps, batch),
        in_specs=[
            pl.BlockSpec(memory_space=pltpu.SMEM),
            pl.BlockSpec((1, seq, gw), lambda h, b: (b, 0, h)),
            pl.BlockSpec((1, seq, kps * LANES), lambda h, b: (b, 0, h)),
            pl.BlockSpec((1, kps * HEAD_DIM, seq), lambda h, b: (b, h, 0)),
            pl.BlockSpec((kps, 2 * WINDOW, SWA_GROUP * WINDOW), lambda h, b: (h, 0, 0)),
        ],
        out_specs=pl.BlockSpec((1, seq, gw), lambda h, b: (b, 0, h)),
        out_shape=jax.ShapeDtypeStruct((batch, seq, SWA_HEADS * HEAD_DIM), BF16),
        scratch_shapes=[pltpu.VMEM((kps, LANES, seq), BF16)],
        compiler_params=pltpu.CompilerParams(dimension_semantics=("parallel", "parallel"),
                                             vmem_limit_bytes=VMEM_LIMIT),
        name="swa_attention",
    )(sinks, q, kk, vt, swbias)


def _outmlp_kernel(*refs, n_mix):
    x_ref = refs[0]
    a_refs = refs[1:1 + n_mix]
    wo_ref, gain_ref, wg_ref, wu_ref, wd_ref, o_ref = refs[1 + n_mix:]
    mix = jnp.concatenate([a_ref[...] for a_ref in a_refs], axis=1)
    x1 = x_ref[...] + _dot(mix, wo_ref[...])
    h = _rms(x1, gain_ref[...]).astype(BF16)
    out = x1
    for c0 in range(0, FFN_HIDDEN, HIDDEN_CHUNK):
        c1 = min(c0 + HIDDEN_CHUNK, FFN_HIDDEN)
        g = _dot(h, wg_ref[:, c0:c1])
        u = _dot(h, wu_ref[:, c0:c1])
        y = (g * (1.0 / (1.0 + jnp.exp(-g))) * u).astype(BF16)
        out = out + _dot(y, wd_ref[c0:c1, :])
    o_ref[...] = out


def _outmlp(x2d, mixes, wo, gain, wg, wu, wd, layer):
    n = x2d.shape[0]
    tm = ROW_TILE
    full = lambda shape: pl.BlockSpec(shape, lambda t: (0,) * len(shape),
                                      pipeline_mode=pl.Buffered(1))
    of_layer = lambda w: pl.BlockSpec((None,) + w.shape[1:], lambda t: (layer, 0, 0),
                                      pipeline_mode=pl.Buffered(1))
    in_specs = [pl.BlockSpec((tm, D_MODEL), lambda t: (t, 0))]
    in_specs += [pl.BlockSpec((tm, a.shape[1]), lambda t: (t, 0)) for a in mixes]
    in_specs += [full(wo.shape), full(gain.shape), of_layer(wg), of_layer(wu), of_layer(wd)]
    return pl.pallas_call(
        functools.partial(_outmlp_kernel, n_mix=len(mixes)),
        grid=(n // tm,),
        in_specs=in_specs,
        out_specs=pl.BlockSpec((tm, D_MODEL), lambda t: (t, 0)),
        out_shape=jax.ShapeDtypeStruct((n, D_MODEL), F32),
        compiler_params=pltpu.CompilerParams(dimension_semantics=("parallel",),
                                             vmem_limit_bytes=VMEM_LIMIT),
        name="outproj_swiglu",
    )(x2d, *mixes, wo, gain, wg, wu, wd)


def _block_diag_mean():
    i = np.arange(MXU_TILE)
    same = (i[:, None] // HEAD_DIM) == (i[None, :] // HEAD_DIM)
    return jnp.asarray(same / HEAD_DIM, BF16)


def kernel(x, rel_bias_table, ev_attn_norm, ev_w_in, ev_b_forget, ev_fox_q_norm, ev_fox_k_norm,
           ev_diff_q_norm, ev_diff_k_norm, ev_lambda_q1, ev_lambda_k1, ev_lambda_q2, ev_lambda_k2,
           ev_diff_subln, ev_w_out, od_attn_norm, od_w_qkv, od_q_norm, od_k_norm, od_sinks, od_w_out,
           ffn_norm, w_gate, w_up, w_down):
    batch, seq, d = x.shape
    n = batch * seq
    x2d = x.reshape(n, d)
    bd = _block_diag_mean()
    swbias, gbias = _bias_tiles(rel_bias_table.astype(F32))
    swbias = swbias.reshape(SWA_KV_HEADS, SWA_GROUP, 2 * WINDOW, WINDOW).transpose(0, 2, 1, 3)
    swbias = swbias.reshape(SWA_KV_HEADS, 2 * WINDOW, SWA_GROUP * WINDOW)

    lambda_init = 0.8 - 0.6 * math.exp(-0.3 * 0)
    w_in = ev_w_in[0]
    ff0 = 3 * FOX_W
    d0 = ff0 + FOX_HEADS
    w0 = jnp.concatenate([w_in[:, 0:ff0], w_in[:, d0:d0 + 3 * DIFF_W]], axis=1).astype(BF16)
    wf = jnp.concatenate(
        [w_in[:, ff0:d0].T, jnp.zeros((BF16_SUBLANES - FOX_HEADS, d), w_in.dtype)],
        axis=0).astype(BF16)
    tile8 = lambda g, scale=1.0: jnp.tile(g.astype(F32) * scale, FOX_W // HEAD_DIM)
    hg0 = jnp.stack([tile8(ev_fox_q_norm[0], Q_GAIN_SCALE), tile8(ev_fox_k_norm[0]),
                     tile8(ev_diff_q_norm[0], Q_GAIN_SCALE), tile8(ev_diff_k_norm[0])])
    bf = ev_b_forget[0].astype(F32).reshape(FOX_HEADS, 1)
    depth, hidden = w_gate.shape[0], w_gate.shape[2]
    later = [w_gate.reshape(depth * d, hidden), w_up.reshape(depth * d, hidden),
             w_down.reshape(depth * hidden, d), ev_w_out[0], od_w_qkv[0], od_w_out[0]]
    fq, fk, fvt, dq, dk, dv, caug, wg, wu, wd, wo0, wqkv, wo1 = _proj0(
        x2d, ev_attn_norm[0].astype(F32).reshape(1, d), w0, wf, bd, hg0, bf, seq,
        [a.astype(F32) for a in later])
    wg, wu = wg.reshape(depth, d, hidden), wu.reshape(depth, d, hidden)
    wd = wd.reshape(depth, hidden, d)
    r3 = lambda a: a.reshape(batch, seq, a.shape[-1])
    fox = _fox_attention(r3(fq), r3(fk), fvt, r3(caug), batch, seq)
    lam4 = jnp.stack([ev_lambda_q1[0], ev_lambda_k1[0], ev_lambda_q2[0], ev_lambda_k2[0]]).astype(F32)
    dout = _diff_attention(lam4, ev_diff_subln[0].astype(F32).reshape(1, 2 * HEAD_DIM),
                           r3(dq), r3(dk), r3(dv), gbias, batch, seq, lambda_init)
    x2d = _outmlp(x2d, [fox.reshape(n, FOX_W), dout.reshape(n, DIFF_W)], wo0,
                  ffn_norm[0].astype(F32).reshape(1, d), wg, wu, wd, 0)

    tile4 = lambda g, scale=1.0: jnp.tile(g.astype(F32) * scale, MXU_TILE // HEAD_DIM)
    hg1 = jnp.stack([tile4(od_q_norm[0], Q_GAIN_SCALE), tile4(od_k_norm[0])])
    q, kk, vt = _proj1(x2d, od_attn_norm[0].astype(F32).reshape(1, d), wqkv,
                       bd, hg1, seq)
    swa = _swa_attention(od_sinks[0].astype(F32), r3(q), r3(kk), vt, swbias, batch, seq)
    x2d = _outmlp(x2d, [swa.reshape(n, SWA_HEADS * HEAD_DIM)], wo1,
                  ffn_norm[1].astype(F32).reshape(1, d), wg, wu, wd, 1)
    return x2d.reshape(batch, seq, d)
```

```python
import functools
import math

import numpy as np
import jax
import jax.numpy as jnp
from jax import lax
from jax.experimental import pallas as pl
from jax.experimental.pallas import tpu as pltpu

F32 = jnp.float32
BF16 = jnp.bfloat16

D_MODEL = 1024
HEAD_DIM = 64
FOX_HEADS = 8
DIFF_HEADS = 4
SWA_HEADS = 16
SWA_KV_HEADS = 4
SWA_GROUP = SWA_HEADS // SWA_KV_HEADS
WINDOW = 128
NUM_BUCKETS = 32
MAX_DISTANCE = 128
FFN_HIDDEN = 2816
RMS_EPS = 1e-6
NEG_INF = -1e30
FOX_W = FOX_HEADS * HEAD_DIM
DIFF_W = DIFF_HEADS * 2 * HEAD_DIM
QK_SCALE = HEAD_DIM ** -0.5
LOG2E = math.log2(math.e)
Q_GAIN_SCALE = QK_SCALE * LOG2E

LANES = 128
BF16_SUBLANES = 16
FOX_VT_ROWS = HEAD_DIM + BF16_SUBLANES
MXU_TILE = 256
ROW_TILE = 1024
ATT_Q_BLOCK = 256
FOX_PAIRS_PER_STEP = 2
DIFF_HEADS_PER_STEP = 2
SWA_KV_PER_STEP = 2
FOX_LAG = 6
DIFF_LAG = 3
PROJ_LAG = 1
SWA_LAG = 6
HIDDEN_CHUNK = 768
V7X_VMEM_BYTES = 64 * 1024 * 1024
VMEM_LIMIT = V7X_VMEM_BYTES - 8 * 1024 * 1024

_NT = (((1,), (1,)), ((), ()))


def _dot(a, b):
    return jnp.dot(a, b, preferred_element_type=F32)


def _dot_nt(a, b):
    return lax.dot_general(a, b, _NT, preferred_element_type=F32)


def _lagged(items, first, second, depth=1):
    pending = []
    for item in items:
        pending.append(first(*item))
        if len(pending) > depth:
            second(*pending.pop(0))
    for p in pending:
        second(*p)


def _interleaved(items, first, second, depth):
    def drive(gens):
        done = {}
        while gens:
            for g in list(gens):
                try:
                    next(g)
                except StopIteration as stop:
                    done[id(g)] = stop.value
                    gens.remove(g)
        return done

    pending = []
    for item in items:
        g1 = first(*item)
        gens = [g1] if len(pending) < depth else [g1, second(*pending.pop(0))]
        pending.append(drive(gens)[id(g1)])
    for p in pending:
        drive([second(*p)])


def _hill_order(nblocks):
    return list(range(0, nblocks, 2)) + list(range(nblocks - 1 - nblocks % 2, 0, -2))


def _rms(xf, gain):
    ms = jnp.mean(xf * xf, axis=-1, keepdims=True)
    return xf * lax.rsqrt(ms + RMS_EPS) * gain


def _head_norm(acc, bd, gain):
    msq = _dot((acc * acc).astype(BF16), bd)
    return acc * lax.rsqrt(msq + RMS_EPS) * gain


def _bucket_ids(delta):
    n = np.maximum(delta, 0)
    max_exact = NUM_BUCKETS // 2
    nf = np.maximum(n, 1).astype(np.float64)
    large = max_exact + (np.log(nf / max_exact) / math.log(MAX_DISTANCE / max_exact)
                         * (NUM_BUCKETS - max_exact)).astype(np.int32)
    large = np.minimum(large, NUM_BUCKETS - 1)
    return np.where(n < max_exact, n, large).astype(np.int32)


def _bias_kernel(table_ref, bsw_ref, bg_ref, sw_ref, g_ref):
    h = pl.program_id(0)
    bsw = bsw_ref[...]
    sw = jnp.zeros(bsw.shape, F32)
    for b in range(NUM_BUCKETS):
        sw = jnp.where(bsw == b, table_ref[b, h] * LOG2E, sw)
    sw_ref[0] = jnp.where(bsw < 0, NEG_INF, sw)

    @pl.when(h < DIFF_HEADS)
    def _():
        bg = bg_ref[...]
        far = table_ref[NUM_BUCKETS - 1, h]
        g = jnp.zeros(bg.shape, F32)
        for b in range(NUM_BUCKETS):
            g = jnp.where(bg == b, (table_ref[b, h] - far) * LOG2E, g)
        g_ref[0] = jnp.where(bg < 0, NEG_INF, g)


def _bias_tiles(table):
    a = np.arange(WINDOW)[None, :]
    b = np.arange(2 * WINDOW)[:, None]
    d_sw = WINDOW + a - b
    bsw = np.where((d_sw >= 0) & (d_sw < WINDOW), _bucket_ids(d_sw), -1).astype(np.int32)
    a = np.arange(ATT_Q_BLOCK)[:, None]
    b = np.arange(2 * ATT_Q_BLOCK)[None, :]
    d_g = ATT_Q_BLOCK + a - b
    bg = np.where(d_g >= 0, _bucket_ids(d_g), -1).astype(np.int32)
    nh = table.shape[1]
    return pl.pallas_call(
        _bias_kernel,
        grid=(nh,),
        in_specs=[
            pl.BlockSpec(memory_space=pltpu.SMEM),
            pl.BlockSpec(bsw.shape, lambda h: (0, 0)),
            pl.BlockSpec(bg.shape, lambda h: (0, 0)),
        ],
        out_specs=[
            pl.BlockSpec((1,) + bsw.shape, lambda h: (h, 0, 0)),
            pl.BlockSpec((1,) + bg.shape, lambda h: (jnp.minimum(h, DIFF_HEADS - 1), 0, 0)),
        ],
        out_shape=[
            jax.ShapeDtypeStruct((nh,) + bsw.shape, F32),
            jax.ShapeDtypeStruct((DIFF_HEADS,) + bg.shape, F32),
        ],
        compiler_params=pltpu.CompilerParams(dimension_semantics=("arbitrary",)),
        name="rel_bias_tiles",
    )(table, jnp.asarray(bsw), jnp.asarray(bg))


def _proj0_kernel(*refs, tiles_per_seq, n_cast):
    x_ref, gain_ref, w_ref, wf_ref, bd_ref, hg_ref, bf_ref = refs[0:7]
    cast_src = refs[7:7 + n_cast]
    fq_ref, fk_ref, fv_ref, dq_ref, dk_ref, dv_ref, caug_ref = refs[7 + n_cast:14 + n_cast]
    cast_dst = refs[14 + n_cast:14 + 2 * n_cast]
    carry_ref = refs[14 + 2 * n_cast]
    t = pl.program_id(0)

    @pl.when(t == 0)
    def _():
        carry_ref[...] = jnp.zeros_like(carry_ref)

    h = _rms(x_ref[...], gain_ref[...]).astype(BF16)
    bd = bd_ref[...]
    rows = h.shape[0]

    outs = (fq_ref, fk_ref, fv_ref, dq_ref, dk_ref, dv_ref)
    gains = (0, 1, None, 2, 3, None)
    feature_major = (False, False, True, False, False, False)

    def project(g):
        return g, _dot(h, w_ref[:, g * FOX_W:(g + 1) * FOX_W])

    def epilogue(g, proj):
        if feature_major[g]:
            outs[g][0] = proj.T.astype(BF16)
            return
        if gains[g] is None:
            outs[g][...] = proj.astype(BF16)
            return
        for i in range(FOX_W // MXU_TILE):
            cols = slice(i * MXU_TILE, (i + 1) * MXU_TILE)
            outs[g][:, cols] = _head_norm(
                proj[:, cols], bd, hg_ref[gains[g]:gains[g] + 1, cols]).astype(BF16)

    pending = [project(g) for g in range(PROJ_LAG)]

    z8 = _dot_nt(wf_ref[...], h)[0:FOX_HEADS, :] + bf_ref[...]
    y = -(jnp.maximum(-z8, 0.0) + jnp.log1p(jnp.exp(-jnp.abs(z8))))
    lane = lax.broadcasted_iota(jnp.int32, y.shape, 1)
    shift = 1
    while shift < rows:
        y = y + jnp.where(lane >= shift, pltpu.roll(y, shift, axis=1), 0.0)
        shift *= 2

    carry = jnp.where(t % tiles_per_seq == 0, 0.0, carry_ref[:, 0:1])
    c = y + carry
    carry_ref[...] = jnp.broadcast_to(c[:, rows - 1:rows], carry_ref.shape)
    cn = c * (-LOG2E)
    hi = cn.astype(BF16).astype(F32)
    r1 = cn - hi
    mid = r1.astype(BF16).astype(F32)
    lo = (r1 - mid).astype(BF16).astype(F32)
    half = jnp.concatenate([hi, mid, lo, jnp.zeros((HEAD_DIM - 3 * FOX_HEADS, rows), F32)], axis=0)
    caug_ref[...] = jnp.concatenate([half, half], axis=0).T.astype(BF16)

    for g in range(PROJ_LAG, len(outs)):
        pending.append(project(g))
        epilogue(*pending.pop(0))
    for p in pending:
        epilogue(*p)

    for src, dst in zip(cast_src, cast_dst):
        dst[...] = src[...].astype(BF16)


def _proj0(x2d, gain, w, wf, bd, hg, bf, seq, later_weights):
    n = x2d.shape[0]
    tm = ROW_TILE
    tps = seq // tm
    steps = n // tm
    slab = lambda a: pl.BlockSpec((a.shape[0] // steps, a.shape[1]), lambda t: (t, 0))
    for a in later_weights:
        assert a.shape[0] % (steps * BF16_SUBLANES) == 0, a.shape
    row_spec = pl.BlockSpec((tm, FOX_W), lambda t: (t, 0))
    col_spec = pl.BlockSpec((1, FOX_W, tm), lambda t: (t // tps, 0, t % tps))
    rows_shape = jax.ShapeDtypeStruct((n, FOX_W), BF16)
    cols_shape = jax.ShapeDtypeStruct((n // seq, FOX_W, seq), BF16)
    full = lambda shape: pl.BlockSpec(shape, lambda t: (0,) * len(shape))
    outs = pl.pallas_call(
        functools.partial(_proj0_kernel, tiles_per_seq=tps, n_cast=len(later_weights)),
        grid=(steps,),
        in_specs=[
            pl.BlockSpec((tm, D_MODEL), lambda t: (t, 0)),
            full((1, D_MODEL)),
            full(w.shape),
            full(wf.shape),
            full((MXU_TILE, MXU_TILE)),
            full(hg.shape),
            full(bf.shape),
        ] + [slab(a) for a in later_weights],
        out_specs=[row_spec, row_spec, col_spec, row_spec, row_spec, row_spec,
                   pl.BlockSpec((tm, LANES), lambda t: (t, 0))] + [slab(a) for a in later_weights],
        out_shape=[rows_shape, rows_shape, cols_shape, rows_shape, rows_shape, rows_shape,
                   jax.ShapeDtypeStruct((n, LANES), BF16)]
        + [jax.ShapeDtypeStruct(a.shape, BF16) for a in later_weights],
        scratch_shapes=[pltpu.VMEM((FOX_HEADS, LANES), F32)],
        compiler_params=pltpu.CompilerParams(dimension_semantics=("arbitrary",),
                                             vmem_limit_bytes=VMEM_LIMIT),
        name="proj_even",
    )(x2d, gain, w, wf, bd, hg, bf, *later_weights)
    return outs


def _fox_kernel(q_ref, k_ref, vt_ref, caug_ref, o_ref, ka_ref, va_ref, *, seq):
    jj = pl.program_id(1)
    qb = ATT_Q_BLOCK
    low_s = lax.broadcasted_iota(jnp.int32, (seq, LANES), 1) < HEAD_DIM
    ca = caug_ref[0]
    sub = lax.broadcasted_iota(jnp.int32, (FOX_VT_ROWS - HEAD_DIM, seq), 0)
    ones_row = jnp.where(sub == 0, 1.0, 0.0).astype(BF16)
    for p in range(FOX_PAIRS_PER_STEP):
        kp = k_ref[0, :, p * LANES:(p + 1) * LANES]
        ka_ref[2 * p] = jnp.where(low_s, kp, ca)
        ka_ref[2 * p + 1] = jnp.where(low_s, ca, kp)
        for hh in range(2):
            r = p * LANES + hh * HEAD_DIM
            va_ref[2 * p + hh, 0:HEAD_DIM, :] = vt_ref[0, r:r + HEAD_DIM, :]
            va_ref[2 * p + hh, HEAD_DIM:FOX_VT_ROWS, :] = ones_row
    lane_q = lax.broadcasted_iota(jnp.int32, (qb, LANES), 1)
    key_i = lax.broadcasted_iota(jnp.int32, (qb, qb), 0)
    qry_i = lax.broadcasted_iota(jnp.int32, (qb, qb), 1)
    causal = key_i <= qry_i
    owns = (lane_q < HEAD_DIM, lane_q >= HEAD_DIM)

    def scores(p, qi, hh):
        r0 = qi * qb
        head = 2 * (FOX_PAIRS_PER_STEP * jj + p) + hh
        rel = lane_q - (HEAD_DIM if hh == 0 else 0) - head
        pick = (rel == 0) | (rel == FOX_HEADS) | (rel == 2 * FOX_HEADS)
        qz = jnp.where(owns[hh], q_ref[0, r0:r0 + qb, p * LANES:(p + 1) * LANES],
                       jnp.where(pick, 1.0, 0.0).astype(BF16))
        s_all =_dot_nt(ka_ref[2 * p + hh, 0:(qi + 1) * qb, :], qz)
        yield
        tiles = [s_all[t * qb:(t + 1) * qb] for t in range(qi)]
        tiles.append(jnp.where(causal, s_all[qi * qb:(qi + 1) * qb], NEG_INF))
        return p, qi, hh, tiles

    halves = []

    def finish(p, qi, hh, tiles):
        r0 = qi * qb
        m = jnp.max(functools.reduce(jnp.maximum, tiles), axis=0, keepdims=True)
        acc = None
        for t, s in enumerate(tiles):
            part = _dot(va_ref[2 * p + hh, :, t * qb:(t + 1) * qb], jnp.exp2(s - m).astype(BF16))
            acc = part if acc is None else acc + part
            yield
        halves.append(acc[0:HEAD_DIM] / acc[HEAD_DIM:HEAD_DIM + 1])
        if hh == 1:
            o_ref[0, r0:r0 + qb, p * LANES:(p + 1) * LANES] = jnp.concatenate(
                [halves[-2], halves[-1]], axis=0).T.astype(BF16)

    _interleaved([(p, qi, hh) for p in range(FOX_PAIRS_PER_STEP)
                  for qi in _hill_order(seq // qb) for hh in range(2)], scores, finish,
                 depth=FOX_LAG)


def _fox_attention(fq, fk, fvt, caug, batch, seq):
    width = FOX_PAIRS_PER_STEP * LANES
    blk = pl.BlockSpec((1, seq, width), lambda b, j: (b, 0, j))
    nheads = 2 * FOX_PAIRS_PER_STEP
    return pl.pallas_call(
        functools.partial(_fox_kernel, seq=seq),
        grid=(batch, FOX_W // width),
        in_specs=[blk, blk, pl.BlockSpec((1, width, seq), lambda b, j: (b, j, 0)),
                  pl.BlockSpec((1, seq, LANES), lambda b, j: (b, 0, 0))],
        out_specs=blk,
        out_shape=jax.ShapeDtypeStruct((batch, seq, FOX_W), BF16),
        scratch_shapes=[pltpu.VMEM((nheads, seq, LANES), BF16),
                        pltpu.VMEM((nheads, FOX_VT_ROWS, seq), BF16)],
        compiler_params=pltpu.CompilerParams(dimension_semantics=("parallel", "parallel"),
                                             vmem_limit_bytes=VMEM_LIMIT),
        name="fox_attention",
    )(fq, fk, fvt, caug)


def _diff_kernel(lam_ref, g_ref, q_ref, k_ref, v_ref, gb_ref, o_ref, va_ref, *, seq, lambda_init):
    qb = ATT_Q_BLOCK
    lv = lam_ref[...]
    lam = (jnp.exp(jnp.sum(lv[0:1] * lv[1:2], axis=-1, keepdims=True))
           - jnp.exp(jnp.sum(lv[2:3] * lv[3:4], axis=-1, keepdims=True)) + lambda_init)
    for p in range(DIFF_HEADS_PER_STEP):
        va_ref[p, :, 0:LANES] = v_ref[0, :, p * LANES:(p + 1) * LANES]
        va_ref[p, :, LANES:2 * LANES] = jnp.ones((seq, LANES), BF16)
    lane_q = lax.broadcasted_iota(jnp.int32, (qb, LANES), 1)
    bias_prev = [jnp.concatenate([gb_ref[p, :, 0:qb]] * 2, axis=0)
                 for p in range(DIFF_HEADS_PER_STEP)]
    bias_diag = [jnp.concatenate([gb_ref[p, :, qb:2 * qb]] * 2, axis=0)
                 for p in range(DIFF_HEADS_PER_STEP)]
    gain = g_ref[...] * (1.0 - lambda_init)
    zero = jnp.zeros((), BF16)

    def scores(p, qi):
        r0 = qi * qb
        qp = q_ref[0, r0:r0 + qb, p * LANES:(p + 1) * LANES]
        qz = jnp.concatenate([jnp.where(lane_q < HEAD_DIM, qp, zero),
                              jnp.where(lane_q < HEAD_DIM, zero, qp)], axis=0)
        tiles = []
        for t in range(qi + 1):
            s = _dot_nt(qz, k_ref[0, t * qb:(t + 1) * qb, p * LANES:(p + 1) * LANES])
            tiles.append(s + bias_diag[p] if t == qi else s + bias_prev[p] if t == qi - 1 else s)
            yield
        return p, qi, tiles

    def finish(p, qi, tiles):
        r0 = qi * qb
        m = jnp.max(functools.reduce(jnp.maximum, tiles), axis=-1, keepdims=True)
        acc = None
        for t, s in enumerate(tiles):
            part = _dot(jnp.exp2(s - m).astype(BF16), va_ref[p, t * qb:(t + 1) * qb, :])
            acc = part if acc is None else acc + part
            yield
        a = acc[:, 0:LANES] / acc[:, LANES:2 * LANES]
        o = a[0:qb] - lam * a[qb:2 * qb]
        o_ref[0, r0:r0 + qb, p * LANES:(p + 1) * LANES] = _rms(o, gain).astype(BF16)

    _interleaved([(p, qi) for p in range(DIFF_HEADS_PER_STEP) for qi in _hill_order(seq // qb)],
                 scores, finish, depth=DIFF_LAG)


def _diff_attention(lam4, subln, dq, dk, dv, gbias, batch, seq, lambda_init):
    hps = DIFF_HEADS_PER_STEP
    blk = pl.BlockSpec((1, seq, hps * LANES), lambda h, b: (b, 0, h))
    return pl.pallas_call(
        functools.partial(_diff_kernel, seq=seq, lambda_init=lambda_init),
        grid=(DIFF_HEADS // hps, batch),
        in_specs=[
            pl.BlockSpec(lam4.shape, lambda h, b: (0, 0)),
            pl.BlockSpec(subln.shape, lambda h, b: (0, 0)),
            blk, blk, blk,
            pl.BlockSpec((hps, ATT_Q_BLOCK, 2 * ATT_Q_BLOCK), lambda h, b: (h, 0, 0)),
        ],
        out_specs=blk,
        out_shape=jax.ShapeDtypeStruct((batch, seq, DIFF_W), BF16),
        scratch_shapes=[pltpu.VMEM((hps, seq, 2 * LANES), BF16)],
        compiler_params=pltpu.CompilerParams(dimension_semantics=("parallel", "parallel"),
                                             vmem_limit_bytes=VMEM_LIMIT),
        name="diff_attention",
    )(lam4, subln, dq, dk, dv, gbias)


def _proj1_kernel(x_ref, gain_ref, w_ref, bd_ref, hg_ref, q_ref, kk_ref, vt_ref):
    h = _rms(x_ref[...], gain_ref[...]).astype(BF16)
    bd = bd_ref[...]
    qw = SWA_HEADS * HEAD_DIM
    kw = SWA_KV_HEADS * HEAD_DIM
    low = lax.broadcasted_iota(jnp.int32, (h.shape[0], LANES), 1) < HEAD_DIM

    def project(c0):
        return c0, _dot(h, w_ref[:, c0:c0 + 2 * MXU_TILE])

    def epilogue(c0, proj):
        if c0 < qw:
            for i in range(2):
                cols = slice(i * MXU_TILE, (i + 1) * MXU_TILE)
                q_ref[:, c0 + i * MXU_TILE:c0 + (i + 1) * MXU_TILE] = _head_norm(
                    proj[:, cols], bd, hg_ref[0:1, :]).astype(BF16)
            return
        kn = _head_norm(proj[:, 0:kw], bd, hg_ref[1:2, :])
        for p in range(kw // LANES):
            pair = kn[:, p * LANES:(p + 1) * LANES]
            swapped = pltpu.roll(pair, HEAD_DIM, axis=1)
            kk_ref[:, (2 * p) * LANES:(2 * p + 1) * LANES] = jnp.where(low, pair, swapped).astype(BF16)
            kk_ref[:, (2 * p + 1) * LANES:(2 * p + 2) * LANES] = jnp.where(low, swapped, pair).astype(BF16)
        vt_ref[0] = proj[:, kw:2 * kw].T.astype(BF16)

    _lagged([(qw,)] + [(c0,) for c0 in range(0, qw, 2 * MXU_TILE)], project, epilogue,
            depth=PROJ_LAG)


def _proj1(x2d, gain, w, bd, hg, seq):
    n = x2d.shape[0]
    tm = ROW_TILE
    tps = seq // tm
    qw = SWA_HEADS * HEAD_DIM
    kw = SWA_KV_HEADS * HEAD_DIM
    full = lambda shape: pl.BlockSpec(shape, lambda t: (0,) * len(shape))
    return pl.pallas_call(
        _proj1_kernel,
        grid=(n // tm,),
        in_specs=[
            pl.BlockSpec((tm, D_MODEL), lambda t: (t, 0)),
            full((1, D_MODEL)),
            full(w.shape),
            full((MXU_TILE, MXU_TILE)),
            full(hg.shape),
        ],
        out_specs=[pl.BlockSpec((tm, qw), lambda t: (t, 0)),
                   pl.BlockSpec((tm, 2 * kw), lambda t: (t, 0)),
                   pl.BlockSpec((1, kw, tm), lambda t: (t // tps, 0, t % tps))],
        out_shape=[jax.ShapeDtypeStruct((n, qw), BF16),
                   jax.ShapeDtypeStruct((n, 2 * kw), BF16),
                   jax.ShapeDtypeStruct((n // seq, kw, seq), BF16)],
        compiler_params=pltpu.CompilerParams(dimension_semantics=("parallel",),
                                             vmem_limit_bytes=VMEM_LIMIT),
        name="proj_odd",
    )(x2d, gain, w, bd, hg)


def _swa_kernel(sink_ref, q_ref, kk_ref, vt_ref, sb_ref, o_ref, va_ref, *, seq):
    jj = pl.program_id(0)
    w = WINDOW
    gw = SWA_GROUP * HEAD_DIM
    sub = lax.broadcasted_iota(jnp.int32, (LANES - HEAD_DIM, seq), 0)
    sinks = []
    for p in range(SWA_KV_PER_STEP):
        va_ref[p, 0:HEAD_DIM, :] = vt_ref[0, p * HEAD_DIM:(p + 1) * HEAD_DIM, :]
        va_ref[p, HEAD_DIM:LANES, :] = jnp.where(sub == 0, 1.0, 0.0).astype(BF16)
        head0 = (jj * SWA_KV_PER_STEP + p) * SWA_GROUP
        sinks.append(jnp.concatenate(
            [jnp.full((1, w), sink_ref[head0 + g] * LOG2E, F32) for g in range(SWA_GROUP)],
            axis=1))
    low = lax.broadcasted_iota(jnp.int32, (w, LANES), 1) < HEAD_DIM
    zero = jnp.zeros((), BF16)

    def scores(p, n):
        r0 = n * w
        qb = q_ref[0, r0:r0 + w, p * gw:(p + 1) * gw]
        qa, qc = qb[:, 0:LANES], qb[:, LANES:2 * LANES]
        q4 = jnp.concatenate([jnp.where(low, qa, zero), jnp.where(low, zero, qa),
                              jnp.where(low, qc, zero), jnp.where(low, zero, qc)], axis=0)
        lanes = slice(p * LANES, (p + 1) * LANES)
        if n == 0:
            return p, n, _dot_nt(kk_ref[0, 0:w, lanes], q4) + sb_ref[p, w:2 * w, :]
        return p, n, _dot_nt(kk_ref[0, r0 - w:r0 + w, lanes], q4) + sb_ref[p]

    def finish(p, n, st):
        r0 = n * w
        vals = va_ref[p, :, 0:w] if n == 0 else va_ref[p, :, r0 - w:r0 + w]
        m = jnp.maximum(jnp.max(st, axis=0, keepdims=True), sinks[p])
        acc = _dot(vals, jnp.exp2(st - m).astype(BF16))
        denom = acc[HEAD_DIM:HEAD_DIM + 1, :] + jnp.exp2(sinks[p] - m)
        ot = acc[0:HEAD_DIM, :] / denom
        o_ref[0, r0:r0 + w, p * gw:(p + 1) * gw] = jnp.concatenate(
            [jnp.concatenate([ot[:, 0:w], ot[:, w:2 * w]], axis=0).T,
             jnp.concatenate([ot[:, 2 * w:3 * w], ot[:, 3 * w:4 * w]], axis=0).T],
            axis=1).astype(BF16)

    _lagged([(p, n) for p in range(SWA_KV_PER_STEP) for n in range(seq // w)], scores, finish,
            depth=SWA_LAG)


def _swa_attention(sinks, q, kk, vt, swbias, batch, seq):
    kps = SWA_KV_PER_STEP
    gw = kps * SWA_GROUP * HEAD_DIM
    return pl.pallas_call(
        functools.partial(_swa_kernel, seq=seq),
        grid=(SWA_KV_HEADS // kps, batch),
        in_specs=[
            pl.BlockSpec(memory_space=pltpu.SMEM),
            pl.BlockSpec((1, seq, gw), lambda h, b: (b, 0, h)),
            pl.BlockSpec((1, seq, kps * LANES), lambda h, b: (b, 0, h)),
            pl.BlockSpec((1, kps * HEAD_DIM, seq), lambda h, b: (b, h, 0)),
            pl.BlockSpec((kps, 2 * WINDOW, SWA_GROUP * WINDOW), lambda h, b: (h, 0, 0)),
        ],
        out_specs=pl.BlockSpec((1, seq, gw), lambda h, b: (b, 0, h)),
        out_shape=jax.ShapeDtypeStruct((batch, seq, SWA_HEADS * HEAD_DIM), BF16),
        scratch_shapes=[pltpu.VMEM((kps, LANES, seq), BF16)],
        compiler_params=pltpu.CompilerParams(dimension_semantics=("parallel", "parallel"),
                                             vmem_limit_bytes=VMEM_LIMIT),
        name="swa_attention",
    )(sinks, q, kk, vt, swbias)


def _outmlp_kernel(*refs, n_mix):
    x_ref = refs[0]
    a_refs = refs[1:1 + n_mix]
    wo_ref, gain_ref, wg_ref, wu_ref, wd_ref, o_ref = refs[1 + n_mix:]
    mix = jnp.concatenate([a_ref[...] for a_ref in a_refs], axis=1)
    x1 = x_ref[...] + _dot(mix, wo_ref[...])
    h = _rms(x1, gain_ref[...]).astype(BF16)
    out = x1
    for c0 in range(0, FFN_HIDDEN, HIDDEN_CHUNK):
        c1 = min(c0 + HIDDEN_CHUNK, FFN_HIDDEN)
        g = _dot(h, wg_ref[:, c0:c1])
        u = _dot(h, wu_ref[:, c0:c1])
        y = (g * (1.0 / (1.0 + jnp.exp(-g))) * u).astype(BF16)
        out = out + _dot(y, wd_ref[c0:c1, :])
    o_ref[...] = out


def _outmlp(x2d, mixes, wo, gain, wg, wu, wd, layer):
    n = x2d.shape[0]
    tm = ROW_TILE
    full = lambda shape: pl.BlockSpec(shape, lambda t: (0,) * len(shape),
                                      pipeline_mode=pl.Buffered(1))
    of_layer = lambda w: pl.BlockSpec((None,) + w.shape[1:], lambda t: (layer, 0, 0),
                                      pipeline_mode=pl.Buffered(1))
    in_specs = [pl.BlockSpec((tm, D_MODEL), lambda t: (t, 0))]
    in_specs += [pl.BlockSpec((tm, a.shape[1]), lambda t: (t, 0)) for a in mixes]
    in_specs += [full(wo.shape), full(gain.shape), of_layer(wg), of_layer(wu), of_layer(wd)]
    return pl.pallas_call(
        functools.partial(_outmlp_kernel, n_mix=len(mixes)),
        grid=(n // tm,),
        in_specs=in_specs,
        out_specs=pl.BlockSpec((tm, D_MODEL), lambda t: (t, 0)),
        out_shape=jax.ShapeDtypeStruct((n, D_MODEL), F32),
        compiler_params=pltpu.CompilerParams(dimension_semantics=("parallel",),
                                             vmem_limit_bytes=VMEM_LIMIT),
        name="outproj_swiglu",
    )(x2d, *mixes, wo, gain, wg, wu, wd)


def _block_diag_mean():
    i = np.arange(MXU_TILE)
    same = (i[:, None] // HEAD_DIM) == (i[None, :] // HEAD_DIM)
    return jnp.asarray(same / HEAD_DIM, BF16)


def kernel(x, rel_bias_table, ev_attn_norm, ev_w_in, ev_b_forget, ev_fox_q_norm, ev_fox_k_norm,
           ev_diff_q_norm, ev_diff_k_norm, ev_lambda_q1, ev_lambda_k1, ev_lambda_q2, ev_lambda_k2,
           ev_diff_subln, ev_w_out, od_attn_norm, od_w_qkv, od_q_norm, od_k_norm, od_sinks, od_w_out,
           ffn_norm, w_gate, w_up, w_down):
    batch, seq, d = x.shape
    n = batch * seq
    x2d = x.reshape(n, d)
    bd = _block_diag_mean()
    swbias, gbias = _bias_tiles(rel_bias_table.astype(F32))
    swbias = swbias.reshape(SWA_KV_HEADS, SWA_GROUP, 2 * WINDOW, WINDOW).transpose(0, 2, 1, 3)
    swbias = swbias.reshape(SWA_KV_HEADS, 2 * WINDOW, SWA_GROUP * WINDOW)

    lambda_init = 0.8 - 0.6 * math.exp(-0.3 * 0)
    w_in = ev_w_in[0]
    ff0 = 3 * FOX_W
    d0 = ff0 + FOX_HEADS
    w0 = jnp.concatenate([w_in[:, 0:ff0], w_in[:, d0:d0 + 3 * DIFF_W]], axis=1).astype(BF16)
    wf = jnp.concatenate(
        [w_in[:, ff0:d0].T, jnp.zeros((BF16_SUBLANES - FOX_HEADS, d), w_in.dtype)],
        axis=0).astype(BF16)
    tile8 = lambda g, scale=1.0: jnp.tile(g.astype(F32) * scale, FOX_W // HEAD_DIM)
    hg0 = jnp.stack([tile8(ev_fox_q_norm[0], Q_GAIN_SCALE), tile8(ev_fox_k_norm[0]),
                     tile8(ev_diff_q_norm[0], Q_GAIN_SCALE), tile8(ev_diff_k_norm[0])])
    bf = ev_b_forget[0].astype(F32).reshape(FOX_HEADS, 1)
    depth, hidden = w_gate.shape[0], w_gate.shape[2]
    later = [w_gate.reshape(depth * d, hidden), w_up.reshape(depth * d, hidden),
             w_down.reshape(depth * hidden, d), ev_w_out[0], od_w_qkv[0], od_w_out[0]]
    fq, fk, fvt, dq, dk, dv, caug, wg, wu, wd, wo0, wqkv, wo1 = _proj0(
        x2d, ev_attn_norm[0].astype(F32).reshape(1, d), w0, wf, bd, hg0, bf, seq,
        [a.astype(F32) for a in later])
    wg, wu = wg.reshape(depth, d, hidden), wu.reshape(depth, d, hidden)
    wd = wd.reshape(depth, hidden, d)
    r3 = lambda a: a.reshape(batch, seq, a.shape[-1])
    fox = _fox_attention(r3(fq), r3(fk), fvt, r3(caug), batch, seq)
    lam4 = jnp.stack([ev_lambda_q1[0], ev_lambda_k1[0], ev_lambda_q2[0], ev_lambda_k2[0]]).astype(F32)
    dout = _diff_attention(lam4, ev_diff_subln[0].astype(F32).reshape(1, 2 * HEAD_DIM),
                           r3(dq), r3(dk), r3(dv), gbias, batch, seq, lambda_init)
    x2d = _outmlp(x2d, [fox.reshape(n, FOX_W), dout.reshape(n, DIFF_W)], wo0,
                  ffn_norm[0].astype(F32).reshape(1, d), wg, wu, wd, 0)

    tile4 = lambda g, scale=1.0: jnp.tile(g.astype(F32) * scale, MXU_TILE // HEAD_DIM)
    hg1 = jnp.stack([tile4(od_q_norm[0], Q_GAIN_SCALE), tile4(od_k_norm[0])])
    q, kk, vt = _proj1(x2d, od_attn_norm[0].astype(F32).reshape(1, d), wqkv,
                       bd, hg1, seq)
    swa = _swa_attention(od_sinks[0].astype(F32), r3(q), r3(kk), vt, swbias, batch, seq)
    x2d = _outmlp(x2d, [swa.reshape(n, SWA_HEADS * HEAD_DIM)], wo1,
                  ffn_norm[1].astype(F32).reshape(1, d), wg, wu, wd, 1)
    return x2d.reshape(batch, seq, d)
```

```python
import functools
import math

import numpy as np
import jax
import jax.numpy as jnp
from jax import lax
from jax.experimental import pallas as pl
from jax.experimental.pallas import tpu as pltpu

F32 = jnp.float32
BF16 = jnp.bfloat16

D_MODEL = 1024
HEAD_DIM = 64
FOX_HEADS = 8
DIFF_HEADS = 4
SWA_HEADS = 16
SWA_KV_HEADS = 4
SWA_GROUP = SWA_HEADS // SWA_KV_HEADS
WINDOW = 128
NUM_BUCKETS = 32
MAX_DISTANCE = 128
FFN_HIDDEN = 2816
RMS_EPS = 1e-6
NEG_INF = -1e30
FOX_W = FOX_HEADS * HEAD_DIM
DIFF_W = DIFF_HEADS * 2 * HEAD_DIM
QK_SCALE = HEAD_DIM ** -0.5
LOG2E = math.log2(math.e)
Q_GAIN_SCALE = QK_SCALE * LOG2E

LANES = 128
BF16_SUBLANES = 16
FOX_VT_ROWS = HEAD_DIM + BF16_SUBLANES
MXU_TILE = 256
ROW_TILE = 1024
ATT_Q_BLOCK = 256
FOX_PAIRS_PER_STEP = 2
DIFF_HEADS_PER_STEP = 2
SWA_KV_PER_STEP = 2
FOX_LAG = 5
DIFF_LAG = 2
PROJ_LAG = 1
SWA_LAG = 3
HIDDEN_CHUNK = 768
V7X_VMEM_BYTES = 64 * 1024 * 1024
VMEM_LIMIT = V7X_VMEM_BYTES - 8 * 1024 * 1024

_NT = (((1,), (1,)), ((), ()))


def _dot(a, b):
    return jnp.dot(a, b, preferred_element_type=F32)


def _dot_nt(a, b):
    return lax.dot_general(a, b, _NT, preferred_element_type=F32)


def _lagged(items, first, second, depth=1):
    pending = []
    for item in items:
        pending.append(first(*item))
        if len(pending) > depth:
            second(*pending.pop(0))
    for p in pending:
        second(*p)


def _interleaved(items, first, second, depth):
    def drive(gens):
        done = {}
        while gens:
            for g in list(gens):
                try:
                    next(g)
                except StopIteration as stop:
                    done[id(g)] = stop.value
                    gens.remove(g)
        return done

    pending = []
    for item in items:
        g1 = first(*item)
        gens = [g1] if len(pending) < depth else [g1, second(*pending.pop(0))]
        pending.append(drive(gens)[id(g1)])
    for p in pending:
        drive([second(*p)])


def _hill_order(nblocks):
    return list(range(0, nblocks, 2)) + list(range(nblocks - 1 - nblocks % 2, 0, -2))


def _rms(xf, gain):
    ms = jnp.mean(xf * xf, axis=-1, keepdims=True)
    return xf * lax.rsqrt(ms + RMS_EPS) * gain


def _head_norm(acc, bd, gain):
    msq = _dot((acc * acc).astype(BF16), bd)
    return acc * lax.rsqrt(msq + RMS_EPS) * gain


def _bucket_ids(delta):
    n = np.maximum(delta, 0)
    max_exact = NUM_BUCKETS // 2
    nf = np.maximum(n, 1).astype(np.float64)
    large = max_exact + (np.log(nf / max_exact) / math.log(MAX_DISTANCE / max_exact)
                         * (NUM_BUCKETS - max_exact)).astype(np.int32)
    large = np.minimum(large, NUM_BUCKETS - 1)
    return np.where(n < max_exact, n, large).astype(np.int32)


def _bias_kernel(table_ref, bsw_ref, bg_ref, sw_ref, g_ref):
    h = pl.program_id(0)
    bsw = bsw_ref[...]
    sw = jnp.zeros(bsw.shape, F32)
    for b in range(NUM_BUCKETS):
        sw = jnp.where(bsw == b, table_ref[b, h] * LOG2E, sw)
    sw_ref[0] = jnp.where(bsw < 0, NEG_INF, sw)

    @pl.when(h < DIFF_HEADS)
    def _():
        bg = bg_ref[...]
        far = table_ref[NUM_BUCKETS - 1, h]
        g = jnp.zeros(bg.shape, F32)
        for b in range(NUM_BUCKETS):
            g = jnp.where(bg == b, (table_ref[b, h] - far) * LOG2E, g)
        g_ref[0] = jnp.where(bg < 0, NEG_INF, g)


def _bias_tiles(table):
    a = np.arange(WINDOW)[None, :]
    b = np.arange(2 * WINDOW)[:, None]
    d_sw = WINDOW + a - b
    bsw = np.where((d_sw >= 0) & (d_sw < WINDOW), _bucket_ids(d_sw), -1).astype(np.int32)
    a = np.arange(ATT_Q_BLOCK)[:, None]
    b = np.arange(2 * ATT_Q_BLOCK)[None, :]
    d_g = ATT_Q_BLOCK + a - b
    bg = np.where(d_g >= 0, _bucket_ids(d_g), -1).astype(np.int32)
    nh = table.shape[1]
    return pl.pallas_call(
        _bias_kernel,
        grid=(nh,),
        in_specs=[
            pl.BlockSpec(memory_space=pltpu.SMEM),
            pl.BlockSpec(bsw.shape, lambda h: (0, 0)),
            pl.BlockSpec(bg.shape, lambda h: (0, 0)),
        ],
        out_specs=[
            pl.BlockSpec((1,) + bsw.shape, lambda h: (h, 0, 0)),
            pl.BlockSpec((1,) + bg.shape, lambda h: (jnp.minimum(h, DIFF_HEADS - 1), 0, 0)),
        ],
        out_shape=[
            jax.ShapeDtypeStruct((nh,) + bsw.shape, F32),
            jax.ShapeDtypeStruct((DIFF_HEADS,) + bg.shape, F32),
        ],
        compiler_params=pltpu.CompilerParams(dimension_semantics=("arbitrary",)),
        name="rel_bias_tiles",
    )(table, jnp.asarray(bsw), jnp.asarray(bg))


def _proj0_kernel(*refs, tiles_per_seq, n_cast):
    x_ref, gain_ref, w_ref, wf_ref, bd_ref, hg_ref, bf_ref = refs[0:7]
    cast_src = refs[7:7 + n_cast]
    fq_ref, fk_ref, fv_ref, dq_ref, dk_ref, dv_ref, caug_ref = refs[7 + n_cast:14 + n_cast]
    cast_dst = refs[14 + n_cast:14 + 2 * n_cast]
    carry_ref = refs[14 + 2 * n_cast]
    t = pl.program_id(0)

    @pl.when(t == 0)
    def _():
        carry_ref[...] = jnp.zeros_like(carry_ref)

    h = _rms(x_ref[...], gain_ref[...]).astype(BF16)
    bd = bd_ref[...]
    rows = h.shape[0]

    outs = (fq_ref, fk_ref, fv_ref, dq_ref, dk_ref, dv_ref)
    gains = (0, 1, None, 2, 3, None)
    feature_major = (False, False, True, False, False, False)

    def project(g):
        return g, _dot(h, w_ref[:, g * FOX_W:(g + 1) * FOX_W])

    def epilogue(g, proj):
        if feature_major[g]:
            outs[g][0] = proj.T.astype(BF16)
            return
        if gains[g] is None:
            outs[g][...] = proj.astype(BF16)
            return
        for i in range(FOX_W // MXU_TILE):
            cols = slice(i * MXU_TILE, (i + 1) * MXU_TILE)
            outs[g][:, cols] = _head_norm(
                proj[:, cols], bd, hg_ref[gains[g]:gains[g] + 1, cols]).astype(BF16)

    pending = [project(g) for g in range(PROJ_LAG)]

    z8 = _dot_nt(wf_ref[...], h)[0:FOX_HEADS, :] + bf_ref[...]
    y = -(jnp.maximum(-z8, 0.0) + jnp.log1p(jnp.exp(-jnp.abs(z8))))
    lane = lax.broadcasted_iota(jnp.int32, y.shape, 1)
    shift = 1
    while shift < rows:
        y = y + jnp.where(lane >= shift, pltpu.roll(y, shift, axis=1), 0.0)
        shift *= 2

    carry = jnp.where(t % tiles_per_seq == 0, 0.0, carry_ref[:, 0:1])
    c = y + carry
    carry_ref[...] = jnp.broadcast_to(c[:, rows - 1:rows], carry_ref.shape)
    cn = c * (-LOG2E)
    hi = cn.astype(BF16).astype(F32)
    r1 = cn - hi
    mid = r1.astype(BF16).astype(F32)
    lo = (r1 - mid).astype(BF16).astype(F32)
    half = jnp.concatenate([hi, mid, lo, jnp.zeros((HEAD_DIM - 3 * FOX_HEADS, rows), F32)], axis=0)
    caug_ref[...] = jnp.concatenate([half, half], axis=0).T.astype(BF16)

    for g in range(PROJ_LAG, len(outs)):
        pending.append(project(g))
        epilogue(*pending.pop(0))
    for p in pending:
        epilogue(*p)

    for src, dst in zip(cast_src, cast_dst):
        dst[...] = src[...].astype(BF16)


def _proj0(x2d, gain, w, wf, bd, hg, bf, seq, later_weights):
    n = x2d.shape[0]
    tm = ROW_TILE
    tps = seq // tm
    steps = n // tm
    slab = lambda a: pl.BlockSpec((a.shape[0] // steps, a.shape[1]), lambda t: (t, 0))
    for a in later_weights:
        assert a.shape[0] % (steps * BF16_SUBLANES) == 0, a.shape
    row_spec = pl.BlockSpec((tm, FOX_W), lambda t: (t, 0))
    col_spec = pl.BlockSpec((1, FOX_W, tm), lambda t: (t // tps, 0, t % tps))
    rows_shape = jax.ShapeDtypeStruct((n, FOX_W), BF16)
    cols_shape = jax.ShapeDtypeStruct((n // seq, FOX_W, seq), BF16)
    full = lambda shape: pl.BlockSpec(shape, lambda t: (0,) * len(shape))
    outs = pl.pallas_call(
        functools.partial(_proj0_kernel, tiles_per_seq=tps, n_cast=len(later_weights)),
        grid=(steps,),
        in_specs=[
            pl.BlockSpec((tm, D_MODEL), lambda t: (t, 0)),
            full((1, D_MODEL)),
            full(w.shape),
            full(wf.shape),
            full((MXU_TILE, MXU_TILE)),
            full(hg.shape),
            full(bf.shape),
        ] + [slab(a) for a in later_weights],
        out_specs=[row_spec, row_spec, col_spec, row_spec, row_spec, row_spec,
                   pl.BlockSpec((tm, LANES), lambda t: (t, 0))] + [slab(a) for a in later_weights],
        out_shape=[rows_shape, rows_shape, cols_shape, rows_shape, rows_shape, rows_shape,
                   jax.ShapeDtypeStruct((n, LANES), BF16)]
        + [jax.ShapeDtypeStruct(a.shape, BF16) for a in later_weights],
        scratch_shapes=[pltpu.VMEM((FOX_HEADS, LANES), F32)],
        compiler_params=pltpu.CompilerParams(dimension_semantics=("arbitrary",),
                                             vmem_limit_bytes=VMEM_LIMIT),
        name="proj_even",
    )(x2d, gain, w, wf, bd, hg, bf, *later_weights)
    return outs


def _fox_kernel(q_ref, k_ref, vt_ref, caug_ref, o_ref, ka_ref, va_ref, *, seq):
    jj = pl.program_id(1)
    qb = ATT_Q_BLOCK
    low_s = lax.broadcasted_iota(jnp.int32, (seq, LANES), 1) < HEAD_DIM
    ca = caug_ref[0]
    sub = lax.broadcasted_iota(jnp.int32, (FOX_VT_ROWS - HEAD_DIM, seq), 0)
    ones_row = jnp.where(sub == 0, 1.0, 0.0).astype(BF16)
    for p in range(FOX_PAIRS_PER_STEP):
        kp = k_ref[0, :, p * LANES:(p + 1) * LANES]
        ka_ref[2 * p] = jnp.where(low_s, kp, ca)
        ka_ref[2 * p + 1] = jnp.where(low_s, ca, kp)
        for hh in range(2):
            r = p * LANES + hh * HEAD_DIM
            va_ref[2 * p + hh, 0:HEAD_DIM, :] = vt_ref[0, r:r + HEAD_DIM, :]
            va_ref[2 * p + hh, HEAD_DIM:FOX_VT_ROWS, :] = ones_row
    lane_q = lax.broadcasted_iota(jnp.int32, (qb, LANES), 1)
    key_i = lax.broadcasted_iota(jnp.int32, (qb, qb), 0)
    qry_i = lax.broadcasted_iota(jnp.int32, (qb, qb), 1)
    causal = key_i <= qry_i
    owns = (lane_q < HEAD_DIM, lane_q >= HEAD_DIM)

    def scores(p, qi, hh):
        r0 = qi * qb
        head = 2 * (FOX_PAIRS_PER_STEP * jj + p) + hh
        rel = lane_q - (HEAD_DIM if hh == 0 else 0) - head
        pick = (rel == 0) | (rel == FOX_HEADS) | (rel == 2 * FOX_HEADS)
        qz = jnp.where(owns[hh], q_ref[0, r0:r0 + qb, p * LANES:(p + 1) * LANES],
                       jnp.where(pick, 1.0, 0.0).astype(BF16))
        s_all =_dot_nt(ka_ref[2 * p + hh, 0:(qi + 1) * qb, :], qz)
        yield
        tiles = [s_all[t * qb:(t + 1) * qb] for t in range(qi)]
        tiles.append(jnp.where(causal, s_all[qi * qb:(qi + 1) * qb], NEG_INF))
        return p, qi, hh, tiles

    halves = []

    def finish(p, qi, hh, tiles):
        r0 = qi * qb
        m = jnp.max(functools.reduce(jnp.maximum, tiles), axis=0, keepdims=True)
        acc = None
        for t, s in enumerate(tiles):
            part = _dot(va_ref[2 * p + hh, :, t * qb:(t + 1) * qb], jnp.exp2(s - m).astype(BF16))
            acc = part if acc is None else acc + part
            yield
        halves.append(acc[0:HEAD_DIM] / acc[HEAD_DIM:HEAD_DIM + 1])
        if hh == 1:
            o_ref[0, r0:r0 + qb, p * LANES:(p + 1) * LANES] = jnp.concatenate(
                [halves[-2], halves[-1]], axis=0).T.astype(BF16)

    _interleaved([(p, qi, hh) for p in range(FOX_PAIRS_PER_STEP)
                  for qi in _hill_order(seq // qb) for hh in range(2)], scores, finish,
                 depth=FOX_LAG)


def _fox_attention(fq, fk, fvt, caug, batch, seq):
    width = FOX_PAIRS_PER_STEP * LANES
    blk = pl.BlockSpec((1, seq, width), lambda b, j: (b, 0, j))
    nheads = 2 * FOX_PAIRS_PER_STEP
    return pl.pallas_call(
        functools.partial(_fox_kernel, seq=seq),
        grid=(batch, FOX_W // width),
        in_specs=[blk, blk, pl.BlockSpec((1, width, seq), lambda b, j: (b, j, 0)),
                  pl.BlockSpec((1, seq, LANES), lambda b, j: (b, 0, 0))],
        out_specs=blk,
        out_shape=jax.ShapeDtypeStruct((batch, seq, FOX_W), BF16),
        scratch_shapes=[pltpu.VMEM((nheads, seq, LANES), BF16),
                        pltpu.VMEM((nheads, FOX_VT_ROWS, seq), BF16)],
        compiler_params=pltpu.CompilerParams(dimension_semantics=("parallel", "parallel"),
                                             vmem_limit_bytes=VMEM_LIMIT),
        name="fox_attention",
    )(fq, fk, fvt, caug)


def _diff_kernel(lam_ref, g_ref, q_ref, k_ref, v_ref, gb_ref, o_ref, va_ref, *, seq, lambda_init):
    qb = ATT_Q_BLOCK
    lv = lam_ref[...]
    lam = (jnp.exp(jnp.sum(lv[0:1] * lv[1:2], axis=-1, keepdims=True))
           - jnp.exp(jnp.sum(lv[2:3] * lv[3:4], axis=-1, keepdims=True)) + lambda_init)
    for p in range(DIFF_HEADS_PER_STEP):
        va_ref[p, :, 0:LANES] = v_ref[0, :, p * LANES:(p + 1) * LANES]
        va_ref[p, :, LANES:2 * LANES] = jnp.ones((seq, LANES), BF16)
    lane_q = lax.broadcasted_iota(jnp.int32, (qb, LANES), 1)
    bias_prev = [jnp.concatenate([gb_ref[p, :, 0:qb]] * 2, axis=0)
                 for p in range(DIFF_HEADS_PER_STEP)]
    bias_diag = [jnp.concatenate([gb_ref[p, :, qb:2 * qb]] * 2, axis=0)
                 for p in range(DIFF_HEADS_PER_STEP)]
    gain = g_ref[...] * (1.0 - lambda_init)
    zero = jnp.zeros((), BF16)

    def scores(p, qi):
        r0 = qi * qb
        qp = q_ref[0, r0:r0 + qb, p * LANES:(p + 1) * LANES]
        qz = jnp.concatenate([jnp.where(lane_q < HEAD_DIM, qp, zero),
                              jnp.where(lane_q < HEAD_DIM, zero, qp)], axis=0)
        tiles = []
        for t in range(qi + 1):
            s = _dot_nt(qz, k_ref[0, t * qb:(t + 1) * qb, p * LANES:(p + 1) * LANES])
            tiles.append(s + bias_diag[p] if t == qi else s + bias_prev[p] if t == qi - 1 else s)
            yield
        return p, qi, tiles

    def finish(p, qi, tiles):
        r0 = qi * qb
        m = jnp.max(functools.reduce(jnp.maximum, tiles), axis=-1, keepdims=True)
        acc = None
        for t, s in enumerate(tiles):
            part = _dot(jnp.exp2(s - m).astype(BF16), va_ref[p, t * qb:(t + 1) * qb, :])
            acc = part if acc is None else acc + part
            yield
        a = acc[:, 0:LANES] / acc[:, LANES:2 * LANES]
        o = a[0:qb] - lam * a[qb:2 * qb]
        o_ref[0, r0:r0 + qb, p * LANES:(p + 1) * LANES] = _rms(o, gain).astype(BF16)

    _interleaved([(p, qi) for p in range(DIFF_HEADS_PER_STEP) for qi in _hill_order(seq // qb)],
                 scores, finish, depth=DIFF_LAG)


def _diff_attention(lam4, subln, dq, dk, dv, gbias, batch, seq, lambda_init):
    hps = DIFF_HEADS_PER_STEP
    blk = pl.BlockSpec((1, seq, hps * LANES), lambda h, b: (b, 0, h))
    return pl.pallas_call(
        functools.partial(_diff_kernel, seq=seq, lambda_init=lambda_init),
        grid=(DIFF_HEADS // hps, batch),
        in_specs=[
            pl.BlockSpec(lam4.shape, lambda h, b: (0, 0)),
            pl.BlockSpec(subln.shape, lambda h, b: (0, 0)),
            blk, blk, blk,
            pl.BlockSpec((hps, ATT_Q_BLOCK, 2 * ATT_Q_BLOCK), lambda h, b: (h, 0, 0)),
        ],
        out_specs=blk,
        out_shape=jax.ShapeDtypeStruct((batch, seq, DIFF_W), BF16),
        scratch_shapes=[pltpu.VMEM((hps, seq, 2 * LANES), BF16)],
        compiler_params=pltpu.CompilerParams(dimension_semantics=("parallel", "parallel"),
                                             vmem_limit_bytes=VMEM_LIMIT),
        name="diff_attention",
    )(lam4, subln, dq, dk, dv, gbias)


def _proj1_kernel(x_ref, gain_ref, w_ref, bd_ref, hg_ref, q_ref, kk_ref, vt_ref):
    h = _rms(x_ref[...], gain_ref[...]).astype(BF16)
    bd = bd_ref[...]
    qw = SWA_HEADS * HEAD_DIM
    kw = SWA_KV_HEADS * HEAD_DIM
    low = lax.broadcasted_iota(jnp.int32, (h.shape[0], LANES), 1) < HEAD_DIM

    def project(c0):
        return c0, _dot(h, w_ref[:, c0:c0 + 2 * MXU_TILE])

    def epilogue(c0, proj):
        if c0 < qw:
            for i in range(2):
                cols = slice(i * MXU_TILE, (i + 1) * MXU_TILE)
                q_ref[:, c0 + i * MXU_TILE:c0 + (i + 1) * MXU_TILE] = _head_norm(
                    proj[:, cols], bd, hg_ref[0:1, :]).astype(BF16)
            return
        kn = _head_norm(proj[:, 0:kw], bd, hg_ref[1:2, :])
        for p in range(kw // LANES):
            pair = kn[:, p * LANES:(p + 1) * LANES]
            swapped = pltpu.roll(pair, HEAD_DIM, axis=1)
            kk_ref[:, (2 * p) * LANES:(2 * p + 1) * LANES] = jnp.where(low, pair, swapped).astype(BF16)
            kk_ref[:, (2 * p + 1) * LANES:(2 * p + 2) * LANES] = jnp.where(low, swapped, pair).astype(BF16)
        vt_ref[0] = proj[:, kw:2 * kw].T.astype(BF16)

    _lagged([(qw,)] + [(c0,) for c0 in range(0, qw, 2 * MXU_TILE)], project, epilogue,
            depth=PROJ_LAG)


def _proj1(x2d, gain, w, bd, hg, seq):
    n = x2d.shape[0]
    tm = ROW_TILE
    tps = seq // tm
    qw = SWA_HEADS * HEAD_DIM
    kw = SWA_KV_HEADS * HEAD_DIM
    full = lambda shape: pl.BlockSpec(shape, lambda t: (0,) * len(shape))
    return pl.pallas_call(
        _proj1_kernel,
        grid=(n // tm,),
        in_specs=[
            pl.BlockSpec((tm, D_MODEL), lambda t: (t, 0)),
            full((1, D_MODEL)),
            full(w.shape),
            full((MXU_TILE, MXU_TILE)),
            full(hg.shape),
        ],
        out_specs=[pl.BlockSpec((tm, qw), lambda t: (t, 0)),
                   pl.BlockSpec((tm, 2 * kw), lambda t: (t, 0)),
                   pl.BlockSpec((1, kw, tm), lambda t: (t // tps, 0, t % tps))],
        out_shape=[jax.ShapeDtypeStruct((n, qw), BF16),
                   jax.ShapeDtypeStruct((n, 2 * kw), BF16),
                   jax.ShapeDtypeStruct((n // seq, kw, seq), BF16)],
        compiler_params=pltpu.CompilerParams(dimension_semantics=("parallel",),
                                             vmem_limit_bytes=VMEM_LIMIT),
        name="proj_odd",
    )(x2d, gain, w, bd, hg)


def _swa_kernel(sink_ref, q_ref, kk_ref, vt_ref, sb_ref, o_ref, va_ref, *, seq):
    jj = pl.program_id(0)
    w = WINDOW
    gw = SWA_GROUP * HEAD_DIM
    sub = lax.broadcasted_iota(jnp.int32, (LANES - HEAD_DIM, seq), 0)
    sinks = []
    for p in range(SWA_KV_PER_STEP):
        va_ref[p, 0:HEAD_DIM, :] = vt_ref[0, p * HEAD_DIM:(p + 1) * HEAD_DIM, :]
        va_ref[p, HEAD_DIM:LANES, :] = jnp.where(sub == 0, 1.0, 0.0).astype(BF16)
        head0 = (jj * SWA_KV_PER_STEP + p) * SWA_GROUP
        sinks.append(jnp.concatenate(
            [jnp.full((1, w), sink_ref[head0 + g] * LOG2E, F32) for g in range(SWA_GROUP)],
            axis=1))
    low = lax.broadcasted_iota(jnp.int32, (w, LANES), 1) < HEAD_DIM
    zero = jnp.zeros((), BF16)

    def scores(p, n):
        r0 = n * w
        qb = q_ref[0, r0:r0 + w, p * gw:(p + 1) * gw]
        qa, qc = qb[:, 0:LANES], qb[:, LANES:2 * LANES]
        q4 = jnp.concatenate([jnp.where(low, qa, zero), jnp.where(low, zero, qa),
                              jnp.where(low, qc, zero), jnp.where(low, zero, qc)], axis=0)
        lanes = slice(p * LANES, (p + 1) * LANES)
        if n == 0:
            return p, n, _dot_nt(kk_ref[0, 0:w, lanes], q4) + sb_ref[p, w:2 * w, :]
        return p, n, _dot_nt(kk_ref[0, r0 - w:r0 + w, lanes], q4) + sb_ref[p]

    def finish(p, n, st):
        r0 = n * w
        vals = va_ref[p, :, 0:w] if n == 0 else va_ref[p, :, r0 - w:r0 + w]
        m = jnp.maximum(jnp.max(st, axis=0, keepdims=True), sinks[p])
        acc = _dot(vals, jnp.exp2(st - m).astype(BF16))
        denom = acc[HEAD_DIM:HEAD_DIM + 1, :] + jnp.exp2(sinks[p] - m)
        ot = acc[0:HEAD_DIM, :] / denom
        o_ref[0, r0:r0 + w, p * gw:(p + 1) * gw] = jnp.concatenate(
            [jnp.concatenate([ot[:, 0:w], ot[:, w:2 * w]], axis=0).T,
             jnp.concatenate([ot[:, 2 * w:3 * w], ot[:, 3 * w:4 * w]], axis=0).T],
            axis=1).astype(BF16)

    _lagged([(p, n) for p in range(SWA_KV_PER_STEP) for n in range(seq // w)], scores, finish,
            depth=SWA_LAG)


def _swa_attention(sinks, q, kk, vt, swbias, batch, seq):
    kps = SWA_KV_PER_STEP
    gw = kps * SWA_GROUP * HEAD_DIM
    return pl.pallas_call(
        functools.partial(_swa_kernel, seq=seq),
        grid=(SWA_KV_HEADS // kps, batch),
        in_specs=[
            pl.BlockSpec(memory_space=pltpu.SMEM),
            pl.BlockSpec((1, seq, gw), lambda h, b: (b, 0, h)),
            pl.BlockSpec((1, seq, kps * LANES), lambda h, b: (b, 0, h)),
            pl.BlockSpec((1, kps * HEAD_DIM, seq), lambda h, b: (b, h, 0)),
            pl.BlockSpec((kps, 2 * WINDOW, SWA_GROUP * WINDOW), lambda h, b: (h, 0, 0)),
        ],
        out_specs=pl.BlockSpec((1, seq, gw), lambda h, b: (b, 0, h)),
        out_shape=jax.ShapeDtypeStruct((batch, seq, SWA_HEADS * HEAD_DIM), BF16),
        scratch_shapes=[pltpu.VMEM((kps, LANES, seq), BF16)],
        compiler_params=pltpu.CompilerParams(dimension_semantics=("parallel", "parallel"),
                                             vmem_limit_bytes=VMEM_LIMIT),
        name="swa_attention",
    )(sinks, q, kk, vt, swbias)


def _outmlp_kernel(*refs, n_mix):
    x_ref = refs[0]
    a_refs = refs[1:1 + n_mix]
    wo_ref, gain_ref, wg_ref, wu_ref, wd_ref, o_ref = refs[1 + n_mix:]
    mix = jnp.concatenate([a_ref[...] for a_ref in a_refs], axis=1)
    x1 = x_ref[...] + _dot(mix, wo_ref[...])
    h = _rms(x1, gain_ref[...]).astype(BF16)
    out = x1
    for c0 in range(0, FFN_HIDDEN, HIDDEN_CHUNK):
        c1 = min(c0 + HIDDEN_CHUNK, FFN_HIDDEN)
        g = _dot(h, wg_ref[:, c0:c1])
        u = _dot(h, wu_ref[:, c0:c1])
        y = (g * (1.0 / (1.0 + jnp.exp(-g))) * u).astype(BF16)
        out = out + _dot(y, wd_ref[c0:c1, :])
    o_ref[...] = out


def _outmlp(x2d, mixes, wo, gain, wg, wu, wd, layer):
    n = x2d.shape[0]
    tm = ROW_TILE
    full = lambda shape: pl.BlockSpec(shape, lambda t: (0,) * len(shape),
                                      pipeline_mode=pl.Buffered(1))
    of_layer = lambda w: pl.BlockSpec((None,) + w.shape[1:], lambda t: (layer, 0, 0),
                                      pipeline_mode=pl.Buffered(1))
    in_specs = [pl.BlockSpec((tm, D_MODEL), lambda t: (t, 0))]
    in_specs += [pl.BlockSpec((tm, a.shape[1]), lambda t: (t, 0)) for a in mixes]
    in_specs += [full(wo.shape), full(gain.shape), of_layer(wg), of_layer(wu), of_layer(wd)]
    return pl.pallas_call(
        functools.partial(_outmlp_kernel, n_mix=len(mixes)),
        grid=(n // tm,),
        in_specs=in_specs,
        out_specs=pl.BlockSpec((tm, D_MODEL), lambda t: (t, 0)),
        out_shape=jax.ShapeDtypeStruct((n, D_MODEL), F32),
        compiler_params=pltpu.CompilerParams(dimension_semantics=("parallel",),
                                             vmem_limit_bytes=VMEM_LIMIT),
        name="outproj_swiglu",
    )(x2d, *mixes, wo, gain, wg, wu, wd)


def _block_diag_mean():
    i = np.arange(MXU_TILE)
    same = (i[:, None] // HEAD_DIM) == (i[None, :] // HEAD_DIM)
    return jnp.asarray(same / HEAD_DIM, BF16)


def kernel(x, rel_bias_table, ev_attn_norm, ev_w_in, ev_b_forget, ev_fox_q_norm, ev_fox_k_norm,
           ev_diff_q_norm, ev_diff_k_norm, ev_lambda_q1, ev_lambda_k1, ev_lambda_q2, ev_lambda_k2,
           ev_diff_subln, ev_w_out, od_attn_norm, od_w_qkv, od_q_norm, od_k_norm, od_sinks, od_w_out,
           ffn_norm, w_gate, w_up, w_down):
    batch, seq, d = x.shape
    n = batch * seq
    x2d = x.reshape(n, d)
    bd = _block_diag_mean()
    swbias, gbias = _bias_tiles(rel_bias_table.astype(F32))
    swbias = swbias.reshape(SWA_KV_HEADS, SWA_GROUP, 2 * WINDOW, WINDOW).transpose(0, 2, 1, 3)
    swbias = swbias.reshape(SWA_KV_HEADS, 2 * WINDOW, SWA_GROUP * WINDOW)

    lambda_init = 0.8 - 0.6 * math.exp(-0.3 * 0)
    w_in = ev_w_in[0]
    ff0 = 3 * FOX_W
    d0 = ff0 + FOX_HEADS
    w0 = jnp.concatenate([w_in[:, 0:ff0], w_in[:, d0:d0 + 3 * DIFF_W]], axis=1).astype(BF16)
    wf = jnp.concatenate(
        [w_in[:, ff0:d0].T, jnp.zeros((BF16_SUBLANES - FOX_HEADS, d), w_in.dtype)],
        axis=0).astype(BF16)
    tile8 = lambda g, scale=1.0: jnp.tile(g.astype(F32) * scale, FOX_W // HEAD_DIM)
    hg0 = jnp.stack([tile8(ev_fox_q_norm[0], Q_GAIN_SCALE), tile8(ev_fox_k_norm[0]),
                     tile8(ev_diff_q_norm[0], Q_GAIN_SCALE), tile8(ev_diff_k_norm[0])])
    bf = ev_b_forget[0].astype(F32).reshape(FOX_HEADS, 1)
    depth, hidden = w_gate.shape[0], w_gate.shape[2]
    later = [w_gate.reshape(depth * d, hidden), w_up.reshape(depth * d, hidden),
             w_down.reshape(depth * hidden, d), ev_w_out[0], od_w_qkv[0], od_w_out[0]]
    fq, fk, fvt, dq, dk, dv, caug, wg, wu, wd, wo0, wqkv, wo1 = _proj0(
        x2d, ev_attn_norm[0].astype(F32).reshape(1, d), w0, wf, bd, hg0, bf, seq,
        [a.astype(F32) for a in later])
    wg, wu = wg.reshape(depth, d, hidden), wu.reshape(depth, d, hidden)
    wd = wd.reshape(depth, hidden, d)
    r3 = lambda a: a.reshape(batch, seq, a.shape[-1])
    fox = _fox_attention(r3(fq), r3(fk), fvt, r3(caug), batch, seq)
    lam4 = jnp.stack([ev_lambda_q1[0], ev_lambda_k1[0], ev_lambda_q2[0], ev_lambda_k2[0]]).astype(F32)
    dout = _diff_attention(lam4, ev_diff_subln[0].astype(F32).reshape(1, 2 * HEAD_DIM),
                           r3(dq), r3(dk), r3(dv), gbias, batch, seq, lambda_init)
    x2d = _outmlp(x2d, [fox.reshape(n, FOX_W), dout.reshape(n, DIFF_W)], wo0,
                  ffn_norm[0].astype(F32).reshape(1, d), wg, wu, wd, 0)

    tile4 = lambda g, scale=1.0: jnp.tile(g.astype(F32) * scale, MXU_TILE // HEAD_DIM)
    hg1 = jnp.stack([tile4(od_q_norm[0], Q_GAIN_SCALE), tile4(od_k_norm[0])])
    q, kk, vt = _proj1(x2d, od_attn_norm[0].astype(F32).reshape(1, d), wqkv,
                       bd, hg1, seq)
    swa = _swa_attention(od_sinks[0].astype(F32), r3(q), r3(kk), vt, swbias, batch, seq)
    x2d = _outmlp(x2d, [swa.reshape(n, SWA_HEADS * HEAD_DIM)], wo1,
                  ffn_norm[1].astype(F32).reshape(1, d), wg, wu, wd, 1)
    return x2d.reshape(batch, seq, d)
```

```python
import functools
import math

import numpy as np
import jax
import jax.numpy as jnp
from jax import lax
from jax.experimental import pallas as pl
from jax.experimental.pallas import tpu as pltpu

F32 = jnp.float32
BF16 = jnp.bfloat16

D_MODEL = 1024
HEAD_DIM = 64
FOX_HEADS = 8
DIFF_HEADS = 4
SWA_HEADS = 16
SWA_KV_HEADS = 4
SWA_GROUP = SWA_HEADS // SWA_KV_HEADS
WINDOW = 128
NUM_BUCKETS = 32
MAX_DISTANCE = 128
FFN_HIDDEN = 2816
RMS_EPS = 1e-6
NEG_INF = -1e30
FOX_W = FOX_HEADS * HEAD_DIM
DIFF_W = DIFF_HEADS * 2 * HEAD_DIM
QK_SCALE = HEAD_DIM ** -0.5
LOG2E = math.log2(math.e)
Q_GAIN_SCALE = QK_SCALE * LOG2E

LANES = 128
BF16_SUBLANES = 16
FOX_VT_ROWS = HEAD_DIM + 2 * BF16_SUBLANES
MXU_TILE = 256
ROW_TILE = 1024
ATT_Q_BLOCK = 256
FOX_PAIRS_PER_STEP = 2
DIFF_HEADS_PER_STEP = 2
SWA_KV_PER_STEP = 2
FOX_LAG = 4
DIFF_LAG = 2
PROJ_LAG = 1
SWA_LAG = 4
HIDDEN_CHUNK = 768
V7X_VMEM_BYTES = 64 * 1024 * 1024
VMEM_LIMIT = V7X_VMEM_BYTES - 8 * 1024 * 1024

_NT = (((1,), (1,)), ((), ()))


def _dot(a, b):
    return jnp.dot(a, b, preferred_element_type=F32)


def _dot_nt(a, b):
    return lax.dot_general(a, b, _NT, preferred_element_type=F32)


def _lagged(items, first, second, depth=1):
    pending = []
    for item in items:
        pending.append(first(*item))
        if len(pending) > depth:
            second(*pending.pop(0))
    for p in pending:
        second(*p)


def _interleaved(items, first, second, depth):
    def drive(gens):
        done = {}
        while gens:
            for g in list(gens):
                try:
                    next(g)
                except StopIteration as stop:
                    done[id(g)] = stop.value
                    gens.remove(g)
        return done

    pending = []
    for item in items:
        g1 = first(*item)
        gens = [g1] if len(pending) < depth else [g1, second(*pending.pop(0))]
        pending.append(drive(gens)[id(g1)])
    for p in pending:
        drive([second(*p)])


def _hill_order(nblocks):
    return list(range(0, nblocks, 2)) + list(range(nblocks - 1 - nblocks % 2, 0, -2))


def _rms(xf, gain):
    ms = jnp.mean(xf * xf, axis=-1, keepdims=True)
    return xf * lax.rsqrt(ms + RMS_EPS) * gain


def _head_norm(acc, bd, gain):
    msq = _dot((acc * acc).astype(BF16), bd)
    return acc * lax.rsqrt(msq + RMS_EPS) * gain


def _bucket_ids(delta):
    n = np.maximum(delta, 0)
    max_exact = NUM_BUCKETS // 2
    nf = np.maximum(n, 1).astype(np.float64)
    large = max_exact + (np.log(nf / max_exact) / math.log(MAX_DISTANCE / max_exact)
                         * (NUM_BUCKETS - max_exact)).astype(np.int32)
    large = np.minimum(large, NUM_BUCKETS - 1)
    return np.where(n < max_exact, n, large).astype(np.int32)


def _bias_kernel(table_ref, bsw_ref, bg_ref, sw_ref, g_ref):
    h = pl.program_id(0)
    bsw = bsw_ref[...]
    sw = jnp.zeros(bsw.shape, F32)
    for b in range(NUM_BUCKETS):
        sw = jnp.where(bsw == b, table_ref[b, h] * LOG2E, sw)
    sw_ref[0] = jnp.where(bsw < 0, NEG_INF, sw)

    @pl.when(h < DIFF_HEADS)
    def _():
        bg = bg_ref[...]
        far = table_ref[NUM_BUCKETS - 1, h]
        g = jnp.zeros(bg.shape, F32)
        for b in range(NUM_BUCKETS):
            g = jnp.where(bg == b, (table_ref[b, h] - far) * LOG2E, g)
        g_ref[0] = jnp.where(bg < 0, NEG_INF, g)


def _bias_tiles(table):
    a = np.arange(WINDOW)[None, :]
    b = np.arange(2 * WINDOW)[:, None]
    d_sw = WINDOW + a - b
    bsw = np.where((d_sw >= 0) & (d_sw < WINDOW), _bucket_ids(d_sw), -1).astype(np.int32)
    a = np.arange(ATT_Q_BLOCK)[:, None]
    b = np.arange(2 * ATT_Q_BLOCK)[None, :]
    d_g = ATT_Q_BLOCK + a - b
    bg = np.where(d_g >= 0, _bucket_ids(d_g), -1).astype(np.int32)
    nh = table.shape[1]
    return pl.pallas_call(
        _bias_kernel,
        grid=(nh,),
        in_specs=[
            pl.BlockSpec(memory_space=pltpu.SMEM),
            pl.BlockSpec(bsw.shape, lambda h: (0, 0)),
            pl.BlockSpec(bg.shape, lambda h: (0, 0)),
        ],
        out_specs=[
            pl.BlockSpec((1,) + bsw.shape, lambda h: (h, 0, 0)),
            pl.BlockSpec((1,) + bg.shape, lambda h: (jnp.minimum(h, DIFF_HEADS - 1), 0, 0)),
        ],
        out_shape=[
            jax.ShapeDtypeStruct((nh,) + bsw.shape, F32),
            jax.ShapeDtypeStruct((DIFF_HEADS,) + bg.shape, F32),
        ],
        compiler_params=pltpu.CompilerParams(dimension_semantics=("arbitrary",)),
        name="rel_bias_tiles",
    )(table, jnp.asarray(bsw), jnp.asarray(bg))


def _proj0_kernel(*refs, tiles_per_seq, n_cast):
    x_ref, gain_ref, w_ref, wf_ref, bd_ref, hg_ref, bf_ref = refs[0:7]
    cast_src = refs[7:7 + n_cast]
    fq_ref, fk_ref, fv_ref, dq_ref, dk_ref, dv_ref, caug_ref = refs[7 + n_cast:14 + n_cast]
    cast_dst = refs[14 + n_cast:14 + 2 * n_cast]
    carry_ref = refs[14 + 2 * n_cast]
    t = pl.program_id(0)

    @pl.when(t == 0)
    def _():
        carry_ref[...] = jnp.zeros_like(carry_ref)

    h = _rms(x_ref[...], gain_ref[...]).astype(BF16)
    bd = bd_ref[...]
    rows = h.shape[0]

    outs = (fq_ref, fk_ref, fv_ref, dq_ref, dk_ref, dv_ref)
    gains = (0, 1, None, 2, 3, None)
    feature_major = (False, False, True, False, False, False)

    def project(g):
        return g, _dot(h, w_ref[:, g * FOX_W:(g + 1) * FOX_W])

    def epilogue(g, proj):
        if feature_major[g]:
            outs[g][0] = proj.T.astype(BF16)
            return
        if gains[g] is None:
            outs[g][...] = proj.astype(BF16)
            return
        for i in range(FOX_W // MXU_TILE):
            cols = slice(i * MXU_TILE, (i + 1) * MXU_TILE)
            outs[g][:, cols] = _head_norm(
                proj[:, cols], bd, hg_ref[gains[g]:gains[g] + 1, cols]).astype(BF16)

    pending = [project(g) for g in range(PROJ_LAG)]

    z8 = _dot_nt(wf_ref[...], h)[0:FOX_HEADS, :] + bf_ref[...]
    y = -(jnp.maximum(-z8, 0.0) + jnp.log1p(jnp.exp(-jnp.abs(z8))))
    lane = lax.broadcasted_iota(jnp.int32, y.shape, 1)
    shift = 1
    while shift < rows:
        y = y + jnp.where(lane >= shift, pltpu.roll(y, shift, axis=1), 0.0)
        shift *= 2

    carry = jnp.where(t % tiles_per_seq == 0, 0.0, carry_ref[:, 0:1])
    c = y + carry
    carry_ref[...] = jnp.broadcast_to(c[:, rows - 1:rows], carry_ref.shape)
    cn = c * (-LOG2E)
    hi = cn.astype(BF16).astype(F32)
    r1 = cn - hi
    mid = r1.astype(BF16).astype(F32)
    lo = (r1 - mid).astype(BF16).astype(F32)
    half = jnp.concatenate([hi, mid, lo, jnp.zeros((HEAD_DIM - 3 * FOX_HEADS, rows), F32)], axis=0)
    caug_ref[...] = jnp.concatenate([half, half], axis=0).T.astype(BF16)

    for g in range(PROJ_LAG, len(outs)):
        pending.append(project(g))
        epilogue(*pending.pop(0))
    for p in pending:
        epilogue(*p)

    for src, dst in zip(cast_src, cast_dst):
        dst[...] = src[...].astype(BF16)


def _proj0(x2d, gain, w, wf, bd, hg, bf, seq, later_weights):
    n = x2d.shape[0]
    tm = ROW_TILE
    tps = seq // tm
    steps = n // tm
    slab = lambda a: pl.BlockSpec((a.shape[0] // steps, a.shape[1]), lambda t: (t, 0))
    for a in later_weights:
        assert a.shape[0] % (steps * BF16_SUBLANES) == 0, a.shape
    row_spec = pl.BlockSpec((tm, FOX_W), lambda t: (t, 0))
    col_spec = pl.BlockSpec((1, FOX_W, tm), lambda t: (t // tps, 0, t % tps))
    rows_shape = jax.ShapeDtypeStruct((n, FOX_W), BF16)
    cols_shape = jax.ShapeDtypeStruct((n // seq, FOX_W, seq), BF16)
    full = lambda shape: pl.BlockSpec(shape, lambda t: (0,) * len(shape))
    outs = pl.pallas_call(
        functools.partial(_proj0_kernel, tiles_per_seq=tps, n_cast=len(later_weights)),
        grid=(steps,),
        in_specs=[
            pl.BlockSpec((tm, D_MODEL), lambda t: (t, 0)),
            full((1, D_MODEL)),
            full(w.shape),
            full(wf.shape),
            full((MXU_TILE, MXU_TILE)),
            full(hg.shape),
            full(bf.shape),
        ] + [slab(a) for a in later_weights],
        out_specs=[row_spec, row_spec, col_spec, row_spec, row_spec, row_spec,
                   pl.BlockSpec((tm, LANES), lambda t: (t, 0))] + [slab(a) for a in later_weights],
        out_shape=[rows_shape, rows_shape, cols_shape, rows_shape, rows_shape, rows_shape,
                   jax.ShapeDtypeStruct((n, LANES), BF16)]
        + [jax.ShapeDtypeStruct(a.shape, BF16) for a in later_weights],
        scratch_shapes=[pltpu.VMEM((FOX_HEADS, LANES), F32)],
        compiler_params=pltpu.CompilerParams(dimension_semantics=("arbitrary",),
                                             vmem_limit_bytes=VMEM_LIMIT),
        name="proj_even",
    )(x2d, gain, w, wf, bd, hg, bf, *later_weights)
    return outs


def _fox_kernel(q_ref, k_ref, vt_ref, caug_ref, o_ref, ka_ref, va_ref, *, seq):
    jj = pl.program_id(1)
    qb = ATT_Q_BLOCK
    low_s = lax.broadcasted_iota(jnp.int32, (seq, LANES), 1) < HEAD_DIM
    ca = caug_ref[0]
    sub = lax.broadcasted_iota(jnp.int32, (FOX_VT_ROWS - HEAD_DIM, seq), 0)
    ones_row = jnp.where(sub == 0, 1.0, 0.0).astype(BF16)
    for p in range(FOX_PAIRS_PER_STEP):
        kp = k_ref[0, :, p * LANES:(p + 1) * LANES]
        ka_ref[2 * p] = jnp.where(low_s, kp, ca)
        ka_ref[2 * p + 1] = jnp.where(low_s, ca, kp)
        for hh in range(2):
            r = p * LANES + hh * HEAD_DIM
            va_ref[2 * p + hh, 0:HEAD_DIM, :] = vt_ref[0, r:r + HEAD_DIM, :]
            va_ref[2 * p + hh, HEAD_DIM:FOX_VT_ROWS, :] = ones_row
    lane_q = lax.broadcasted_iota(jnp.int32, (qb, LANES), 1)
    key_i = lax.broadcasted_iota(jnp.int32, (qb, qb), 0)
    qry_i = lax.broadcasted_iota(jnp.int32, (qb, qb), 1)
    causal = key_i <= qry_i
    owns = (lane_q < HEAD_DIM, lane_q >= HEAD_DIM)

    def scores(p, qi, hh):
        r0 = qi * qb
        head = 2 * (FOX_PAIRS_PER_STEP * jj + p) + hh
        rel = lane_q - (HEAD_DIM if hh == 0 else 0) - head
        pick = (rel == 0) | (rel == FOX_HEADS) | (rel == 2 * FOX_HEADS)
        qz = jnp.where(owns[hh], q_ref[0, r0:r0 + qb, p * LANES:(p + 1) * LANES],
                       jnp.where(pick, 1.0, 0.0).astype(BF16))
        s_all =_dot_nt(ka_ref[2 * p + hh, 0:(qi + 1) * qb, :], qz)
        yield
        tiles = [s_all[t * qb:(t + 1) * qb] for t in range(qi)]
        tiles.append(jnp.where(causal, s_all[qi * qb:(qi + 1) * qb], NEG_INF))
        return p, qi, hh, tiles

    halves = []

    def finish(p, qi, hh, tiles):
        r0 = qi * qb
        m = jnp.max(functools.reduce(jnp.maximum, tiles), axis=0, keepdims=True)
        acc = None
        for t, s in enumerate(tiles):
            part = _dot(va_ref[2 * p + hh, :, t * qb:(t + 1) * qb], jnp.exp2(s - m).astype(BF16))
            acc = part if acc is None else acc + part
            yield
        halves.append(acc[0:HEAD_DIM] / acc[HEAD_DIM:HEAD_DIM + 1])
        if hh == 1:
            o_ref[0, r0:r0 + qb, p * LANES:(p + 1) * LANES] = jnp.concatenate(
                [halves[-2], halves[-1]], axis=0).T.astype(BF16)

    _interleaved([(p, qi, hh) for p in range(FOX_PAIRS_PER_STEP)
                  for qi in _hill_order(seq // qb) for hh in range(2)], scores, finish,
                 depth=FOX_LAG)


def _fox_attention(fq, fk, fvt, caug, batch, seq):
    width = FOX_PAIRS_PER_STEP * LANES
    blk = pl.BlockSpec((1, seq, width), lambda b, j: (b, 0, j))
    nheads = 2 * FOX_PAIRS_PER_STEP
    return pl.pallas_call(
        functools.partial(_fox_kernel, seq=seq),
        grid=(batch, FOX_W // width),
        in_specs=[blk, blk, pl.BlockSpec((1, width, seq), lambda b, j: (b, j, 0)),
                  pl.BlockSpec((1, seq, LANES), lambda b, j: (b, 0, 0))],
        out_specs=blk,
        out_shape=jax.ShapeDtypeStruct((batch, seq, FOX_W), BF16),
        scratch_shapes=[pltpu.VMEM((nheads, seq, LANES), BF16),
                        pltpu.VMEM((nheads, FOX_VT_ROWS, seq), BF16)],
        compiler_params=pltpu.CompilerParams(dimension_semantics=("parallel", "parallel"),
                                             vmem_limit_bytes=VMEM_LIMIT),
        name="fox_attention",
    )(fq, fk, fvt, caug)


def _diff_kernel(lam_ref, g_ref, q_ref, k_ref, v_ref, gb_ref, o_ref, va_ref, *, seq, lambda_init):
    qb = ATT_Q_BLOCK
    lv = lam_ref[...]
    lam = (jnp.exp(jnp.sum(lv[0:1] * lv[1:2], axis=-1, keepdims=True))
           - jnp.exp(jnp.sum(lv[2:3] * lv[3:4], axis=-1, keepdims=True)) + lambda_init)
    for p in range(DIFF_HEADS_PER_STEP):
        va_ref[p, :, 0:LANES] = v_ref[0, :, p * LANES:(p + 1) * LANES]
        va_ref[p, :, LANES:2 * LANES] = jnp.ones((seq, LANES), BF16)
    lane_q = lax.broadcasted_iota(jnp.int32, (qb, LANES), 1)
    bias_prev = [jnp.concatenate([gb_ref[p, :, 0:qb]] * 2, axis=0)
                 for p in range(DIFF_HEADS_PER_STEP)]
    bias_diag = [jnp.concatenate([gb_ref[p, :, qb:2 * qb]] * 2, axis=0)
                 for p in range(DIFF_HEADS_PER_STEP)]
    gain = g_ref[...] * (1.0 - lambda_init)
    zero = jnp.zeros((), BF16)

    def scores(p, qi):
        r0 = qi * qb
        qp = q_ref[0, r0:r0 + qb, p * LANES:(p + 1) * LANES]
        qz = jnp.concatenate([jnp.where(lane_q < HEAD_DIM, qp, zero),
                              jnp.where(lane_q < HEAD_DIM, zero, qp)], axis=0)
        tiles = []
        for t in range(qi + 1):
            s = _dot_nt(qz, k_ref[0, t * qb:(t + 1) * qb, p * LANES:(p + 1) * LANES])
            tiles.append(s + bias_diag[p] if t == qi else s + bias_prev[p] if t == qi - 1 else s)
            yield
        return p, qi, tiles

    def finish(p, qi, tiles):
        r0 = qi * qb
        m = jnp.max(functools.reduce(jnp.maximum, tiles), axis=-1, keepdims=True)
        acc = None
        for t, s in enumerate(tiles):
            part = _dot(jnp.exp2(s - m).astype(BF16), va_ref[p, t * qb:(t + 1) * qb, :])
            acc = part if acc is None else acc + part
            yield
        a = acc[:, 0:LANES] / acc[:, LANES:2 * LANES]
        o = a[0:qb] - lam * a[qb:2 * qb]
        o_ref[0, r0:r0 + qb, p * LANES:(p + 1) * LANES] = _rms(o, gain).astype(BF16)

    _interleaved([(p, qi) for p in range(DIFF_HEADS_PER_STEP) for qi in range(seq // qb)],
                 scores, finish, depth=DIFF_LAG)


def _diff_attention(lam4, subln, dq, dk, dv, gbias, batch, seq, lambda_init):
    hps = DIFF_HEADS_PER_STEP
    blk = pl.BlockSpec((1, seq, hps * LANES), lambda h, b: (b, 0, h))
    return pl.pallas_call(
        functools.partial(_diff_kernel, seq=seq, lambda_init=lambda_init),
        grid=(DIFF_HEADS // hps, batch),
        in_specs=[
            pl.BlockSpec(lam4.shape, lambda h, b: (0, 0)),
            pl.BlockSpec(subln.shape, lambda h, b: (0, 0)),
            blk, blk, blk,
            pl.BlockSpec((hps, ATT_Q_BLOCK, 2 * ATT_Q_BLOCK), lambda h, b: (h, 0, 0)),
        ],
        out_specs=blk,
        out_shape=jax.ShapeDtypeStruct((batch, seq, DIFF_W), BF16),
        scratch_shapes=[pltpu.VMEM((hps, seq, 2 * LANES), BF16)],
        compiler_params=pltpu.CompilerParams(dimension_semantics=("parallel", "parallel"),
                                             vmem_limit_bytes=VMEM_LIMIT),
        name="diff_attention",
    )(lam4, subln, dq, dk, dv, gbias)


def _proj1_kernel(x_ref, gain_ref, w_ref, bd_ref, hg_ref, q_ref, kk_ref, vt_ref):
    h = _rms(x_ref[...], gain_ref[...]).astype(BF16)
    bd = bd_ref[...]
    qw = SWA_HEADS * HEAD_DIM
    kw = SWA_KV_HEADS * HEAD_DIM
    low = lax.broadcasted_iota(jnp.int32, (h.shape[0], LANES), 1) < HEAD_DIM

    def project(c0):
        return c0, _dot(h, w_ref[:, c0:c0 + 2 * MXU_TILE])

    def epilogue(c0, proj):
        if c0 < qw:
            for i in range(2):
                cols = slice(i * MXU_TILE, (i + 1) * MXU_TILE)
                q_ref[:, c0 + i * MXU_TILE:c0 + (i + 1) * MXU_TILE] = _head_norm(
                    proj[:, cols], bd, hg_ref[0:1, :]).astype(BF16)
            return
        kn = _head_norm(proj[:, 0:kw], bd, hg_ref[1:2, :])
        for p in range(kw // LANES):
            pair = kn[:, p * LANES:(p + 1) * LANES]
            swapped = pltpu.roll(pair, HEAD_DIM, axis=1)
            kk_ref[:, (2 * p) * LANES:(2 * p + 1) * LANES] = jnp.where(low, pair, swapped).astype(BF16)
            kk_ref[:, (2 * p + 1) * LANES:(2 * p + 2) * LANES] = jnp.where(low, swapped, pair).astype(BF16)
        vt_ref[0] = proj[:, kw:2 * kw].T.astype(BF16)

    _lagged([(qw,)] + [(c0,) for c0 in range(0, qw, 2 * MXU_TILE)], project, epilogue,
            depth=PROJ_LAG)


def _proj1(x2d, gain, w, bd, hg, seq):
    n = x2d.shape[0]
    tm = ROW_TILE
    tps = seq // tm
    qw = SWA_HEADS * HEAD_DIM
    kw = SWA_KV_HEADS * HEAD_DIM
    full = lambda shape: pl.BlockSpec(shape, lambda t: (0,) * len(shape))
    return pl.pallas_call(
        _proj1_kernel,
        grid=(n // tm,),
        in_specs=[
            pl.BlockSpec((tm, D_MODEL), lambda t: (t, 0)),
            full((1, D_MODEL)),
            full(w.shape),
            full((MXU_TILE, MXU_TILE)),
            full(hg.shape),
        ],
        out_specs=[pl.BlockSpec((tm, qw), lambda t: (t, 0)),
                   pl.BlockSpec((tm, 2 * kw), lambda t: (t, 0)),
                   pl.BlockSpec((1, kw, tm), lambda t: (t // tps, 0, t % tps))],
        out_shape=[jax.ShapeDtypeStruct((n, qw), BF16),
                   jax.ShapeDtypeStruct((n, 2 * kw), BF16),
                   jax.ShapeDtypeStruct((n // seq, kw, seq), BF16)],
        compiler_params=pltpu.CompilerParams(dimension_semantics=("parallel",),
                                             vmem_limit_bytes=VMEM_LIMIT),
        name="proj_odd",
    )(x2d, gain, w, bd, hg)


def _swa_kernel(sink_ref, q_ref, kk_ref, vt_ref, sb_ref, o_ref, va_ref, *, seq):
    jj = pl.program_id(0)
    w = WINDOW
    gw = SWA_GROUP * HEAD_DIM
    sub = lax.broadcasted_iota(jnp.int32, (LANES - HEAD_DIM, seq), 0)
    sinks = []
    for p in range(SWA_KV_PER_STEP):
        va_ref[p, 0:HEAD_DIM, :] = vt_ref[0, p * HEAD_DIM:(p + 1) * HEAD_DIM, :]
        va_ref[p, HEAD_DIM:LANES, :] = jnp.where(sub == 0, 1.0, 0.0).astype(BF16)
        head0 = (jj * SWA_KV_PER_STEP + p) * SWA_GROUP
        sinks.append(jnp.concatenate(
            [jnp.full((1, w), sink_ref[head0 + g] * LOG2E, F32) for g in range(SWA_GROUP)],
            axis=1))
    low = lax.broadcasted_iota(jnp.int32, (w, LANES), 1) < HEAD_DIM
    zero = jnp.zeros((), BF16)

    def scores(p, n):
        r0 = n * w
        qb = q_ref[0, r0:r0 + w, p * gw:(p + 1) * gw]
        qa, qc = qb[:, 0:LANES], qb[:, LANES:2 * LANES]
        q4 = jnp.concatenate([jnp.where(low, qa, zero), jnp.where(low, zero, qa),
                              jnp.where(low, qc, zero), jnp.where(low, zero, qc)], axis=0)
        lanes = slice(p * LANES, (p + 1) * LANES)
        if n == 0:
            return p, n, _dot_nt(kk_ref[0, 0:w, lanes], q4) + sb_ref[p, w:2 * w, :]
        return p, n, _dot_nt(kk_ref[0, r0 - w:r0 + w, lanes], q4) + sb_ref[p]

    def finish(p, n, st):
        r0 = n * w
        vals = va_ref[p, :, 0:w] if n == 0 else va_ref[p, :, r0 - w:r0 + w]
        m = jnp.maximum(jnp.max(st, axis=0, keepdims=True), sinks[p])
        acc = _dot(vals, jnp.exp2(st - m).astype(BF16))
        denom = acc[HEAD_DIM:HEAD_DIM + 1, :] + jnp.exp2(sinks[p] - m)
        ot = acc[0:HEAD_DIM, :] / denom
        o_ref[0, r0:r0 + w, p * gw:(p + 1) * gw] = jnp.concatenate(
            [jnp.concatenate([ot[:, 0:w], ot[:, w:2 * w]], axis=0).T,
             jnp.concatenate([ot[:, 2 * w:3 * w], ot[:, 3 * w:4 * w]], axis=0).T],
            axis=1).astype(BF16)

    _lagged([(p, n) for p in range(SWA_KV_PER_STEP) for n in range(seq // w)], scores, finish,
            depth=SWA_LAG)


def _swa_attention(sinks, q, kk, vt, swbias, batch, seq):
    kps = SWA_KV_PER_STEP
    gw = kps * SWA_GROUP * HEAD_DIM
    return pl.pallas_call(
        functools.partial(_swa_kernel, seq=seq),
        grid=(SWA_KV_HEADS // kps, batch),
        in_specs=[
            pl.BlockSpec(memory_space=pltpu.SMEM),
            pl.BlockSpec((1, seq, gw), lambda h, b: (b, 0, h)),
            pl.BlockSpec((1, seq, kps * LANES), lambda h, b: (b, 0, h)),
            pl.BlockSpec((1, kps * HEAD_DIM, seq), lambda h, b: (b, h, 0)),
            pl.BlockSpec((kps, 2 * WINDOW, SWA_GROUP * WINDOW), lambda h, b: (h, 0, 0)),
        ],
        out_specs=pl.BlockSpec((1, seq, gw), lambda h, b: (b, 0, h)),
        out_shape=jax.ShapeDtypeStruct((batch, seq, SWA_HEADS * HEAD_DIM), BF16),
        scratch_shapes=[pltpu.VMEM((kps, LANES, seq), BF16)],
        compiler_params=pltpu.CompilerParams(dimension_semantics=("parallel", "parallel"),
                                             vmem_limit_bytes=VMEM_LIMIT),
        name="swa_attention",
    )(sinks, q, kk, vt, swbias)


def _outmlp_kernel(*refs, n_mix):
    x_ref = refs[0]
    a_refs = refs[1:1 + n_mix]
    wo_ref, gain_ref, wg_ref, wu_ref, wd_ref, o_ref = refs[1 + n_mix:]
    mix = jnp.concatenate([a_ref[...] for a_ref in a_refs], axis=1)
    x1 = x_ref[...] + _dot(mix, wo_ref[...])
    h = _rms(x1, gain_ref[...]).astype(BF16)
    out = x1
    for c0 in range(0, FFN_HIDDEN, HIDDEN_CHUNK):
        c1 = min(c0 + HIDDEN_CHUNK, FFN_HIDDEN)
        g = _dot(h, wg_ref[:, c0:c1])
        u = _dot(h, wu_ref[:, c0:c1])
        y = (g * (1.0 / (1.0 + jnp.exp(-g))) * u).astype(BF16)
        out = out + _dot(y, wd_ref[c0:c1, :])
    o_ref[...] = out


def _outmlp(x2d, mixes, wo, gain, wg, wu, wd, layer):
    n = x2d.shape[0]
    tm = ROW_TILE
    full = lambda shape: pl.BlockSpec(shape, lambda t: (0,) * len(shape),
                                      pipeline_mode=pl.Buffered(1))
    of_layer = lambda w: pl.BlockSpec((None,) + w.shape[1:], lambda t: (layer, 0, 0),
                                      pipeline_mode=pl.Buffered(1))
    in_specs = [pl.BlockSpec((tm, D_MODEL), lambda t: (t, 0))]
    in_specs += [pl.BlockSpec((tm, a.shape[1]), lambda t: (t, 0)) for a in mixes]
    in_specs += [full(wo.shape), full(gain.shape), of_layer(wg), of_layer(wu), of_layer(wd)]
    return pl.pallas_call(
        functools.partial(_outmlp_kernel, n_mix=len(mixes)),
        grid=(n // tm,),
        in_specs=in_specs,
        out_specs=pl.BlockSpec((tm, D_MODEL), lambda t: (t, 0)),
        out_shape=jax.ShapeDtypeStruct((n, D_MODEL), F32),
        compiler_params=pltpu.CompilerParams(dimension_semantics=("parallel",),
                                             vmem_limit_bytes=VMEM_LIMIT),
        name="outproj_swiglu",
    )(x2d, *mixes, wo, gain, wg, wu, wd)


def _block_diag_mean():
    i = np.arange(MXU_TILE)
    same = (i[:, None] // HEAD_DIM) == (i[None, :] // HEAD_DIM)
    return jnp.asarray(same / HEAD_DIM, BF16)


def kernel(x, rel_bias_table, ev_attn_norm, ev_w_in, ev_b_forget, ev_fox_q_norm, ev_fox_k_norm,
           ev_diff_q_norm, ev_diff_k_norm, ev_lambda_q1, ev_lambda_k1, ev_lambda_q2, ev_lambda_k2,
           ev_diff_subln, ev_w_out, od_attn_norm, od_w_qkv, od_q_norm, od_k_norm, od_sinks, od_w_out,
           ffn_norm, w_gate, w_up, w_down):
    batch, seq, d = x.shape
    n = batch * seq
    x2d = x.reshape(n, d)
    bd = _block_diag_mean()
    swbias, gbias = _bias_tiles(rel_bias_table.astype(F32))
    swbias = swbias.reshape(SWA_KV_HEADS, SWA_GROUP, 2 * WINDOW, WINDOW).transpose(0, 2, 1, 3)
    swbias = swbias.reshape(SWA_KV_HEADS, 2 * WINDOW, SWA_GROUP * WINDOW)

    lambda_init = 0.8 - 0.6 * math.exp(-0.3 * 0)
    w_in = ev_w_in[0]
    ff0 = 3 * FOX_W
    d0 = ff0 + FOX_HEADS
    w0 = jnp.concatenate([w_in[:, 0:ff0], w_in[:, d0:d0 + 3 * DIFF_W]], axis=1).astype(BF16)
    wf = jnp.concatenate(
        [w_in[:, ff0:d0].T, jnp.zeros((BF16_SUBLANES - FOX_HEADS, d), w_in.dtype)],
        axis=0).astype(BF16)
    tile8 = lambda g, scale=1.0: jnp.tile(g.astype(F32) * scale, FOX_W // HEAD_DIM)
    hg0 = jnp.stack([tile8(ev_fox_q_norm[0], Q_GAIN_SCALE), tile8(ev_fox_k_norm[0]),
                     tile8(ev_diff_q_norm[0], Q_GAIN_SCALE), tile8(ev_diff_k_norm[0])])
    bf = ev_b_forget[0].astype(F32).reshape(FOX_HEADS, 1)
    depth, hidden = w_gate.shape[0], w_gate.shape[2]
    later = [w_gate.reshape(depth * d, hidden), w_up.reshape(depth * d, hidden),
             w_down.reshape(depth * hidden, d), ev_w_out[0], od_w_qkv[0], od_w_out[0]]
    fq, fk, fvt, dq, dk, dv, caug, wg, wu, wd, wo0, wqkv, wo1 = _proj0(
        x2d, ev_attn_norm[0].astype(F32).reshape(1, d), w0, wf, bd, hg0, bf, seq,
        [a.astype(F32) for a in later])
    wg, wu = wg.reshape(depth, d, hidden), wu.reshape(depth, d, hidden)
    wd = wd.reshape(depth, hidden, d)
    r3 = lambda a: a.reshape(batch, seq, a.shape[-1])
    fox = _fox_attention(r3(fq), r3(fk), fvt, r3(caug), batch, seq)
    lam4 = jnp.stack([ev_lambda_q1[0], ev_lambda_k1[0], ev_lambda_q2[0], ev_lambda_k2[0]]).astype(F32)
    dout = _diff_attention(lam4, ev_diff_subln[0].astype(F32).reshape(1, 2 * HEAD_DIM),
                           r3(dq), r3(dk), r3(dv), gbias, batch, seq, lambda_init)
    x2d = _outmlp(x2d, [fox.reshape(n, FOX_W), dout.reshape(n, DIFF_W)], wo0,
                  ffn_norm[0].astype(F32).reshape(1, d), wg, wu, wd, 0)

    tile4 = lambda g, scale=1.0: jnp.tile(g.astype(F32) * scale, MXU_TILE // HEAD_DIM)
    hg1 = jnp.stack([tile4(od_q_norm[0], Q_GAIN_SCALE), tile4(od_k_norm[0])])
    q, kk, vt = _proj1(x2d, od_attn_norm[0].astype(F32).reshape(1, d), wqkv,
                       bd, hg1, seq)
    swa = _swa_attention(od_sinks[0].astype(F32), r3(q), r3(kk), vt, swbias, batch, seq)
    x2d = _outmlp(x2d, [swa.reshape(n, SWA_HEADS * HEAD_DIM)], wo1,
                  ffn_norm[1].astype(F32).reshape(1, d), wg, wu, wd, 1)
    return x2d.reshape(batch, seq, d)
```

```python
import functools
import math

import numpy as np
import jax
import jax.numpy as jnp
from jax import lax
from jax.experimental import pallas as pl
from jax.experimental.pallas import tpu as pltpu

F32 = jnp.float32
BF16 = jnp.bfloat16

D_MODEL = 1024
HEAD_DIM = 64
FOX_HEADS = 8
DIFF_HEADS = 4
SWA_HEADS = 16
SWA_KV_HEADS = 4
SWA_GROUP = SWA_HEADS // SWA_KV_HEADS
WINDOW = 128
NUM_BUCKETS = 32
MAX_DISTANCE = 128
FFN_HIDDEN = 2816
RMS_EPS = 1e-6
NEG_INF = -1e30
FOX_W = FOX_HEADS * HEAD_DIM
DIFF_W = DIFF_HEADS * 2 * HEAD_DIM
QK_SCALE = HEAD_DIM ** -0.5
LOG2E = math.log2(math.e)
Q_GAIN_SCALE = QK_SCALE * LOG2E

LANES = 128
BF16_SUBLANES = 16
FOX_VT_ROWS = HEAD_DIM + BF16_SUBLANES
MXU_TILE = 256
ROW_TILE = 1024
ATT_Q_BLOCK = 256
FOX_PAIRS_PER_STEP = 2
DIFF_HEADS_PER_STEP = 2
SWA_KV_PER_STEP = 2
FOX_LAG = 4
DIFF_LAG = 2
PROJ_LAG = 1
SWA_LAG = 4
HIDDEN_CHUNK = 1024
V7X_VMEM_BYTES = 64 * 1024 * 1024
VMEM_LIMIT = V7X_VMEM_BYTES - 8 * 1024 * 1024

_NT = (((1,), (1,)), ((), ()))


def _dot(a, b):
    return jnp.dot(a, b, preferred_element_type=F32)


def _dot_nt(a, b):
    return lax.dot_general(a, b, _NT, preferred_element_type=F32)


def _lagged(items, first, second, depth=1):
    pending = []
    for item in items:
        pending.append(first(*item))
        if len(pending) > depth:
            second(*pending.pop(0))
    for p in pending:
        second(*p)


def _interleaved(items, first, second, depth):
    def drive(gens):
        done = {}
        while gens:
            for g in list(gens):
                try:
                    next(g)
                except StopIteration as stop:
                    done[id(g)] = stop.value
                    gens.remove(g)
        return done

    pending = []
    for item in items:
        g1 = first(*item)
        gens = [g1] if len(pending) < depth else [g1, second(*pending.pop(0))]
        pending.append(drive(gens)[id(g1)])
    for p in pending:
        drive([second(*p)])


def _hill_order(nblocks):
    return list(range(0, nblocks, 2)) + list(range(nblocks - 1 - nblocks % 2, 0, -2))


def _rms(xf, gain):
    ms = jnp.mean(xf * xf, axis=-1, keepdims=True)
    return xf * lax.rsqrt(ms + RMS_EPS) * gain


def _head_norm(acc, bd, gain):
    msq = _dot((acc * acc).astype(BF16), bd)
    return acc * lax.rsqrt(msq + RMS_EPS) * gain


def _bucket_ids(delta):
    n = np.maximum(delta, 0)
    max_exact = NUM_BUCKETS // 2
    nf = np.maximum(n, 1).astype(np.float64)
    large = max_exact + (np.log(nf / max_exact) / math.log(MAX_DISTANCE / max_exact)
                         * (NUM_BUCKETS - max_exact)).astype(np.int32)
    large = np.minimum(large, NUM_BUCKETS - 1)
    return np.where(n < max_exact, n, large).astype(np.int32)


def _bias_kernel(table_ref, bsw_ref, bg_ref, sw_ref, g_ref):
    h = pl.program_id(0)
    bsw = bsw_ref[...]
    sw = jnp.zeros(bsw.shape, F32)
    for b in range(NUM_BUCKETS):
        sw = jnp.where(bsw == b, table_ref[b, h] * LOG2E, sw)
    sw_ref[0] = jnp.where(bsw < 0, NEG_INF, sw)

    @pl.when(h < DIFF_HEADS)
    def _():
        bg = bg_ref[...]
        far = table_ref[NUM_BUCKETS - 1, h]
        g = jnp.zeros(bg.shape, F32)
        for b in range(NUM_BUCKETS):
            g = jnp.where(bg == b, (table_ref[b, h] - far) * LOG2E, g)
        g_ref[0] = jnp.where(bg < 0, NEG_INF, g)


def _bias_tiles(table):
    a = np.arange(WINDOW)[None, :]
    b = np.arange(2 * WINDOW)[:, None]
    d_sw = WINDOW + a - b
    bsw = np.where((d_sw >= 0) & (d_sw < WINDOW), _bucket_ids(d_sw), -1).astype(np.int32)
    a = np.arange(ATT_Q_BLOCK)[:, None]
    b = np.arange(2 * ATT_Q_BLOCK)[None, :]
    d_g = ATT_Q_BLOCK + a - b
    bg = np.where(d_g >= 0, _bucket_ids(d_g), -1).astype(np.int32)
    nh = table.shape[1]
    return pl.pallas_call(
        _bias_kernel,
        grid=(nh,),
        in_specs=[
            pl.BlockSpec(memory_space=pltpu.SMEM),
            pl.BlockSpec(bsw.shape, lambda h: (0, 0)),
            pl.BlockSpec(bg.shape, lambda h: (0, 0)),
        ],
        out_specs=[
            pl.BlockSpec((1,) + bsw.shape, lambda h: (h, 0, 0)),
            pl.BlockSpec((1,) + bg.shape, lambda h: (jnp.minimum(h, DIFF_HEADS - 1), 0, 0)),
        ],
        out_shape=[
            jax.ShapeDtypeStruct((nh,) + bsw.shape, F32),
            jax.ShapeDtypeStruct((DIFF_HEADS,) + bg.shape, F32),
        ],
        compiler_params=pltpu.CompilerParams(dimension_semantics=("arbitrary",)),
        name="rel_bias_tiles",
    )(table, jnp.asarray(bsw), jnp.asarray(bg))


def _proj0_kernel(*refs, tiles_per_seq, n_cast):
    x_ref, gain_ref, w_ref, wf_ref, bd_ref, hg_ref, bf_ref = refs[0:7]
    cast_src = refs[7:7 + n_cast]
    fq_ref, fk_ref, fv_ref, dq_ref, dk_ref, dv_ref, caug_ref = refs[7 + n_cast:14 + n_cast]
    cast_dst = refs[14 + n_cast:14 + 2 * n_cast]
    carry_ref = refs[14 + 2 * n_cast]
    t = pl.program_id(0)

    @pl.when(t == 0)
    def _():
        carry_ref[...] = jnp.zeros_like(carry_ref)

    h = _rms(x_ref[...], gain_ref[...]).astype(BF16)
    bd = bd_ref[...]
    rows = h.shape[0]

    outs = (fq_ref, fk_ref, fv_ref, dq_ref, dk_ref, dv_ref)
    gains = (0, 1, None, 2, 3, None)
    feature_major = (False, False, True, False, False, False)

    def project(g):
        return g, _dot(h, w_ref[:, g * FOX_W:(g + 1) * FOX_W])

    def epilogue(g, proj):
        if feature_major[g]:
            outs[g][0] = proj.T.astype(BF16)
            return
        if gains[g] is None:
            outs[g][...] = proj.astype(BF16)
            return
        for i in range(FOX_W // MXU_TILE):
            cols = slice(i * MXU_TILE, (i + 1) * MXU_TILE)
            outs[g][:, cols] = _head_norm(
                proj[:, cols], bd, hg_ref[gains[g]:gains[g] + 1, cols]).astype(BF16)

    pending = [project(g) for g in range(PROJ_LAG)]

    z8 = _dot_nt(wf_ref[...], h)[0:FOX_HEADS, :] + bf_ref[...]
    y = -(jnp.maximum(-z8, 0.0) + jnp.log1p(jnp.exp(-jnp.abs(z8))))
    lane = lax.broadcasted_iota(jnp.int32, y.shape, 1)
    shift = 1
    while shift < rows:
        y = y + jnp.where(lane >= shift, pltpu.roll(y, shift, axis=1), 0.0)
        shift *= 2

    carry = jnp.where(t % tiles_per_seq == 0, 0.0, carry_ref[:, 0:1])
    c = y + carry
    carry_ref[...] = jnp.broadcast_to(c[:, rows - 1:rows], carry_ref.shape)
    cn = c * (-LOG2E)
    hi = cn.astype(BF16).astype(F32)
    r1 = cn - hi
    mid = r1.astype(BF16).astype(F32)
    lo = (r1 - mid).astype(BF16).astype(F32)
    half = jnp.concatenate([hi, mid, lo, jnp.zeros((HEAD_DIM - 3 * FOX_HEADS, rows), F32)], axis=0)
    caug_ref[...] = jnp.concatenate([half, half], axis=0).T.astype(BF16)

    for g in range(PROJ_LAG, len(outs)):
        pending.append(project(g))
        epilogue(*pending.pop(0))
    for p in pending:
        epilogue(*p)

    for src, dst in zip(cast_src, cast_dst):
        dst[...] = src[...].astype(BF16)


def _proj0(x2d, gain, w, wf, bd, hg, bf, seq, later_weights):
    n = x2d.shape[0]
    tm = ROW_TILE
    tps = seq // tm
    steps = n // tm
    slab = lambda a: pl.BlockSpec((a.shape[0] // steps, a.shape[1]), lambda t: (t, 0))
    for a in later_weights:
        assert a.shape[0] % (steps * BF16_SUBLANES) == 0, a.shape
    row_spec = pl.BlockSpec((tm, FOX_W), lambda t: (t, 0))
    col_spec = pl.BlockSpec((1, FOX_W, tm), lambda t: (t // tps, 0, t % tps))
    rows_shape = jax.ShapeDtypeStruct((n, FOX_W), BF16)
    cols_shape = jax.ShapeDtypeStruct((n // seq, FOX_W, seq), BF16)
    full = lambda shape: pl.BlockSpec(shape, lambda t: (0,) * len(shape))
    outs = pl.pallas_call(
        functools.partial(_proj0_kernel, tiles_per_seq=tps, n_cast=len(later_weights)),
        grid=(steps,),
        in_specs=[
            pl.BlockSpec((tm, D_MODEL), lambda t: (t, 0)),
            full((1, D_MODEL)),
            full(w.shape),
            full(wf.shape),
            full((MXU_TILE, MXU_TILE)),
            full(hg.shape),
            full(bf.shape),
        ] + [slab(a) for a in later_weights],
        out_specs=[row_spec, row_spec, col_spec, row_spec, row_spec, row_spec,
                   pl.BlockSpec((tm, LANES), lambda t: (t, 0))] + [slab(a) for a in later_weights],
        out_shape=[rows_shape, rows_shape, cols_shape, rows_shape, rows_shape, rows_shape,
                   jax.ShapeDtypeStruct((n, LANES), BF16)]
        + [jax.ShapeDtypeStruct(a.shape, BF16) for a in later_weights],
        scratch_shapes=[pltpu.VMEM((FOX_HEADS, LANES), F32)],
        compiler_params=pltpu.CompilerParams(dimension_semantics=("arbitrary",),
                                             vmem_limit_bytes=VMEM_LIMIT),
        name="proj_even",
    )(x2d, gain, w, wf, bd, hg, bf, *later_weights)
    return outs


def _fox_kernel(q_ref, k_ref, vt_ref, caug_ref, o_ref, ka_ref, va_ref, *, seq):
    jj = pl.program_id(1)
    qb = ATT_Q_BLOCK
    low_s = lax.broadcasted_iota(jnp.int32, (seq, LANES), 1) < HEAD_DIM
    ca = caug_ref[0]
    sub = lax.broadcasted_iota(jnp.int32, (FOX_VT_ROWS - HEAD_DIM, seq), 0)
    ones_row = jnp.where(sub == 0, 1.0, 0.0).astype(BF16)
    for p in range(FOX_PAIRS_PER_STEP):
        kp = k_ref[0, :, p * LANES:(p + 1) * LANES]
        ka_ref[2 * p] = jnp.where(low_s, kp, ca)
        ka_ref[2 * p + 1] = jnp.where(low_s, ca, kp)
        for hh in range(2):
            r = p * LANES + hh * HEAD_DIM
            va_ref[2 * p + hh, 0:HEAD_DIM, :] = vt_ref[0, r:r + HEAD_DIM, :]
            va_ref[2 * p + hh, HEAD_DIM:FOX_VT_ROWS, :] = ones_row
    lane_q = lax.broadcasted_iota(jnp.int32, (qb, LANES), 1)
    key_i = lax.broadcasted_iota(jnp.int32, (qb, qb), 0)
    qry_i = lax.broadcasted_iota(jnp.int32, (qb, qb), 1)
    causal = key_i <= qry_i
    owns = (lane_q < HEAD_DIM, lane_q >= HEAD_DIM)

    def scores(p, qi, hh):
        r0 = qi * qb
        head = 2 * (FOX_PAIRS_PER_STEP * jj + p) + hh
        rel = lane_q - (HEAD_DIM if hh == 0 else 0) - head
        pick = (rel == 0) | (rel == FOX_HEADS) | (rel == 2 * FOX_HEADS)
        qz = jnp.where(owns[hh], q_ref[0, r0:r0 + qb, p * LANES:(p + 1) * LANES],
                       jnp.where(pick, 1.0, 0.0).astype(BF16))
        tiles = []
        for t in range(qi + 1):
            s = _dot_nt(ka_ref[2 * p + hh, t * qb:(t + 1) * qb, :], qz)
            tiles.append(jnp.where(causal, s, NEG_INF) if t == qi else s)
            yield
        return p, qi, hh, tiles

    halves = []

    def finish(p, qi, hh, tiles):
        r0 = qi * qb
        m = jnp.max(functools.reduce(jnp.maximum, tiles), axis=0, keepdims=True)
        acc = None
        for t, s in enumerate(tiles):
            part = _dot(va_ref[2 * p + hh, :, t * qb:(t + 1) * qb], jnp.exp2(s - m).astype(BF16))
            acc = part if acc is None else acc + part
            yield
        halves.append(acc[0:HEAD_DIM] / acc[HEAD_DIM:HEAD_DIM + 1])
        if hh == 1:
            o_ref[0, r0:r0 + qb, p * LANES:(p + 1) * LANES] = jnp.concatenate(
                [halves[-2], halves[-1]], axis=0).T.astype(BF16)

    _interleaved([(p, qi, hh) for p in range(FOX_PAIRS_PER_STEP)
                  for qi in _hill_order(seq // qb) for hh in range(2)], scores, finish,
                 depth=FOX_LAG)


def _fox_attention(fq, fk, fvt, caug, batch, seq):
    width = FOX_PAIRS_PER_STEP * LANES
    blk = pl.BlockSpec((1, seq, width), lambda b, j: (b, 0, j))
    nheads = 2 * FOX_PAIRS_PER_STEP
    return pl.pallas_call(
        functools.partial(_fox_kernel, seq=seq),
        grid=(batch, FOX_W // width),
        in_specs=[blk, blk, pl.BlockSpec((1, width, seq), lambda b, j: (b, j, 0)),
                  pl.BlockSpec((1, seq, LANES), lambda b, j: (b, 0, 0))],
        out_specs=blk,
        out_shape=jax.ShapeDtypeStruct((batch, seq, FOX_W), BF16),
        scratch_shapes=[pltpu.VMEM((nheads, seq, LANES), BF16),
                        pltpu.VMEM((nheads, FOX_VT_ROWS, seq), BF16)],
        compiler_params=pltpu.CompilerParams(dimension_semantics=("parallel", "parallel"),
                                             vmem_limit_bytes=VMEM_LIMIT),
        name="fox_attention",
    )(fq, fk, fvt, caug)


def _diff_kernel(lam_ref, g_ref, q_ref, k_ref, v_ref, gb_ref, o_ref, va_ref, *, seq, lambda_init):
    qb = ATT_Q_BLOCK
    lv = lam_ref[...]
    lam = (jnp.exp(jnp.sum(lv[0:1] * lv[1:2], axis=-1, keepdims=True))
           - jnp.exp(jnp.sum(lv[2:3] * lv[3:4], axis=-1, keepdims=True)) + lambda_init)
    for p in range(DIFF_HEADS_PER_STEP):
        va_ref[p, :, 0:LANES] = v_ref[0, :, p * LANES:(p + 1) * LANES]
        va_ref[p, :, LANES:2 * LANES] = jnp.ones((seq, LANES), BF16)
    lane_q = lax.broadcasted_iota(jnp.int32, (qb, LANES), 1)
    bias_prev = [jnp.concatenate([gb_ref[p, :, 0:qb]] * 2, axis=0)
                 for p in range(DIFF_HEADS_PER_STEP)]
    bias_diag = [jnp.concatenate([gb_ref[p, :, qb:2 * qb]] * 2, axis=0)
                 for p in range(DIFF_HEADS_PER_STEP)]
    gain = g_ref[...] * (1.0 - lambda_init)
    zero = jnp.zeros((), BF16)

    def scores(p, qi):
        r0 = qi * qb
        qp = q_ref[0, r0:r0 + qb, p * LANES:(p + 1) * LANES]
        qz = jnp.concatenate([jnp.where(lane_q < HEAD_DIM, qp, zero),
                              jnp.where(lane_q < HEAD_DIM, zero, qp)], axis=0)
        tiles = []
        for t in range(qi + 1):
            s = _dot_nt(qz, k_ref[0, t * qb:(t + 1) * qb, p * LANES:(p + 1) * LANES])
            tiles.append(s + bias_diag[p] if t == qi else s + bias_prev[p] if t == qi - 1 else s)
            yield
        return p, qi, tiles

    def finish(p, qi, tiles):
        r0 = qi * qb
        m = jnp.max(functools.reduce(jnp.maximum, tiles), axis=-1, keepdims=True)
        acc = None
        for t, s in enumerate(tiles):
            part = _dot(jnp.exp2(s - m).astype(BF16), va_ref[p, t * qb:(t + 1) * qb, :])
            acc = part if acc is None else acc + part
            yield
        a = acc[:, 0:LANES] / acc[:, LANES:2 * LANES]
        o = a[0:qb] - lam * a[qb:2 * qb]
        o_ref[0, r0:r0 + qb, p * LANES:(p + 1) * LANES] = _rms(o, gain).astype(BF16)

    _interleaved([(p, qi) for p in range(DIFF_HEADS_PER_STEP) for qi in _hill_order(seq // qb)],
                 scores, finish, depth=DIFF_LAG)


def _diff_attention(lam4, subln, dq, dk, dv, gbias, batch, seq, lambda_init):
    hps = DIFF_HEADS_PER_STEP
    blk = pl.BlockSpec((1, seq, hps * LANES), lambda h, b: (b, 0, h))
    return pl.pallas_call(
        functools.partial(_diff_kernel, seq=seq, lambda_init=lambda_init),
        grid=(DIFF_HEADS // hps, batch),
        in_specs=[
            pl.BlockSpec(lam4.shape, lambda h, b: (0, 0)),
            pl.BlockSpec(subln.shape, lambda h, b: (0, 0)),
            blk, blk, blk,
            pl.BlockSpec((hps, ATT_Q_BLOCK, 2 * ATT_Q_BLOCK), lambda h, b: (h, 0, 0)),
        ],
        out_specs=blk,
        out_shape=jax.ShapeDtypeStruct((batch, seq, DIFF_W), BF16),
        scratch_shapes=[pltpu.VMEM((hps, seq, 2 * LANES), BF16)],
        compiler_params=pltpu.CompilerParams(dimension_semantics=("parallel", "parallel"),
                                             vmem_limit_bytes=VMEM_LIMIT),
        name="diff_attention",
    )(lam4, subln, dq, dk, dv, gbias)


def _proj1_kernel(x_ref, gain_ref, w_ref, bd_ref, hg_ref, q_ref, kk_ref, vt_ref):
    h = _rms(x_ref[...], gain_ref[...]).astype(BF16)
    bd = bd_ref[...]
    qw = SWA_HEADS * HEAD_DIM
    kw = SWA_KV_HEADS * HEAD_DIM
    low = lax.broadcasted_iota(jnp.int32, (h.shape[0], LANES), 1) < HEAD_DIM

    def project(c0):
        return c0, _dot(h, w_ref[:, c0:c0 + 2 * MXU_TILE])

    def epilogue(c0, proj):
        if c0 < qw:
            for i in range(2):
                cols = slice(i * MXU_TILE, (i + 1) * MXU_TILE)
                q_ref[:, c0 + i * MXU_TILE:c0 + (i + 1) * MXU_TILE] = _head_norm(
                    proj[:, cols], bd, hg_ref[0:1, :]).astype(BF16)
            return
        kn = _head_norm(proj[:, 0:kw], bd, hg_ref[1:2, :])
        for p in range(kw // LANES):
            pair = kn[:, p * LANES:(p + 1) * LANES]
            swapped = pltpu.roll(pair, HEAD_DIM, axis=1)
            kk_ref[:, (2 * p) * LANES:(2 * p + 1) * LANES] = jnp.where(low, pair, swapped).astype(BF16)
            kk_ref[:, (2 * p + 1) * LANES:(2 * p + 2) * LANES] = jnp.where(low, swapped, pair).astype(BF16)
        vt_ref[0] = proj[:, kw:2 * kw].T.astype(BF16)

    _lagged([(qw,)] + [(c0,) for c0 in range(0, qw, 2 * MXU_TILE)], project, epilogue,
            depth=PROJ_LAG)


def _proj1(x2d, gain, w, bd, hg, seq):
    n = x2d.shape[0]
    tm = ROW_TILE
    tps = seq // tm
    qw = SWA_HEADS * HEAD_DIM
    kw = SWA_KV_HEADS * HEAD_DIM
    full = lambda shape: pl.BlockSpec(shape, lambda t: (0,) * len(shape))
    return pl.pallas_call(
        _proj1_kernel,
        grid=(n // tm,),
        in_specs=[
            pl.BlockSpec((tm, D_MODEL), lambda t: (t, 0)),
            full((1, D_MODEL)),
            full(w.shape),
            full((MXU_TILE, MXU_TILE)),
            full(hg.shape),
        ],
        out_specs=[pl.BlockSpec((tm, qw), lambda t: (t, 0)),
                   pl.BlockSpec((tm, 2 * kw), lambda t: (t, 0)),
                   pl.BlockSpec((1, kw, tm), lambda t: (t // tps, 0, t % tps))],
        out_shape=[jax.ShapeDtypeStruct((n, qw), BF16),
                   jax.ShapeDtypeStruct((n, 2 * kw), BF16),
                   jax.ShapeDtypeStruct((n // seq, kw, seq), BF16)],
        compiler_params=pltpu.CompilerParams(dimension_semantics=("parallel",),
                                             vmem_limit_bytes=VMEM_LIMIT),
        name="proj_odd",
    )(x2d, gain, w, bd, hg)


def _swa_kernel(sink_ref, q_ref, kk_ref, vt_ref, sb_ref, o_ref, va_ref, *, seq):
    jj = pl.program_id(0)
    w = WINDOW
    gw = SWA_GROUP * HEAD_DIM
    sub = lax.broadcasted_iota(jnp.int32, (LANES - HEAD_DIM, seq), 0)
    sinks = []
    for p in range(SWA_KV_PER_STEP):
        va_ref[p, 0:HEAD_DIM, :] = vt_ref[0, p * HEAD_DIM:(p + 1) * HEAD_DIM, :]
        va_ref[p, HEAD_DIM:LANES, :] = jnp.where(sub == 0, 1.0, 0.0).astype(BF16)
        head0 = (jj * SWA_KV_PER_STEP + p) * SWA_GROUP
        sinks.append(jnp.concatenate(
            [jnp.full((1, w), sink_ref[head0 + g] * LOG2E, F32) for g in range(SWA_GROUP)],
            axis=1))
    low = lax.broadcasted_iota(jnp.int32, (w, LANES), 1) < HEAD_DIM
    zero = jnp.zeros((), BF16)

    def scores(p, n):
        r0 = n * w
        qb = q_ref[0, r0:r0 + w, p * gw:(p + 1) * gw]
        qa, qc = qb[:, 0:LANES], qb[:, LANES:2 * LANES]
        q4 = jnp.concatenate([jnp.where(low, qa, zero), jnp.where(low, zero, qa),
                              jnp.where(low, qc, zero), jnp.where(low, zero, qc)], axis=0)
        lanes = slice(p * LANES, (p + 1) * LANES)
        if n == 0:
            return p, n, _dot_nt(kk_ref[0, 0:w, lanes], q4) + sb_ref[p, w:2 * w, :]
        return p, n, _dot_nt(kk_ref[0, r0 - w:r0 + w, lanes], q4) + sb_ref[p]

    def finish(p, n, st):
        r0 = n * w
        vals = va_ref[p, :, 0:w] if n == 0 else va_ref[p, :, r0 - w:r0 + w]
        m = jnp.maximum(jnp.max(st, axis=0, keepdims=True), sinks[p])
        acc = _dot(vals, jnp.exp2(st - m).astype(BF16))
        denom = acc[HEAD_DIM:HEAD_DIM + 1, :] + jnp.exp2(sinks[p] - m)
        ot = acc[0:HEAD_DIM, :] / denom
        o_ref[0, r0:r0 + w, p * gw:(p + 1) * gw] = jnp.concatenate(
            [jnp.concatenate([ot[:, 0:w], ot[:, w:2 * w]], axis=0).T,
             jnp.concatenate([ot[:, 2 * w:3 * w], ot[:, 3 * w:4 * w]], axis=0).T],
            axis=1).astype(BF16)

    _lagged([(p, n) for p in range(SWA_KV_PER_STEP) for n in range(seq // w)], scores, finish,
            depth=SWA_LAG)


def _swa_attention(sinks, q, kk, vt, swbias, batch, seq):
    kps = SWA_KV_PER_STEP
    gw = kps * SWA_GROUP * HEAD_DIM
    return pl.pallas_call(
        functools.partial(_swa_kernel, seq=seq),
        grid=(SWA_KV_HEADS // kps, batch),
        in_specs=[
            pl.BlockSpec(memory_space=pltpu.SMEM),
            pl.BlockSpec((1, seq, gw), lambda h, b: (b, 0, h)),
            pl.BlockSpec((1, seq, kps * LANES), lambda h, b: (b, 0, h)),
            pl.BlockSpec((1, kps * HEAD_DIM, seq), lambda h, b: (b, h, 0)),
            pl.BlockSpec((kps, 2 * WINDOW, SWA_GROUP * WINDOW), lambda h, b: (h, 0, 0)),
        ],
        out_specs=pl.BlockSpec((1, seq, gw), lambda h, b: (b, 0, h)),
        out_shape=jax.ShapeDtypeStruct((batch, seq, SWA_HEADS * HEAD_DIM), BF16),
        scratch_shapes=[pltpu.VMEM((kps, LANES, seq), BF16)],
        compiler_params=pltpu.CompilerParams(dimension_semantics=("parallel", "parallel"),
                                             vmem_limit_bytes=VMEM_LIMIT),
        name="swa_attention",
    )(sinks, q, kk, vt, swbias)


def _outmlp_kernel(*refs, n_mix):
    x_ref = refs[0]
    a_refs = refs[1:1 + n_mix]
    wo_ref, gain_ref, wg_ref, wu_ref, wd_ref, o_ref = refs[1 + n_mix:]
    mix = jnp.concatenate([a_ref[...] for a_ref in a_refs], axis=1)
    x1 = x_ref[...] + _dot(mix, wo_ref[...])
    h = _rms(x1, gain_ref[...]).astype(BF16)
    out = x1
    for c0 in range(0, FFN_HIDDEN, HIDDEN_CHUNK):
        c1 = min(c0 + HIDDEN_CHUNK, FFN_HIDDEN)
        g = _dot(h, wg_ref[:, c0:c1])
        u = _dot(h, wu_ref[:, c0:c1])
        y = (g * (1.0 / (1.0 + jnp.exp(-g))) * u).astype(BF16)
        out = out + _dot(y, wd_ref[c0:c1, :])
    o_ref[...] = out


def _outmlp(x2d, mixes, wo, gain, wg, wu, wd, layer):
    n = x2d.shape[0]
    tm = ROW_TILE
    full = lambda shape: pl.BlockSpec(shape, lambda t: (0,) * len(shape),
                                      pipeline_mode=pl.Buffered(1))
    of_layer = lambda w: pl.BlockSpec((None,) + w.shape[1:], lambda t: (layer, 0, 0),
                                      pipeline_mode=pl.Buffered(1))
    in_specs = [pl.BlockSpec((tm, D_MODEL), lambda t: (t, 0))]
    in_specs += [pl.BlockSpec((tm, a.shape[1]), lambda t: (t, 0)) for a in mixes]
    in_specs += [full(wo.shape), full(gain.shape), of_layer(wg), of_layer(wu), of_layer(wd)]
    return pl.pallas_call(
        functools.partial(_outmlp_kernel, n_mix=len(mixes)),
        grid=(n // tm,),
        in_specs=in_specs,
        out_specs=pl.BlockSpec((tm, D_MODEL), lambda t: (t, 0)),
        out_shape=jax.ShapeDtypeStruct((n, D_MODEL), F32),
        compiler_params=pltpu.CompilerParams(dimension_semantics=("parallel",),
                                             vmem_limit_bytes=VMEM_LIMIT),
        name="outproj_swiglu",
    )(x2d, *mixes, wo, gain, wg, wu, wd)


def _block_diag_mean():
    i = np.arange(MXU_TILE)
    same = (i[:, None] // HEAD_DIM) == (i[None, :] // HEAD_DIM)
    return jnp.asarray(same / HEAD_DIM, BF16)


def kernel(x, rel_bias_table, ev_attn_norm, ev_w_in, ev_b_forget, ev_fox_q_norm, ev_fox_k_norm,
           ev_diff_q_norm, ev_diff_k_norm, ev_lambda_q1, ev_lambda_k1, ev_lambda_q2, ev_lambda_k2,
           ev_diff_subln, ev_w_out, od_attn_norm, od_w_qkv, od_q_norm, od_k_norm, od_sinks, od_w_out,
           ffn_norm, w_gate, w_up, w_down):
    batch, seq, d = x.shape
    n = batch * seq
    x2d = x.reshape(n, d)
    bd = _block_diag_mean()
    swbias, gbias = _bias_tiles(rel_bias_table.astype(F32))
    swbias = swbias.reshape(SWA_KV_HEADS, SWA_GROUP, 2 * WINDOW, WINDOW).transpose(0, 2, 1, 3)
    swbias = swbias.reshape(SWA_KV_HEADS, 2 * WINDOW, SWA_GROUP * WINDOW)

    lambda_init = 0.8 - 0.6 * math.exp(-0.3 * 0)
    w_in = ev_w_in[0]
    ff0 = 3 * FOX_W
    d0 = ff0 + FOX_HEADS
    w0 = jnp.concatenate([w_in[:, 0:ff0], w_in[:, d0:d0 + 3 * DIFF_W]], axis=1).astype(BF16)
    wf = jnp.concatenate(
        [w_in[:, ff0:d0].T, jnp.zeros((BF16_SUBLANES - FOX_HEADS, d), w_in.dtype)],
        axis=0).astype(BF16)
    tile8 = lambda g, scale=1.0: jnp.tile(g.astype(F32) * scale, FOX_W // HEAD_DIM)
    hg0 = jnp.stack([tile8(ev_fox_q_norm[0], Q_GAIN_SCALE), tile8(ev_fox_k_norm[0]),
                     tile8(ev_diff_q_norm[0], Q_GAIN_SCALE), tile8(ev_diff_k_norm[0])])
    bf = ev_b_forget[0].astype(F32).reshape(FOX_HEADS, 1)
    depth, hidden = w_gate.shape[0], w_gate.shape[2]
    later = [w_gate.reshape(depth * d, hidden), w_up.reshape(depth * d, hidden),
             w_down.reshape(depth * hidden, d), ev_w_out[0], od_w_qkv[0], od_w_out[0]]
    fq, fk, fvt, dq, dk, dv, caug, wg, wu, wd, wo0, wqkv, wo1 = _proj0(
        x2d, ev_attn_norm[0].astype(F32).reshape(1, d), w0, wf, bd, hg0, bf, seq,
        [a.astype(F32) for a in later])
    wg, wu = wg.reshape(depth, d, hidden), wu.reshape(depth, d, hidden)
    wd = wd.reshape(depth, hidden, d)
    r3 = lambda a: a.reshape(batch, seq, a.shape[-1])
    fox = _fox_attention(r3(fq), r3(fk), fvt, r3(caug), batch, seq)
    lam4 = jnp.stack([ev_lambda_q1[0], ev_lambda_k1[0], ev_lambda_q2[0], ev_lambda_k2[0]]).astype(F32)
    dout = _diff_attention(lam4, ev_diff_subln[0].astype(F32).reshape(1, 2 * HEAD_DIM),
                           r3(dq), r3(dk), r3(dv), gbias, batch, seq, lambda_init)
    x2d = _outmlp(x2d, [fox.reshape(n, FOX_W), dout.reshape(n, DIFF_W)], wo0,
                  ffn_norm[0].astype(F32).reshape(1, d), wg, wu, wd, 0)

    tile4 = lambda g, scale=1.0: jnp.tile(g.astype(F32) * scale, MXU_TILE // HEAD_DIM)
    hg1 = jnp.stack([tile4(od_q_norm[0], Q_GAIN_SCALE), tile4(od_k_norm[0])])
    q, kk, vt = _proj1(x2d, od_attn_norm[0].astype(F32).reshape(1, d), wqkv,
                       bd, hg1, seq)
    swa = _swa_attention(od_sinks[0].astype(F32), r3(q), r3(kk), vt, swbias, batch, seq)
    x2d = _outmlp(x2d, [swa.reshape(n, SWA_HEADS * HEAD_DIM)], wo1,
                  ffn_norm[1].astype(F32).reshape(1, d), wg, wu, wd, 1)
    return x2d.reshape(batch, seq, d)
```

```python
import functools
import math

import numpy as np
import jax
import jax.numpy as jnp
from jax import lax
from jax.experimental import pallas as pl
from jax.experimental.pallas import tpu as pltpu

F32 = jnp.float32
BF16 = jnp.bfloat16

D_MODEL = 1024
HEAD_DIM = 64
FOX_HEADS = 8
DIFF_HEADS = 4
SWA_HEADS = 16
SWA_KV_HEADS = 4
SWA_GROUP = SWA_HEADS // SWA_KV_HEADS
WINDOW = 128
NUM_BUCKETS = 32
MAX_DISTANCE = 128
FFN_HIDDEN = 2816
RMS_EPS = 1e-6
NEG_INF = -1e30
FOX_W = FOX_HEADS * HEAD_DIM
DIFF_W = DIFF_HEADS * 2 * HEAD_DIM
QK_SCALE = HEAD_DIM ** -0.5
LOG2E = math.log2(math.e)
Q_GAIN_SCALE = QK_SCALE * LOG2E

LANES = 128
BF16_SUBLANES = 16
FOX_VT_ROWS = HEAD_DIM + BF16_SUBLANES
MXU_TILE = 256
ROW_TILE = 1024
ATT_Q_BLOCK = 256
FOX_PAIRS_PER_STEP = 2
DIFF_HEADS_PER_STEP = 2
SWA_KV_PER_STEP = 2
FOX_LAG = 4
DIFF_LAG = 2
PROJ_LAG = 1
SWA_LAG = 4
HIDDEN_CHUNK = 512
V7X_VMEM_BYTES = 64 * 1024 * 1024
VMEM_LIMIT = V7X_VMEM_BYTES - 8 * 1024 * 1024

_NT = (((1,), (1,)), ((), ()))


def _dot(a, b):
    return jnp.dot(a, b, preferred_element_type=F32)


def _dot_nt(a, b):
    return lax.dot_general(a, b, _NT, preferred_element_type=F32)


def _lagged(items, first, second, depth=1):
    pending = []
    for item in items:
        pending.append(first(*item))
        if len(pending) > depth:
            second(*pending.pop(0))
    for p in pending:
        second(*p)


def _interleaved(items, first, second, depth):
    def drive(gens):
        done = {}
        while gens:
            for g in list(gens):
                try:
                    next(g)
                except StopIteration as stop:
                    done[id(g)] = stop.value
                    gens.remove(g)
        return done

    pending = []
    for item in items:
        g1 = first(*item)
        gens = [g1] if len(pending) < depth else [g1, second(*pending.pop(0))]
        pending.append(drive(gens)[id(g1)])
    for p in pending:
        drive([second(*p)])


def _hill_order(nblocks):
    return list(range(0, nblocks, 2)) + list(range(nblocks - 1 - nblocks % 2, 0, -2))


def _rms(xf, gain):
    ms = jnp.mean(xf * xf, axis=-1, keepdims=True)
    return xf * lax.rsqrt(ms + RMS_EPS) * gain


def _head_norm(acc, bd, gain):
    msq = _dot((acc * acc).astype(BF16), bd)
    return acc * lax.rsqrt(msq + RMS_EPS) * gain


def _bucket_ids(delta):
    n = np.maximum(delta, 0)
    max_exact = NUM_BUCKETS // 2
    nf = np.maximum(n, 1).astype(np.float64)
    large = max_exact + (np.log(nf / max_exact) / math.log(MAX_DISTANCE / max_exact)
                         * (NUM_BUCKETS - max_exact)).astype(np.int32)
    large = np.minimum(large, NUM_BUCKETS - 1)
    return np.where(n < max_exact, n, large).astype(np.int32)


def _bias_kernel(table_ref, bsw_ref, bg_ref, sw_ref, g_ref):
    h = pl.program_id(0)
    bsw = bsw_ref[...]
    sw = jnp.zeros(bsw.shape, F32)
    for b in range(NUM_BUCKETS):
        sw = jnp.where(bsw == b, table_ref[b, h] * LOG2E, sw)
    sw_ref[0] = jnp.where(bsw < 0, NEG_INF, sw)

    @pl.when(h < DIFF_HEADS)
    def _():
        bg = bg_ref[...]
        far = table_ref[NUM_BUCKETS - 1, h]
        g = jnp.zeros(bg.shape, F32)
        for b in range(NUM_BUCKETS):
            g = jnp.where(bg == b, (table_ref[b, h] - far) * LOG2E, g)
        g_ref[0] = jnp.where(bg < 0, NEG_INF, g)


def _bias_tiles(table):
    a = np.arange(WINDOW)[None, :]
    b = np.arange(2 * WINDOW)[:, None]
    d_sw = WINDOW + a - b
    bsw = np.where((d_sw >= 0) & (d_sw < WINDOW), _bucket_ids(d_sw), -1).astype(np.int32)
    a = np.arange(ATT_Q_BLOCK)[:, None]
    b = np.arange(2 * ATT_Q_BLOCK)[None, :]
    d_g = ATT_Q_BLOCK + a - b
    bg = np.where(d_g >= 0, _bucket_ids(d_g), -1).astype(np.int32)
    nh = table.shape[1]
    return pl.pallas_call(
        _bias_kernel,
        grid=(nh,),
        in_specs=[
            pl.BlockSpec(memory_space=pltpu.SMEM),
            pl.BlockSpec(bsw.shape, lambda h: (0, 0)),
            pl.BlockSpec(bg.shape, lambda h: (0, 0)),
        ],
        out_specs=[
            pl.BlockSpec((1,) + bsw.shape, lambda h: (h, 0, 0)),
            pl.BlockSpec((1,) + bg.shape, lambda h: (jnp.minimum(h, DIFF_HEADS - 1), 0, 0)),
        ],
        out_shape=[
            jax.ShapeDtypeStruct((nh,) + bsw.shape, F32),
            jax.ShapeDtypeStruct((DIFF_HEADS,) + bg.shape, F32),
        ],
        compiler_params=pltpu.CompilerParams(dimension_semantics=("arbitrary",)),
        name="rel_bias_tiles",
    )(table, jnp.asarray(bsw), jnp.asarray(bg))


def _proj0_kernel(*refs, tiles_per_seq, n_cast):
    x_ref, gain_ref, w_ref, wf_ref, bd_ref, hg_ref, bf_ref = refs[0:7]
    cast_src = refs[7:7 + n_cast]
    fq_ref, fk_ref, fv_ref, dq_ref, dk_ref, dv_ref, caug_ref = refs[7 + n_cast:14 + n_cast]
    cast_dst = refs[14 + n_cast:14 + 2 * n_cast]
    carry_ref = refs[14 + 2 * n_cast]
    t = pl.program_id(0)

    @pl.when(t == 0)
    def _():
        carry_ref[...] = jnp.zeros_like(carry_ref)

    h = _rms(x_ref[...], gain_ref[...]).astype(BF16)
    bd = bd_ref[...]
    rows = h.shape[0]

    outs = (fq_ref, fk_ref, fv_ref, dq_ref, dk_ref, dv_ref)
    gains = (0, 1, None, 2, 3, None)
    feature_major = (False, False, True, False, False, False)

    def project(g):
        return g, _dot(h, w_ref[:, g * FOX_W:(g + 1) * FOX_W])

    def epilogue(g, proj):
        if feature_major[g]:
            outs[g][0] = proj.T.astype(BF16)
            return
        if gains[g] is None:
            outs[g][...] = proj.astype(BF16)
            return
        for i in range(FOX_W // MXU_TILE):
            cols = slice(i * MXU_TILE, (i + 1) * MXU_TILE)
            outs[g][:, cols] = _head_norm(
                proj[:, cols], bd, hg_ref[gains[g]:gains[g] + 1, cols]).astype(BF16)

    pending = [project(g) for g in range(PROJ_LAG)]

    z8 = _dot_nt(wf_ref[...], h)[0:FOX_HEADS, :] + bf_ref[...]
    y = -(jnp.maximum(-z8, 0.0) + jnp.log1p(jnp.exp(-jnp.abs(z8))))
    lane = lax.broadcasted_iota(jnp.int32, y.shape, 1)
    shift = 1
    while shift < rows:
        y = y + jnp.where(lane >= shift, pltpu.roll(y, shift, axis=1), 0.0)
        shift *= 2

    carry = jnp.where(t % tiles_per_seq == 0, 0.0, carry_ref[:, 0:1])
    c = y + carry
    carry_ref[...] = jnp.broadcast_to(c[:, rows - 1:rows], carry_ref.shape)
    cn = c * (-LOG2E)
    hi = cn.astype(BF16).astype(F32)
    r1 = cn - hi
    mid = r1.astype(BF16).astype(F32)
    lo = (r1 - mid).astype(BF16).astype(F32)
    half = jnp.concatenate([hi, mid, lo, jnp.zeros((HEAD_DIM - 3 * FOX_HEADS, rows), F32)], axis=0)
    caug_ref[...] = jnp.concatenate([half, half], axis=0).T.astype(BF16)

    for g in range(PROJ_LAG, len(outs)):
        pending.append(project(g))
        epilogue(*pending.pop(0))
    for p in pending:
        epilogue(*p)

    for src, dst in zip(cast_src, cast_dst):
        dst[...] = src[...].astype(BF16)


def _proj0(x2d, gain, w, wf, bd, hg, bf, seq, later_weights):
    n = x2d.shape[0]
    tm = ROW_TILE
    tps = seq // tm
    steps = n // tm
    slab = lambda a: pl.BlockSpec((a.shape[0] // steps, a.shape[1]), lambda t: (t, 0))
    for a in later_weights:
        assert a.shape[0] % (steps * BF16_SUBLANES) == 0, a.shape
    row_spec = pl.BlockSpec((tm, FOX_W), lambda t: (t, 0))
    col_spec = pl.BlockSpec((1, FOX_W, tm), lambda t: (t // tps, 0, t % tps))
    rows_shape = jax.ShapeDtypeStruct((n, FOX_W), BF16)
    cols_shape = jax.ShapeDtypeStruct((n // seq, FOX_W, seq), BF16)
    full = lambda shape: pl.BlockSpec(shape, lambda t: (0,) * len(shape))
    outs = pl.pallas_call(
        functools.partial(_proj0_kernel, tiles_per_seq=tps, n_cast=len(later_weights)),
        grid=(steps,),
        in_specs=[
            pl.BlockSpec((tm, D_MODEL), lambda t: (t, 0)),
            full((1, D_MODEL)),
            full(w.shape),
            full(wf.shape),
            full((MXU_TILE, MXU_TILE)),
            full(hg.shape),
            full(bf.shape),
        ] + [slab(a) for a in later_weights],
        out_specs=[row_spec, row_spec, col_spec, row_spec, row_spec, row_spec,
                   pl.BlockSpec((tm, LANES), lambda t: (t, 0))] + [slab(a) for a in later_weights],
        out_shape=[rows_shape, rows_shape, cols_shape, rows_shape, rows_shape, rows_shape,
                   jax.ShapeDtypeStruct((n, LANES), BF16)]
        + [jax.ShapeDtypeStruct(a.shape, BF16) for a in later_weights],
        scratch_shapes=[pltpu.VMEM((FOX_HEADS, LANES), F32)],
        compiler_params=pltpu.CompilerParams(dimension_semantics=("arbitrary",),
                                             vmem_limit_bytes=VMEM_LIMIT),
        name="proj_even",
    )(x2d, gain, w, wf, bd, hg, bf, *later_weights)
    return outs


def _fox_kernel(q_ref, k_ref, vt_ref, caug_ref, o_ref, ka_ref, va_ref, *, seq):
    jj = pl.program_id(1)
    qb = ATT_Q_BLOCK
    low_s = lax.broadcasted_iota(jnp.int32, (seq, LANES), 1) < HEAD_DIM
    ca = caug_ref[0]
    sub = lax.broadcasted_iota(jnp.int32, (FOX_VT_ROWS - HEAD_DIM, seq), 0)
    ones_row = jnp.where(sub == 0, 1.0, 0.0).astype(BF16)
    for p in range(FOX_PAIRS_PER_STEP):
        kp = k_ref[0, :, p * LANES:(p + 1) * LANES]
        ka_ref[2 * p] = jnp.where(low_s, kp, ca)
        ka_ref[2 * p + 1] = jnp.where(low_s, ca, kp)
        for hh in range(2):
            r = p * LANES + hh * HEAD_DIM
            va_ref[2 * p + hh, 0:HEAD_DIM, :] = vt_ref[0, r:r + HEAD_DIM, :]
            va_ref[2 * p + hh, HEAD_DIM:FOX_VT_ROWS, :] = ones_row
    lane_q = lax.broadcasted_iota(jnp.int32, (qb, LANES), 1)
    key_i = lax.broadcasted_iota(jnp.int32, (qb, qb), 0)
    qry_i = lax.broadcasted_iota(jnp.int32, (qb, qb), 1)
    causal = key_i <= qry_i
    owns = (lane_q < HEAD_DIM, lane_q >= HEAD_DIM)

    def scores(p, qi, hh):
        r0 = qi * qb
        head = 2 * (FOX_PAIRS_PER_STEP * jj + p) + hh
        rel = lane_q - (HEAD_DIM if hh == 0 else 0) - head
        pick = (rel == 0) | (rel == FOX_HEADS) | (rel == 2 * FOX_HEADS)
        qz = jnp.where(owns[hh], q_ref[0, r0:r0 + qb, p * LANES:(p + 1) * LANES],
                       jnp.where(pick, 1.0, 0.0).astype(BF16))
        s_all = _dot_nt(ka_ref[2 * p + hh, 0:(qi + 1) * qb, :], qz)
        yield
        tiles = [s_all[t * qb:(t + 1) * qb] for t in range(qi)]
        tiles.append(jnp.where(causal, s_all[qi * qb:(qi + 1) * qb], NEG_INF))
        return p, qi, hh, tiles

    halves = []

    def finish(p, qi, hh, tiles):
        r0 = qi * qb
        m = jnp.max(functools.reduce(jnp.maximum, tiles), axis=0, keepdims=True)
        acc = None
        for t, s in enumerate(tiles):
            part = _dot(va_ref[2 * p + hh, :, t * qb:(t + 1) * qb], jnp.exp2(s - m).astype(BF16))
            acc = part if acc is None else acc + part
            yield
        halves.append(acc[0:HEAD_DIM] / acc[HEAD_DIM:HEAD_DIM + 1])
        if hh == 1:
            o_ref[0, r0:r0 + qb, p * LANES:(p + 1) * LANES] = jnp.concatenate(
                [halves[-2], halves[-1]], axis=0).T.astype(BF16)

    _interleaved([(p, qi, hh) for p in range(FOX_PAIRS_PER_STEP)
                  for qi in _hill_order(seq // qb) for hh in range(2)], scores, finish,
                 depth=FOX_LAG)


def _fox_attention(fq, fk, fvt, caug, batch, seq):
    width = FOX_PAIRS_PER_STEP * LANES
    blk = pl.BlockSpec((1, seq, width), lambda b, j: (b, 0, j))
    nheads = 2 * FOX_PAIRS_PER_STEP
    return pl.pallas_call(
        functools.partial(_fox_kernel, seq=seq),
        grid=(batch, FOX_W // width),
        in_specs=[blk, blk, pl.BlockSpec((1, width, seq), lambda b, j: (b, j, 0)),
                  pl.BlockSpec((1, seq, LANES), lambda b, j: (b, 0, 0))],
        out_specs=blk,
        out_shape=jax.ShapeDtypeStruct((batch, seq, FOX_W), BF16),
        scratch_shapes=[pltpu.VMEM((nheads, seq, LANES), BF16),
                        pltpu.VMEM((nheads, FOX_VT_ROWS, seq), BF16)],
        compiler_params=pltpu.CompilerParams(dimension_semantics=("parallel", "parallel"),
                                             vmem_limit_bytes=VMEM_LIMIT),
        name="fox_attention",
    )(fq, fk, fvt, caug)


def _diff_kernel(lam_ref, g_ref, q_ref, k_ref, v_ref, gb_ref, o_ref, va_ref, *, seq, lambda_init):
    qb = ATT_Q_BLOCK
    lv = lam_ref[...]
    lam = (jnp.exp(jnp.sum(lv[0:1] * lv[1:2], axis=-1, keepdims=True))
           - jnp.exp(jnp.sum(lv[2:3] * lv[3:4], axis=-1, keepdims=True)) + lambda_init)
    for p in range(DIFF_HEADS_PER_STEP):
        va_ref[p, :, 0:LANES] = v_ref[0, :, p * LANES:(p + 1) * LANES]
        va_ref[p, :, LANES:2 * LANES] = jnp.ones((seq, LANES), BF16)
    lane_q = lax.broadcasted_iota(jnp.int32, (qb, LANES), 1)
    bias_prev = [jnp.concatenate([gb_ref[p, :, 0:qb]] * 2, axis=0)
                 for p in range(DIFF_HEADS_PER_STEP)]
    bias_diag = [jnp.concatenate([gb_ref[p, :, qb:2 * qb]] * 2, axis=0)
                 for p in range(DIFF_HEADS_PER_STEP)]
    gain = g_ref[...] * (1.0 - lambda_init)
    zero = jnp.zeros((), BF16)

    def scores(p, qi):
        r0 = qi * qb
        qp = q_ref[0, r0:r0 + qb, p * LANES:(p + 1) * LANES]
        qz = jnp.concatenate([jnp.where(lane_q < HEAD_DIM, qp, zero),
                              jnp.where(lane_q < HEAD_DIM, zero, qp)], axis=0)
        tiles = []
        for t in range(qi + 1):
            s = _dot_nt(qz, k_ref[0, t * qb:(t + 1) * qb, p * LANES:(p + 1) * LANES])
            tiles.append(s + bias_diag[p] if t == qi else s + bias_prev[p] if t == qi - 1 else s)
            yield
        return p, qi, tiles

    def finish(p, qi, tiles):
        r0 = qi * qb
        m = jnp.max(functools.reduce(jnp.maximum, tiles), axis=-1, keepdims=True)
        acc = None
        for t, s in enumerate(tiles):
            part = _dot(jnp.exp2(s - m).astype(BF16), va_ref[p, t * qb:(t + 1) * qb, :])
            acc = part if acc is None else acc + part
            yield
        a = acc[:, 0:LANES] / acc[:, LANES:2 * LANES]
        o = a[0:qb] - lam * a[qb:2 * qb]
        o_ref[0, r0:r0 + qb, p * LANES:(p + 1) * LANES] = _rms(o, gain).astype(BF16)

    _interleaved([(p, qi) for p in range(DIFF_HEADS_PER_STEP) for qi in _hill_order(seq // qb)],
                 scores, finish, depth=DIFF_LAG)


def _diff_attention(lam4, subln, dq, dk, dv, gbias, batch, seq, lambda_init):
    hps = DIFF_HEADS_PER_STEP
    blk = pl.BlockSpec((1, seq, hps * LANES), lambda h, b: (b, 0, h))
    return pl.pallas_call(
        functools.partial(_diff_kernel, seq=seq, lambda_init=lambda_init),
        grid=(DIFF_HEADS // hps, batch),
        in_specs=[
            pl.BlockSpec(lam4.shape, lambda h, b: (0, 0)),
            pl.BlockSpec(subln.shape, lambda h, b: (0, 0)),
            blk, blk, blk,
            pl.BlockSpec((hps, ATT_Q_BLOCK, 2 * ATT_Q_BLOCK), lambda h, b: (h, 0, 0)),
        ],
        out_specs=blk,
        out_shape=jax.ShapeDtypeStruct((batch, seq, DIFF_W), BF16),
        scratch_shapes=[pltpu.VMEM((hps, seq, 2 * LANES), BF16)],
        compiler_params=pltpu.CompilerParams(dimension_semantics=("parallel", "parallel"),
                                             vmem_limit_bytes=VMEM_LIMIT),
        name="diff_attention",
    )(lam4, subln, dq, dk, dv, gbias)


def _proj1_kernel(x_ref, gain_ref, w_ref, bd_ref, hg_ref, q_ref, kk_ref, vt_ref):
    h = _rms(x_ref[...], gain_ref[...]).astype(BF16)
    bd = bd_ref[...]
    qw = SWA_HEADS * HEAD_DIM
    kw = SWA_KV_HEADS * HEAD_DIM
    low = lax.broadcasted_iota(jnp.int32, (h.shape[0], LANES), 1) < HEAD_DIM

    def project(c0):
        return c0, _dot(h, w_ref[:, c0:c0 + 2 * MXU_TILE])

    def epilogue(c0, proj):
        if c0 < qw:
            for i in range(2):
                cols = slice(i * MXU_TILE, (i + 1) * MXU_TILE)
                q_ref[:, c0 + i * MXU_TILE:c0 + (i + 1) * MXU_TILE] = _head_norm(
                    proj[:, cols], bd, hg_ref[0:1, :]).astype(BF16)
            return
        kn = _head_norm(proj[:, 0:kw], bd, hg_ref[1:2, :])
        for p in range(kw // LANES):
            pair = kn[:, p * LANES:(p + 1) * LANES]
            swapped = pltpu.roll(pair, HEAD_DIM, axis=1)
            kk_ref[:, (2 * p) * LANES:(2 * p + 1) * LANES] = jnp.where(low, pair, swapped).astype(BF16)
            kk_ref[:, (2 * p + 1) * LANES:(2 * p + 2) * LANES] = jnp.where(low, swapped, pair).astype(BF16)
        vt_ref[0] = proj[:, kw:2 * kw].T.astype(BF16)

    _lagged([(qw,)] + [(c0,) for c0 in range(0, qw, 2 * MXU_TILE)], project, epilogue,
            depth=PROJ_LAG)


def _proj1(x2d, gain, w, bd, hg, seq):
    n = x2d.shape[0]
    tm = ROW_TILE
    tps = seq // tm
    qw = SWA_HEADS * HEAD_DIM
    kw = SWA_KV_HEADS * HEAD_DIM
    full = lambda shape: pl.BlockSpec(shape, lambda t: (0,) * len(shape))
    return pl.pallas_call(
        _proj1_kernel,
        grid=(n // tm,),
        in_specs=[
            pl.BlockSpec((tm, D_MODEL), lambda t: (t, 0)),
            full((1, D_MODEL)),
            full(w.shape),
            full((MXU_TILE, MXU_TILE)),
            full(hg.shape),
        ],
        out_specs=[pl.BlockSpec((tm, qw), lambda t: (t, 0)),
                   pl.BlockSpec((tm, 2 * kw), lambda t: (t, 0)),
                   pl.BlockSpec((1, kw, tm), lambda t: (t // tps, 0, t % tps))],
        out_shape=[jax.ShapeDtypeStruct((n, qw), BF16),
                   jax.ShapeDtypeStruct((n, 2 * kw), BF16),
                   jax.ShapeDtypeStruct((n // seq, kw, seq), BF16)],
        compiler_params=pltpu.CompilerParams(dimension_semantics=("parallel",),
                                             vmem_limit_bytes=VMEM_LIMIT),
        name="proj_odd",
    )(x2d, gain, w, bd, hg)


def _swa_kernel(sink_ref, q_ref, kk_ref, vt_ref, sb_ref, o_ref, va_ref, *, seq):
    jj = pl.program_id(0)
    w = WINDOW
    gw = SWA_GROUP * HEAD_DIM
    sub = lax.broadcasted_iota(jnp.int32, (LANES - HEAD_DIM, seq), 0)
    sinks = []
    for p in range(SWA_KV_PER_STEP):
        va_ref[p, 0:HEAD_DIM, :] = vt_ref[0, p * HEAD_DIM:(p + 1) * HEAD_DIM, :]
        va_ref[p, HEAD_DIM:LANES, :] = jnp.where(sub == 0, 1.0, 0.0).astype(BF16)
        head0 = (jj * SWA_KV_PER_STEP + p) * SWA_GROUP
        sinks.append(jnp.concatenate(
            [jnp.full((1, w), sink_ref[head0 + g] * LOG2E, F32) for g in range(SWA_GROUP)],
            axis=1))
    low = lax.broadcasted_iota(jnp.int32, (w, LANES), 1) < HEAD_DIM
    zero = jnp.zeros((), BF16)

    def scores(p, n):
        r0 = n * w
        qb = q_ref[0, r0:r0 + w, p * gw:(p + 1) * gw]
        qa, qc = qb[:, 0:LANES], qb[:, LANES:2 * LANES]
        q4 = jnp.concatenate([jnp.where(low, qa, zero), jnp.where(low, zero, qa),
                              jnp.where(low, qc, zero), jnp.where(low, zero, qc)], axis=0)
        lanes = slice(p * LANES, (p + 1) * LANES)
        if n == 0:
            return p, n, _dot_nt(kk_ref[0, 0:w, lanes], q4) + sb_ref[p, w:2 * w, :]
        return p, n, _dot_nt(kk_ref[0, r0 - w:r0 + w, lanes], q4) + sb_ref[p]

    def finish(p, n, st):
        r0 = n * w
        vals = va_ref[p, :, 0:w] if n == 0 else va_ref[p, :, r0 - w:r0 + w]
        m = jnp.maximum(jnp.max(st, axis=0, keepdims=True), sinks[p])
        acc = _dot(vals, jnp.exp2(st - m).astype(BF16))
        denom = acc[HEAD_DIM:HEAD_DIM + 1, :] + jnp.exp2(sinks[p] - m)
        ot = acc[0:HEAD_DIM, :] / denom
        o_ref[0, r0:r0 + w, p * gw:(p + 1) * gw] = jnp.concatenate(
            [jnp.concatenate([ot[:, 0:w], ot[:, w:2 * w]], axis=0).T,
             jnp.concatenate([ot[:, 2 * w:3 * w], ot[:, 3 * w:4 * w]], axis=0).T],
            axis=1).astype(BF16)

    _lagged([(p, n) for p in range(SWA_KV_PER_STEP) for n in range(seq // w)], scores, finish,
            depth=SWA_LAG)


def _swa_attention(sinks, q, kk, vt, swbias, batch, seq):
    kps = SWA_KV_PER_STEP
    gw = kps * SWA_GROUP * HEAD_DIM
    return pl.pallas_call(
        functools.partial(_swa_kernel, seq=seq),
        grid=(SWA_KV_HEADS // kps, batch),
        in_specs=[
            pl.BlockSpec(memory_space=pltpu.SMEM),
            pl.BlockSpec((1, seq, gw), lambda h, b: (b, 0, h)),
            pl.BlockSpec((1, seq, kps * LANES), lambda h, b: (b, 0, h)),
            pl.BlockSpec((1, kps * HEAD_DIM, seq), lambda h, b: (b, h, 0)),
            pl.BlockSpec((kps, 2 * WINDOW, SWA_GROUP * WINDOW), lambda h, b: (h, 0, 0)),
        ],
        out_specs=pl.BlockSpec((1, seq, gw), lambda h, b: (b, 0, h)),
        out_shape=jax.ShapeDtypeStruct((batch, seq, SWA_HEADS * HEAD_DIM), BF16),
        scratch_shapes=[pltpu.VMEM((kps, LANES, seq), BF16)],
        compiler_params=pltpu.CompilerParams(dimension_semantics=("parallel", "parallel"),
                                             vmem_limit_bytes=VMEM_LIMIT),
        name="swa_attention",
    )(sinks, q, kk, vt, swbias)


def _outmlp_kernel(*refs, n_mix):
    x_ref = refs[0]
    a_refs = refs[1:1 + n_mix]
    wo_ref, gain_ref, wg_ref, wu_ref, wd_ref, o_ref = refs[1 + n_mix:]
    mix = jnp.concatenate([a_ref[...] for a_ref in a_refs], axis=1)
    x1 = x_ref[...] + _dot(mix, wo_ref[...])
    h = _rms(x1, gain_ref[...]).astype(BF16)
    out = x1
    for c0 in range(0, FFN_HIDDEN, HIDDEN_CHUNK):
        c1 = min(c0 + HIDDEN_CHUNK, FFN_HIDDEN)
        g = _dot(h, wg_ref[:, c0:c1])
        u = _dot(h, wu_ref[:, c0:c1])
        y = (g * (1.0 / (1.0 + jnp.exp(-g))) * u).astype(BF16)
        out = out + _dot(y, wd_ref[c0:c1, :])
    o_ref[...] = out


def _outmlp(x2d, mixes, wo, gain, wg, wu, wd, layer):
    n = x2d.shape[0]
    tm = ROW_TILE
    full = lambda shape: pl.BlockSpec(shape, lambda t: (0,) * len(shape),
                                      pipeline_mode=pl.Buffered(1))
    of_layer = lambda w: pl.BlockSpec((None,) + w.shape[1:], lambda t: (layer, 0, 0),
                                      pipeline_mode=pl.Buffered(1))
    in_specs = [pl.BlockSpec((tm, D_MODEL), lambda t: (t, 0))]
    in_specs += [pl.BlockSpec((tm, a.shape[1]), lambda t: (t, 0)) for a in mixes]
    in_specs += [full(wo.shape), full(gain.shape), of_layer(wg), of_layer(wu), of_layer(wd)]
    return pl.pallas_call(
        functools.partial(_outmlp_kernel, n_mix=len(mixes)),
        grid=(n // tm,),
        in_specs=in_specs,
        out_specs=pl.BlockSpec((tm, D_MODEL), lambda t: (t, 0)),
        out_shape=jax.ShapeDtypeStruct((n, D_MODEL), F32),
        compiler_params=pltpu.CompilerParams(dimension_semantics=("parallel",),
                                             vmem_limit_bytes=VMEM_LIMIT),
        name="outproj_swiglu",
    )(x2d, *mixes, wo, gain, wg, wu, wd)


def _block_diag_mean():
    i = np.arange(MXU_TILE)
    same = (i[:, None] // HEAD_DIM) == (i[None, :] // HEAD_DIM)
    return jnp.asarray(same / HEAD_DIM, BF16)


def kernel(x, rel_bias_table, ev_attn_norm, ev_w_in, ev_b_forget, ev_fox_q_norm, ev_fox_k_norm,
           ev_diff_q_norm, ev_diff_k_norm, ev_lambda_q1, ev_lambda_k1, ev_lambda_q2, ev_lambda_k2,
           ev_diff_subln, ev_w_out, od_attn_norm, od_w_qkv, od_q_norm, od_k_norm, od_sinks, od_w_out,
           ffn_norm, w_gate, w_up, w_down):
    batch, seq, d = x.shape
    n = batch * seq
    x2d = x.reshape(n, d)
    bd = _block_diag_mean()
    swbias, gbias = _bias_tiles(rel_bias_table.astype(F32))
    swbias = swbias.reshape(SWA_KV_HEADS, SWA_GROUP, 2 * WINDOW, WINDOW).transpose(0, 2, 1, 3)
    swbias = swbias.reshape(SWA_KV_HEADS, 2 * WINDOW, SWA_GROUP * WINDOW)

    lambda_init = 0.8 - 0.6 * math.exp(-0.3 * 0)
    w_in = ev_w_in[0]
    ff0 = 3 * FOX_W
    d0 = ff0 + FOX_HEADS
    w0 = jnp.concatenate([w_in[:, 0:ff0], w_in[:, d0:d0 + 3 * DIFF_W]], axis=1).astype(BF16)
    wf = jnp.concatenate(
        [w_in[:, ff0:d0].T, jnp.zeros((BF16_SUBLANES - FOX_HEADS, d), w_in.dtype)],
        axis=0).astype(BF16)
    tile8 = lambda g, scale=1.0: jnp.tile(g.astype(F32) * scale, FOX_W // HEAD_DIM)
    hg0 = jnp.stack([tile8(ev_fox_q_norm[0], Q_GAIN_SCALE), tile8(ev_fox_k_norm[0]),
                     tile8(ev_diff_q_norm[0], Q_GAIN_SCALE), tile8(ev_diff_k_norm[0])])
    bf = ev_b_forget[0].astype(F32).reshape(FOX_HEADS, 1)
    depth, hidden = w_gate.shape[0], w_gate.shape[2]
    later = [w_gate.reshape(depth * d, hidden), w_up.reshape(depth * d, hidden),
             w_down.reshape(depth * hidden, d), ev_w_out[0], od_w_qkv[0], od_w_out[0]]
    fq, fk, fvt, dq, dk, dv, caug, wg, wu, wd, wo0, wqkv, wo1 = _proj0(
        x2d, ev_attn_norm[0].astype(F32).reshape(1, d), w0, wf, bd, hg0, bf, seq,
        [a.astype(F32) for a in later])
    wg, wu = wg.reshape(depth, d, hidden), wu.reshape(depth, d, hidden)
    wd = wd.reshape(depth, hidden, d)
    r3 = lambda a: a.reshape(batch, seq, a.shape[-1])
    fox = _fox_attention(r3(fq), r3(fk), fvt, r3(caug), batch, seq)
    lam4 = jnp.stack([ev_lambda_q1[0], ev_lambda_k1[0], ev_lambda_q2[0], ev_lambda_k2[0]]).astype(F32)
    dout = _diff_attention(lam4, ev_diff_subln[0].astype(F32).reshape(1, 2 * HEAD_DIM),
                           r3(dq), r3(dk), r3(dv), gbias, batch, seq, lambda_init)
    x2d = _outmlp(x2d, [fox.reshape(n, FOX_W), dout.reshape(n, DIFF_W)], wo0,
                  ffn_norm[0].astype(F32).reshape(1, d), wg, wu, wd, 0)

    tile4 = lambda g, scale=1.0: jnp.tile(g.astype(F32) * scale, MXU_TILE // HEAD_DIM)
    hg1 = jnp.stack([tile4(od_q_norm[0], Q_GAIN_SCALE), tile4(od_k_norm[0])])
    q, kk, vt = _proj1(x2d, od_attn_norm[0].astype(F32).reshape(1, d), wqkv,
                       bd, hg1, seq)
    swa = _swa_attention(od_sinks[0].astype(F32), r3(q), r3(kk), vt, swbias, batch, seq)
    x2d = _outmlp(x2d, [swa.reshape(n, SWA_HEADS * HEAD_DIM)], wo1,
                  ffn_norm[1].astype(F32).reshape(1, d), wg, wu, wd, 1)
    return x2d.reshape(batch, seq, d)
```

```python
import functools
import math

import numpy as np
import jax
import jax.numpy as jnp
from jax import lax
from jax.experimental import pallas as pl
from jax.experimental.pallas import tpu as pltpu

F32 = jnp.float32
BF16 = jnp.bfloat16

D_MODEL = 1024
HEAD_DIM = 64
FOX_HEADS = 8
DIFF_HEADS = 4
SWA_HEADS = 16
SWA_KV_HEADS = 4
SWA_GROUP = SWA_HEADS // SWA_KV_HEADS
WINDOW = 128
NUM_BUCKETS = 32
MAX_DISTANCE = 128
FFN_HIDDEN = 2816
RMS_EPS = 1e-6
NEG_INF = -1e30
FOX_W = FOX_HEADS * HEAD_DIM
DIFF_W = DIFF_HEADS * 2 * HEAD_DIM
QK_SCALE = HEAD_DIM ** -0.5
LOG2E = math.log2(math.e)
Q_GAIN_SCALE = QK_SCALE * LOG2E

LANES = 128
BF16_SUBLANES = 16
FOX_VT_ROWS = HEAD_DIM + BF16_SUBLANES
MXU_TILE = 256
ROW_TILE = 1024
ATT_Q_BLOCK = 256
FOX_PAIRS_PER_STEP = 2
DIFF_HEADS_PER_STEP = 2
SWA_KV_PER_STEP = 2
FOX_LAG = 4
DIFF_LAG = 2
PROJ_LAG = 1
SWA_LAG = 4
HIDDEN_CHUNK = 768
V7X_VMEM_BYTES = 64 * 1024 * 1024
VMEM_LIMIT = V7X_VMEM_BYTES - 8 * 1024 * 1024

_NT = (((1,), (1,)), ((), ()))


def _dot(a, b):
    return jnp.dot(a, b, preferred_element_type=F32)


def _dot_nt(a, b):
    return lax.dot_general(a, b, _NT, preferred_element_type=F32)


def _lagged(items, first, second, depth=1):
    pending = []
    for item in items:
        pending.append(first(*item))
        if len(pending) > depth:
            second(*pending.pop(0))
    for p in pending:
        second(*p)


def _interleaved(items, first, second, depth):
    def drive(gens):
        done = {}
        while gens:
            for g in list(gens):
                try:
                    next(g)
                except StopIteration as stop:
                    done[id(g)] = stop.value
                    gens.remove(g)
        return done

    pending = []
    for item in items:
        g1 = first(*item)
        gens = [g1] if len(pending) < depth else [g1, second(*pending.pop(0))]
        pending.append(drive(gens)[id(g1)])
    for p in pending:
        drive([second(*p)])


def _hill_order(nblocks):
    return list(range(0, nblocks, 2)) + list(range(nblocks - 1 - nblocks % 2, 0, -2))


def _rms(xf, gain):
    ms = jnp.mean(xf * xf, axis=-1, keepdims=True)
    return xf * lax.rsqrt(ms + RMS_EPS) * gain


def _head_norm(acc, bd, gain):
    msq = _dot((acc * acc).astype(BF16), bd)
    return acc * lax.rsqrt(msq + RMS_EPS) * gain


def _bucket_ids(delta):
    n = np.maximum(delta, 0)
    max_exact = NUM_BUCKETS // 2
    nf = np.maximum(n, 1).astype(np.float64)
    large = max_exact + (np.log(nf / max_exact) / math.log(MAX_DISTANCE / max_exact)
                         * (NUM_BUCKETS - max_exact)).astype(np.int32)
    large = np.minimum(large, NUM_BUCKETS - 1)
    return np.where(n < max_exact, n, large).astype(np.int32)


def _bias_kernel(table_ref, bsw_ref, bg_ref, sw_ref, g_ref):
    h = pl.program_id(0)
    bsw = bsw_ref[...]
    sw = jnp.zeros(bsw.shape, F32)
    for b in range(NUM_BUCKETS):
        sw = jnp.where(bsw == b, table_ref[b, h] * LOG2E, sw)
    sw_ref[0] = jnp.where(bsw < 0, NEG_INF, sw)

    @pl.when(h < DIFF_HEADS)
    def _():
        bg = bg_ref[...]
        far = table_ref[NUM_BUCKETS - 1, h]
        g = jnp.zeros(bg.shape, F32)
        for b in range(NUM_BUCKETS):
            g = jnp.where(bg == b, (table_ref[b, h] - far) * LOG2E, g)
        g_ref[0] = jnp.where(bg < 0, NEG_INF, g)


def _bias_tiles(table):
    a = np.arange(WINDOW)[None, :]
    b = np.arange(2 * WINDOW)[:, None]
    d_sw = WINDOW + a - b
    bsw = np.where((d_sw >= 0) & (d_sw < WINDOW), _bucket_ids(d_sw), -1).astype(np.int32)
    a = np.arange(ATT_Q_BLOCK)[:, None]
    b = np.arange(2 * ATT_Q_BLOCK)[None, :]
    d_g = ATT_Q_BLOCK + a - b
    bg = np.where(d_g >= 0, _bucket_ids(d_g), -1).astype(np.int32)
    nh = table.shape[1]
    return pl.pallas_call(
        _bias_kernel,
        grid=(nh,),
        in_specs=[
            pl.BlockSpec(memory_space=pltpu.SMEM),
            pl.BlockSpec(bsw.shape, lambda h: (0, 0)),
            pl.BlockSpec(bg.shape, lambda h: (0, 0)),
        ],
        out_specs=[
            pl.BlockSpec((1,) + bsw.shape, lambda h: (h, 0, 0)),
            pl.BlockSpec((1,) + bg.shape, lambda h: (jnp.minimum(h, DIFF_HEADS - 1), 0, 0)),
        ],
        out_shape=[
            jax.ShapeDtypeStruct((nh,) + bsw.shape, F32),
            jax.ShapeDtypeStruct((DIFF_HEADS,) + bg.shape, F32),
        ],
        compiler_params=pltpu.CompilerParams(dimension_semantics=("arbitrary",)),
        name="rel_bias_tiles",
    )(table, jnp.asarray(bsw), jnp.asarray(bg))


def _proj0_kernel(*refs, tiles_per_seq, n_cast):
    x_ref, gain_ref, w_ref, wf_ref, bd_ref, hg_ref, bf_ref = refs[0:7]
    cast_src = refs[7:7 + n_cast]
    fq_ref, fk_ref, fv_ref, dq_ref, dk_ref, dv_ref, caug_ref = refs[7 + n_cast:14 + n_cast]
    cast_dst = refs[14 + n_cast:14 + 2 * n_cast]
    carry_ref = refs[14 + 2 * n_cast]
    t = pl.program_id(0)

    @pl.when(t == 0)
    def _():
        carry_ref[...] = jnp.zeros_like(carry_ref)

    h = _rms(x_ref[...], gain_ref[...]).astype(BF16)
    bd = bd_ref[...]
    rows = h.shape[0]

    outs = (fq_ref, fk_ref, fv_ref, dq_ref, dk_ref, dv_ref)
    gains = (0, 1, None, 2, 3, None)
    feature_major = (False, False, True, False, False, False)

    def project(g):
        return g, _dot(h, w_ref[:, g * FOX_W:(g + 1) * FOX_W])

    def epilogue(g, proj):
        if feature_major[g]:
            outs[g][0] = proj.T.astype(BF16)
            return
        if gains[g] is None:
            outs[g][...] = proj.astype(BF16)
            return
        for i in range(FOX_W // MXU_TILE):
            cols = slice(i * MXU_TILE, (i + 1) * MXU_TILE)
            outs[g][:, cols] = _head_norm(
                proj[:, cols], bd, hg_ref[gains[g]:gains[g] + 1, cols]).astype(BF16)

    pending = [project(g) for g in range(PROJ_LAG)]

    z8 = _dot_nt(wf_ref[...], h)[0:FOX_HEADS, :] + bf_ref[...]
    y = -(jnp.maximum(-z8, 0.0) + jnp.log1p(jnp.exp(-jnp.abs(z8))))
    lane = lax.broadcasted_iota(jnp.int32, y.shape, 1)
    shift = 1
    while shift < rows:
        y = y + jnp.where(lane >= shift, pltpu.roll(y, shift, axis=1), 0.0)
        shift *= 2

    carry = jnp.where(t % tiles_per_seq == 0, 0.0, carry_ref[:, 0:1])
    c = y + carry
    carry_ref[...] = jnp.broadcast_to(c[:, rows - 1:rows], carry_ref.shape)
    cn = c * (-LOG2E)
    hi = cn.astype(BF16).astype(F32)
    r1 = cn - hi
    mid = r1.astype(BF16).astype(F32)
    lo = (r1 - mid).astype(BF16).astype(F32)
    half = jnp.concatenate([hi, mid, lo, jnp.zeros((HEAD_DIM - 3 * FOX_HEADS, rows), F32)], axis=0)
    caug_ref[...] = jnp.concatenate([half, half], axis=0).T.astype(BF16)

    for g in range(PROJ_LAG, len(outs)):
        pending.append(project(g))
        epilogue(*pending.pop(0))
    for p in pending:
        epilogue(*p)

    for src, dst in zip(cast_src, cast_dst):
        dst[...] = src[...].astype(BF16)


def _proj0(x2d, gain, w, wf, bd, hg, bf, seq, later_weights):
    n = x2d.shape[0]
    tm = ROW_TILE
    tps = seq // tm
    steps = n // tm
    slab = lambda a: pl.BlockSpec((a.shape[0] // steps, a.shape[1]), lambda t: (t, 0))
    for a in later_weights:
        assert a.shape[0] % (steps * BF16_SUBLANES) == 0, a.shape
    row_spec = pl.BlockSpec((tm, FOX_W), lambda t: (t, 0))
    col_spec = pl.BlockSpec((1, FOX_W, tm), lambda t: (t // tps, 0, t % tps))
    rows_shape = jax.ShapeDtypeStruct((n, FOX_W), BF16)
    cols_shape = jax.ShapeDtypeStruct((n // seq, FOX_W, seq), BF16)
    full = lambda shape: pl.BlockSpec(shape, lambda t: (0,) * len(shape))
    outs = pl.pallas_call(
        functools.partial(_proj0_kernel, tiles_per_seq=tps, n_cast=len(later_weights)),
        grid=(steps,),
        in_specs=[
            pl.BlockSpec((tm, D_MODEL), lambda t: (t, 0)),
            full((1, D_MODEL)),
            full(w.shape),
            full(wf.shape),
            full((MXU_TILE, MXU_TILE)),
            full(hg.shape),
            full(bf.shape),
        ] + [slab(a) for a in later_weights],
        out_specs=[row_spec, row_spec, col_spec, row_spec, row_spec, row_spec,
                   pl.BlockSpec((tm, LANES), lambda t: (t, 0))] + [slab(a) for a in later_weights],
        out_shape=[rows_shape, rows_shape, cols_shape, rows_shape, rows_shape, rows_shape,
                   jax.ShapeDtypeStruct((n, LANES), BF16)]
        + [jax.ShapeDtypeStruct(a.shape, BF16) for a in later_weights],
        scratch_shapes=[pltpu.VMEM((FOX_HEADS, LANES), F32)],
        compiler_params=pltpu.CompilerParams(dimension_semantics=("arbitrary",),
                                             vmem_limit_bytes=VMEM_LIMIT),
        name="proj_even",
    )(x2d, gain, w, wf, bd, hg, bf, *later_weights)
    return outs


def _fox_kernel(q_ref, k_ref, vt_ref, caug_ref, o_ref, ka_ref, va_ref, *, seq):
    jj = pl.program_id(1)
    qb = ATT_Q_BLOCK
    low_s = lax.broadcasted_iota(jnp.int32, (seq, LANES), 1) < HEAD_DIM
    ca = caug_ref[0]
    sub = lax.broadcasted_iota(jnp.int32, (FOX_VT_ROWS - HEAD_DIM, seq), 0)
    ones_row = jnp.where(sub == 0, 1.0, 0.0).astype(BF16)
    for p in range(FOX_PAIRS_PER_STEP):
        kp = k_ref[0, :, p * LANES:(p + 1) * LANES]
        ka_ref[2 * p] = jnp.where(low_s, kp, ca)
        ka_ref[2 * p + 1] = jnp.where(low_s, ca, kp)
        for hh in range(2):
            r = p * LANES + hh * HEAD_DIM
            va_ref[2 * p + hh, 0:HEAD_DIM, :] = vt_ref[0, r:r + HEAD_DIM, :]
            va_ref[2 * p + hh, HEAD_DIM:FOX_VT_ROWS, :] = ones_row
    lane_q = lax.broadcasted_iota(jnp.int32, (qb, LANES), 1)
    key_i = lax.broadcasted_iota(jnp.int32, (qb, qb), 0)
    qry_i = lax.broadcasted_iota(jnp.int32, (qb, qb), 1)
    causal = key_i <= qry_i
    owns = (lane_q < HEAD_DIM, lane_q >= HEAD_DIM)

    def scores(p, qi, hh):
        r0 = qi * qb
        head = 2 * (FOX_PAIRS_PER_STEP * jj + p) + hh
        rel = lane_q - (HEAD_DIM if hh == 0 else 0) - head
        pick = (rel == 0) | (rel == FOX_HEADS) | (rel == 2 * FOX_HEADS)
        qz = jnp.where(owns[hh], q_ref[0, r0:r0 + qb, p * LANES:(p + 1) * LANES],
                       jnp.where(pick, 1.0, 0.0).astype(BF16))
        s_all = _dot_nt(ka_ref[2 * p + hh, 0:(qi + 1) * qb, :], qz)
        yield
        tiles = [s_all[t * qb:(t + 1) * qb] for t in range(qi)]
        tiles.append(jnp.where(causal, s_all[qi * qb:(qi + 1) * qb], NEG_INF))
        return p, qi, hh, tiles

    halves = []

    def finish(p, qi, hh, tiles):
        r0 = qi * qb
        m = jnp.max(functools.reduce(jnp.maximum, tiles), axis=0, keepdims=True)
        acc = None
        for t, s in enumerate(tiles):
            part = _dot(va_ref[2 * p + hh, :, t * qb:(t + 1) * qb], jnp.exp2(s - m).astype(BF16))
            acc = part if acc is None else acc + part
            yield
        halves.append(acc[0:HEAD_DIM] / acc[HEAD_DIM:HEAD_DIM + 1])
        if hh == 1:
            o_ref[0, r0:r0 + qb, p * LANES:(p + 1) * LANES] = jnp.concatenate(
                [halves[-2], halves[-1]], axis=0).T.astype(BF16)

    _interleaved([(p, qi, hh) for p in range(FOX_PAIRS_PER_STEP)
                  for qi in _hill_order(seq // qb) for hh in range(2)], scores, finish,
                 depth=FOX_LAG)


def _fox_attention(fq, fk, fvt, caug, batch, seq):
    width = FOX_PAIRS_PER_STEP * LANES
    blk = pl.BlockSpec((1, seq, width), lambda b, j: (b, 0, j))
    nheads = 2 * FOX_PAIRS_PER_STEP
    return pl.pallas_call(
        functools.partial(_fox_kernel, seq=seq),
        grid=(batch, FOX_W // width),
        in_specs=[blk, blk, pl.BlockSpec((1, width, seq), lambda b, j: (b, j, 0)),
                  pl.BlockSpec((1, seq, LANES), lambda b, j: (b, 0, 0))],
        out_specs=blk,
        out_shape=jax.ShapeDtypeStruct((batch, seq, FOX_W), BF16),
        scratch_shapes=[pltpu.VMEM((nheads, seq, LANES), BF16),
                        pltpu.VMEM((nheads, FOX_VT_ROWS, seq), BF16)],
        compiler_params=pltpu.CompilerParams(dimension_semantics=("parallel", "parallel"),
                                             vmem_limit_bytes=VMEM_LIMIT),
        name="fox_attention",
    )(fq, fk, fvt, caug)


def _diff_kernel(lam_ref, g_ref, q_ref, k_ref, v_ref, gb_ref, o_ref, va_ref, *, seq, lambda_init):
    qb = ATT_Q_BLOCK
    lv = lam_ref[...]
    lam = (jnp.exp(jnp.sum(lv[0:1] * lv[1:2], axis=-1, keepdims=True))
           - jnp.exp(jnp.sum(lv[2:3] * lv[3:4], axis=-1, keepdims=True)) + lambda_init)
    for p in range(DIFF_HEADS_PER_STEP):
        va_ref[p, :, 0:LANES] = v_ref[0, :, p * LANES:(p + 1) * LANES]
        va_ref[p, :, LANES:2 * LANES] = jnp.ones((seq, LANES), BF16)
    lane_q = lax.broadcasted_iota(jnp.int32, (qb, LANES), 1)
    bias_prev = [jnp.concatenate([gb_ref[p, :, 0:qb]] * 2, axis=0)
                 for p in range(DIFF_HEADS_PER_STEP)]
    bias_diag = [jnp.concatenate([gb_ref[p, :, qb:2 * qb]] * 2, axis=0)
                 for p in range(DIFF_HEADS_PER_STEP)]
    gain = g_ref[...] * (1.0 - lambda_init)
    zero = jnp.zeros((), BF16)

    def scores(p, qi):
        r0 = qi * qb
        qp = q_ref[0, r0:r0 + qb, p * LANES:(p + 1) * LANES]
        qz = jnp.concatenate([jnp.where(lane_q < HEAD_DIM, qp, zero),
                              jnp.where(lane_q < HEAD_DIM, zero, qp)], axis=0)
        tiles = []
        for t in range(qi + 1):
            s = _dot_nt(qz, k_ref[0, t * qb:(t + 1) * qb, p * LANES:(p + 1) * LANES])
            tiles.append(s + bias_diag[p] if t == qi else s + bias_prev[p] if t == qi - 1 else s)
            yield
        return p, qi, tiles

    def finish(p, qi, tiles):
        r0 = qi * qb
        m = jnp.max(functools.reduce(jnp.maximum, tiles), axis=-1, keepdims=True)
        acc = None
        for t, s in enumerate(tiles):
            part = _dot(jnp.exp2(s - m).astype(BF16), va_ref[p, t * qb:(t + 1) * qb, :])
            acc = part if acc is None else acc + part
            yield
        a = acc[:, 0:LANES] / acc[:, LANES:2 * LANES]
        o = a[0:qb] - lam * a[qb:2 * qb]
        o_ref[0, r0:r0 + qb, p * LANES:(p + 1) * LANES] = _rms(o, gain).astype(BF16)

    _interleaved([(p, qi) for p in range(DIFF_HEADS_PER_STEP) for qi in _hill_order(seq // qb)],
                 scores, finish, depth=DIFF_LAG)


def _diff_attention(lam4, subln, dq, dk, dv, gbias, batch, seq, lambda_init):
    hps = DIFF_HEADS_PER_STEP
    blk = pl.BlockSpec((1, seq, hps * LANES), lambda h, b: (b, 0, h))
    return pl.pallas_call(
        functools.partial(_diff_kernel, seq=seq, lambda_init=lambda_init),
        grid=(DIFF_HEADS // hps, batch),
        in_specs=[
            pl.BlockSpec(lam4.shape, lambda h, b: (0, 0)),
            pl.BlockSpec(subln.shape, lambda h, b: (0, 0)),
            blk, blk, blk,
            pl.BlockSpec((hps, ATT_Q_BLOCK, 2 * ATT_Q_BLOCK), lambda h, b: (h, 0, 0)),
        ],
        out_specs=blk,
        out_shape=jax.ShapeDtypeStruct((batch, seq, DIFF_W), BF16),
        scratch_shapes=[pltpu.VMEM((hps, seq, 2 * LANES), BF16)],
        compiler_params=pltpu.CompilerParams(dimension_semantics=("parallel", "parallel"),
                                             vmem_limit_bytes=VMEM_LIMIT),
        name="diff_attention",
    )(lam4, subln, dq, dk, dv, gbias)


def _proj1_kernel(x_ref, gain_ref, w_ref, bd_ref, hg_ref, q_ref, kk_ref, vt_ref):
    h = _rms(x_ref[...], gain_ref[...]).astype(BF16)
    bd = bd_ref[...]
    qw = SWA_HEADS * HEAD_DIM
    kw = SWA_KV_HEADS * HEAD_DIM
    low = lax.broadcasted_iota(jnp.int32, (h.shape[0], LANES), 1) < HEAD_DIM

    def project(c0):
        return c0, _dot(h, w_ref[:, c0:c0 + 2 * MXU_TILE])

    def epilogue(c0, proj):
        if c0 < qw:
            for i in range(2):
                cols = slice(i * MXU_TILE, (i + 1) * MXU_TILE)
                q_ref[:, c0 + i * MXU_TILE:c0 + (i + 1) * MXU_TILE] = _head_norm(
                    proj[:, cols], bd, hg_ref[0:1, :]).astype(BF16)
            return
        kn = _head_norm(proj[:, 0:kw], bd, hg_ref[1:2, :])
        for p in range(kw // LANES):
            pair = kn[:, p * LANES:(p + 1) * LANES]
            swapped = pltpu.roll(pair, HEAD_DIM, axis=1)
            kk_ref[:, (2 * p) * LANES:(2 * p + 1) * LANES] = jnp.where(low, pair, swapped).astype(BF16)
            kk_ref[:, (2 * p + 1) * LANES:(2 * p + 2) * LANES] = jnp.where(low, swapped, pair).astype(BF16)
        vt_ref[0] = proj[:, kw:2 * kw].T.astype(BF16)

    _lagged([(qw,)] + [(c0,) for c0 in range(0, qw, 2 * MXU_TILE)], project, epilogue,
            depth=PROJ_LAG)


def _proj1(x2d, gain, w, bd, hg, seq):
    n = x2d.shape[0]
    tm = ROW_TILE
    tps = seq // tm
    qw = SWA_HEADS * HEAD_DIM
    kw = SWA_KV_HEADS * HEAD_DIM
    full = lambda shape: pl.BlockSpec(shape, lambda t: (0,) * len(shape))
    return pl.pallas_call(
        _proj1_kernel,
        grid=(n // tm,),
        in_specs=[
            pl.BlockSpec((tm, D_MODEL), lambda t: (t, 0)),
            full((1, D_MODEL)),
            full(w.shape),
            full((MXU_TILE, MXU_TILE)),
            full(hg.shape),
        ],
        out_specs=[pl.BlockSpec((tm, qw), lambda t: (t, 0)),
                   pl.BlockSpec((tm, 2 * kw), lambda t: (t, 0)),
                   pl.BlockSpec((1, kw, tm), lambda t: (t // tps, 0, t % tps))],
        out_shape=[jax.ShapeDtypeStruct((n, qw), BF16),
                   jax.ShapeDtypeStruct((n, 2 * kw), BF16),
                   jax.ShapeDtypeStruct((n // seq, kw, seq), BF16)],
        compiler_params=pltpu.CompilerParams(dimension_semantics=("parallel",),
                                             vmem_limit_bytes=VMEM_LIMIT),
        name="proj_odd",
    )(x2d, gain, w, bd, hg)


def _swa_kernel(sink_ref, q_ref, kk_ref, vt_ref, sb_ref, o_ref, va_ref, *, seq):
    jj = pl.program_id(0)
    w = WINDOW
    gw = SWA_GROUP * HEAD_DIM
    sub = lax.broadcasted_iota(jnp.int32, (LANES - HEAD_DIM, seq), 0)
    sinks = []
    for p in range(SWA_KV_PER_STEP):
        va_ref[p, 0:HEAD_DIM, :] = vt_ref[0, p * HEAD_DIM:(p + 1) * HEAD_DIM, :]
        va_ref[p, HEAD_DIM:LANES, :] = jnp.where(sub == 0, 1.0, 0.0).astype(BF16)
        head0 = (jj * SWA_KV_PER_STEP + p) * SWA_GROUP
        sinks.append(jnp.concatenate(
            [jnp.full((1, w), sink_ref[head0 + g] * LOG2E, F32) for g in range(SWA_GROUP)],
            axis=1))
    low = lax.broadcasted_iota(jnp.int32, (w, LANES), 1) < HEAD_DIM
    zero = jnp.zeros((), BF16)

    def scores(p, n):
        r0 = n * w
        qb = q_ref[0, r0:r0 + w, p * gw:(p + 1) * gw]
        qa, qc = qb[:, 0:LANES], qb[:, LANES:2 * LANES]
        q4 = jnp.concatenate([jnp.where(low, qa, zero), jnp.where(low, zero, qa),
                              jnp.where(low, qc, zero), jnp.where(low, zero, qc)], axis=0)
        lanes = slice(p * LANES, (p + 1) * LANES)
        if n == 0:
            return p, n, _dot_nt(kk_ref[0, 0:w, lanes], q4) + sb_ref[p, w:2 * w, :]
        return p, n, _dot_nt(kk_ref[0, r0 - w:r0 + w, lanes], q4) + sb_ref[p]

    def finish(p, n, st):
        r0 = n * w
        vals = va_ref[p, :, 0:w] if n == 0 else va_ref[p, :, r0 - w:r0 + w]
        m = jnp.maximum(jnp.max(st, axis=0, keepdims=True), sinks[p])
        acc = _dot(vals, jnp.exp2(st - m).astype(BF16))
        denom = acc[HEAD_DIM:HEAD_DIM + 1, :] + jnp.exp2(sinks[p] - m)
        ot = acc[0:HEAD_DIM, :] / denom
        o_ref[0, r0:r0 + w, p * gw:(p + 1) * gw] = jnp.concatenate(
            [jnp.concatenate([ot[:, 0:w], ot[:, w:2 * w]], axis=0).T,
             jnp.concatenate([ot[:, 2 * w:3 * w], ot[:, 3 * w:4 * w]], axis=0).T],
            axis=1).astype(BF16)

    _lagged([(p, n) for p in range(SWA_KV_PER_STEP) for n in range(seq // w)], scores, finish,
            depth=SWA_LAG)


def _swa_attention(sinks, q, kk, vt, swbias, batch, seq):
    kps = SWA_KV_PER_STEP
    gw = kps * SWA_GROUP * HEAD_DIM
    return pl.pallas_call(
        functools.partial(_swa_kernel, seq=seq),
        grid=(SWA_KV_HEADS // kps, batch),
        in_specs=[
            pl.BlockSpec(memory_space=pltpu.SMEM),
            pl.BlockSpec((1, seq, gw), lambda h, b: (b, 0, h)),
            pl.BlockSpec((1, seq, kps * LANES), lambda h, b: (b, 0, h)),
            pl.BlockSpec((1, kps * HEAD_DIM, seq), lambda h, b: (b, h, 0)),
            pl.BlockSpec((kps, 2 * WINDOW, SWA_GROUP * WINDOW), lambda h, b: (h, 0, 0)),
        ],
        out_specs=pl.BlockSpec((1, seq, gw), lambda h, b: (b, 0, h)),
        out_shape=jax.ShapeDtypeStruct((batch, seq, SWA_HEADS * HEAD_DIM), BF16),
        scratch_shapes=[pltpu.VMEM((kps, LANES, seq), BF16)],
        compiler_params=pltpu.CompilerParams(dimension_semantics=("parallel", "parallel"),
                                             vmem_limit_bytes=VMEM_LIMIT),
        name="swa_attention",
    )(sinks, q, kk, vt, swbias)


def _outmlp_kernel(*refs, n_mix):
    x_ref = refs[0]
    a_refs = refs[1:1 + n_mix]
    wo_ref, gain_ref, wg_ref, wu_ref, wd_ref, o_ref = refs[1 + n_mix:]
    mix = jnp.concatenate([a_ref[...] for a_ref in a_refs], axis=1)
    x1 = x_ref[...] + _dot(mix, wo_ref[...])
    h = _rms(x1, gain_ref[...]).astype(BF16)
    out = x1
    for c0 in range(0, FFN_HIDDEN, HIDDEN_CHUNK):
        c1 = min(c0 + HIDDEN_CHUNK, FFN_HIDDEN)
        g = _dot(h, wg_ref[:, c0:c1])
        u = _dot(h, wu_ref[:, c0:c1])
        y = (g * (1.0 / (1.0 + jnp.exp(-g))) * u).astype(BF16)
        out = out + _dot(y, wd_ref[c0:c1, :])
    o_ref[...] = out


def _outmlp(x2d, mixes, wo, gain, wg, wu, wd, layer):
    n = x2d.shape[0]
    tm = ROW_TILE
    full = lambda shape: pl.BlockSpec(shape, lambda t: (0,) * len(shape),
                                      pipeline_mode=pl.Buffered(1))
    of_layer = lambda w: pl.BlockSpec((None,) + w.shape[1:], lambda t: (layer, 0, 0),
                                      pipeline_mode=pl.Buffered(1))
    in_specs = [pl.BlockSpec((tm, D_MODEL), lambda t: (t, 0))]
    in_specs += [pl.BlockSpec((tm, a.shape[1]), lambda t: (t, 0)) for a in mixes]
    in_specs += [full(wo.shape), full(gain.shape), of_layer(wg), of_layer(wu), of_layer(wd)]
    return pl.pallas_call(
        functools.partial(_outmlp_kernel, n_mix=len(mixes)),
        grid=(n // tm,),
        in_specs=in_specs,
        out_specs=pl.BlockSpec((tm, D_MODEL), lambda t: (t, 0)),
        out_shape=jax.ShapeDtypeStruct((n, D_MODEL), F32),
        compiler_params=pltpu.CompilerParams(dimension_semantics=("parallel",),
                                             vmem_limit_bytes=VMEM_LIMIT),
        name="outproj_swiglu",
    )(x2d, *mixes, wo, gain, wg, wu, wd)


def _block_diag_mean():
    i = np.arange(MXU_TILE)
    same = (i[:, None] // HEAD_DIM) == (i[None, :] // HEAD_DIM)
    return jnp.asarray(same / HEAD_DIM, BF16)


def kernel(x, rel_bias_table, ev_attn_norm, ev_w_in, ev_b_forget, ev_fox_q_norm, ev_fox_k_norm,
           ev_diff_q_norm, ev_diff_k_norm, ev_lambda_q1, ev_lambda_k1, ev_lambda_q2, ev_lambda_k2,
           ev_diff_subln, ev_w_out, od_attn_norm, od_w_qkv, od_q_norm, od_k_norm, od_sinks, od_w_out,
           ffn_norm, w_gate, w_up, w_down):
    batch, seq, d = x.shape
    n = batch * seq
    x2d = x.reshape(n, d)
    bd = _block_diag_mean()
    swbias, gbias = _bias_tiles(rel_bias_table.astype(F32))
    swbias = swbias.reshape(SWA_KV_HEADS, SWA_GROUP, 2 * WINDOW, WINDOW).transpose(0, 2, 1, 3)
    swbias = swbias.reshape(SWA_KV_HEADS, 2 * WINDOW, SWA_GROUP * WINDOW)

    lambda_init = 0.8 - 0.6 * math.exp(-0.3 * 0)
    w_in = ev_w_in[0]
    ff0 = 3 * FOX_W
    d0 = ff0 + FOX_HEADS
    w0 = jnp.concatenate([w_in[:, 0:ff0], w_in[:, d0:d0 + 3 * DIFF_W]], axis=1).astype(BF16)
    wf = jnp.concatenate(
        [w_in[:, ff0:d0].T, jnp.zeros((BF16_SUBLANES - FOX_HEADS, d), w_in.dtype)],
        axis=0).astype(BF16)
    tile8 = lambda g, scale=1.0: jnp.tile(g.astype(F32) * scale, FOX_W // HEAD_DIM)
    hg0 = jnp.stack([tile8(ev_fox_q_norm[0], Q_GAIN_SCALE), tile8(ev_fox_k_norm[0]),
                     tile8(ev_diff_q_norm[0], Q_GAIN_SCALE), tile8(ev_diff_k_norm[0])])
    bf = ev_b_forget[0].astype(F32).reshape(FOX_HEADS, 1)
    depth, hidden = w_gate.shape[0], w_gate.shape[2]
    later = [w_gate.reshape(depth * d, hidden), w_up.reshape(depth * d, hidden),
             w_down.reshape(depth * hidden, d), ev_w_out[0], od_w_qkv[0], od_w_out[0]]
    fq, fk, fvt, dq, dk, dv, caug, wg, wu, wd, wo0, wqkv, wo1 = _proj0(
        x2d, ev_attn_norm[0].astype(F32).reshape(1, d), w0, wf, bd, hg0, bf, seq,
        [a.astype(F32) for a in later])
    wg, wu = wg.reshape(depth, d, hidden), wu.reshape(depth, d, hidden)
    wd = wd.reshape(depth, hidden, d)
    r3 = lambda a: a.reshape(batch, seq, a.shape[-1])
    fox = _fox_attention(r3(fq), r3(fk), fvt, r3(caug), batch, seq)
    lam4 = jnp.stack([ev_lambda_q1[0], ev_lambda_k1[0], ev_lambda_q2[0], ev_lambda_k2[0]]).astype(F32)
    dout = _diff_attention(lam4, ev_diff_subln[0].astype(F32).reshape(1, 2 * HEAD_DIM),
                           r3(dq), r3(dk), r3(dv), gbias, batch, seq, lambda_init)
    x2d = _outmlp(x2d, [fox.reshape(n, FOX_W), dout.reshape(n, DIFF_W)], wo0,
                  ffn_norm[0].astype(F32).reshape(1, d), wg, wu, wd, 0)

    tile4 = lambda g, scale=1.0: jnp.tile(g.astype(F32) * scale, MXU_TILE // HEAD_DIM)
    hg1 = jnp.stack([tile4(od_q_norm[0], Q_GAIN_SCALE), tile4(od_k_norm[0])])
    q, kk, vt = _proj1(x2d, od_attn_norm[0].astype(F32).reshape(1, d), wqkv,
                       bd, hg1, seq)
    swa = _swa_attention(od_sinks[0].astype(F32), r3(q), r3(kk), vt, swbias, batch, seq)
    x2d = _outmlp(x2d, [swa.reshape(n, SWA_HEADS * HEAD_DIM)], wo1,
                  ffn_norm[1].astype(F32).reshape(1, d), wg, wu, wd, 1)
    return x2d.reshape(batch, seq, d)
```

```python
import functools
import math

import numpy as np
import jax
import jax.numpy as jnp
from jax import lax
from jax.experimental import pallas as pl
from jax.experimental.pallas import tpu as pltpu

F32 = jnp.float32
BF16 = jnp.bfloat16

D_MODEL = 1024
HEAD_DIM = 64
FOX_HEADS = 8
DIFF_HEADS = 4
SWA_HEADS = 16
SWA_KV_HEADS = 4
SWA_GROUP = SWA_HEADS // SWA_KV_HEADS
WINDOW = 128
NUM_BUCKETS = 32
MAX_DISTANCE = 128
FFN_HIDDEN = 2816
RMS_EPS = 1e-6
NEG_INF = -1e30
FOX_W = FOX_HEADS * HEAD_DIM
DIFF_W = DIFF_HEADS * 2 * HEAD_DIM
QK_SCALE = HEAD_DIM ** -0.5
LOG2E = math.log2(math.e)
Q_GAIN_SCALE = QK_SCALE * LOG2E

LANES = 128
BF16_SUBLANES = 16
FOX_VT_ROWS = HEAD_DIM + BF16_SUBLANES
MXU_TILE = 256
ROW_TILE = 1024
ATT_Q_BLOCK = 256
FOX_PAIRS_PER_STEP = 2
DIFF_HEADS_PER_STEP = 2
SWA_KV_PER_STEP = 2
FOX_LAG = 4
DIFF_LAG = 2
PROJ_LAG = 1
SWA_LAG = 4
HIDDEN_CHUNK = 768
V7X_VMEM_BYTES = 64 * 1024 * 1024
VMEM_LIMIT = V7X_VMEM_BYTES - 8 * 1024 * 1024

_NT = (((1,), (1,)), ((), ()))


def _dot(a, b):
    return jnp.dot(a, b, preferred_element_type=F32)


def _dot_nt(a, b):
    return lax.dot_general(a, b, _NT, preferred_element_type=F32)


def _lagged(items, first, second, depth=1):
    pending = []
    for item in items:
        pending.append(first(*item))
        if len(pending) > depth:
            second(*pending.pop(0))
    for p in pending:
        second(*p)


def _interleaved(items, first, second, depth):
    def drive(gens):
        done = {}
        while gens:
            for g in list(gens):
                try:
                    next(g)
                except StopIteration as stop:
                    done[id(g)] = stop.value
                    gens.remove(g)
        return done

    pending = []
    for item in items:
        g1 = first(*item)
        gens = [g1] if len(pending) < depth else [g1, second(*pending.pop(0))]
        pending.append(drive(gens)[id(g1)])
    for p in pending:
        drive([second(*p)])


def _hill_order(nblocks):
    return list(range(0, nblocks, 2)) + list(range(nblocks - 1 - nblocks % 2, 0, -2))


def _rms(xf, gain):
    ms = jnp.mean(xf * xf, axis=-1, keepdims=True)
    return xf * lax.rsqrt(ms + RMS_EPS) * gain


def _head_norm(acc, bd, gain):
    msq = _dot((acc * acc).astype(BF16), bd)
    return acc * lax.rsqrt(msq + RMS_EPS) * gain


def _bucket_ids(delta):
    n = np.maximum(delta, 0)
    max_exact = NUM_BUCKETS // 2
    nf = np.maximum(n, 1).astype(np.float64)
    large = max_exact + (np.log(nf / max_exact) / math.log(MAX_DISTANCE / max_exact)
                         * (NUM_BUCKETS - max_exact)).astype(np.int32)
    large = np.minimum(large, NUM_BUCKETS - 1)
    return np.where(n < max_exact, n, large).astype(np.int32)


def _bias_kernel(table_ref, bsw_ref, bg_ref, sw_ref, g_ref):
    h = pl.program_id(0)
    bsw = bsw_ref[...]
    sw = jnp.zeros(bsw.shape, F32)
    for b in range(NUM_BUCKETS):
        sw = jnp.where(bsw == b, table_ref[b, h] * LOG2E, sw)
    sw_ref[0] = jnp.where(bsw < 0, NEG_INF, sw)

    @pl.when(h < DIFF_HEADS)
    def _():
        bg = bg_ref[...]
        far = table_ref[NUM_BUCKETS - 1, h]
        g = jnp.zeros(bg.shape, F32)
        for b in range(NUM_BUCKETS):
            g = jnp.where(bg == b, (table_ref[b, h] - far) * LOG2E, g)
        g_ref[0] = jnp.where(bg < 0, NEG_INF, g)


def _bias_tiles(table):
    a = np.arange(WINDOW)[None, :]
    b = np.arange(2 * WINDOW)[:, None]
    d_sw = WINDOW + a - b
    bsw = np.where((d_sw >= 0) & (d_sw < WINDOW), _bucket_ids(d_sw), -1).astype(np.int32)
    a = np.arange(ATT_Q_BLOCK)[:, None]
    b = np.arange(2 * ATT_Q_BLOCK)[None, :]
    d_g = ATT_Q_BLOCK + a - b
    bg = np.where(d_g >= 0, _bucket_ids(d_g), -1).astype(np.int32)
    nh = table.shape[1]
    return pl.pallas_call(
        _bias_kernel,
        grid=(nh,),
        in_specs=[
            pl.BlockSpec(memory_space=pltpu.SMEM),
            pl.BlockSpec(bsw.shape, lambda h: (0, 0)),
            pl.BlockSpec(bg.shape, lambda h: (0, 0)),
        ],
        out_specs=[
            pl.BlockSpec((1,) + bsw.shape, lambda h: (h, 0, 0)),
            pl.BlockSpec((1,) + bg.shape, lambda h: (jnp.minimum(h, DIFF_HEADS - 1), 0, 0)),
        ],
        out_shape=[
            jax.ShapeDtypeStruct((nh,) + bsw.shape, F32),
            jax.ShapeDtypeStruct((DIFF_HEADS,) + bg.shape, F32),
        ],
        compiler_params=pltpu.CompilerParams(dimension_semantics=("arbitrary",)),
        name="rel_bias_tiles",
    )(table, jnp.asarray(bsw), jnp.asarray(bg))


def _proj0_kernel(*refs, tiles_per_seq, n_cast):
    x_ref, gain_ref, w_ref, wf_ref, bd_ref, hg_ref, bf_ref = refs[0:7]
    cast_src = refs[7:7 + n_cast]
    fq_ref, fk_ref, fv_ref, dq_ref, dk_ref, dv_ref, caug_ref = refs[7 + n_cast:14 + n_cast]
    cast_dst = refs[14 + n_cast:14 + 2 * n_cast]
    carry_ref = refs[14 + 2 * n_cast]
    t = pl.program_id(0)

    @pl.when(t == 0)
    def _():
        carry_ref[...] = jnp.zeros_like(carry_ref)

    h = _rms(x_ref[...], gain_ref[...]).astype(BF16)
    bd = bd_ref[...]
    rows = h.shape[0]

    outs = (fq_ref, fk_ref, fv_ref, dq_ref, dk_ref, dv_ref)
    gains = (0, 1, None, 2, 3, None)
    feature_major = (False, False, True, False, False, False)

    def project(g):
        return g, _dot(h, w_ref[:, g * FOX_W:(g + 1) * FOX_W])

    def epilogue(g, proj):
        if feature_major[g]:
            outs[g][0] = proj.T.astype(BF16)
            return
        if gains[g] is None:
            outs[g][...] = proj.astype(BF16)
            return
        for i in range(FOX_W // MXU_TILE):
            cols = slice(i * MXU_TILE, (i + 1) * MXU_TILE)
            outs[g][:, cols] = _head_norm(
                proj[:, cols], bd, hg_ref[gains[g]:gains[g] + 1, cols]).astype(BF16)

    pending = [project(g) for g in range(PROJ_LAG)]
    bits = lax.bitcast_convert_type(pending[0][1][0:8, 0:LANES], jnp.uint32)
    sixteen = jnp.full(bits.shape, 16, jnp.uint32)
    zero_row = lax.bitcast_convert_type(
        lax.shift_right_logical(lax.shift_right_logical(bits, sixteen), sixteen), F32)[0:1, :]

    z8 = _dot_nt(wf_ref[...], h)[0:FOX_HEADS, :] + bf_ref[...]
    y = -(jnp.maximum(-z8, 0.0) + jnp.log1p(jnp.exp(-jnp.abs(z8))))
    lane = lax.broadcasted_iota(jnp.int32, y.shape, 1)
    shift = 1
    while shift < rows:
        y = y + jnp.where(lane >= shift, pltpu.roll(y, shift, axis=1), 0.0)
        shift *= 2

    carry = jnp.where(t % tiles_per_seq == 0, 0.0, carry_ref[:, 0:1])
    c = y + carry
    carry_ref[...] = jnp.broadcast_to(c[:, rows - 1:rows], carry_ref.shape)
    cn = c * (-LOG2E)
    hi = cn.astype(BF16).astype(F32)
    r1 = cn - hi
    mid = r1.astype(BF16).astype(F32)
    lo = (r1 - mid).astype(BF16).astype(F32)
    half = jnp.concatenate([hi, mid, lo, jnp.zeros((HEAD_DIM - 3 * FOX_HEADS, rows), F32)], axis=0)
    caug_ref[...] = jnp.concatenate([half, half], axis=0).T.astype(BF16)

    for g in range(PROJ_LAG, len(outs)):
        pending.append(project(g))
        epilogue(*pending.pop(0))
    for p in pending:
        epilogue(*p)

    for src, dst in zip(cast_src, cast_dst):
        dst[...] = (src[...] + jnp.tile(zero_row, (1, src.shape[1] // LANES))).astype(BF16)


def _proj0(x2d, gain, w, wf, bd, hg, bf, seq, later_weights):
    n = x2d.shape[0]
    tm = ROW_TILE
    tps = seq // tm
    steps = n // tm
    slab = lambda a: pl.BlockSpec((a.shape[0] // steps, a.shape[1]), lambda t: (t, 0))
    for a in later_weights:
        assert a.shape[0] % (steps * BF16_SUBLANES) == 0, a.shape
    row_spec = pl.BlockSpec((tm, FOX_W), lambda t: (t, 0))
    col_spec = pl.BlockSpec((1, FOX_W, tm), lambda t: (t // tps, 0, t % tps))
    rows_shape = jax.ShapeDtypeStruct((n, FOX_W), BF16)
    cols_shape = jax.ShapeDtypeStruct((n // seq, FOX_W, seq), BF16)
    full = lambda shape: pl.BlockSpec(shape, lambda t: (0,) * len(shape))
    outs = pl.pallas_call(
        functools.partial(_proj0_kernel, tiles_per_seq=tps, n_cast=len(later_weights)),
        grid=(steps,),
        in_specs=[
            pl.BlockSpec((tm, D_MODEL), lambda t: (t, 0)),
            full((1, D_MODEL)),
            full(w.shape),
            full(wf.shape),
            full((MXU_TILE, MXU_TILE)),
            full(hg.shape),
            full(bf.shape),
        ] + [slab(a) for a in later_weights],
        out_specs=[row_spec, row_spec, col_spec, row_spec, row_spec, row_spec,
                   pl.BlockSpec((tm, LANES), lambda t: (t, 0))] + [slab(a) for a in later_weights],
        out_shape=[rows_shape, rows_shape, cols_shape, rows_shape, rows_shape, rows_shape,
                   jax.ShapeDtypeStruct((n, LANES), BF16)]
        + [jax.ShapeDtypeStruct(a.shape, BF16) for a in later_weights],
        scratch_shapes=[pltpu.VMEM((FOX_HEADS, LANES), F32)],
        compiler_params=pltpu.CompilerParams(dimension_semantics=("arbitrary",),
                                             vmem_limit_bytes=VMEM_LIMIT),
        name="proj_even",
    )(x2d, gain, w, wf, bd, hg, bf, *later_weights)
    return outs


def _fox_kernel(q_ref, k_ref, vt_ref, caug_ref, o_ref, ka_ref, va_ref, *, seq):
    jj = pl.program_id(1)
    qb = ATT_Q_BLOCK
    low_s = lax.broadcasted_iota(jnp.int32, (seq, LANES), 1) < HEAD_DIM
    ca = caug_ref[0]
    sub = lax.broadcasted_iota(jnp.int32, (FOX_VT_ROWS - HEAD_DIM, seq), 0)
    ones_row = jnp.where(sub == 0, 1.0, 0.0).astype(BF16)
    for p in range(FOX_PAIRS_PER_STEP):
        kp = k_ref[0, :, p * LANES:(p + 1) * LANES]
        ka_ref[2 * p] = jnp.where(low_s, kp, ca)
        ka_ref[2 * p + 1] = jnp.where(low_s, ca, kp)
        for hh in range(2):
            r = p * LANES + hh * HEAD_DIM
            va_ref[2 * p + hh, 0:HEAD_DIM, :] = vt_ref[0, r:r + HEAD_DIM, :]
            va_ref[2 * p + hh, HEAD_DIM:FOX_VT_ROWS, :] = ones_row
    lane_q = lax.broadcasted_iota(jnp.int32, (qb, LANES), 1)
    key_i = lax.broadcasted_iota(jnp.int32, (qb, qb), 0)
    qry_i = lax.broadcasted_iota(jnp.int32, (qb, qb), 1)
    causal = key_i <= qry_i
    owns = (lane_q < HEAD_DIM, lane_q >= HEAD_DIM)

    def scores(p, qi, hh):
        r0 = qi * qb
        head = 2 * (FOX_PAIRS_PER_STEP * jj + p) + hh
        rel = lane_q - (HEAD_DIM if hh == 0 else 0) - head
        pick = (rel == 0) | (rel == FOX_HEADS) | (rel == 2 * FOX_HEADS)
        qz = jnp.where(owns[hh], q_ref[0, r0:r0 + qb, p * LANES:(p + 1) * LANES],
                       jnp.where(pick, 1.0, 0.0).astype(BF16))
        s_all = _dot_nt(ka_ref[2 * p + hh, 0:(qi + 1) * qb, :], qz)
        yield
        tiles = [s_all[t * qb:(t + 1) * qb] for t in range(qi)]
        tiles.append(jnp.where(causal, s_all[qi * qb:(qi + 1) * qb], NEG_INF))
        return p, qi, hh, tiles

    halves = []

    def finish(p, qi, hh, tiles):
        r0 = qi * qb
        m = jnp.max(functools.reduce(jnp.maximum, tiles), axis=0, keepdims=True)
        acc = None
        for t, s in enumerate(tiles):
            part = _dot(va_ref[2 * p + hh, :, t * qb:(t + 1) * qb], jnp.exp2(s - m).astype(BF16))
            acc = part if acc is None else acc + part
            yield
        halves.append(acc[0:HEAD_DIM] / acc[HEAD_DIM:HEAD_DIM + 1])
        if hh == 1:
            o_ref[0, r0:r0 + qb, p * LANES:(p + 1) * LANES] = jnp.concatenate(
                [halves[-2], halves[-1]], axis=0).T.astype(BF16)

    _interleaved([(p, qi, hh) for p in range(FOX_PAIRS_PER_STEP)
                  for qi in _hill_order(seq // qb) for hh in range(2)], scores, finish,
                 depth=FOX_LAG)


def _fox_attention(fq, fk, fvt, caug, batch, seq):
    width = FOX_PAIRS_PER_STEP * LANES
    blk = pl.BlockSpec((1, seq, width), lambda b, j: (b, 0, j))
    nheads = 2 * FOX_PAIRS_PER_STEP
    return pl.pallas_call(
        functools.partial(_fox_kernel, seq=seq),
        grid=(batch, FOX_W // width),
        in_specs=[blk, blk, pl.BlockSpec((1, width, seq), lambda b, j: (b, j, 0)),
                  pl.BlockSpec((1, seq, LANES), lambda b, j: (b, 0, 0))],
        out_specs=blk,
        out_shape=jax.ShapeDtypeStruct((batch, seq, FOX_W), BF16),
        scratch_shapes=[pltpu.VMEM((nheads, seq, LANES), BF16),
                        pltpu.VMEM((nheads, FOX_VT_ROWS, seq), BF16)],
        compiler_params=pltpu.CompilerParams(dimension_semantics=("parallel", "parallel"),
                                             vmem_limit_bytes=VMEM_LIMIT),
        name="fox_attention",
    )(fq, fk, fvt, caug)


def _diff_kernel(lam_ref, g_ref, q_ref, k_ref, v_ref, gb_ref, o_ref, va_ref, *, seq, lambda_init):
    qb = ATT_Q_BLOCK
    lv = lam_ref[...]
    lam = (jnp.exp(jnp.sum(lv[0:1] * lv[1:2], axis=-1, keepdims=True))
           - jnp.exp(jnp.sum(lv[2:3] * lv[3:4], axis=-1, keepdims=True)) + lambda_init)
    for p in range(DIFF_HEADS_PER_STEP):
        va_ref[p, :, 0:LANES] = v_ref[0, :, p * LANES:(p + 1) * LANES]
        va_ref[p, :, LANES:2 * LANES] = jnp.ones((seq, LANES), BF16)
    lane_q = lax.broadcasted_iota(jnp.int32, (qb, LANES), 1)
    bias_prev = [jnp.concatenate([gb_ref[p, :, 0:qb]] * 2, axis=0)
                 for p in range(DIFF_HEADS_PER_STEP)]
    bias_diag = [jnp.concatenate([gb_ref[p, :, qb:2 * qb]] * 2, axis=0)
                 for p in range(DIFF_HEADS_PER_STEP)]
    gain = g_ref[...] * (1.0 - lambda_init)
    zero = jnp.zeros((), BF16)

    def scores(p, qi):
        r0 = qi * qb
        qp = q_ref[0, r0:r0 + qb, p * LANES:(p + 1) * LANES]
        qz = jnp.concatenate([jnp.where(lane_q < HEAD_DIM, qp, zero),
                              jnp.where(lane_q < HEAD_DIM, zero, qp)], axis=0)
        tiles = []
        for t in range(qi + 1):
            s = _dot_nt(qz, k_ref[0, t * qb:(t + 1) * qb, p * LANES:(p + 1) * LANES])
            tiles.append(s + bias_diag[p] if t == qi else s + bias_prev[p] if t == qi - 1 else s)
            yield
        return p, qi, tiles

    def finish(p, qi, tiles):
        r0 = qi * qb
        m = jnp.max(functools.reduce(jnp.maximum, tiles), axis=-1, keepdims=True)
        acc = None
        for t, s in enumerate(tiles):
            part = _dot(jnp.exp2(s - m).astype(BF16), va_ref[p, t * qb:(t + 1) * qb, :])
            acc = part if acc is None else acc + part
            yield
        a = acc[:, 0:LANES] / acc[:, LANES:2 * LANES]
        o = a[0:qb] - lam * a[qb:2 * qb]
        o_ref[0, r0:r0 + qb, p * LANES:(p + 1) * LANES] = _rms(o, gain).astype(BF16)

    _interleaved([(p, qi) for p in range(DIFF_HEADS_PER_STEP) for qi in _hill_order(seq // qb)],
                 scores, finish, depth=DIFF_LAG)


def _diff_attention(lam4, subln, dq, dk, dv, gbias, batch, seq, lambda_init):
    hps = DIFF_HEADS_PER_STEP
    blk = pl.BlockSpec((1, seq, hps * LANES), lambda h, b: (b, 0, h))
    return pl.pallas_call(
        functools.partial(_diff_kernel, seq=seq, lambda_init=lambda_init),
        grid=(DIFF_HEADS // hps, batch),
        in_specs=[
            pl.BlockSpec(lam4.shape, lambda h, b: (0, 0)),
            pl.BlockSpec(subln.shape, lambda h, b: (0, 0)),
            blk, blk, blk,
            pl.BlockSpec((hps, ATT_Q_BLOCK, 2 * ATT_Q_BLOCK), lambda h, b: (h, 0, 0)),
        ],
        out_specs=blk,
        out_shape=jax.ShapeDtypeStruct((batch, seq, DIFF_W), BF16),
        scratch_shapes=[pltpu.VMEM((hps, seq, 2 * LANES), BF16)],
        compiler_params=pltpu.CompilerParams(dimension_semantics=("parallel", "parallel"),
                                             vmem_limit_bytes=VMEM_LIMIT),
        name="diff_attention",
    )(lam4, subln, dq, dk, dv, gbias)


def _proj1_kernel(x_ref, gain_ref, w_ref, bd_ref, hg_ref, q_ref, kk_ref, vt_ref):
    h = _rms(x_ref[...], gain_ref[...]).astype(BF16)
    bd = bd_ref[...]
    qw = SWA_HEADS * HEAD_DIM
    kw = SWA_KV_HEADS * HEAD_DIM
    low = lax.broadcasted_iota(jnp.int32, (h.shape[0], LANES), 1) < HEAD_DIM

    def project(c0):
        return c0, _dot(h, w_ref[:, c0:c0 + 2 * MXU_TILE])

    def epilogue(c0, proj):
        if c0 < qw:
            for i in range(2):
                cols = slice(i * MXU_TILE, (i + 1) * MXU_TILE)
                q_ref[:, c0 + i * MXU_TILE:c0 + (i + 1) * MXU_TILE] = _head_norm(
                    proj[:, cols], bd, hg_ref[0:1, :]).astype(BF16)
            return
        kn = _head_norm(proj[:, 0:kw], bd, hg_ref[1:2, :])
        for p in range(kw // LANES):
            pair = kn[:, p * LANES:(p + 1) * LANES]
            swapped = pltpu.roll(pair, HEAD_DIM, axis=1)
            kk_ref[:, (2 * p) * LANES:(2 * p + 1) * LANES] = jnp.where(low, pair, swapped).astype(BF16)
            kk_ref[:, (2 * p + 1) * LANES:(2 * p + 2) * LANES] = jnp.where(low, swapped, pair).astype(BF16)
        vt_ref[0] = proj[:, kw:2 * kw].T.astype(BF16)

    _lagged([(qw,)] + [(c0,) for c0 in range(0, qw, 2 * MXU_TILE)], project, epilogue,
            depth=PROJ_LAG)


def _proj1(x2d, gain, w, bd, hg, seq):
    n = x2d.shape[0]
    tm = ROW_TILE
    tps = seq // tm
    qw = SWA_HEADS * HEAD_DIM
    kw = SWA_KV_HEADS * HEAD_DIM
    full = lambda shape: pl.BlockSpec(shape, lambda t: (0,) * len(shape))
    return pl.pallas_call(
        _proj1_kernel,
        grid=(n // tm,),
        in_specs=[
            pl.BlockSpec((tm, D_MODEL), lambda t: (t, 0)),
            full((1, D_MODEL)),
            full(w.shape),
            full((MXU_TILE, MXU_TILE)),
            full(hg.shape),
        ],
        out_specs=[pl.BlockSpec((tm, qw), lambda t: (t, 0)),
                   pl.BlockSpec((tm, 2 * kw), lambda t: (t, 0)),
                   pl.BlockSpec((1, kw, tm), lambda t: (t // tps, 0, t % tps))],
        out_shape=[jax.ShapeDtypeStruct((n, qw), BF16),
                   jax.ShapeDtypeStruct((n, 2 * kw), BF16),
                   jax.ShapeDtypeStruct((n // seq, kw, seq), BF16)],
        compiler_params=pltpu.CompilerParams(dimension_semantics=("parallel",),
                                             vmem_limit_bytes=VMEM_LIMIT),
        name="proj_odd",
    )(x2d, gain, w, bd, hg)


def _swa_kernel(sink_ref, q_ref, kk_ref, vt_ref, sb_ref, o_ref, va_ref, *, seq):
    jj = pl.program_id(0)
    w = WINDOW
    gw = SWA_GROUP * HEAD_DIM
    sub = lax.broadcasted_iota(jnp.int32, (LANES - HEAD_DIM, seq), 0)
    sinks = []
    for p in range(SWA_KV_PER_STEP):
        va_ref[p, 0:HEAD_DIM, :] = vt_ref[0, p * HEAD_DIM:(p + 1) * HEAD_DIM, :]
        va_ref[p, HEAD_DIM:LANES, :] = jnp.where(sub == 0, 1.0, 0.0).astype(BF16)
        head0 = (jj * SWA_KV_PER_STEP + p) * SWA_GROUP
        sinks.append(jnp.concatenate(
            [jnp.full((1, w), sink_ref[head0 + g] * LOG2E, F32) for g in range(SWA_GROUP)],
            axis=1))
    low = lax.broadcasted_iota(jnp.int32, (w, LANES), 1) < HEAD_DIM
    zero = jnp.zeros((), BF16)

    def scores(p, n):
        r0 = n * w
        qb = q_ref[0, r0:r0 + w, p * gw:(p + 1) * gw]
        qa, qc = qb[:, 0:LANES], qb[:, LANES:2 * LANES]
        q4 = jnp.concatenate([jnp.where(low, qa, zero), jnp.where(low, zero, qa),
                              jnp.where(low, qc, zero), jnp.where(low, zero, qc)], axis=0)
        lanes = slice(p * LANES, (p + 1) * LANES)
        if n == 0:
            return p, n, _dot_nt(kk_ref[0, 0:w, lanes], q4) + sb_ref[p, w:2 * w, :]
        return p, n, _dot_nt(kk_ref[0, r0 - w:r0 + w, lanes], q4) + sb_ref[p]

    def finish(p, n, st):
        r0 = n * w
        vals = va_ref[p, :, 0:w] if n == 0 else va_ref[p, :, r0 - w:r0 + w]
        m = jnp.maximum(jnp.max(st, axis=0, keepdims=True), sinks[p])
        acc = _dot(vals, jnp.exp2(st - m).astype(BF16))
        denom = acc[HEAD_DIM:HEAD_DIM + 1, :] + jnp.exp2(sinks[p] - m)
        ot = acc[0:HEAD_DIM, :] / denom
        o_ref[0, r0:r0 + w, p * gw:(p + 1) * gw] = jnp.concatenate(
            [jnp.concatenate([ot[:, 0:w], ot[:, w:2 * w]], axis=0).T,
             jnp.concatenate([ot[:, 2 * w:3 * w], ot[:, 3 * w:4 * w]], axis=0).T],
            axis=1).astype(BF16)

    _lagged([(p, n) for p in range(SWA_KV_PER_STEP) for n in range(seq // w)], scores, finish,
            depth=SWA_LAG)


def _swa_attention(sinks, q, kk, vt, swbias, batch, seq):
    kps = SWA_KV_PER_STEP
    gw = kps * SWA_GROUP * HEAD_DIM
    return pl.pallas_call(
        functools.partial(_swa_kernel, seq=seq),
        grid=(SWA_KV_HEADS // kps, batch),
        in_specs=[
            pl.BlockSpec(memory_space=pltpu.SMEM),
            pl.BlockSpec((1, seq, gw), lambda h, b: (b, 0, h)),
            pl.BlockSpec((1, seq, kps * LANES), lambda h, b: (b, 0, h)),
            pl.BlockSpec((1, kps * HEAD_DIM, seq), lambda h, b: (b, h, 0)),
            pl.BlockSpec((kps, 2 * WINDOW, SWA_GROUP * WINDOW), lambda h, b: (h, 0, 0)),
        ],
        out_specs=pl.BlockSpec((1, seq, gw), lambda h, b: (b, 0, h)),
        out_shape=jax.ShapeDtypeStruct((batch, seq, SWA_HEADS * HEAD_DIM), BF16),
        scratch_shapes=[pltpu.VMEM((kps, LANES, seq), BF16)],
        compiler_params=pltpu.CompilerParams(dimension_semantics=("parallel", "parallel"),
                                             vmem_limit_bytes=VMEM_LIMIT),
        name="swa_attention",
    )(sinks, q, kk, vt, swbias)


def _outmlp_kernel(*refs, n_mix):
    x_ref = refs[0]
    a_refs = refs[1:1 + n_mix]
    wo_ref, gain_ref, wg_ref, wu_ref, wd_ref, o_ref = refs[1 + n_mix:]
    mix = jnp.concatenate([a_ref[...] for a_ref in a_refs], axis=1)
    x1 = x_ref[...] + _dot(mix, wo_ref[...])
    h = _rms(x1, gain_ref[...]).astype(BF16)
    out = x1
    for c0 in range(0, FFN_HIDDEN, HIDDEN_CHUNK):
        c1 = min(c0 + HIDDEN_CHUNK, FFN_HIDDEN)
        g = _dot(h, wg_ref[:, c0:c1])
        u = _dot(h, wu_ref[:, c0:c1])
        y = (g * (1.0 / (1.0 + jnp.exp(-g))) * u).astype(BF16)
        out = out + _dot(y, wd_ref[c0:c1, :])
    o_ref[...] = out


def _outmlp(x2d, mixes, wo, gain, wg, wu, wd, layer):
    n = x2d.shape[0]
    tm = ROW_TILE
    full = lambda shape: pl.BlockSpec(shape, lambda t: (0,) * len(shape),
                                      pipeline_mode=pl.Buffered(1))
    of_layer = lambda w: pl.BlockSpec((None,) + w.shape[1:], lambda t: (layer, 0, 0),
                                      pipeline_mode=pl.Buffered(1))
    in_specs = [pl.BlockSpec((tm, D_MODEL), lambda t: (t, 0))]
    in_specs += [pl.BlockSpec((tm, a.shape[1]), lambda t: (t, 0)) for a in mixes]
    in_specs += [full(wo.shape), full(gain.shape), of_layer(wg), of_layer(wu), of_layer(wd)]
    return pl.pallas_call(
        functools.partial(_outmlp_kernel, n_mix=len(mixes)),
        grid=(n // tm,),
        in_specs=in_specs,
        out_specs=pl.BlockSpec((tm, D_MODEL), lambda t: (t, 0)),
        out_shape=jax.ShapeDtypeStruct((n, D_MODEL), F32),
        compiler_params=pltpu.CompilerParams(dimension_semantics=("parallel",),
                                             vmem_limit_bytes=VMEM_LIMIT),
        name="outproj_swiglu",
    )(x2d, *mixes, wo, gain, wg, wu, wd)


def _block_diag_mean():
    i = np.arange(MXU_TILE)
    same = (i[:, None] // HEAD_DIM) == (i[None, :] // HEAD_DIM)
    return jnp.asarray(same / HEAD_DIM, BF16)


def kernel(x, rel_bias_table, ev_attn_norm, ev_w_in, ev_b_forget, ev_fox_q_norm, ev_fox_k_norm,
           ev_diff_q_norm, ev_diff_k_norm, ev_lambda_q1, ev_lambda_k1, ev_lambda_q2, ev_lambda_k2,
           ev_diff_subln, ev_w_out, od_attn_norm, od_w_qkv, od_q_norm, od_k_norm, od_sinks, od_w_out,
           ffn_norm, w_gate, w_up, w_down):
    batch, seq, d = x.shape
    n = batch * seq
    x2d = x.reshape(n, d)
    bd = _block_diag_mean()
    swbias, gbias = _bias_tiles(rel_bias_table.astype(F32))
    swbias = swbias.reshape(SWA_KV_HEADS, SWA_GROUP, 2 * WINDOW, WINDOW).transpose(0, 2, 1, 3)
    swbias = swbias.reshape(SWA_KV_HEADS, 2 * WINDOW, SWA_GROUP * WINDOW)

    lambda_init = 0.8 - 0.6 * math.exp(-0.3 * 0)
    w_in = ev_w_in[0]
    ff0 = 3 * FOX_W
    d0 = ff0 + FOX_HEADS
    w0 = jnp.concatenate([w_in[:, 0:ff0], w_in[:, d0:d0 + 3 * DIFF_W]], axis=1).astype(BF16)
    wf = jnp.concatenate(
        [w_in[:, ff0:d0].T, jnp.zeros((BF16_SUBLANES - FOX_HEADS, d), w_in.dtype)],
        axis=0).astype(BF16)
    tile8 = lambda g, scale=1.0: jnp.tile(g.astype(F32) * scale, FOX_W // HEAD_DIM)
    hg0 = jnp.stack([tile8(ev_fox_q_norm[0], Q_GAIN_SCALE), tile8(ev_fox_k_norm[0]),
                     tile8(ev_diff_q_norm[0], Q_GAIN_SCALE), tile8(ev_diff_k_norm[0])])
    bf = ev_b_forget[0].astype(F32).reshape(FOX_HEADS, 1)
    depth, hidden = w_gate.shape[0], w_gate.shape[2]
    later = [w_gate.reshape(depth * d, hidden), w_up.reshape(depth * d, hidden),
             w_down.reshape(depth * hidden, d), ev_w_out[0], od_w_qkv[0], od_w_out[0]]
    fq, fk, fvt, dq, dk, dv, caug, wg, wu, wd, wo0, wqkv, wo1 = _proj0(
        x2d, ev_attn_norm[0].astype(F32).reshape(1, d), w0, wf, bd, hg0, bf, seq,
        [a.astype(F32) for a in later])
    wg, wu = wg.reshape(depth, d, hidden), wu.reshape(depth, d, hidden)
    wd = wd.reshape(depth, hidden, d)
    r3 = lambda a: a.reshape(batch, seq, a.shape[-1])
    fox = _fox_attention(r3(fq), r3(fk), fvt, r3(caug), batch, seq)
    lam4 = jnp.stack([ev_lambda_q1[0], ev_lambda_k1[0], ev_lambda_q2[0], ev_lambda_k2[0]]).astype(F32)
    dout = _diff_attention(lam4, ev_diff_subln[0].astype(F32).reshape(1, 2 * HEAD_DIM),
                           r3(dq), r3(dk), r3(dv), gbias, batch, seq, lambda_init)
    x2d = _outmlp(x2d, [fox.reshape(n, FOX_W), dout.reshape(n, DIFF_W)], wo0,
                  ffn_norm[0].astype(F32).reshape(1, d), wg, wu, wd, 0)

    tile4 = lambda g, scale=1.0: jnp.tile(g.astype(F32) * scale, MXU_TILE // HEAD_DIM)
    hg1 = jnp.stack([tile4(od_q_norm[0], Q_GAIN_SCALE), tile4(od_k_norm[0])])
    q, kk, vt = _proj1(x2d, od_attn_norm[0].astype(F32).reshape(1, d), wqkv,
                       bd, hg1, seq)
    swa = _swa_attention(od_sinks[0].astype(F32), r3(q), r3(kk), vt, swbias, batch, seq)
    x2d = _outmlp(x2d, [swa.reshape(n, SWA_HEADS * HEAD_DIM)], wo1,
                  ffn_norm[1].astype(F32).reshape(1, d), wg, wu, wd, 1)
    return x2d.reshape(batch, seq, d)
```
